```python
import math
import jax
import jax.numpy as jnp
from jax import lax
import numpy as np

D_MODEL = 2048
BATCH = 8
SEQ = 4096
DEPTH = 4

ATT_PATTERNS = ((128, 1), (512, 4), (2048, 16))
N_GROUPS_A = len(ATT_PATTERNS)
HEADS_PER_GROUP = 8
HEAD_DIM = 64
N_HEADS_A = N_GROUPS_A * HEADS_PER_GROUP
QKV_WIDTH_A = N_HEADS_A * HEAD_DIM
WIDTH_A = HEADS_PER_GROUP * HEAD_DIM
ATT_BLOCK = 128
N_REL_BUCKETS = 32
REL_MAX_DIST = 2048
NEG_INF = -1e30
CHUNK = 128
WIDTH_B = 768
N_GROUPS_B = 6
GROUP_B = WIDTH_B // N_GROUPS_B
WIDTH_C = 768
SSM_GROUP = 16
N_GROUPS_C = WIDTH_C // SSM_GROUP
SSM_STATE = 64
DT_MIN = 1e-3
DT_MAX = 1e-1
N_BRANCH = 3
D_FF = -(-8 * D_MODEL // (3 * 256)) * 256
ALPHA = (2 * DEPTH) ** 0.25
BETA = (8 * DEPTH) ** -0.25
IN_SPLIT = (QKV_WIDTH_A, QKV_WIDTH_A, QKV_WIDTH_A, 2 * WIDTH_B, WIDTH_C, N_BRANCH * D_MODEL)
IN_COLS = sum(IN_SPLIT)

kernel_name = 'hybrid_dilated_attn_gmlp_s5_deepnorm'


def _layer_norm(x, g, b, eps=1e-5):
    xf = x.astype(jnp.float32)
    mu = jnp.mean(xf, axis=-1, keepdims=True)
    var = jnp.mean(jnp.square(xf - mu), axis=-1, keepdims=True)
    return ((xf - mu) * lax.rsqrt(var + eps) * g + b).astype(x.dtype)


def _t5_bucket(dist):
    max_exact = N_REL_BUCKETS // 2
    d = np.maximum(dist, 1).astype(np.float32)
    scale = (N_REL_BUCKETS - max_exact) / math.log(REL_MAX_DIST / max_exact)
    large = max_exact + (np.log(d / max_exact) * scale).astype(np.int32)
    large = np.minimum(large, N_REL_BUCKETS - 1)
    return np.where(dist < max_exact, dist, large).astype(np.int32)


def _band_steps():
    i = np.arange(ATT_BLOCK)[:, None]
    kk = np.arange(2 * ATT_BLOCK)[None, :]
    return ATT_BLOCK + i - kk


def _group_rel_bias(rel_bias, g, dilation):
    bucket = _t5_bucket(np.maximum(_band_steps(), 0) * dilation)
    cols = rel_bias[:, g * HEADS_PER_GROUP:(g + 1) * HEADS_PER_GROUP]
    return jnp.transpose(cols[bucket], (2, 0, 1)).astype(jnp.float32)


def _dilated_window_attention(q, k, v, bias, dilation, n_steps):
    bsz, s, h, hd = q.shape
    L = s // dilation
    nb = -(-L // ATT_BLOCK)
    lp = nb * ATT_BLOCK

    def to_streams(t):
        t = t.reshape(bsz, L, dilation, h, hd).transpose(0, 2, 1, 3, 4)
        return jnp.pad(t, ((0, 0), (0, 0), (0, lp - L), (0, 0), (0, 0)))

    def to_band(t):
        t = jnp.pad(t, ((0, 0), (0, 0), (ATT_BLOCK, 0), (0, 0), (0, 0)))
        t = t.reshape(bsz, dilation, nb + 1, ATT_BLOCK, h, hd)
        return jnp.concatenate([t[:, :, :-1], t[:, :, 1:]], axis=3)

    qb = to_streams(q).reshape(bsz, dilation, nb, ATT_BLOCK, h, hd)
    kb = to_band(to_streams(k))
    vb = to_band(to_streams(v))
    steps = _band_steps()
    key_idx = (np.arange(nb)[:, None] - 1) * ATT_BLOCK + np.arange(2 * ATT_BLOCK)[None, :]
    mask = ((steps >= 0) & (steps <= n_steps))[None, None] & (key_idx >= 0)[:, None, None, :]
    logits = jnp.einsum('brnqhd,brnkhd->brnhqk', qb, kb, preferred_element_type=jnp.float32)
    logits = jnp.where(mask, logits * (hd ** -0.5) + bias, NEG_INF)
    m = jnp.max(logits, axis=-1, keepdims=True)
    p = jnp.exp(logits - m)
    den = jnp.sum(p, axis=-1, keepdims=True)
    o = jnp.einsum('brnhqk,brnkhd->brnqhd', p, vb.astype(jnp.float32)) / jnp.swapaxes(den, 3, 4)
    lse = jnp.swapaxes((m + jnp.log(den))[..., 0], 3, 4)
    o = o.reshape(bsz, dilation, lp, h, hd)[:, :, :L].transpose(0, 2, 1, 3, 4).reshape(bsz, s, h, hd)
    lse = lse.reshape(bsz, dilation, lp, h)[:, :, :L].transpose(0, 2, 1, 3).reshape(bsz, s, h)
    return o, lse


def _spatial_gating(z, ln_g, ln_b, w_s, b_s):
    bsz, s, _ = z.shape
    u, v = jnp.split(z, 2, axis=-1)
    v = _layer_norm(v, ln_g, ln_b)
    vc = v.reshape(bsz, s // CHUNK, CHUNK, N_GROUPS_B, GROUP_B)
    w = jnp.tril(w_s)
    mixed = jnp.einsum('gts,bnsgc->bntgc', w, vc) + jnp.transpose(b_s)[:, :, None]
    return u * mixed.reshape(bsz, s, WIDTH_B)


def _ssm_combine(e1, e2):
    a1r, a1i, b1r, b1i = e1
    a2r, a2i, b2r, b2i = e2
    return (a2r * a1r - a2i * a1i,
            a2r * a1i + a2i * a1r,
            a2r * b1r - a2i * b1i + b2r,
            a2r * b1i + a2i * b1r + b2i)


def _s5(u, lam_re, lam_im, log_dt, b_re, b_im, c_re, c_im, d_skip):
    bsz, s, _ = u.shape
    f32 = jnp.float32
    uf = u.astype(f32)
    ug = uf.reshape(bsz, s, N_GROUPS_C, SSM_GROUP)
    lr = lam_re.astype(f32)
    li = lam_im.astype(f32)
    dt = jnp.exp(log_dt.astype(f32))[:, None]
    mag = jnp.exp(lr * dt)
    ab_re = mag * jnp.cos(li * dt)
    ab_im = mag * jnp.sin(li * dt)
    nrm = lr * lr + li * li
    cr = ((ab_re - 1.0) * lr + ab_im * li) / nrm
    ci = (ab_im * lr - (ab_re - 1.0) * li) / nrm
    bb_re = cr[..., None] * b_re - ci[..., None] * b_im
    bb_im = cr[..., None] * b_im + ci[..., None] * b_re
    bu_re = jnp.einsum('bsgh,gph->bsgp', ug, bb_re.astype(f32))
    bu_im = jnp.einsum('bsgh,gph->bsgp', ug, bb_im.astype(f32))
    a_re = jnp.broadcast_to(ab_re[None, None], (1, s, N_GROUPS_C, SSM_STATE))
    a_im = jnp.broadcast_to(ab_im[None, None], (1, s, N_GROUPS_C, SSM_STATE))
    _, _, xr, xi = lax.associative_scan(_ssm_combine, (a_re, a_im, bu_re, bu_im), axis=1)
    y = (jnp.einsum('bsgp,ghp->bsgh', xr, c_re.astype(f32))
         - jnp.einsum('bsgp,ghp->bsgh', xi, c_im.astype(f32)))
    return y.reshape(bsz, s, WIDTH_C) + d_skip.astype(f32) * uf


def _fwd_setup_inputs(seed: int = 0) -> dict:
    key = jax.random.key(seed)
    ks = iter(jax.random.split(key, 32))
    nrm = lambda shape, scale: jax.random.normal(next(ks), shape, jnp.float32) * scale
    gain = lambda shape: 1.0 + nrm(shape, 0.02)
    n_idx = jnp.arange(SSM_STATE, dtype=jnp.float32)
    return {
        'x': nrm((BATCH, SEQ, D_MODEL), 1.0),
        'w_in': nrm((DEPTH, D_MODEL, IN_COLS), D_MODEL ** -0.5),
        'b_in': nrm((DEPTH, IN_COLS), 0.02),
        'rel_bias': nrm((N_REL_BUCKETS, N_HEADS_A), 0.1),
        'sgu_ln_g': gain((DEPTH, WIDTH_B)),
        'sgu_ln_b': nrm((DEPTH, WIDTH_B), 0.02),
        'w_s': nrm((DEPTH, N_GROUPS_B, CHUNK, CHUNK), CHUNK ** -0.5),
        'b_s': gain((DEPTH, N_GROUPS_B, CHUNK)),
        'lam_re': -0.5 + nrm((DEPTH, N_GROUPS_C, SSM_STATE), 0.01),
        'lam_im': math.pi * n_idx + nrm((DEPTH, N_GROUPS_C, SSM_STATE), 0.01),
        'log_dt': jax.random.uniform(next(ks), (DEPTH, N_GROUPS_C), jnp.float32,
                                     minval=math.log(DT_MIN), maxval=math.log(DT_MAX)),
        'b_re': nrm((DEPTH, N_GROUPS_C, SSM_STATE, SSM_GROUP), (2 * SSM_GROUP) ** -0.5),
        'b_im': nrm((DEPTH, N_GROUPS_C, SSM_STATE, SSM_GROUP), (2 * SSM_GROUP) ** -0.5),
        'c_re': nrm((DEPTH, N_GROUPS_C, SSM_GROUP, SSM_STATE), SSM_STATE ** -0.5),
        'c_im': nrm((DEPTH, N_GROUPS_C, SSM_GROUP, SSM_STATE), SSM_STATE ** -0.5),
        'd_skip': nrm((DEPTH, WIDTH_C), 1.0),
        'w_glu': nrm((DEPTH, WIDTH_C, WIDTH_C), WIDTH_C ** -0.5),
        'b_glu': nrm((DEPTH, WIDTH_C), 0.02),
        'w_pa': nrm((DEPTH, WIDTH_A, D_MODEL), WIDTH_A ** -0.5),
        'w_pb': nrm((DEPTH, WIDTH_B, D_MODEL), WIDTH_B ** -0.5),
        'w_pc': nrm((DEPTH, WIDTH_C, D_MODEL), WIDTH_C ** -0.5),
        'w_o': nrm((DEPTH, D_MODEL, D_MODEL), BETA * D_MODEL ** -0.5),
        'ln1_g': gain((DEPTH, D_MODEL)),
        'ln1_b': nrm((DEPTH, D_MODEL), 0.02),
        'w_ffn_in': nrm((DEPTH, D_MODEL, 2 * D_FF), D_MODEL ** -0.5),
        'w_ffn_out': nrm((DEPTH, D_FF, D_MODEL), BETA * D_FF ** -0.5),
        'ln2_g': gain((DEPTH, D_MODEL)),
        'ln2_b': nrm((DEPTH, D_MODEL), 0.02),
    }


def _fwd_reference(x, w_in, b_in, rel_bias, sgu_ln_g, sgu_ln_b, w_s, b_s, lam_re, lam_im, log_dt,
              b_re, b_im, c_re, c_im, d_skip, w_glu, b_glu, w_pa, w_pb, w_pc, w_o,
              ln1_g, ln1_b, w_ffn_in, w_ffn_out, ln2_g, ln2_b):
    dt = x.dtype
    bsz, s, _ = x.shape
    offs = np.cumsum(IN_SPLIT)[:-1].tolist()
    group_bias = [_group_rel_bias(rel_bias, g, dil) for g, (_, dil) in enumerate(ATT_PATTERNS)]
    for l in range(DEPTH):
        proj = x @ w_in[l] + b_in[l]
        q, k, v, zb, uc, gl = jnp.split(proj, offs, axis=-1)
        q = q.reshape(bsz, s, N_HEADS_A, HEAD_DIM)
        k = k.reshape(bsz, s, N_HEADS_A, HEAD_DIM)
        v = v.reshape(bsz, s, N_HEADS_A, HEAD_DIM)
        outs, lses = [], []
        for g, (window, dil) in enumerate(ATT_PATTERNS):
            sl = slice(g * HEADS_PER_GROUP, (g + 1) * HEADS_PER_GROUP)
            o_g, lse_g = _dilated_window_attention(q[:, :, sl], k[:, :, sl], v[:, :, sl],
                                                   group_bias[g], dil, window // dil)
            outs.append(o_g)
            lses.append(lse_g)
        wts = jax.nn.softmax(jnp.stack(lses, axis=0), axis=0)
        ya = jnp.sum(wts[..., None] * jnp.stack(outs, axis=0), axis=0)
        ya = ya.reshape(bsz, s, WIDTH_A).astype(dt)
        yb = _spatial_gating(jax.nn.gelu(zb), sgu_ln_g[l], sgu_ln_b[l], w_s[l], b_s[l]).astype(dt)
        yc = jax.nn.gelu(_s5(uc, lam_re[l], lam_im[l], log_dt[l], b_re[l], b_im[l],
                             c_re[l], c_im[l], d_skip[l]))
        yc = (yc * jax.nn.sigmoid(yc @ w_glu[l] + b_glu[l])).astype(dt)
        gates = jax.nn.sigmoid(gl.reshape(bsz, s, N_BRANCH, D_MODEL))
        merged = (gates[:, :, 0] * (ya @ w_pa[l]) + gates[:, :, 1] * (yb @ w_pb[l])
                  + gates[:, :, 2] * (yc @ w_pc[l]))
        x = _layer_norm(ALPHA * x + merged @ w_o[l], ln1_g[l], ln1_b[l]).astype(dt)
        gate_f, up = jnp.split(x @ w_ffn_in[l], 2, axis=-1)
        f = (jax.nn.silu(gate_f) * up) @ w_ffn_out[l]
        x = _layer_norm(ALPHA * x + f, ln2_g[l], ln2_b[l]).astype(dt)
    return x


import jax as _jax
import jax.numpy as _jnp

TWIN_FORMAT = 'train_step'
FWD_PARAMS = ['x', 'w_in', 'b_in', 'rel_bias', 'sgu_ln_g', 'sgu_ln_b', 'w_s', 'b_s', 'lam_re', 'lam_im', 'log_dt', 'b_re', 'b_im', 'c_re', 'c_im', 'd_skip', 'w_glu', 'b_glu', 'w_pa', 'w_pb', 'w_pc', 'w_o', 'ln1_g', 'ln1_b', 'w_ffn_in', 'w_ffn_out', 'ln2_g', 'ln2_b']
TWIN_WEIGHTS = ['w_in', 'b_in', 'rel_bias', 'sgu_ln_g', 'sgu_ln_b', 'w_s', 'b_s', 'lam_re', 'lam_im', 'log_dt', 'b_re', 'b_im', 'c_re', 'c_im', 'd_skip', 'w_glu', 'b_glu', 'w_pa', 'w_pb', 'w_pc', 'w_o', 'ln1_g', 'ln1_b', 'w_ffn_in', 'w_ffn_out', 'ln2_g', 'ln2_b']
TWIN_DIFF_INPUT = 'x'
TWIN_INPUTS = ['x', 'w_in', 'b_in', 'rel_bias', 'sgu_ln_g', 'sgu_ln_b', 'w_s', 'b_s', 'lam_re', 'lam_im', 'log_dt', 'b_re', 'b_im', 'c_re', 'c_im', 'd_skip', 'w_glu', 'b_glu', 'w_pa', 'w_pb', 'w_pc', 'w_o', 'ln1_g', 'ln1_b', 'w_ffn_in', 'w_ffn_out', 'ln2_g', 'ln2_b', 'loss_target', 'm_w_in', 'm_b_in', 'm_rel_bias', 'm_sgu_ln_g', 'm_sgu_ln_b', 'm_w_s', 'm_b_s', 'm_lam_re', 'm_lam_im', 'm_log_dt', 'm_b_re', 'm_b_im', 'm_c_re', 'm_c_im', 'm_d_skip', 'm_w_glu', 'm_b_glu', 'm_w_pa', 'm_w_pb', 'm_w_pc', 'm_w_o', 'm_ln1_g', 'm_ln1_b', 'm_w_ffn_in', 'm_w_ffn_out', 'm_ln2_g', 'm_ln2_b', 'v_w_in', 'v_b_in', 'v_rel_bias', 'v_sgu_ln_g', 'v_sgu_ln_b', 'v_w_s', 'v_b_s', 'v_lam_re', 'v_lam_im', 'v_log_dt', 'v_b_re', 'v_b_im', 'v_c_re', 'v_c_im', 'v_d_skip', 'v_w_glu', 'v_b_glu', 'v_w_pa', 'v_w_pb', 'v_w_pc', 'v_w_o', 'v_ln1_g', 'v_ln1_b', 'v_w_ffn_in', 'v_w_ffn_out', 'v_ln2_g', 'v_ln2_b']
TWIN_OUTPUTS = ['loss', 'grad_x', 'grad_w_in', 'grad_b_in', 'grad_rel_bias', 'grad_sgu_ln_g', 'grad_sgu_ln_b', 'grad_w_s', 'grad_b_s', 'grad_lam_re', 'grad_lam_im', 'grad_log_dt', 'grad_b_re', 'grad_b_im', 'grad_c_re', 'grad_c_im', 'grad_d_skip', 'grad_w_glu', 'grad_b_glu', 'grad_w_pa', 'grad_w_pb', 'grad_w_pc', 'grad_w_o', 'grad_ln1_g', 'grad_ln1_b', 'grad_w_ffn_in', 'grad_w_ffn_out', 'grad_ln2_g', 'grad_ln2_b', 'delta_w_in', 'delta_b_in', 'delta_rel_bias', 'delta_sgu_ln_g', 'delta_sgu_ln_b', 'delta_w_s', 'delta_b_s', 'delta_lam_re', 'delta_lam_im', 'delta_log_dt', 'delta_b_re', 'delta_b_im', 'delta_c_re', 'delta_c_im', 'delta_d_skip', 'delta_w_glu', 'delta_b_glu', 'delta_w_pa', 'delta_w_pb', 'delta_w_pc', 'delta_w_o', 'delta_ln1_g', 'delta_ln1_b', 'delta_w_ffn_in', 'delta_w_ffn_out', 'delta_ln2_g', 'delta_ln2_b', 'new_m_w_in', 'new_m_b_in', 'new_m_rel_bias', 'new_m_sgu_ln_g', 'new_m_sgu_ln_b', 'new_m_w_s', 'new_m_b_s', 'new_m_lam_re', 'new_m_lam_im', 'new_m_log_dt', 'new_m_b_re', 'new_m_b_im', 'new_m_c_re', 'new_m_c_im', 'new_m_d_skip', 'new_m_w_glu', 'new_m_b_glu', 'new_m_w_pa', 'new_m_w_pb', 'new_m_w_pc', 'new_m_w_o', 'new_m_ln1_g', 'new_m_ln1_b', 'new_m_w_ffn_in', 'new_m_w_ffn_out', 'new_m_ln2_g', 'new_m_ln2_b', 'new_v_w_in', 'new_v_b_in', 'new_v_rel_bias', 'new_v_sgu_ln_g', 'new_v_sgu_ln_b', 'new_v_w_s', 'new_v_b_s', 'new_v_lam_re', 'new_v_lam_im', 'new_v_log_dt', 'new_v_b_re', 'new_v_b_im', 'new_v_c_re', 'new_v_c_im', 'new_v_d_skip', 'new_v_w_glu', 'new_v_b_glu', 'new_v_w_pa', 'new_v_w_pb', 'new_v_w_pc', 'new_v_w_o', 'new_v_ln1_g', 'new_v_ln1_b', 'new_v_w_ffn_in', 'new_v_w_ffn_out', 'new_v_ln2_g', 'new_v_ln2_b']
TWIN_LEAF_KINDS = {'loss': 'loss', 'grad_x': 'grad_x', 'grad_w_in': 'grad_w', 'grad_b_in': 'grad_w', 'grad_rel_bias': 'grad_w', 'grad_sgu_ln_g': 'grad_w', 'grad_sgu_ln_b': 'grad_w', 'grad_w_s': 'grad_w', 'grad_b_s': 'grad_w', 'grad_lam_re': 'grad_w', 'grad_lam_im': 'grad_w', 'grad_log_dt': 'grad_w', 'grad_b_re': 'grad_w', 'grad_b_im': 'grad_w', 'grad_c_re': 'grad_w', 'grad_c_im': 'grad_w', 'grad_d_skip': 'grad_w', 'grad_w_glu': 'grad_w', 'grad_b_glu': 'grad_w', 'grad_w_pa': 'grad_w', 'grad_w_pb': 'grad_w', 'grad_w_pc': 'grad_w', 'grad_w_o': 'grad_w', 'grad_ln1_g': 'grad_w', 'grad_ln1_b': 'grad_w', 'grad_w_ffn_in': 'grad_w', 'grad_w_ffn_out': 'grad_w', 'grad_ln2_g': 'grad_w', 'grad_ln2_b': 'grad_w', 'delta_w_in': 'delta_w', 'delta_b_in': 'delta_w', 'delta_rel_bias': 'delta_w', 'delta_sgu_ln_g': 'delta_w', 'delta_sgu_ln_b': 'delta_w', 'delta_w_s': 'delta_w', 'delta_b_s': 'delta_w', 'delta_lam_re': 'delta_w', 'delta_lam_im': 'delta_w', 'delta_log_dt': 'delta_w', 'delta_b_re': 'delta_w', 'delta_b_im': 'delta_w', 'delta_c_re': 'delta_w', 'delta_c_im': 'delta_w', 'delta_d_skip': 'delta_w', 'delta_w_glu': 'delta_w', 'delta_b_glu': 'delta_w', 'delta_w_pa': 'delta_w', 'delta_w_pb': 'delta_w', 'delta_w_pc': 'delta_w', 'delta_w_o': 'delta_w', 'delta_ln1_g': 'delta_w', 'delta_ln1_b': 'delta_w', 'delta_w_ffn_in': 'delta_w', 'delta_w_ffn_out': 'delta_w', 'delta_ln2_g': 'delta_w', 'delta_ln2_b': 'delta_w', 'new_m_w_in': 'new_m', 'new_m_b_in': 'new_m', 'new_m_rel_bias': 'new_m', 'new_m_sgu_ln_g': 'new_m', 'new_m_sgu_ln_b': 'new_m', 'new_m_w_s': 'new_m', 'new_m_b_s': 'new_m', 'new_m_lam_re': 'new_m', 'new_m_lam_im': 'new_m', 'new_m_log_dt': 'new_m', 'new_m_b_re': 'new_m', 'new_m_b_im': 'new_m', 'new_m_c_re': 'new_m', 'new_m_c_im': 'new_m', 'new_m_d_skip': 'new_m', 'new_m_w_glu': 'new_m', 'new_m_b_glu': 'new_m', 'new_m_w_pa': 'new_m', 'new_m_w_pb': 'new_m', 'new_m_w_pc': 'new_m', 'new_m_w_o': 'new_m', 'new_m_ln1_g': 'new_m', 'new_m_ln1_b': 'new_m', 'new_m_w_ffn_in': 'new_m', 'new_m_w_ffn_out': 'new_m', 'new_m_ln2_g': 'new_m', 'new_m_ln2_b': 'new_m', 'new_v_w_in': 'new_v', 'new_v_b_in': 'new_v', 'new_v_rel_bias': 'new_v', 'new_v_sgu_ln_g': 'new_v', 'new_v_sgu_ln_b': 'new_v', 'new_v_w_s': 'new_v', 'new_v_b_s': 'new_v', 'new_v_lam_re': 'new_v', 'new_v_lam_im': 'new_v', 'new_v_log_dt': 'new_v', 'new_v_b_re': 'new_v', 'new_v_b_im': 'new_v', 'new_v_c_re': 'new_v', 'new_v_c_im': 'new_v', 'new_v_d_skip': 'new_v', 'new_v_w_glu': 'new_v', 'new_v_b_glu': 'new_v', 'new_v_w_pa': 'new_v', 'new_v_w_pb': 'new_v', 'new_v_w_pc': 'new_v', 'new_v_w_o': 'new_v', 'new_v_ln1_g': 'new_v', 'new_v_ln1_b': 'new_v', 'new_v_w_ffn_in': 'new_v', 'new_v_w_ffn_out': 'new_v', 'new_v_ln2_g': 'new_v', 'new_v_ln2_b': 'new_v'}


def _forward(args):
    return _fwd_reference(*[args[k] for k in FWD_PARAMS])


def _output_shape():
    out = _jax.eval_shape(lambda: _forward(_fwd_setup_inputs(0)))
    return out.shape, out.dtype

N_MICROBATCH = 1
ADAM_LR = 0.001
ADAM_B1 = 0.9
ADAM_B2 = 0.999
ADAM_EPS = 1e-08
ADAM_WD = 0.01
ADAM_STEP = 10
PER_EXAMPLE_BATCH_AXIS = {'x': 0, 'loss_target': 0}
SHARED_INPUTS = []
_WEIGHT_DTYPES = {'w_in': _jnp.float32, 'b_in': _jnp.float32, 'rel_bias': _jnp.float32, 'sgu_ln_g': _jnp.float32, 'sgu_ln_b': _jnp.float32, 'w_s': _jnp.float32, 'b_s': _jnp.float32, 'lam_re': _jnp.float32, 'lam_im': _jnp.float32, 'log_dt': _jnp.float32, 'b_re': _jnp.float32, 'b_im': _jnp.float32, 'c_re': _jnp.float32, 'c_im': _jnp.float32, 'd_skip': _jnp.float32, 'w_glu': _jnp.float32, 'b_glu': _jnp.float32, 'w_pa': _jnp.float32, 'w_pb': _jnp.float32, 'w_pc': _jnp.float32, 'w_o': _jnp.float32, 'ln1_g': _jnp.float32, 'ln1_b': _jnp.float32, 'w_ffn_in': _jnp.float32, 'w_ffn_out': _jnp.float32, 'ln2_g': _jnp.float32, 'ln2_b': _jnp.float32}
MOMENT_SCALE = {'w_in': 5.615085e-03, 'b_in': 8.983490e-03, 'rel_bias': 6.362627e-03, 'sgu_ln_g': 9.500288e-03, 'sgu_ln_b': 9.407287e-03, 'w_s': 9.308789e-03, 'b_s': 1.309730e-02, 'lam_re': 5.399354e-04, 'lam_im': 5.506996e-04, 'log_dt': 4.102029e-01, 'b_re': 3.631379e-04, 'b_im': 3.566253e-04, 'c_re': 5.024461e-04, 'c_im': 5.016430e-04, 'd_skip': 1.007611e-02, 'w_glu': 2.355066e-03, 'b_glu': 4.344060e-03, 'w_pa': 2.438084e-03, 'w_pb': 1.138789e-02, 'w_pc': 5.597535e-03, 'w_o': 3.000016e-02, 'ln1_g': 4.945544e-01, 'ln1_b': 2.243245e-01, 'w_ffn_in': 8.194105e-03, 'w_ffn_out': 3.184742e-02, 'ln2_g': 8.037583e+00, 'ln2_b': 7.376724e-01}


def _to_microbatches(a, axis):
    t = _jnp.moveaxis(a, axis, 0)
    t = t.reshape((N_MICROBATCH, t.shape[0] // N_MICROBATCH) + t.shape[1:])
    return _jnp.moveaxis(t, 1, axis + 1)


def setup_inputs(seed: int = 0) -> dict:
    inp = _fwd_setup_inputs(seed)
    key = _jax.random.fold_in(_jax.random.key(seed), 7919)
    shape, _ = _output_shape()
    out = dict(inp)
    out["loss_target"] = _jax.random.normal(_jax.random.fold_in(key, 0), shape, _jnp.float32)
    for i, name in enumerate(TWIN_WEIGHTS):
        w = inp[name].astype(_jnp.float32)
        if MOMENT_SCALE is None:
            s = _jnp.sqrt(_jnp.mean(_jnp.square(w)) + 1e-30)
        else:
            s = MOMENT_SCALE[name]
        km, kv = _jax.random.split(_jax.random.fold_in(key, i + 1))
        out[name] = w
        out["m_" + name] = s * _jax.random.normal(km, w.shape, _jnp.float32)
        out["v_" + name] = (s * s) * _jax.random.uniform(kv, w.shape, _jnp.float32, 0.5, 1.5)
    if N_MICROBATCH > 1:
        for name, axis in PER_EXAMPLE_BATCH_AXIS.items():
            out[name] = _to_microbatches(out[name], axis)
    return {'x': out['x'], 'w_in': out['w_in'], 'b_in': out['b_in'], 'rel_bias': out['rel_bias'], 'sgu_ln_g': out['sgu_ln_g'], 'sgu_ln_b': out['sgu_ln_b'], 'w_s': out['w_s'], 'b_s': out['b_s'], 'lam_re': out['lam_re'], 'lam_im': out['lam_im'], 'log_dt': out['log_dt'], 'b_re': out['b_re'], 'b_im': out['b_im'], 'c_re': out['c_re'], 'c_im': out['c_im'], 'd_skip': out['d_skip'], 'w_glu': out['w_glu'], 'b_glu': out['b_glu'], 'w_pa': out['w_pa'], 'w_pb': out['w_pb'], 'w_pc': out['w_pc'], 'w_o': out['w_o'], 'ln1_g': out['ln1_g'], 'ln1_b': out['ln1_b'], 'w_ffn_in': out['w_ffn_in'], 'w_ffn_out': out['w_ffn_out'], 'ln2_g': out['ln2_g'], 'ln2_b': out['ln2_b'], 'loss_target': out['loss_target'], 'm_w_in': out['m_w_in'], 'm_b_in': out['m_b_in'], 'm_rel_bias': out['m_rel_bias'], 'm_sgu_ln_g': out['m_sgu_ln_g'], 'm_sgu_ln_b': out['m_sgu_ln_b'], 'm_w_s': out['m_w_s'], 'm_b_s': out['m_b_s'], 'm_lam_re': out['m_lam_re'], 'm_lam_im': out['m_lam_im'], 'm_log_dt': out['m_log_dt'], 'm_b_re': out['m_b_re'], 'm_b_im': out['m_b_im'], 'm_c_re': out['m_c_re'], 'm_c_im': out['m_c_im'], 'm_d_skip': out['m_d_skip'], 'm_w_glu': out['m_w_glu'], 'm_b_glu': out['m_b_glu'], 'm_w_pa': out['m_w_pa'], 'm_w_pb': out['m_w_pb'], 'm_w_pc': out['m_w_pc'], 'm_w_o': out['m_w_o'], 'm_ln1_g': out['m_ln1_g'], 'm_ln1_b': out['m_ln1_b'], 'm_w_ffn_in': out['m_w_ffn_in'], 'm_w_ffn_out': out['m_w_ffn_out'], 'm_ln2_g': out['m_ln2_g'], 'm_ln2_b': out['m_ln2_b'], 'v_w_in': out['v_w_in'], 'v_b_in': out['v_b_in'], 'v_rel_bias': out['v_rel_bias'], 'v_sgu_ln_g': out['v_sgu_ln_g'], 'v_sgu_ln_b': out['v_sgu_ln_b'], 'v_w_s': out['v_w_s'], 'v_b_s': out['v_b_s'], 'v_lam_re': out['v_lam_re'], 'v_lam_im': out['v_lam_im'], 'v_log_dt': out['v_log_dt'], 'v_b_re': out['v_b_re'], 'v_b_im': out['v_b_im'], 'v_c_re': out['v_c_re'], 'v_c_im': out['v_c_im'], 'v_d_skip': out['v_d_skip'], 'v_w_glu': out['v_w_glu'], 'v_b_glu': out['v_b_glu'], 'v_w_pa': out['v_w_pa'], 'v_w_pb': out['v_w_pb'], 'v_w_pc': out['v_w_pc'], 'v_w_o': out['v_w_o'], 'v_ln1_g': out['v_ln1_g'], 'v_ln1_b': out['v_ln1_b'], 'v_w_ffn_in': out['v_w_ffn_in'], 'v_w_ffn_out': out['v_w_ffn_out'], 'v_ln2_g': out['v_ln2_g'], 'v_ln2_b': out['v_ln2_b']}


def _loss(weights, diff, rest, loss_target):
    with _jax.named_scope("forward"):
        args = {**rest, TWIN_DIFF_INPUT: diff, **{k: w.astype(_WEIGHT_DTYPES[k]) for k, w in weights.items()}}
        y = _forward(args)
    with _jax.named_scope("loss_head"):
        err = _jnp.square(y.astype(_jnp.float32) - loss_target)
        return 0.5 * _jnp.sum(_jnp.mean(err, axis=-1)) if err.ndim else 0.5 * err


def _adamw(w, g, m, v):
    m = ADAM_B1 * m + (1.0 - ADAM_B1) * g
    v = ADAM_B2 * v + (1.0 - ADAM_B2) * _jnp.square(g)
    m_hat = m / (1.0 - ADAM_B1 ** ADAM_STEP)
    v_hat = v / (1.0 - ADAM_B2 ** ADAM_STEP)
    delta = -ADAM_LR * (m_hat / (_jnp.sqrt(v_hat) + ADAM_EPS) + ADAM_WD * w)
    return delta, m, v


def reference(x, w_in, b_in, rel_bias, sgu_ln_g, sgu_ln_b, w_s, b_s, lam_re, lam_im, log_dt, b_re, b_im, c_re, c_im, d_skip, w_glu, b_glu, w_pa, w_pb, w_pc, w_o, ln1_g, ln1_b, w_ffn_in, w_ffn_out, ln2_g, ln2_b, loss_target, m_w_in, m_b_in, m_rel_bias, m_sgu_ln_g, m_sgu_ln_b, m_w_s, m_b_s, m_lam_re, m_lam_im, m_log_dt, m_b_re, m_b_im, m_c_re, m_c_im, m_d_skip, m_w_glu, m_b_glu, m_w_pa, m_w_pb, m_w_pc, m_w_o, m_ln1_g, m_ln1_b, m_w_ffn_in, m_w_ffn_out, m_ln2_g, m_ln2_b, v_w_in, v_b_in, v_rel_bias, v_sgu_ln_g, v_sgu_ln_b, v_w_s, v_b_s, v_lam_re, v_lam_im, v_log_dt, v_b_re, v_b_im, v_c_re, v_c_im, v_d_skip, v_w_glu, v_b_glu, v_w_pa, v_w_pb, v_w_pc, v_w_o, v_ln1_g, v_ln1_b, v_w_ffn_in, v_w_ffn_out, v_ln2_g, v_ln2_b):
    given = dict(x=x, w_in=w_in, b_in=b_in, rel_bias=rel_bias, sgu_ln_g=sgu_ln_g, sgu_ln_b=sgu_ln_b, w_s=w_s, b_s=b_s, lam_re=lam_re, lam_im=lam_im, log_dt=log_dt, b_re=b_re, b_im=b_im, c_re=c_re, c_im=c_im, d_skip=d_skip, w_glu=w_glu, b_glu=b_glu, w_pa=w_pa, w_pb=w_pb, w_pc=w_pc, w_o=w_o, ln1_g=ln1_g, ln1_b=ln1_b, w_ffn_in=w_ffn_in, w_ffn_out=w_ffn_out, ln2_g=ln2_g, ln2_b=ln2_b, loss_target=loss_target, m_w_in=m_w_in, m_b_in=m_b_in, m_rel_bias=m_rel_bias, m_sgu_ln_g=m_sgu_ln_g, m_sgu_ln_b=m_sgu_ln_b, m_w_s=m_w_s, m_b_s=m_b_s, m_lam_re=m_lam_re, m_lam_im=m_lam_im, m_log_dt=m_log_dt, m_b_re=m_b_re, m_b_im=m_b_im, m_c_re=m_c_re, m_c_im=m_c_im, m_d_skip=m_d_skip, m_w_glu=m_w_glu, m_b_glu=m_b_glu, m_w_pa=m_w_pa, m_w_pb=m_w_pb, m_w_pc=m_w_pc, m_w_o=m_w_o, m_ln1_g=m_ln1_g, m_ln1_b=m_ln1_b, m_w_ffn_in=m_w_ffn_in, m_w_ffn_out=m_w_ffn_out, m_ln2_g=m_ln2_g, m_ln2_b=m_ln2_b, v_w_in=v_w_in, v_b_in=v_b_in, v_rel_bias=v_rel_bias, v_sgu_ln_g=v_sgu_ln_g, v_sgu_ln_b=v_sgu_ln_b, v_w_s=v_w_s, v_b_s=v_b_s, v_lam_re=v_lam_re, v_lam_im=v_lam_im, v_log_dt=v_log_dt, v_b_re=v_b_re, v_b_im=v_b_im, v_c_re=v_c_re, v_c_im=v_c_im, v_d_skip=v_d_skip, v_w_glu=v_w_glu, v_b_glu=v_b_glu, v_w_pa=v_w_pa, v_w_pb=v_w_pb, v_w_pc=v_w_pc, v_w_o=v_w_o, v_ln1_g=v_ln1_g, v_ln1_b=v_ln1_b, v_w_ffn_in=v_w_ffn_in, v_w_ffn_out=v_w_ffn_out, v_ln2_g=v_ln2_g, v_ln2_b=v_ln2_b)
    weights = {n: given[n] for n in TWIN_WEIGHTS}
    shared = {n: given[n] for n in SHARED_INPUTS}
    per_example = {n: given[n] for n in ['x']}
    grad_fn = _jax.value_and_grad(_loss, argnums=(0, 1))

    def one_microbatch(ex, loss_target):
        ex = dict(ex)
        diff = ex.pop(TWIN_DIFF_INPUT)
        return grad_fn(weights, diff, {**shared, **ex}, loss_target)

    if N_MICROBATCH == 1:
        loss, (grad_w, grad_x) = one_microbatch(per_example, given["loss_target"])
    else:
        def body(carry, xs):
            loss_sum, grad_sum = carry
            l_k, (gw_k, gx_k) = one_microbatch(xs[0], xs[1])
            with _jax.named_scope("update"):
                return (loss_sum + l_k, _jax.tree.map(_jnp.add, grad_sum, gw_k)), gx_k

        init = (_jnp.zeros((), _jnp.float32), _jax.tree.map(_jnp.zeros_like, weights))
        (loss, grad_w), grad_x = _jax.lax.scan(body, init, (per_example, given["loss_target"]))
    with _jax.named_scope("update"):
        delta_w, new_m, new_v = {}, {}, {}
        for n in TWIN_WEIGHTS:
            delta_w[n], new_m[n], new_v[n] = _adamw(weights[n], grad_w[n], given["m_" + n], given["v_" + n])
    return (loss, grad_x, *[grad_w[n] for n in TWIN_WEIGHTS], *[delta_w[n] for n in TWIN_WEIGHTS],
            *[new_m[n] for n in TWIN_WEIGHTS], *[new_v[n] for n in TWIN_WEIGHTS])
```

```python
import functools
import math

import numpy as np
import jax
import jax.numpy as jnp
from jax import lax
from jax.experimental import pallas as pl
from jax.experimental.pallas import tpu as pltpu

F32 = jnp.float32
BF16 = jnp.bfloat16
MESH = pl.DeviceIdType.MESH

ATT_PATTERNS = ((128, 1), (512, 4), (2048, 16))
N_GROUPS_A = 3
HEADS = 8
HEAD_DIM = 64
WIDTH_A = HEADS * HEAD_DIM
QKV_W = N_GROUPS_A * WIDTH_A
ATT_BLOCK = 128
N_REL_BUCKETS = 32
REL_MAX_DIST = 2048
NEG_INF = -1e30
CHUNK = 128
WIDTH_B = 768
N_GROUPS_B = 6
GROUP_B = 128
WIDTH_C = 768
SSM_GROUP = 16
N_GROUPS_C = 48
SSM_STATE = 64
SSM_CH = N_GROUPS_C * SSM_STATE
N_BRANCH = 3
LN_EPS = 1e-5
ADAM_LR = 0.001
ADAM_B1 = 0.9
ADAM_B2 = 0.999
ADAM_EPS = 1e-08
ADAM_WD = 0.01
ADAM_STEP = 10

LANES = 128
SUBLANES_BF16 = 16
V7X_VMEM_BYTES = 64 * 1024 * 1024
VMEM_LIMIT = V7X_VMEM_BYTES * 7 // 8

OFF_ZB = 3 * QKV_W
REST_ZB = 0
REST_UC = 2 * WIDTH_B
REST_GL = 2 * WIDTH_B + WIDTH_C

BIG = (("w_in", "col"), ("w_glu", "row"), ("w_pa", "col"), ("w_pb", "col"), ("w_pc", "col"),
       ("w_o", "row"), ("w_ffn_in", "col"), ("w_ffn_out", "row"))
WEIGHTS = ("w_in", "b_in", "rel_bias", "sgu_ln_g", "sgu_ln_b", "w_s", "b_s", "lam_re", "lam_im", "log_dt",
           "b_re", "b_im", "c_re", "c_im", "d_skip", "w_glu", "b_glu", "w_pa", "w_pb", "w_pc", "w_o",
           "ln1_g", "ln1_b", "w_ffn_in", "w_ffn_out", "ln2_g", "ln2_b")
BIG_NAMES = tuple(n for n, _ in BIG)
SMALL = tuple(n for n in WEIGHTS if n not in BIG_NAMES)


def _tile(n, target, mult):
    t = (min(target, n) // mult) * mult
    while t >= mult:
        if n % t == 0:
            return t
        t -= mult
    return n


def _cparams(n_axes):
    return pltpu.CompilerParams(dimension_semantics=("arbitrary",) * n_axes, vmem_limit_bytes=VMEM_LIMIT)


_DIMS = {"nn": (((1,), (0,)), ((), ())), "nt": (((1,), (1,)), ((), ())), "tn": (((0,), (0,)), ((), ()))}


def _mm(name, mode, grid, a, a_spec, b, b_spec, out_sds, o_spec, acc_shape, *, bias=None, bias_spec=None,
        add=None, add_spec=None, add_scale=1.0, exact=False):
    nk = grid[2]
    has_bias = bias is not None
    has_add = add is not None

    def body(*refs):
        a_ref, b_ref = refs[0], refs[1]
        pos = 2
        bias_ref = add_ref = None
        if has_bias:
            bias_ref = refs[pos]
            pos += 1
        if has_add:
            add_ref = refs[pos]
            pos += 1
        o_ref = refs[pos]
        if exact:
            part = lax.dot_general(a_ref[...].astype(F32), b_ref[...].astype(F32), _DIMS[mode],
                                   preferred_element_type=F32, precision=lax.Precision.HIGHEST)
        else:
            part = lax.dot_general(a_ref[...].astype(BF16), b_ref[...].astype(BF16), _DIMS[mode],
                                   preferred_element_type=F32)

        def finish(r):
            if has_bias:
                r = r + bias_ref[...]
            if has_add:
                r = r + add_scale * add_ref[...].astype(F32)
            o_ref[...] = r.astype(o_ref.dtype)

        if nk == 1:
            finish(part)
        else:
            acc_ref = refs[pos + 1]
            k = pl.program_id(2)

            @pl.when(k == 0)
            def _():
                acc_ref[...] = part

            @pl.when(k > 0)
            def _():
                acc_ref[...] += part

            @pl.when(k == nk - 1)
            def _():
                finish(acc_ref[...])

    ins, specs = [a, b], [a_spec, b_spec]
    if has_bias:
        ins.append(bias)
        specs.append(bias_spec)
    if has_add:
        ins.append(add)
        specs.append(add_spec)
    scratch = [pltpu.VMEM(acc_shape, F32)] if nk > 1 else []
    return pl.pallas_call(body, out_shape=out_sds, grid=grid, in_specs=specs, out_specs=o_spec,
                          scratch_shapes=scratch, compiler_params=_cparams(3), name=name)(*ins)


def _mm_nn(name, a, b, *, n0=0, n=None, a0=0, bias=None, add=None, add_scale=1.0, out_dtype=F32,
           tm=512, tn=512, tk=None):
    m = a.shape[0]
    if b.ndim == 3:
        k, cc = b.shape[1:]
        n = 4 * cc
        tn = _tile(cc, tn, LANES)
        per = cc // tn
        tk = _tile(k, k if tk is None else tk, LANES)
        b_spec = pl.BlockSpec((None, tk, tn), lambda i, j, kk: (j // per, kk, j % per))
    else:
        k = b.shape[0]
        n = b.shape[1] - n0 if n is None else n
        tn = _tile(math.gcd(n, n0) if n0 else n, tn, LANES)
        tk = _tile(math.gcd(k, a0) if a0 else k, k if tk is None else tk, LANES)
        b_spec = pl.BlockSpec((tk, tn), lambda i, j, kk: (kk, n0 // tn + j))
    tm = _tile(m, tm, SUBLANES_BF16)
    jn0, ka0 = n0 // tn, a0 // tk
    o_spec = pl.BlockSpec((tm, tn), lambda i, j, kk: (i, j))
    return _mm(name, "nn", (m // tm, n // tn, k // tk),
               a, pl.BlockSpec((tm, tk), lambda i, j, kk: (i, ka0 + kk)), b, b_spec,
               jax.ShapeDtypeStruct((m, n), out_dtype), o_spec, (tm, tn),
               bias=bias, bias_spec=pl.BlockSpec((1, tn), lambda i, j, kk: (0, jn0 + j)),
               add=add, add_spec=o_spec, add_scale=add_scale)


def _mm_nt(name, a, b, *, n0=0, add=None, add_scale=1.0, out_dtype=F32, tm=512, tn=2048, tk=512):
    m, n = a.shape
    if b.ndim == 3:
        k, cc = b.shape[1:]
        tn = _tile(k, tn, LANES)
        tk = _tile(cc, tk, LANES)
        per = cc // tk
        b_spec = pl.BlockSpec((None, tn, tk), lambda i, j, kk: (kk // per, j, kk % per))
    else:
        k = b.shape[0]
        tn = _tile(k, tn, LANES)
        tk = _tile(math.gcd(n, n0) if n0 else n, tk, LANES)
        b_spec = pl.BlockSpec((tn, tk), lambda i, j, kk: (j, n0 // tk + kk))
    tm = _tile(m, tm, SUBLANES_BF16)
    o_spec = pl.BlockSpec((tm, tn), lambda i, j, kk: (i, j))
    return _mm(name, "nt", (m // tm, k // tn, n // tk),
               a, pl.BlockSpec((tm, tk), lambda i, j, kk: (i, kk)), b, b_spec,
               jax.ShapeDtypeStruct((m, k), out_dtype), o_spec, (tm, tn),
               add=add, add_spec=o_spec, add_scale=add_scale)


def _mm_tn(name, a, b, *, a0=0, ka=None, out_dtype=BF16, tm=2048, tn=1024, tk=512, col_shards=False):
    s = a.shape[0]
    ka = a.shape[1] - a0 if ka is None else ka
    n = b.shape[1]
    tm = _tile(math.gcd(ka, a0) if a0 else ka, tm, LANES)
    tk = _tile(s, tk, SUBLANES_BF16)
    ia0 = a0 // tm
    if col_shards:
        cc = n // 4
        tn = _tile(cc, tn, LANES)
        per = cc // tn
        out_sds = jax.ShapeDtypeStruct((4, ka, cc), out_dtype)
        o_spec = pl.BlockSpec((None, tm, tn), lambda i, j, kk: (j // per, i, j % per))
    else:
        tn = _tile(n, tn, LANES)
        out_sds = jax.ShapeDtypeStruct((ka, n), out_dtype)
        o_spec = pl.BlockSpec((tm, tn), lambda i, j, kk: (i, j))
    return _mm(name, "tn", (ka // tm, n // tn, s // tk),
               a, pl.BlockSpec((tk, tm), lambda i, j, kk: (kk, ia0 + i)),
               b, pl.BlockSpec((tk, tn), lambda i, j, kk: (kk, j)), out_sds, o_spec, (tm, tn))


def R(arr, bw=None, c0=0, j=False):
    return ("r", arr, arr.shape[1] if bw is None else bw, c0, j)


def P(arr, bw=None, c0=0, j=False):
    return ("p", arr, arr.shape[1] if bw is None else bw, c0, j)


def _ew(name, fn, rows, ins, outs, accs=(), *, tr=256, nj=1):
    tr = _tile(rows, tr, SUBLANES_BF16)
    ni = rows // tr
    n_in, n_out, n_acc = len(ins), len(outs), len(accs)

    def spec(kind, bw, c0, follows):
        rows_b = tr if kind == "r" else 1
        if kind == "r":
            return pl.BlockSpec((rows_b, bw), (lambda j, i: (i, c0 + j)) if follows else (lambda j, i: (i, c0)))
        return pl.BlockSpec((rows_b, bw), (lambda j, i: (0, c0 + j)) if follows else (lambda j, i: (0, c0)))

    def body(*refs):
        res = fn(*[r[...] for r in refs[:n_in]])
        res = tuple(res) if isinstance(res, (tuple, list)) else (res,)
        for r, v in zip(refs[n_in:n_in + n_out], res[:n_out]):
            r[...] = v.astype(r.dtype)
        if n_acc:
            i = pl.program_id(1)
            for r, v in zip(refs[n_in + n_out:], res[n_out:]):
                @pl.when(i == 0)
                def _(r=r, v=v):
                    r[...] = v

                @pl.when(i > 0)
                def _(r=r, v=v):
                    r[...] += v

    out_shape = [jax.ShapeDtypeStruct((rows, bw * nj), dt) for bw, dt in outs]
    out_shape += [jax.ShapeDtypeStruct((1, bw * nj), F32) for bw in accs]
    out_specs = [pl.BlockSpec((tr, bw), lambda j, i: (i, j)) for bw, _ in outs]
    out_specs += [pl.BlockSpec((1, bw), lambda j, i: (0, j)) for bw in accs]
    res = pl.pallas_call(body, out_shape=out_shape, grid=(nj, ni),
                         in_specs=[spec(k, bw, c0, f) for k, _, bw, c0, f in ins], out_specs=out_specs,
                         compiler_params=_cparams(2), name=name)(*[a for _, a, _, _, _ in ins])
    return res if len(res) > 1 else res[0]


def _gelu(x):
    c = math.sqrt(2.0 / math.pi)
    return 0.5 * x * (1.0 + jnp.tanh(c * (x + 0.044715 * (x * x * x))))


def _sigmoid(x):
    return lax.logistic(x)


def _f_ln(h, g, b):
    mu = jnp.mean(h, axis=-1, keepdims=True)
    xc = h - mu
    var = jnp.mean(xc * xc, axis=-1, keepdims=True)
    return xc * lax.rsqrt(var + LN_EPS) * g + b


def _f_combine(o1, o2, o3, l1, l2, l3):
    m = jnp.maximum(jnp.maximum(l1, l2), l3)
    e1, e2, e3 = jnp.exp(l1 - m), jnp.exp(l2 - m), jnp.exp(l3 - m)
    den = e1 + e2 + e3
    return (e1 * o1 + e2 * o2 + e3 * o3) / den, m + jnp.log(den)


def _f_ssm_out(y_lin, uc, d_skip):
    return _gelu(y_lin + d_skip * uc)


def _f_glu(yc0, t):
    return yc0 * _sigmoid(t)


def _f_merge(g0, g1, g2, pa, pb, pc):
    return _sigmoid(g0) * pa + _sigmoid(g1) * pb + _sigmoid(g2) * pc


def _f_swiglu(gf, up):
    return gf * _sigmoid(gf) * up


def _f_sgu_pre(zu, zv, g, b):
    return _gelu(zu), _f_ln(_gelu(zv), g, b)


def _f_ssm_params(lr, li, ld, lr_rep, li_rep, ld_rep, br_t, bi_t):
    def disc(lr, li, ld):
        dt = jnp.exp(ld)
        mag = jnp.exp(lr * dt)
        th = li * dt
        abr, abi = mag * jnp.cos(th), mag * jnp.sin(th)
        nrm = lr * lr + li * li
        cr = ((abr - 1.0) * lr + abi * li) / nrm
        ci = (abi * lr - (abr - 1.0) * li) / nrm
        return abr, abi, cr, ci

    abr, abi, _, _ = disc(lr, li, ld)
    _, _, cr, ci = disc(lr_rep, li_rep, ld_rep)
    return abr, abi, cr * br_t - ci * bi_t, cr * bi_t + ci * br_t


def _rowsum(x):
    return jnp.sum(x, axis=0, keepdims=True)


def _to_streams(qkv):
    s = qkv.shape[0]
    t = qkv.reshape(s, 3, N_GROUPS_A, HEADS, HEAD_DIM)
    outs = []
    for g, (_, dil) in enumerate(ATT_PATTERNS):
        tg = t[:, :, g].reshape(s // dil, dil, 3, HEADS, HEAD_DIM)
        outs.append(tg.transpose(2, 3, 1, 0, 4).reshape(3, HEADS, s, HEAD_DIM))
    return jnp.stack(outs, 0)


def _tok_to_streams(a):
    s = a.shape[0]
    outs = []
    for _, dil in ATT_PATTERNS:
        t = a.reshape(s // dil, dil, HEADS, HEAD_DIM)
        outs.append(t.transpose(2, 1, 0, 3).reshape(HEADS, s, HEAD_DIM))
    return jnp.stack(outs, 0)


def _streams_to_tok(o):
    s = o.shape[2]
    outs = []
    for g, (_, dil) in enumerate(ATT_PATTERNS):
        t = o[g].reshape(HEADS, dil, s // dil, HEAD_DIM)
        outs.append(t.transpose(2, 1, 0, 3).reshape(s, WIDTH_A))
    return outs


def _t5_bucket(dist):
    max_exact = N_REL_BUCKETS // 2
    d = np.maximum(dist, 1).astype(np.float32)
    scale = (N_REL_BUCKETS - max_exact) / math.log(REL_MAX_DIST / max_exact)
    large = max_exact + (np.log(d / max_exact) * scale).astype(np.int32)
    large = np.minimum(large, N_REL_BUCKETS - 1)
    return np.where(dist < max_exact, dist, large).astype(np.int32)


def _bucket_maps():
    i = np.arange(ATT_BLOCK)[:, None]
    kk = np.arange(2 * ATT_BLOCK)[None, :]
    steps = np.maximum(ATT_BLOCK + i - kk, 0)
    return np.stack([_t5_bucket(steps * dil) for _, dil in ATT_PATTERNS], 0)


def _band_bias(rel_bias):
    q = ATT_BLOCK

    def body(rel_ref, m_ref, o_ref):
        g = pl.program_id(0)
        bm = m_ref[...]
        for h in range(HEADS):
            acc = jnp.zeros((q, 2 * q), F32)
            for bk in range(N_REL_BUCKETS):
                acc = jnp.where(bm == bk, rel_ref[bk, g * HEADS + h], acc)
            o_ref[h] = acc

    return pl.pallas_call(body, out_shape=jax.ShapeDtypeStruct((N_GROUPS_A, HEADS, q, 2 * q), F32), grid=(N_GROUPS_A,),
                          in_specs=[pl.BlockSpec(memory_space=pltpu.SMEM),
                                    pl.BlockSpec((None, q, 2 * q), lambda g: (g, 0, 0))],
                          out_specs=pl.BlockSpec((None, HEADS, q, 2 * q), lambda g: (g, 0, 0, 0)),
                          compiler_params=_cparams(1), name="rel_bias_band")(rel_bias, jnp.asarray(_bucket_maps()))


def _attn_masks(nbs):
    q = ATT_BLOCK
    g, b = pl.program_id(0), pl.program_id(1)
    nb = jnp.where(g == 0, nbs[0], jnp.where(g == 1, nbs[1], nbs[2]))
    shift = jnp.where(lax.rem(b, nb) != 0, 0, q)
    ii = lax.broadcasted_iota(jnp.int32, (q, q), 0)
    kk = lax.broadcasted_iota(jnp.int32, (q, q), 1)
    return kk >= ii + shift, kk <= ii


def _attn_logits(q, kp, kc, bias_h, mask_p, mask_c):
    scale = HEAD_DIM ** -0.5
    sp = lax.dot_general(q, kp, _DIMS["nt"], preferred_element_type=F32) * scale + bias_h[:, :ATT_BLOCK]
    sc = lax.dot_general(q, kc, _DIMS["nt"], preferred_element_type=F32) * scale + bias_h[:, ATT_BLOCK:]
    return jnp.where(mask_p, sp, NEG_INF), jnp.where(mask_c, sc, NEG_INF)


def _attn_specs(s):
    q, h, e = ATT_BLOCK, HEADS, HEAD_DIM
    blk = (None, None, h, q, e)
    prev = lambda b: jnp.maximum(b - 1, 0)
    qkv_specs = [pl.BlockSpec(blk, lambda g, b: (g, 0, 0, b, 0)),
                 pl.BlockSpec(blk, lambda g, b: (g, 1, 0, prev(b), 0)),
                 pl.BlockSpec(blk, lambda g, b: (g, 1, 0, b, 0)),
                 pl.BlockSpec(blk, lambda g, b: (g, 2, 0, prev(b), 0)),
                 pl.BlockSpec(blk, lambda g, b: (g, 2, 0, b, 0))]
    bias_spec = pl.BlockSpec((None, h, q, 2 * q), lambda g, b: (g, 0, 0, 0))
    row_spec = pl.BlockSpec((None, h, q, e), lambda g, b: (g, 0, b, 0))
    return qkv_specs, bias_spec, row_spec


def _attn_fwd(qkv_s, bias):
    s = qkv_s.shape[3]
    nbs = tuple(s // dil // ATT_BLOCK for _, dil in ATT_PATTERNS)
    qkv_specs, bias_spec, row_spec = _attn_specs(s)

    def body(q_ref, kp_ref, kc_ref, vp_ref, vc_ref, b_ref, o_ref, l_ref):
        mask_p, mask_c = _attn_masks(nbs)
        for h in range(HEADS):
            sp, sc = _attn_logits(q_ref[h], kp_ref[h], kc_ref[h], b_ref[h], mask_p, mask_c)
            m = jnp.maximum(jnp.max(sp, axis=1, keepdims=True), jnp.max(sc, axis=1, keepdims=True))
            pp, pc = jnp.exp(sp - m), jnp.exp(sc - m)
            den = jnp.sum(pp, axis=1, keepdims=True) + jnp.sum(pc, axis=1, keepdims=True)
            o = (lax.dot_general(pp.astype(BF16), vp_ref[h], _DIMS["nn"], preferred_element_type=F32)
                 + lax.dot_general(pc.astype(BF16), vc_ref[h], _DIMS["nn"], preferred_element_type=F32))
            o_ref[h] = o / den
            l_ref[h] = jnp.broadcast_to(m + jnp.log(den), (ATT_BLOCK, HEAD_DIM))

    sds = jax.ShapeDtypeStruct((N_GROUPS_A, HEADS, s, HEAD_DIM), F32)
    return pl.pallas_call(body, out_shape=(sds, sds), grid=(N_GROUPS_A, s // ATT_BLOCK),
                          in_specs=qkv_specs + [bias_spec], out_specs=(row_spec, row_spec),
                          compiler_params=_cparams(2), name="attn_fwd")(qkv_s, qkv_s, qkv_s, qkv_s, qkv_s, bias)


def _attn_bwd(qkv_s, bias, do_s, ya_s, lse_s):
    s = qkv_s.shape[3]
    nbs = tuple(s // dil // ATT_BLOCK for _, dil in ATT_PATTERNS)
    qkv_specs, bias_spec, row_spec = _attn_specs(s)
    scale = HEAD_DIM ** -0.5

    def body(q_ref, kp_ref, kc_ref, vp_ref, vc_ref, b_ref, do_ref, ya_ref, l_ref,
             dq_ref, dkc_ref, dkp_ref, dvc_ref, dvp_ref, db_ref):
        mask_p, mask_c = _attn_masks(nbs)

        @pl.when(pl.program_id(1) == 0)
        def _():
            db_ref[...] = jnp.zeros_like(db_ref)

        for h in range(HEADS):
            q, kp, kc, vp, vc = q_ref[h], kp_ref[h], kc_ref[h], vp_ref[h], vc_ref[h]
            sp, sc = _attn_logits(q, kp, kc, b_ref[h], mask_p, mask_c)
            lse = l_ref[h][:, 0:1]
            pp, pc = jnp.exp(sp - lse), jnp.exp(sc - lse)
            do = do_ref[h]
            dsum = jnp.sum(do * ya_ref[h], axis=1, keepdims=True)
            dob = do.astype(BF16)
            dsp = pp * (lax.dot_general(dob, vp, _DIMS["nt"], preferred_element_type=F32) - dsum)
            dsc = pc * (lax.dot_general(dob, vc, _DIMS["nt"], preferred_element_type=F32) - dsum)
            db_ref[h, :, :ATT_BLOCK] += dsp
            db_ref[h, :, ATT_BLOCK:] += dsc
            dspb, dscb = dsp.astype(BF16), dsc.astype(BF16)
            dq_ref[h] = scale * (lax.dot_general(dspb, kp, _DIMS["nn"], preferred_element_type=F32)
                                 + lax.dot_general(dscb, kc, _DIMS["nn"], preferred_element_type=F32))
            dkp_ref[h] = scale * lax.dot_general(dspb, q, _DIMS["tn"], preferred_element_type=F32)
            dkc_ref[h] = scale * lax.dot_general(dscb, q, _DIMS["tn"], preferred_element_type=F32)
            dvp_ref[h] = lax.dot_general(pp.astype(BF16), dob, _DIMS["tn"], preferred_element_type=F32)
            dvc_ref[h] = lax.dot_general(pc.astype(BF16), dob, _DIMS["tn"], preferred_element_type=F32)

    sds = jax.ShapeDtypeStruct((N_GROUPS_A, HEADS, s, HEAD_DIM), F32)
    return pl.pallas_call(body, out_shape=(sds,) * 5 + (jax.ShapeDtypeStruct(bias.shape, F32),),
                          grid=(N_GROUPS_A, s // ATT_BLOCK),
                          in_specs=qkv_specs + [bias_spec, row_spec, row_spec, row_spec],
                          out_specs=(row_spec,) * 5 + (bias_spec,), compiler_params=_cparams(2), name="attn_bwd")(
        qkv_s, qkv_s, qkv_s, qkv_s, qkv_s, bias, do_s, ya_s, lse_s)


def _shift_add(cur, prv):
    s = cur.shape[2]
    nb = s // ATT_BLOCK
    blk = (None, HEADS, ATT_BLOCK, HEAD_DIM)

    def body(c_ref, p_ref, o_ref):
        live = (pl.program_id(1) < nb - 1).astype(F32)
        o_ref[...] = c_ref[...] + live * p_ref[...]

    return pl.pallas_call(body, out_shape=jax.ShapeDtypeStruct(cur.shape, F32), grid=(N_GROUPS_A, nb),
                          in_specs=[pl.BlockSpec(blk, lambda g, b: (g, 0, b, 0)),
                                    pl.BlockSpec(blk, lambda g, b: (g, 0, jnp.minimum(b + 1, nb - 1), 0))],
                          out_specs=pl.BlockSpec(blk, lambda g, b: (g, 0, b, 0)),
                          compiler_params=_cparams(2), name="attn_shift_add")(cur, prv)


def _bias_to_buckets(dbias):
    bmap = jnp.asarray(_bucket_maps())
    q = ATT_BLOCK

    def body(db_ref, m_ref, o_ref):
        lane = lax.broadcasted_iota(jnp.int32, (HEADS, LANES), 1)
        row = lax.broadcasted_iota(jnp.int32, (HEADS, LANES), 0)
        acc = jnp.zeros((HEADS, LANES), F32)
        bm = m_ref[...]
        for h in range(HEADS):
            dbh = db_ref[h]
            for bk in range(N_REL_BUCKETS):
                sv = jnp.sum(jnp.sum(jnp.where(bm == bk, dbh, 0.0), axis=1, keepdims=True), axis=0, keepdims=True)
                acc = acc + jnp.where((lane == bk) & (row == h), sv, 0.0)
        o_ref[...] = acc

    out = pl.pallas_call(body, out_shape=jax.ShapeDtypeStruct((N_GROUPS_A, HEADS, LANES), F32), grid=(N_GROUPS_A,),
                         in_specs=[pl.BlockSpec((None, HEADS, q, 2 * q), lambda g: (g, 0, 0, 0)),
                                   pl.BlockSpec((None, q, 2 * q), lambda g: (g, 0, 0))],
                         out_specs=pl.BlockSpec((None, HEADS, LANES), lambda g: (g, 0, 0)),
                         compiler_params=_cparams(1), name="rel_bias_grad")(dbias, bmap)
    return out[:, :, :N_REL_BUCKETS].reshape(N_GROUPS_A * HEADS, N_REL_BUCKETS).T


def _sgu_specs():
    c, w = CHUNK, WIDTH_B
    return [pl.BlockSpec((c, w), lambda i: (i, 0)), pl.BlockSpec((c, w), lambda i: (i, 1)),
            pl.BlockSpec((1, w), lambda i: (0, 0)), pl.BlockSpec((1, w), lambda i: (0, 0)),
            pl.BlockSpec((N_GROUPS_B, c, c), lambda i: (0, 0, 0)), pl.BlockSpec((N_GROUPS_B, c, 1), lambda i: (0, 0, 0))]


def _sgu_fwd(rest, ln_g, ln_b, w_tril, b_col):
    s = rest.shape[0]

    def body(zu_ref, zv_ref, g_ref, b_ref, w_ref, bs_ref, y_ref):
        u, vn = _f_sgu_pre(zu_ref[...], zv_ref[...], g_ref[...], b_ref[...])
        for gi in range(N_GROUPS_B):
            sl = slice(gi * GROUP_B, (gi + 1) * GROUP_B)
            mixed = lax.dot_general(w_ref[gi], vn[:, sl].astype(BF16), _DIMS["nn"], preferred_element_type=F32)
            y_ref[:, sl] = u[:, sl] * (mixed + bs_ref[gi])

    return pl.pallas_call(body, out_shape=jax.ShapeDtypeStruct((s, WIDTH_B), F32), grid=(s // CHUNK,),
                          in_specs=_sgu_specs(), out_specs=pl.BlockSpec((CHUNK, WIDTH_B), lambda i: (i, 0)),
                          compiler_params=_cparams(1), name="sgu_fwd")(rest, rest, ln_g, ln_b, w_tril, b_col)


def _sgu_bwd(rest, ln_g, ln_b, w_tril, b_col, dyb):
    s = rest.shape[0]
    c, w, ng = CHUNK, WIDTH_B, N_GROUPS_B

    def body(zu_ref, zv_ref, g_ref, b_ref, w_ref, bs_ref, dy_ref, dzu_ref, dzv_ref, dw_ref, dbs_ref, dg_ref, db_ref):
        (u, vn), vjp = jax.vjp(_f_sgu_pre, zu_ref[...], zv_ref[...], g_ref[...], b_ref[...])
        first = pl.program_id(0) == 0
        du, dvn = [], []
        for gi in range(ng):
            sl = slice(gi * GROUP_B, (gi + 1) * GROUP_B)
            vg = vn[:, sl].astype(BF16)
            mixed = lax.dot_general(w_ref[gi], vg, _DIMS["nn"], preferred_element_type=F32) + bs_ref[gi]
            dy = dy_ref[:, sl]
            dmix = dy * u[:, sl]
            du.append(dy * mixed)
            dmb = dmix.astype(BF16)
            dvn.append(lax.dot_general(w_ref[gi], dmb, _DIMS["tn"], preferred_element_type=F32))
            dwg = lax.dot_general(dmb, vg, _DIMS["nt"], preferred_element_type=F32)
            dbg = jnp.sum(dmix, axis=1, keepdims=True)

            @pl.when(first)
            def _(gi=gi, dwg=dwg, dbg=dbg):
                dw_ref[gi] = dwg
                dbs_ref[gi] = dbg

            @pl.when(jnp.logical_not(first))
            def _(gi=gi, dwg=dwg, dbg=dbg):
                dw_ref[gi] += dwg
                dbs_ref[gi] += dbg

        dzu, dzv, dg, db = vjp((jnp.concatenate(du, axis=1), jnp.concatenate(dvn, axis=1)))
        dzu_ref[...] = dzu.astype(dzu_ref.dtype)
        dzv_ref[...] = dzv.astype(dzv_ref.dtype)

        @pl.when(first)
        def _():
            dg_ref[...] = dg
            db_ref[...] = db

        @pl.when(jnp.logical_not(first))
        def _():
            dg_ref[...] += dg
            db_ref[...] += db

    row = pl.BlockSpec((c, w), lambda i: (i, 0))
    par = pl.BlockSpec((1, w), lambda i: (0, 0))
    return pl.pallas_call(
        body, grid=(s // c,),
        out_shape=(jax.ShapeDtypeStruct((s, w), BF16), jax.ShapeDtypeStruct((s, w), BF16),
                   jax.ShapeDtypeStruct((ng, c, c), F32), jax.ShapeDtypeStruct((ng, c, 1), F32),
                   jax.ShapeDtypeStruct((1, w), F32), jax.ShapeDtypeStruct((1, w), F32)),
        in_specs=_sgu_specs() + [row],
        out_specs=(row, row, pl.BlockSpec((ng, c, c), lambda i: (0, 0, 0)), pl.BlockSpec((ng, c, 1), lambda i: (0, 0, 0)),
                   par, par),
        compiler_params=_cparams(1), name="sgu_bwd")(rest, rest, ln_g, ln_b, w_tril, b_col, dyb)


SCAN_T = 128


def _to_time_major(dst, src_ref, lead):
    for r in range(SSM_CH // LANES):
        dst[:, r, :] = src_ref[lead, :, r * LANES:(r + 1) * LANES]


def _from_time_major(dst_ref, lead, src):
    for r in range(SSM_CH // LANES):
        dst_ref[lead, :, r * LANES:(r + 1) * LANES] = src[:, r, :]


def _scan_fwd(bu, a3):
    s = bu.shape[1]
    t_blk = _tile(s, SCAN_T, 8)
    rows = SSM_CH // LANES

    def body(bu_ref, a_ref, x_ref, b3r, b3i, x3r, x3i, carry):
        @pl.when(pl.program_id(0) == 0)
        def _():
            carry[...] = jnp.zeros_like(carry)

        _to_time_major(b3r, bu_ref, 0)
        _to_time_major(b3i, bu_ref, 1)
        ar, ai = a_ref[0], a_ref[1]

        def step(t, c):
            xr, xi = c
            nr = ar * xr - ai * xi + b3r[t]
            ni = ar * xi + ai * xr + b3i[t]
            x3r[t] = nr
            x3i[t] = ni
            return nr, ni

        xr, xi = lax.fori_loop(0, t_blk, step, (carry[0], carry[1]), unroll=8)
        carry[0] = xr
        carry[1] = xi
        _from_time_major(x_ref, 0, x3r)
        _from_time_major(x_ref, 1, x3i)

    blk = pl.BlockSpec((2, t_blk, SSM_CH), lambda i: (0, i, 0))
    tm = pltpu.VMEM((t_blk, rows, LANES), F32)
    return pl.pallas_call(body, out_shape=jax.ShapeDtypeStruct(bu.shape, F32), grid=(s // t_blk,),
                          in_specs=[blk, pl.BlockSpec((2, rows, LANES), lambda i: (0, 0, 0))], out_specs=blk,
                          scratch_shapes=[tm, tm, tm, tm, pltpu.VMEM((2, rows, LANES), F32)],
                          compiler_params=_cparams(1), name="ssm_scan_fwd")(bu, a3)


def _scan_bwd(dx, x, a3):
    s = dx.shape[1]
    t_blk = _tile(s, SCAN_T, 8)
    nb = s // t_blk
    rows = SSM_CH // LANES

    def body(dx_ref, x_ref, a_ref, g_ref, da_ref, d3r, d3i, x3r, x3i, g3r, g3i, carry):
        first = pl.program_id(0) == 0

        @pl.when(first)
        def _():
            carry[...] = jnp.zeros_like(carry)

        _to_time_major(d3r, dx_ref, 0)
        _to_time_major(d3i, dx_ref, 1)
        _to_time_major(x3r, x_ref, 0)
        _to_time_major(x3i, x_ref, 1)
        ar, ai = a_ref[0], a_ref[1]

        def step(k, c):
            t = t_blk - 1 - k
            gr, gi, dar, dai = c
            xr, xi = x3r[t], x3i[t]
            dar = dar + gr * xr + gi * xi
            dai = dai + gi * xr - gr * xi
            ngr = d3r[t] + ar * gr + ai * gi
            ngi = d3i[t] + ar * gi - ai * gr
            g3r[t] = ngr
            g3i[t] = ngi
            return ngr, ngi, dar, dai

        zero = jnp.zeros((rows, LANES), F32)
        gr, gi, dar, dai = lax.fori_loop(0, t_blk, step, (carry[0], carry[1], zero, zero), unroll=8)
        carry[0] = gr
        carry[1] = gi

        @pl.when(first)
        def _():
            da_ref[0] = dar
            da_ref[1] = dai

        @pl.when(jnp.logical_not(first))
        def _():
            da_ref[0] += dar
            da_ref[1] += dai

        _from_time_major(g_ref, 0, g3r)
        _from_time_major(g_ref, 1, g3i)

    blk = pl.BlockSpec((2, t_blk, SSM_CH), lambda i: (0, nb - 1 - i, 0))
    par = pl.BlockSpec((2, rows, LANES), lambda i: (0, 0, 0))
    tm = pltpu.VMEM((t_blk, rows, LANES), F32)
    return pl.pallas_call(body, out_shape=(jax.ShapeDtypeStruct(dx.shape, F32), jax.ShapeDtypeStruct((2, rows, LANES), F32)),
                          grid=(nb,), in_specs=[blk, blk, par], out_specs=(blk, par),
                          scratch_shapes=[tm] * 6 + [pltpu.VMEM((2, rows, LANES), F32)],
                          compiler_params=_cparams(1), name="ssm_scan_bwd")(dx, x, a3)


SSM_TILES = WIDTH_C // LANES
SSM_TILE_W = SSM_CH // SSM_TILES


def _block_diag(m):
    gpt = N_GROUPS_C // SSM_TILES
    t = m.reshape(SSM_TILES, gpt, SSM_GROUP, SSM_STATE)
    eye = jnp.eye(gpt, dtype=m.dtype)
    return (t[:, :, :, None, :] * eye[None, :, None, :, None]).reshape(SSM_TILES, LANES, SSM_TILE_W)


def _block_diag_extract(t):
    gpt = N_GROUPS_C // SSM_TILES
    t = t.reshape(SSM_TILES, gpt, SSM_GROUP, gpt, SSM_STATE)
    eye = jnp.eye(gpt, dtype=t.dtype)
    return jnp.sum(t * eye[None, :, None, :, None], axis=3).reshape(WIDTH_C, SSM_STATE)


def _ssm_in(uc_src, uc_col0, bd):
    s = uc_src.shape[0]
    tm = _tile(s, 512, 8)
    j0 = uc_col0 // LANES
    return _mm("ssm_in", "nn", (s // tm, 2 * SSM_TILES, 1),
               uc_src, pl.BlockSpec((tm, LANES), lambda i, j, k: (i, j0 + j % SSM_TILES)),
               bd, pl.BlockSpec((None, None, LANES, SSM_TILE_W), lambda i, j, k: (j // SSM_TILES, j % SSM_TILES, 0, 0)),
               jax.ShapeDtypeStruct((2, s, SSM_CH), F32),
               pl.BlockSpec((None, tm, SSM_TILE_W), lambda i, j, k: (j // SSM_TILES, i, j % SSM_TILES)), None, exact=True)


def _ssm_in_dgrad(g, bd):
    s = g.shape[1]
    tm = _tile(s, 512, 8)
    return _mm("ssm_in_dgrad", "nt", (s // tm, SSM_TILES, 2),
               g, pl.BlockSpec((None, tm, SSM_TILE_W), lambda i, j, k: (k, i, j)),
               bd, pl.BlockSpec((None, None, LANES, SSM_TILE_W), lambda i, j, k: (k, j, 0, 0)),
               jax.ShapeDtypeStruct((s, WIDTH_C), F32), pl.BlockSpec((tm, LANES), lambda i, j, k: (i, j)),
               (tm, LANES), exact=True)


def _ssm_in_wgrad(uc_src, uc_col0, g):
    s = g.shape[1]
    tk = _tile(s, 512, 8)
    j0 = uc_col0 // LANES
    return _mm("ssm_in_wgrad", "tn", (2, SSM_TILES, s // tk),
               uc_src, pl.BlockSpec((tk, LANES), lambda i, j, k: (k, j0 + j)),
               g, pl.BlockSpec((None, tk, SSM_TILE_W), lambda i, j, k: (i, k, j)),
               jax.ShapeDtypeStruct((2, SSM_TILES, LANES, SSM_TILE_W), F32),
               pl.BlockSpec((None, None, LANES, SSM_TILE_W), lambda i, j, k: (i, j, 0, 0)), (LANES, SSM_TILE_W), exact=True)


def _ssm_out(x, cd):
    s = x.shape[1]
    tm = _tile(s, 512, 8)
    return _mm("ssm_out", "nn", (s // tm, SSM_TILES, 2),
               x, pl.BlockSpec((None, tm, SSM_TILE_W), lambda i, j, k: (k, i, j)),
               cd, pl.BlockSpec((None, None, SSM_TILE_W, LANES), lambda i, j, k: (k, j, 0, 0)),
               jax.ShapeDtypeStruct((s, WIDTH_C), F32), pl.BlockSpec((tm, LANES), lambda i, j, k: (i, j)),
               (tm, LANES), exact=True)


def _ssm_out_dgrad(dy, cd):
    s = dy.shape[0]
    tm = _tile(s, 512, 8)
    return _mm("ssm_out_dgrad", "nt", (s // tm, 2 * SSM_TILES, 1),
               dy, pl.BlockSpec((tm, LANES), lambda i, j, k: (i, j % SSM_TILES)),
               cd, pl.BlockSpec((None, None, SSM_TILE_W, LANES), lambda i, j, k: (j // SSM_TILES, j % SSM_TILES, 0, 0)),
               jax.ShapeDtypeStruct((2, s, SSM_CH), F32),
               pl.BlockSpec((None, tm, SSM_TILE_W), lambda i, j, k: (j // SSM_TILES, i, j % SSM_TILES)), None, exact=True)


def _ssm_out_wgrad(x, dy):
    s = dy.shape[0]
    tk = _tile(s, 512, 8)
    return _mm("ssm_out_wgrad", "tn", (2, SSM_TILES, s // tk),
               x, pl.BlockSpec((None, tk, SSM_TILE_W), lambda i, j, k: (i, k, j)),
               dy, pl.BlockSpec((tk, LANES), lambda i, j, k: (k, j)),
               jax.ShapeDtypeStruct((2, SSM_TILES, SSM_TILE_W, LANES), F32),
               pl.BlockSpec((None, None, SSM_TILE_W, LANES), lambda i, j, k: (i, j, 0, 0)), (SSM_TILE_W, LANES), exact=True)


def _ssm_param_inputs(w, l):
    rep = lambda a: jnp.repeat(a, SSM_GROUP, axis=0)
    lr, li, ld = w["lam_re"][l], w["lam_im"][l], w["log_dt"][l][:, None]
    br_t = jnp.transpose(w["b_re"][l], (0, 2, 1)).reshape(WIDTH_C, SSM_STATE)
    bi_t = jnp.transpose(w["b_im"][l], (0, 2, 1)).reshape(WIDTH_C, SSM_STATE)
    return lr, li, ld, rep(lr), rep(li), rep(ld), br_t, bi_t


def _ssm_params_fwd(pin):
    def body(*refs):
        res = _f_ssm_params(*[r[...] for r in refs[:8]])
        for r, v in zip(refs[8:], res):
            r[...] = v

    g, p = N_GROUPS_C, SSM_STATE
    return pl.pallas_call(body, out_shape=(jax.ShapeDtypeStruct((g, p), F32),) * 2
                          + (jax.ShapeDtypeStruct((WIDTH_C, p), F32),) * 2, name="ssm_params_fwd")(*pin)


def _ssm_params_bwd(pin, d_abr, d_abi, d_bbr, d_bbi):
    g, p = N_GROUPS_C, SSM_STATE
    group_sum = jnp.asarray(np.kron(np.eye(g, dtype=np.float32), np.ones((1, SSM_GROUP), np.float32)))

    def body(*refs):
        ins = [r[...] for r in refs[:8]]
        cts = tuple(r[...] for r in refs[8:12])
        gs = refs[12][...]
        d_lr_ref, d_li_ref, d_ld_ref, d_br_ref, d_bi_ref = refs[13:]
        _, vjp = jax.vjp(_f_ssm_params, *ins)
        d = vjp(cts)
        fold = lambda v: lax.dot_general(gs, v, _DIMS["nn"], preferred_element_type=F32, precision=lax.Precision.HIGHEST)
        d_lr_ref[...] = d[0] + fold(d[3])
        d_li_ref[...] = d[1] + fold(d[4])
        d_ld_ref[...] = d[2] + fold(jnp.broadcast_to(d[5], (WIDTH_C, p)))[:, 0:1]
        d_br_ref[...] = d[6]
        d_bi_ref[...] = d[7]

    return pl.pallas_call(body, out_shape=(jax.ShapeDtypeStruct((g, p), F32), jax.ShapeDtypeStruct((g, p), F32),
                                           jax.ShapeDtypeStruct((g, 1), F32), jax.ShapeDtypeStruct((WIDTH_C, p), F32),
                                           jax.ShapeDtypeStruct((WIDTH_C, p), F32)), name="ssm_params_bwd")(
        *pin, d_abr, d_abi, d_bbr, d_bbi, group_sum)


def _c_block_diag(w, l):
    def one(c):
        return jnp.transpose(_block_diag(c.reshape(WIDTH_C, SSM_STATE)), (0, 2, 1))
    return jnp.stack([one(w["c_re"][l]), -one(w["c_im"][l])], 0)


ANY = pl.BlockSpec(memory_space=pl.ANY)


def _ag8(name, xb):
    def body(x_ref, out_ref, send_sems, recv_sems, local_sem):
        x, y, c = lax.axis_index("x"), lax.axis_index("y"), lax.axis_index("c")
        me, sibling = (x, y, c), (x, y, 1 - c)
        chips = [(1 - x, y), (x, 1 - y), (1 - x, 1 - y)]

        def rows(px, py, pc):
            return out_ref.at[4 * px + 2 * py + pc]

        def copy(k, block, to, src=None):
            return pltpu.make_async_remote_copy(src_ref=rows(*block) if src is None else src, dst_ref=rows(*block),
                                                send_sem=send_sems.at[k], recv_sem=recv_sems.at[k],
                                                device_id=to, device_id_type=MESH)

        mine = pltpu.make_async_copy(x_ref, rows(*me), local_sem)
        mine.start()
        first = [copy(0, me, sibling, src=x_ref)]
        first += [copy(1 + j, me, (*chip, c), src=x_ref) for j, chip in enumerate(chips)]
        for cp in first:
            cp.start()
        passed = [copy(4 + j, (*chip, c), sibling) for j, chip in enumerate(chips)]
        for j, chip in enumerate(chips):
            copy(1 + j, (*chip, c), me).wait_recv()
            passed[j].start()
        copy(0, sibling, me).wait_recv()
        for j, chip in enumerate(chips):
            copy(4 + j, (*chip, 1 - c), me).wait_recv()
        for cp in first + passed:
            cp.wait_send()
        mine.wait()

    return pl.pallas_call(body, out_shape=jax.ShapeDtypeStruct((8,) + xb.shape, xb.dtype), in_specs=[ANY], out_specs=ANY,
                          scratch_shapes=[pltpu.SemaphoreType.DMA((7,)), pltpu.SemaphoreType.DMA((7,)),
                                          pltpu.SemaphoreType.DMA(())], name=name)(xb)


def _mesh_place():
    x, y, c = lax.axis_index("x"), lax.axis_index("y"), lax.axis_index("c")
    return x, y, c, [(1 - x, y), (x, 1 - y), (1 - x, 1 - y)]


def _gather_weights(wb, l):
    nw = len(wb)
    halves = [a.shape[1] // 2 for a in wb]

    def body(*refs):
        ins, outs = refs[:nw], refs[nw:2 * nw]
        send_sems, recv_sems, local_sems = refs[2 * nw:]
        x, y, c, chips = _mesh_place()
        me = 2 * x + y

        def half(ref, i, hc):
            return ref.at[pl.ds(hc * halves[i], halves[i])]

        def copy(i, k, src, dst, to):
            return pltpu.make_async_remote_copy(src_ref=src, dst_ref=dst, send_sem=send_sems.at[6 * i + k],
                                                recv_sem=recv_sems.at[6 * i + k], device_id=to, device_id_type=MESH)

        own = [pltpu.make_async_copy(ins[i].at[l], outs[i].at[me], local_sems.at[i]) for i in range(nw)]
        for cp in own:
            cp.start()
        sends = [copy(i, k, half(ins[i].at[l], i, c), half(outs[i].at[me], i, c), (px, py, c))
                 for i in range(nw) for k, (px, py) in enumerate(chips)]
        for cp in sends:
            cp.start()
        for k, (px, py) in enumerate(chips):
            for i in range(nw):
                landed = half(outs[i].at[2 * px + py], i, c)
                copy(i, k, landed, landed, (px, py, c)).wait_recv()
                onward = copy(i, 3 + k, landed, landed, (x, y, 1 - c))
                onward.start()
                sends.append(onward)
        for k, (px, py) in enumerate(chips):
            for i in range(nw):
                other = half(outs[i].at[2 * px + py], i, 1 - c)
                copy(i, 3 + k, other, other, (x, y, 1 - c)).wait_recv()
        for cp in sends:
            cp.wait_send()
        for cp in own:
            cp.wait()

    return pl.pallas_call(body, out_shape=[jax.ShapeDtypeStruct((4,) + a.shape[1:], a.dtype) for a in wb],
                          in_specs=[ANY] * nw, out_specs=[ANY] * nw,
                          scratch_shapes=[pltpu.SemaphoreType.DMA((6 * nw,)), pltpu.SemaphoreType.DMA((6 * nw,)),
                                          pltpu.SemaphoreType.DMA((nw,))], name="gather_weights")(*wb)


def _grad_core_swap(g4):
    nw = len(g4)
    halves = [a.shape[1] // 2 for a in g4]

    def body(*refs):
        ins, mine, theirs = refs[:nw], refs[nw:2 * nw], refs[2 * nw:3 * nw]
        send_sems, recv_sems, local_sems = refs[3 * nw:]
        x, y, c, _ = _mesh_place()
        keep = [pltpu.make_async_copy(ins[i].at[:, pl.ds(c * halves[i], halves[i])], mine[i], local_sems.at[i])
                for i in range(nw)]
        give = [pltpu.make_async_remote_copy(src_ref=ins[i].at[:, pl.ds((1 - c) * halves[i], halves[i])],
                                             dst_ref=theirs[i], send_sem=send_sems.at[i], recv_sem=recv_sems.at[i],
                                             device_id=(x, y, 1 - c), device_id_type=MESH) for i in range(nw)]
        for cp in keep + give:
            cp.start()
        for cp in give + keep:
            cp.wait()

    sds = [jax.ShapeDtypeStruct((4, h) + a.shape[2:], a.dtype) for a, h in zip(g4, halves)]
    res = pl.pallas_call(body, out_shape=sds + sds, in_specs=[ANY] * nw, out_specs=[ANY] * (2 * nw),
                         scratch_shapes=[pltpu.SemaphoreType.DMA((nw,)), pltpu.SemaphoreType.DMA((nw,)),
                                         pltpu.SemaphoreType.DMA((nw,))], name="grad_core_swap")(*g4)
    return res[:nw], res[nw:]


def _grad_chip_exchange(t4):
    nw = len(t4)

    def body(*refs):
        ins, outs = refs[:nw], refs[nw:2 * nw]
        send_sems, recv_sems, local_sems = refs[2 * nw:]
        x, y, c, chips = _mesh_place()
        me = 2 * x + y

        def copy(i, k, src, dst, px, py):
            return pltpu.make_async_remote_copy(src_ref=src, dst_ref=dst, send_sem=send_sems.at[3 * i + k],
                                                recv_sem=recv_sems.at[3 * i + k], device_id=(px, py, c),
                                                device_id_type=MESH)

        own = [pltpu.make_async_copy(ins[i].at[me], outs[i].at[me], local_sems.at[i]) for i in range(nw)]
        sends = [copy(i, k, ins[i].at[2 * px + py], outs[i].at[me], px, py)
                 for i in range(nw) for k, (px, py) in enumerate(chips)]
        for cp in own + sends:
            cp.start()
        for i in range(nw):
            for k, (px, py) in enumerate(chips):
                copy(i, k, ins[i].at[me], outs[i].at[2 * px + py], px, py).wait_recv()
        for cp in sends:
            cp.wait_send()
        for cp in own:
            cp.wait()

    return pl.pallas_call(body, out_shape=[jax.ShapeDtypeStruct(a.shape, a.dtype) for a in t4],
                          in_specs=[ANY] * nw, out_specs=[ANY] * nw,
                          scratch_shapes=[pltpu.SemaphoreType.DMA((3 * nw,)), pltpu.SemaphoreType.DMA((3 * nw,)),
                                          pltpu.SemaphoreType.DMA((nw,))], name="grad_chip_exchange")(*t4)


def _grad_half_swap(h):
    nw = len(h)
    halves = [a.shape[0] for a in h]

    def body(*refs):
        ins, outs = refs[:nw], refs[nw:2 * nw]
        send_sems, recv_sems, local_sems = refs[2 * nw:]
        x, y, c, _ = _mesh_place()
        keep = [pltpu.make_async_copy(ins[i], outs[i].at[pl.ds(c * halves[i], halves[i])], local_sems.at[i])
                for i in range(nw)]
        give = [pltpu.make_async_remote_copy(src_ref=ins[i], dst_ref=outs[i].at[pl.ds(c * halves[i], halves[i])],
                                             send_sem=send_sems.at[i], recv_sem=recv_sems.at[i],
                                             device_id=(x, y, 1 - c), device_id_type=MESH) for i in range(nw)]
        for cp in keep + give:
            cp.start()
        for i in range(nw):
            give[i].wait_send()
            arrival = outs[i].at[pl.ds((1 - c) * halves[i], halves[i])]
            pltpu.make_async_remote_copy(src_ref=ins[i], dst_ref=arrival, send_sem=send_sems.at[i],
                                         recv_sem=recv_sems.at[i], device_id=(x, y, 1 - c),
                                         device_id_type=MESH).wait_recv()
        for cp in keep:
            cp.wait()

    return pl.pallas_call(body, out_shape=[jax.ShapeDtypeStruct((2 * a.shape[0],) + a.shape[1:], a.dtype) for a in h],
                          in_specs=[ANY] * nw, out_specs=[ANY] * nw,
                          scratch_shapes=[pltpu.SemaphoreType.DMA((nw,)), pltpu.SemaphoreType.DMA((nw,)),
                                          pltpu.SemaphoreType.DMA((nw,))], name="grad_half_swap")(*h)


def _sum_lead(name, xb, out_dtype=F32):
    n, rows, w = xb.shape
    tr = _tile(rows, max(SUBLANES_BF16, (1 << 18) // w), SUBLANES_BF16)

    def body(x_ref, o_ref):
        acc = x_ref[0].astype(F32)
        for k in range(1, n):
            acc = acc + x_ref[k].astype(F32)
        o_ref[...] = acc.astype(o_ref.dtype)

    return pl.pallas_call(body, out_shape=jax.ShapeDtypeStruct((rows, w), out_dtype), grid=(rows // tr,),
                          in_specs=[pl.BlockSpec((n, tr, w), lambda i: (0, i, 0))],
                          out_specs=pl.BlockSpec((tr, w), lambda i: (i, 0)), compiler_params=_cparams(1), name=name)(xb)


def _pad_to(v, mult):
    n = v.shape[-1]
    pad = (-n) % mult
    return v if pad == 0 else jnp.pad(v, [(0, 0)] * (v.ndim - 1) + [(0, pad)])


RELAYOUT_ROWS = 256


def _cols_from_shards(a):
    _, k, cc = a.shape
    tr = _tile(k, RELAYOUT_ROWS, SUBLANES_BF16)

    def body(i_ref, o_ref):
        for j in range(4):
            o_ref[:, j * cc:(j + 1) * cc] = i_ref[j]

    return pl.pallas_call(body, out_shape=jax.ShapeDtypeStruct((k, 4 * cc), a.dtype), grid=(k // tr,),
                          in_specs=[pl.BlockSpec((4, tr, cc), lambda i: (0, i, 0))],
                          out_specs=pl.BlockSpec((tr, 4 * cc), lambda i: (i, 0)),
                          compiler_params=_cparams(1), name="cols_from_shards")(a)


def _cols_to_shards(a):
    k, n = a.shape
    cc = n // 4
    tr = _tile(k, RELAYOUT_ROWS, SUBLANES_BF16)

    def body(i_ref, o_ref):
        for j in range(4):
            o_ref[j] = i_ref[:, j * cc:(j + 1) * cc]

    return pl.pallas_call(body, out_shape=jax.ShapeDtypeStruct((4, k, cc), a.dtype), grid=(k // tr,),
                          in_specs=[pl.BlockSpec((tr, n), lambda i: (i, 0))],
                          out_specs=pl.BlockSpec((4, tr, cc), lambda i: (0, i, 0)),
                          compiler_params=_cparams(1), name="cols_to_shards")(a)


def _gather_layer_weights(wb, l):
    got = _gather_weights([wb[n] for n in BIG_NAMES], l)
    out = {}
    for (n, kind), a in zip(BIG, got):
        if kind == "row":
            out[n] = a.reshape(4 * a.shape[1], a.shape[2])
        else:
            out[n] = _cols_from_shards(a) if n == "w_in" else a
    return out


def _reduce_layer_grads(g):
    g4 = []
    for n, kind in BIG:
        a = g[n]
        if kind == "row":
            a = a.reshape(4, a.shape[0] // 4, a.shape[1])
        elif a.ndim == 2:
            a = _cols_to_shards(a)
        g4.append(a)
    mine, theirs = _grad_core_swap(g4)
    pairs = []
    for a, b in zip(mine, theirs):
        _, rh, cc = a.shape
        t = _ew("grad_core_add", lambda p, q: p.astype(F32) + q.astype(F32), 4 * rh,
                [R(a.reshape(4 * rh, cc)), R(b.reshape(4 * rh, cc))], [(cc, BF16)], tr=max(SUBLANES_BF16, (1 << 19) // cc))
        pairs.append(t.reshape(4, rh, cc))
    arrived = _grad_chip_exchange(pairs)
    halves = [_sum_lead("grad_chip_sum", a) for a in arrived]
    full = _grad_half_swap(halves)
    return dict(zip(BIG_NAMES, full))


def _adam_fn(w, g, m, v):
    m = ADAM_B1 * m + (1.0 - ADAM_B1) * g
    v = ADAM_B2 * v + (1.0 - ADAM_B2) * (g * g)
    m_hat = m / (1.0 - ADAM_B1 ** ADAM_STEP)
    v_hat = v / (1.0 - ADAM_B2 ** ADAM_STEP)
    return -ADAM_LR * (m_hat / (jnp.sqrt(v_hat) + ADAM_EPS) + ADAM_WD * w), m, v


def _adamw(name, w, g, m, v):
    shape = w.shape
    cols = shape[-1]
    f = lambda a: a.reshape(-1, cols)
    rows = f(w).shape[0]
    tr = max(8, (1 << 19) // cols)
    d, nm, nv = _ew(name, _adam_fn, rows, [R(f(w)), R(f(g)), R(f(m)), R(f(v))], [(cols, F32)] * 3, tr=tr)
    return d.reshape(shape), nm.reshape(shape), nv.reshape(shape)


def _layer_fwd(x, wl, bias, alpha):
    s, d = x.shape
    d_ff = wl["w_ffn_out"].shape[0]
    b_in = wl["b_in"][None, :]
    qkv = _mm_nn("proj_qkv", x, wl["w_in"], n0=0, n=3 * QKV_W, bias=b_in, out_dtype=BF16, tn=768)
    rest = _mm_nn("proj_rest", x, wl["w_in"], n0=3 * QKV_W, bias=b_in, tn=768)
    qkv_s = _to_streams(qkv)
    o_s, l_s = _attn_fwd(qkv_s, bias)
    o_t, l_t = _streams_to_tok(o_s), _streams_to_tok(l_s)
    ya, lse = _ew("attn_combine", _f_combine, s, [R(a) for a in o_t + l_t], [(WIDTH_A, F32), (WIDTH_A, F32)])
    yb = _sgu_fwd(rest, wl["sgu_ln_g"][None], wl["sgu_ln_b"][None], wl["w_tril"], wl["b_col"])
    bu = _ssm_in(rest, REST_UC, wl["bd"])
    xs = _scan_fwd(bu, wl["a3"])
    y_lin = _ssm_out(xs, wl["cd"])
    uc_blk = REST_UC // WIDTH_C
    yc0 = _ew("ssm_skip_gelu", _f_ssm_out, s, [R(y_lin), R(rest, WIDTH_C, uc_blk), P(wl["d_skip"][None])], [(WIDTH_C, F32)])
    t_glu = _mm_nn("glu_proj", yc0, wl["w_glu"], bias=wl["b_glu"][None], tn=768)
    yc = _ew("glu", _f_glu, s, [R(yc0), R(t_glu)], [(WIDTH_C, F32)])
    pa = _mm_nn("proj_a", ya, wl["w_pa"])
    pb = _mm_nn("proj_b", yb, wl["w_pb"])
    pc = _mm_nn("proj_c", yc, wl["w_pc"])
    gw = _tile(math.gcd(d, REST_GL), 256, LANES)
    gl_ins = [R(rest, gw, (REST_GL + i * d) // gw, True) for i in range(N_BRANCH)]
    merged = _ew("merge", _f_merge, s, gl_ins + [R(pa, gw, 0, True), R(pb, gw, 0, True), R(pc, gw, 0, True)],
                 [(gw, F32)], nj=d // gw, tr=512)
    h1 = _mm_nn("proj_o", merged, wl["w_o"], add=x, add_scale=alpha)
    x1 = _ew("ln1", _f_ln, s, [R(h1), P(wl["ln1_g"][None]), P(wl["ln1_b"][None])], [(d, F32)])
    ff = _mm_nn("ffn_in", x1, wl["w_ffn_in"], tn=1408)
    fw = _tile(d_ff, 512, LANES)
    act = _ew("swiglu", _f_swiglu, s, [R(ff, fw, 0, True), R(ff, fw, d_ff // fw, True)], [(fw, F32)], nj=d_ff // fw, tr=512)
    h2 = _mm_nn("ffn_out", act, wl["w_ffn_out"], add=x1, add_scale=alpha, tk=1408)
    x2 = _ew("ln2", _f_ln, s, [R(h2), P(wl["ln2_g"][None]), P(wl["ln2_b"][None])], [(d, F32)])
    saved = dict(x=x, qkv_s=qkv_s, rest=rest, ya=ya, lse=lse, yb=yb, xs=xs, y_lin=y_lin, yc0=yc0, t_glu=t_glu, yc=yc,
                 pa=pa, pb=pb, pc=pc, merged=merged, h1=h1, x1=x1, ff=ff, act=act, h2=h2)
    return x2, saved


def _vjp_rows(f, n_primal):
    def fn(*args):
        n_ct = len(args) - n_primal
        cts, primals = args[:n_ct], args[n_ct:]
        out, vjp = jax.vjp(f, *primals)
        ct = tuple(c.astype(F32) for c in cts)
        return vjp(ct if isinstance(out, (tuple, list)) else ct[0])
    return fn


def _layer_bwd(dx2, sv, wl, bias, alpha):
    s, d = dx2.shape
    d_ff = wl["w_ffn_out"].shape[0]
    g = {}
    row = lambda a: a[None]
    dh2, g["ln2_g"], g["ln2_b"] = _ew("ln2_bwd", _vjp_rows(_f_ln, 3), s,
                                      [R(dx2), R(sv["h2"]), P(row(wl["ln2_g"])), P(row(wl["ln2_b"]))],
                                      [(d, F32)], accs=[d, d])
    g["w_ffn_out"] = _mm_tn("ffn_out_wgrad", sv["act"], dh2)
    dact = _mm_nt("ffn_out_dgrad", dh2, wl["w_ffn_out"], tn=1408, tk=2048)
    fw = _tile(d_ff, 512, LANES)
    nf = d_ff // fw
    dgf, dup = _ew("swiglu_bwd", _vjp_rows(_f_swiglu, 2), s,
                   [R(dact, fw, 0, True), R(sv["ff"], fw, 0, True), R(sv["ff"], fw, nf, True)],
                   [(fw, BF16), (fw, BF16)], nj=nf, tr=512)
    dff = jnp.concatenate([dgf, dup], axis=1)
    g["w_ffn_in"] = _mm_tn("ffn_in_wgrad", sv["x1"], dff, tn=1408, col_shards=True)
    dx1 = _mm_nt("ffn_in_dgrad", dff, wl["w_ffn_in"], add=dh2, add_scale=alpha, tk=1408)
    dh1, g["ln1_g"], g["ln1_b"] = _ew("ln1_bwd", _vjp_rows(_f_ln, 3), s,
                                      [R(dx1), R(sv["h1"]), P(row(wl["ln1_g"])), P(row(wl["ln1_b"]))],
                                      [(d, F32)], accs=[d, d])
    g["w_o"] = _mm_tn("proj_o_wgrad", sv["merged"], dh1)
    dmerged = _mm_nt("proj_o_dgrad", dh1, wl["w_o"])
    rest = sv["rest"]
    gw = _tile(math.gcd(d, REST_GL), 256, LANES)
    gl_ins = [R(rest, gw, (REST_GL + i * d) // gw, True) for i in range(N_BRANCH)]
    dg0, dg1, dg2, dpa, dpb, dpc = _ew(
        "merge_bwd", _vjp_rows(_f_merge, 6), s,
        [R(dmerged, gw, 0, True)] + gl_ins + [R(sv[k], gw, 0, True) for k in ("pa", "pb", "pc")],
        [(gw, BF16)] * 6, nj=d // gw, tr=512)
    g["w_pa"] = _mm_tn("proj_a_wgrad", sv["ya"], dpa, col_shards=True)
    g["w_pb"] = _mm_tn("proj_b_wgrad", sv["yb"], dpb, col_shards=True)
    g["w_pc"] = _mm_tn("proj_c_wgrad", sv["yc"], dpc, col_shards=True)
    dya = _mm_nt("proj_a_dgrad", dpa, wl["w_pa"], tn=512)
    dyb = _mm_nt("proj_b_dgrad", dpb, wl["w_pb"], tn=768)
    dyc = _mm_nt("proj_c_dgrad", dpc, wl["w_pc"], tn=768)
    dyc0_a, dt_glu, g["b_glu"] = _ew("glu_bwd", lambda ct, a, t: (lambda r: (r[0], r[1], _rowsum(r[1])))(_vjp_rows(_f_glu, 2)(ct, a, t)),
                                     s, [R(dyc), R(sv["yc0"]), R(sv["t_glu"])], [(WIDTH_C, F32), (WIDTH_C, F32)], accs=[WIDTH_C])
    g["w_glu"] = _mm_tn("glu_wgrad", sv["yc0"], dt_glu)
    dyc0 = _mm_nt("glu_dgrad", dt_glu, wl["w_glu"], add=dyc0_a, tn=768)
    uc_blk = REST_UC // WIDTH_C
    dy_lin, duc_skip, g["d_skip"] = _ew("ssm_skip_gelu_bwd", _vjp_rows(_f_ssm_out, 3), s,
                                        [R(dyc0), R(sv["y_lin"]), R(rest, WIDTH_C, uc_blk), P(row(wl["d_skip"]))],
                                        [(WIDTH_C, F32), (WIDTH_C, F32)], accs=[WIDTH_C])
    d_cd = _ssm_out_wgrad(sv["xs"], dy_lin)
    dxs = _ssm_out_dgrad(dy_lin, wl["cd"])
    gs, da = _scan_bwd(dxs, sv["xs"], wl["a3"])
    d_bd = _ssm_in_wgrad(rest, REST_UC, gs)
    duc_lin = _ssm_in_dgrad(gs, wl["bd"])
    duc = _ew("ssm_duc", lambda a, b: a + b, s, [R(duc_lin), R(duc_skip)], [(WIDTH_C, BF16)])
    d_abr = da[0].reshape(N_GROUPS_C, SSM_STATE)
    d_abi = da[1].reshape(N_GROUPS_C, SSM_STATE)
    d_lr, d_li, d_ld, d_br_t, d_bi_t = _ssm_params_bwd(wl["ssm_pin"], d_abr, d_abi,
                                                       _block_diag_extract(d_bd[0]), _block_diag_extract(d_bd[1]))
    g["lam_re"], g["lam_im"], g["log_dt"] = d_lr, d_li, d_ld[:, 0]
    un_t = lambda a: jnp.transpose(a.reshape(N_GROUPS_C, SSM_GROUP, SSM_STATE), (0, 2, 1))
    g["b_re"], g["b_im"] = un_t(d_br_t), un_t(d_bi_t)
    cd_ex = lambda a: _block_diag_extract(jnp.transpose(a, (0, 2, 1))).reshape(N_GROUPS_C, SSM_GROUP, SSM_STATE)
    g["c_re"], g["c_im"] = cd_ex(d_cd[0]), -cd_ex(d_cd[1])
    dzu, dzv, dws, dbs, g["sgu_ln_g"], g["sgu_ln_b"] = _sgu_bwd(rest, row(wl["sgu_ln_g"]), row(wl["sgu_ln_b"]),
                                                                  wl["w_tril"], wl["b_col"], dyb)
    g["w_s"] = jnp.tril(dws)
    g["b_s"] = dbs[:, :, 0]
    do_s, ya_s, lse_s = _tok_to_streams(dya), _tok_to_streams(sv["ya"]), _tok_to_streams(sv["lse"])
    dq, dkc, dkp, dvc, dvp, dbias = _attn_bwd(sv["qkv_s"], bias, do_s, ya_s, lse_s)
    dk, dv = _shift_add(dkc, dkp), _shift_add(dvc, dvp)
    dqkv = jnp.stack([jnp.stack(_streams_to_tok(t), 1) for t in (dq, dk, dv)], 1)
    dproj = jnp.concatenate([dqkv.reshape(s, 3 * QKV_W).astype(BF16), dzu, dzv, duc, dg0, dg1, dg2], axis=1)
    n_in = dproj.shape[1]
    cw = _tile(n_in, 1024, LANES)
    g["b_in"] = _ew("b_in_grad", lambda a: _rowsum(a.astype(F32)), s, [R(dproj, cw, 0, True)], [], accs=[cw],
                    nj=n_in // cw, tr=512)
    g["w_in"] = _mm_tn("proj_in_wgrad", sv["x"], dproj, tn=768)
    dx = _mm_nt("proj_in_dgrad", dproj, wl["w_in"], add=dh1, add_scale=alpha, tk=768)
    for k in ("ln2_g", "ln2_b", "ln1_g", "ln1_b", "b_glu", "d_skip", "sgu_ln_g", "sgu_ln_b", "b_in"):
        g[k] = g[k][0]
    return dx, g, dbias


def _loss_head(y, target):
    s, d = y.shape

    def fn(yb, tb):
        err = yb - tb
        return err / d, _rowsum(err * err)

    dy, sq = _ew("loss_head", fn, s, [R(y), R(target)], [(d, F32)], accs=[d])
    return dy, 0.5 * jnp.sum(sq) / d


def _step(x, target, w, m, v):
    depth = w["w_in"].shape[0]
    alpha = (2 * depth) ** 0.25
    bias = _band_bias(w["rel_bias"])
    wb = {n: w[n].astype(BF16) for n in BIG_NAMES}
    xl = x[0]
    layers, saved = [], []
    for l in range(depth):
        wl = _gather_layer_weights(wb, l)
        for n in SMALL:
            if n != "rel_bias":
                wl[n] = w[n][l]
        wl["w_tril"] = jnp.tril(w["w_s"][l]).astype(BF16)
        wl["b_col"] = w["b_s"][l][:, :, None]
        pin = _ssm_param_inputs(w, l)
        abr, abi, bbr, bbi = _ssm_params_fwd(pin)
        wl["ssm_pin"] = pin
        wl["a3"] = jnp.stack([abr.reshape(-1, LANES), abi.reshape(-1, LANES)], 0)
        wl["bd"] = jnp.stack([_block_diag(bbr), _block_diag(bbi)], 0)
        wl["cd"] = _c_block_diag(w, l)
        xl, sv = _layer_fwd(xl, wl, bias, alpha)
        layers.append(wl)
        saved.append(sv)
    dx, loss = _loss_head(xl, target[0])
    loss = lax.psum(loss, ("x", "y", "c"))
    grads = {n: [None] * depth for n in WEIGHTS if n != "rel_bias"}
    dbias_sum = None
    for l in reversed(range(depth)):
        dx, g, dbias = _layer_bwd(dx, saved[l], layers[l], bias, alpha)
        dbias_sum = dbias if dbias_sum is None else dbias_sum + dbias
        red = _reduce_layer_grads({n: g[n] for n in BIG_NAMES})
        for n in BIG_NAMES:
            grads[n][l] = red[n]
        for n in SMALL:
            if n != "rel_bias":
                grads[n][l] = g[n]
    grads = {n: jnp.stack(gl, 0) for n, gl in grads.items()}
    grads["rel_bias"] = _bias_to_buckets(dbias_sum)
    small_vec = _pad_to(jnp.concatenate([grads[n].reshape(-1) for n in SMALL]), 8 * LANES).reshape(-1, LANES)
    small_sum = _sum_lead("small_grad_sum", _ag8("small_grad_gather", small_vec)).reshape(-1)
    off = 0
    for n in SMALL:
        size = math.prod(w[n].shape)
        grads[n] = small_sum[off:off + size].reshape(w[n].shape)
        off += size
    pack = lambda t: _pad_to(jnp.concatenate([t[n].reshape(-1) for n in SMALL]), 8 * LANES).reshape(-1, LANES)
    sd, sm, sv_ = _adamw("adamw_small", pack(w), small_sum.reshape(-1, LANES), pack(m), pack(v))
    delta, new_m, new_v = {}, {}, {}
    off = 0
    for n in SMALL:
        size = math.prod(w[n].shape)
        take = lambda t: t.reshape(-1)[off:off + size].reshape(w[n].shape)
        delta[n], new_m[n], new_v[n] = take(sd), take(sm), take(sv_)
        off += size
    for n in BIG_NAMES:
        delta[n], new_m[n], new_v[n] = _adamw("adamw_" + n, w[n], grads[n], m[n], v[n])
    return loss, dx[None], grads, delta, new_m, new_v


def kernel(x, w_in, b_in, rel_bias, sgu_ln_g, sgu_ln_b, w_s, b_s, lam_re, lam_im, log_dt, b_re, b_im, c_re, c_im, d_skip, w_glu, b_glu, w_pa, w_pb, w_pc, w_o, ln1_g, ln1_b, w_ffn_in, w_ffn_out, ln2_g, ln2_b, loss_target, m_w_in, m_b_in, m_rel_bias, m_sgu_ln_g, m_sgu_ln_b, m_w_s, m_b_s, m_lam_re, m_lam_im, m_log_dt, m_b_re, m_b_im, m_c_re, m_c_im, m_d_skip, m_w_glu, m_b_glu, m_w_pa, m_w_pb, m_w_pc, m_w_o, m_ln1_g, m_ln1_b, m_w_ffn_in, m_w_ffn_out, m_ln2_g, m_ln2_b, v_w_in, v_b_in, v_rel_bias, v_sgu_ln_g, v_sgu_ln_b, v_w_s, v_b_s, v_lam_re, v_lam_im, v_log_dt, v_b_re, v_b_im, v_c_re, v_c_im, v_d_skip, v_w_glu, v_b_glu, v_w_pa, v_w_pb, v_w_pc, v_w_o, v_ln1_g, v_ln1_b, v_w_ffn_in, v_w_ffn_out, v_ln2_g, v_ln2_b):
    w = dict(w_in=w_in, b_in=b_in, rel_bias=rel_bias, sgu_ln_g=sgu_ln_g, sgu_ln_b=sgu_ln_b, w_s=w_s, b_s=b_s,
             lam_re=lam_re, lam_im=lam_im, log_dt=log_dt, b_re=b_re, b_im=b_im, c_re=c_re, c_im=c_im, d_skip=d_skip,
             w_glu=w_glu, b_glu=b_glu, w_pa=w_pa, w_pb=w_pb, w_pc=w_pc, w_o=w_o, ln1_g=ln1_g, ln1_b=ln1_b,
             w_ffn_in=w_ffn_in, w_ffn_out=w_ffn_out, ln2_g=ln2_g, ln2_b=ln2_b)
    m = dict(w_in=m_w_in, b_in=m_b_in, rel_bias=m_rel_bias, sgu_ln_g=m_sgu_ln_g, sgu_ln_b=m_sgu_ln_b, w_s=m_w_s,
             b_s=m_b_s, lam_re=m_lam_re, lam_im=m_lam_im, log_dt=m_log_dt, b_re=m_b_re, b_im=m_b_im, c_re=m_c_re,
             c_im=m_c_im, d_skip=m_d_skip, w_glu=m_w_glu, b_glu=m_b_glu, w_pa=m_w_pa, w_pb=m_w_pb, w_pc=m_w_pc,
             w_o=m_w_o, ln1_g=m_ln1_g, ln1_b=m_ln1_b, w_ffn_in=m_w_ffn_in, w_ffn_out=m_w_ffn_out, ln2_g=m_ln2_g,
             ln2_b=m_ln2_b)
    v = dict(w_in=v_w_in, b_in=v_b_in, rel_bias=v_rel_bias, sgu_ln_g=v_sgu_ln_g, sgu_ln_b=v_sgu_ln_b, w_s=v_w_s,
             b_s=v_b_s, lam_re=v_lam_re, lam_im=v_lam_im, log_dt=v_log_dt, b_re=v_b_re, b_im=v_b_im, c_re=v_c_re,
             c_im=v_c_im, d_skip=v_d_skip, w_glu=v_w_glu, b_glu=v_b_glu, w_pa=v_w_pa, w_pb=v_w_pb, w_pc=v_w_pc,
             w_o=v_w_o, ln1_g=v_ln1_g, ln1_b=v_ln1_b, w_ffn_in=v_w_ffn_in, w_ffn_out=v_w_ffn_out, ln2_g=v_ln2_g,
             ln2_b=v_ln2_b)
    loss, grad_x, grads, delta, new_m, new_v = _step(x, loss_target, w, m, v)
    return (loss, grad_x, *[grads[n] for n in WEIGHTS], *[delta[n] for n in WEIGHTS],
            *[new_m[n] for n in WEIGHTS], *[new_v[n] for n in WEIGHTS])
```

```python
import functools
import math

import numpy as np
import jax
import jax.numpy as jnp
from jax import lax
from jax.experimental import pallas as pl
from jax.experimental.pallas import tpu as pltpu

F32 = jnp.float32
BF16 = jnp.bfloat16
MESH = pl.DeviceIdType.MESH

ATT_PATTERNS = ((128, 1), (512, 4), (2048, 16))
N_GROUPS_A = 3
HEADS = 8
HEAD_DIM = 64
WIDTH_A = HEADS * HEAD_DIM
QKV_W = N_GROUPS_A * WIDTH_A
ATT_BLOCK = 128
N_REL_BUCKETS = 32
REL_MAX_DIST = 2048
NEG_INF = -1e30
CHUNK = 128
WIDTH_B = 768
N_GROUPS_B = 6
GROUP_B = 128
WIDTH_C = 768
SSM_GROUP = 16
N_GROUPS_C = 48
SSM_STATE = 64
SSM_CH = N_GROUPS_C * SSM_STATE
N_BRANCH = 3
LN_EPS = 1e-5
ADAM_LR = 0.001
ADAM_B1 = 0.9
ADAM_B2 = 0.999
ADAM_EPS = 1e-08
ADAM_WD = 0.01
ADAM_STEP = 10

LANES = 128
SUBLANES_BF16 = 16
V7X_VMEM_BYTES = 64 * 1024 * 1024
VMEM_LIMIT = V7X_VMEM_BYTES * 7 // 8

OFF_ZB = 3 * QKV_W
REST_ZB = 0
REST_UC = 2 * WIDTH_B
REST_GL = 2 * WIDTH_B + WIDTH_C

BIG = (("w_in", "col"), ("w_glu", "row"), ("w_pa", "col"), ("w_pb", "col"), ("w_pc", "col"),
       ("w_o", "row"), ("w_ffn_in", "col"), ("w_ffn_out", "row"))
WEIGHTS = ("w_in", "b_in", "rel_bias", "sgu_ln_g", "sgu_ln_b", "w_s", "b_s", "lam_re", "lam_im", "log_dt",
           "b_re", "b_im", "c_re", "c_im", "d_skip", "w_glu", "b_glu", "w_pa", "w_pb", "w_pc", "w_o",
           "ln1_g", "ln1_b", "w_ffn_in", "w_ffn_out", "ln2_g", "ln2_b")
BIG_NAMES = tuple(n for n, _ in BIG)
SMALL = tuple(n for n in WEIGHTS if n not in BIG_NAMES)


def _tile(n, target, mult):
    t = (min(target, n) // mult) * mult
    while t >= mult:
        if n % t == 0:
            return t
        t -= mult
    return n


def _cparams(n_axes):
    return pltpu.CompilerParams(dimension_semantics=("arbitrary",) * n_axes, vmem_limit_bytes=VMEM_LIMIT)


_DIMS = {"nn": (((1,), (0,)), ((), ())), "nt": (((1,), (1,)), ((), ())), "tn": (((0,), (0,)), ((), ()))}


def _mm(name, mode, grid, a, a_spec, b, b_spec, out_sds, o_spec, acc_shape, *, bias=None, bias_spec=None,
        add=None, add_spec=None, add_scale=1.0, exact=False):
    nk = grid[2]
    has_bias = bias is not None
    has_add = add is not None

    def body(*refs):
        a_ref, b_ref = refs[0], refs[1]
        pos = 2
        bias_ref = add_ref = None
        if has_bias:
            bias_ref = refs[pos]
            pos += 1
        if has_add:
            add_ref = refs[pos]
            pos += 1
        o_ref = refs[pos]
        if exact:
            part = lax.dot_general(a_ref[...].astype(F32), b_ref[...].astype(F32), _DIMS[mode],
                                   preferred_element_type=F32, precision=lax.Precision.HIGHEST)
        else:
            part = lax.dot_general(a_ref[...].astype(BF16), b_ref[...].astype(BF16), _DIMS[mode],
                                   preferred_element_type=F32)

        def finish(r):
            if has_bias:
                r = r + bias_ref[...]
            if has_add:
                r = r + add_scale * add_ref[...].astype(F32)
            o_ref[...] = r.astype(o_ref.dtype)

        if nk == 1:
            finish(part)
        else:
            acc_ref = refs[pos + 1]
            k = pl.program_id(2)

            @pl.when(k == 0)
            def _():
                acc_ref[...] = part

            @pl.when(k > 0)
            def _():
                acc_ref[...] += part

            @pl.when(k == nk - 1)
            def _():
                finish(acc_ref[...])

    ins, specs = [a, b], [a_spec, b_spec]
    if has_bias:
        ins.append(bias)
        specs.append(bias_spec)
    if has_add:
        ins.append(add)
        specs.append(add_spec)
    scratch = [pltpu.VMEM(acc_shape, F32)] if nk > 1 else []
    return pl.pallas_call(body, out_shape=out_sds, grid=grid, in_specs=specs, out_specs=o_spec,
                          scratch_shapes=scratch, compiler_params=_cparams(3), name=name)(*ins)


def _mm_nn(name, a, b, *, n0=0, n=None, a0=0, bias=None, add=None, add_scale=1.0, out_dtype=F32,
           tm=512, tn=512, tk=None):
    m = a.shape[0]
    if b.ndim == 3:
        k, cc = b.shape[1:]
        n = 4 * cc
        tn = _tile(cc, tn, LANES)
        per = cc // tn
        tk = _tile(k, k if tk is None else tk, LANES)
        b_spec = pl.BlockSpec((None, tk, tn), lambda i, j, kk: (j // per, kk, j % per))
    else:
        k = b.shape[0]
        n = b.shape[1] - n0 if n is None else n
        tn = _tile(math.gcd(n, n0) if n0 else n, tn, LANES)
        tk = _tile(math.gcd(k, a0) if a0 else k, k if tk is None else tk, LANES)
        b_spec = pl.BlockSpec((tk, tn), lambda i, j, kk: (kk, n0 // tn + j))
    tm = _tile(m, tm, SUBLANES_BF16)
    jn0, ka0 = n0 // tn, a0 // tk
    o_spec = pl.BlockSpec((tm, tn), lambda i, j, kk: (i, j))
    return _mm(name, "nn", (m // tm, n // tn, k // tk),
               a, pl.BlockSpec((tm, tk), lambda i, j, kk: (i, ka0 + kk)), b, b_spec,
               jax.ShapeDtypeStruct((m, n), out_dtype), o_spec, (tm, tn),
               bias=bias, bias_spec=pl.BlockSpec((1, tn), lambda i, j, kk: (0, jn0 + j)),
               add=add, add_spec=o_spec, add_scale=add_scale)


def _mm_nt(name, a, b, *, n0=0, add=None, add_scale=1.0, out_dtype=F32, tm=512, tn=2048, tk=512):
    m, n = a.shape
    if b.ndim == 3:
        k, cc = b.shape[1:]
        tn = _tile(k, tn, LANES)
        tk = _tile(cc, tk, LANES)
        per = cc // tk
        b_spec = pl.BlockSpec((None, tn, tk), lambda i, j, kk: (kk // per, j, kk % per))
    else:
        k = b.shape[0]
        tn = _tile(k, tn, LANES)
        tk = _tile(math.gcd(n, n0) if n0 else n, tk, LANES)
        b_spec = pl.BlockSpec((tn, tk), lambda i, j, kk: (j, n0 // tk + kk))
    tm = _tile(m, tm, SUBLANES_BF16)
    o_spec = pl.BlockSpec((tm, tn), lambda i, j, kk: (i, j))
    return _mm(name, "nt", (m // tm, k // tn, n // tk),
               a, pl.BlockSpec((tm, tk), lambda i, j, kk: (i, kk)), b, b_spec,
               jax.ShapeDtypeStruct((m, k), out_dtype), o_spec, (tm, tn),
               add=add, add_spec=o_spec, add_scale=add_scale)


def _mm_tn(name, a, b, *, a0=0, ka=None, out_dtype=BF16, tm=2048, tn=1024, tk=512, col_shards=False):
    s = a.shape[0]
    ka = a.shape[1] - a0 if ka is None else ka
    n = b.shape[1]
    tm = _tile(math.gcd(ka, a0) if a0 else ka, tm, LANES)
    tk = _tile(s, tk, SUBLANES_BF16)
    ia0 = a0 // tm
    if col_shards:
        cc = n // 4
        tn = _tile(cc, tn, LANES)
        per = cc // tn
        out_sds = jax.ShapeDtypeStruct((4, ka, cc), out_dtype)
        o_spec = pl.BlockSpec((None, tm, tn), lambda i, j, kk: (j // per, i, j % per))
    else:
        tn = _tile(n, tn, LANES)
        out_sds = jax.ShapeDtypeStruct((ka, n), out_dtype)
        o_spec = pl.BlockSpec((tm, tn), lambda i, j, kk: (i, j))
    return _mm(name, "tn", (ka // tm, n // tn, s // tk),
               a, pl.BlockSpec((tk, tm), lambda i, j, kk: (kk, ia0 + i)),
               b, pl.BlockSpec((tk, tn), lambda i, j, kk: (kk, j)), out_sds, o_spec, (tm, tn))


def R(arr, bw=None, c0=0, j=False):
    return ("r", arr, arr.shape[1] if bw is None else bw, c0, j)


def P(arr, bw=None, c0=0, j=False):
    return ("p", arr, arr.shape[1] if bw is None else bw, c0, j)


def _ew(name, fn, rows, ins, outs, accs=(), *, tr=256, nj=1):
    tr = _tile(rows, tr, SUBLANES_BF16)
    ni = rows // tr
    n_in, n_out, n_acc = len(ins), len(outs), len(accs)

    def spec(kind, bw, c0, follows):
        rows_b = tr if kind == "r" else 1
        if kind == "r":
            return pl.BlockSpec((rows_b, bw), (lambda j, i: (i, c0 + j)) if follows else (lambda j, i: (i, c0)))
        return pl.BlockSpec((rows_b, bw), (lambda j, i: (0, c0 + j)) if follows else (lambda j, i: (0, c0)))

    def body(*refs):
        res = fn(*[r[...] for r in refs[:n_in]])
        res = tuple(res) if isinstance(res, (tuple, list)) else (res,)
        for r, v in zip(refs[n_in:n_in + n_out], res[:n_out]):
            r[...] = v.astype(r.dtype)
        if n_acc:
            i = pl.program_id(1)
            for r, v in zip(refs[n_in + n_out:], res[n_out:]):
                @pl.when(i == 0)
                def _(r=r, v=v):
                    r[...] = v

                @pl.when(i > 0)
                def _(r=r, v=v):
                    r[...] += v

    out_shape = [jax.ShapeDtypeStruct((rows, bw * nj), dt) for bw, dt in outs]
    out_shape += [jax.ShapeDtypeStruct((1, bw * nj), F32) for bw in accs]
    out_specs = [pl.BlockSpec((tr, bw), lambda j, i: (i, j)) for bw, _ in outs]
    out_specs += [pl.BlockSpec((1, bw), lambda j, i: (0, j)) for bw in accs]
    res = pl.pallas_call(body, out_shape=out_shape, grid=(nj, ni),
                         in_specs=[spec(k, bw, c0, f) for k, _, bw, c0, f in ins], out_specs=out_specs,
                         compiler_params=_cparams(2), name=name)(*[a for _, a, _, _, _ in ins])
    return res if len(res) > 1 else res[0]


def _gelu(x):
    c = math.sqrt(2.0 / math.pi)
    return 0.5 * x * (1.0 + jnp.tanh(c * (x + 0.044715 * (x * x * x))))


def _sigmoid(x):
    return lax.logistic(x)


def _f_ln(h, g, b):
    mu = jnp.mean(h, axis=-1, keepdims=True)
    xc = h - mu
    var = jnp.mean(xc * xc, axis=-1, keepdims=True)
    return xc * lax.rsqrt(var + LN_EPS) * g + b


def _f_combine(o1, o2, o3, l1, l2, l3):
    m = jnp.maximum(jnp.maximum(l1, l2), l3)
    e1, e2, e3 = jnp.exp(l1 - m), jnp.exp(l2 - m), jnp.exp(l3 - m)
    den = e1 + e2 + e3
    return (e1 * o1 + e2 * o2 + e3 * o3) / den, m + jnp.log(den)


def _f_ssm_out(y_lin, uc, d_skip):
    return _gelu(y_lin + d_skip * uc)


def _f_glu(yc0, t):
    return yc0 * _sigmoid(t)


def _f_merge(g0, g1, g2, pa, pb, pc):
    return _sigmoid(g0) * pa + _sigmoid(g1) * pb + _sigmoid(g2) * pc


def _f_swiglu(gf, up):
    return gf * _sigmoid(gf) * up


def _f_sgu_pre(zu, zv, g, b):
    return _gelu(zu), _f_ln(_gelu(zv), g, b)


def _f_ssm_params(lr, li, ld, lr_rep, li_rep, ld_rep, br_t, bi_t):
    def disc(lr, li, ld):
        dt = jnp.exp(ld)
        mag = jnp.exp(lr * dt)
        th = li * dt
        abr, abi = mag * jnp.cos(th), mag * jnp.sin(th)
        nrm = lr * lr + li * li
        cr = ((abr - 1.0) * lr + abi * li) / nrm
        ci = (abi * lr - (abr - 1.0) * li) / nrm
        return abr, abi, cr, ci

    abr, abi, _, _ = disc(lr, li, ld)
    _, _, cr, ci = disc(lr_rep, li_rep, ld_rep)
    return abr, abi, cr * br_t - ci * bi_t, cr * bi_t + ci * br_t


def _rowsum(x):
    return jnp.sum(x, axis=0, keepdims=True)


def _to_streams(qkv):
    s = qkv.shape[0]
    t = qkv.reshape(s, 3, N_GROUPS_A, HEADS, HEAD_DIM)
    outs = []
    for g, (_, dil) in enumerate(ATT_PATTERNS):
        tg = t[:, :, g].reshape(s // dil, dil, 3, HEADS, HEAD_DIM)
        outs.append(tg.transpose(2, 3, 1, 0, 4).reshape(3, HEADS, s, HEAD_DIM))
    return jnp.stack(outs, 0)


def _tok_to_streams(a):
    s = a.shape[0]
    outs = []
    for _, dil in ATT_PATTERNS:
        t = a.reshape(s // dil, dil, HEADS, HEAD_DIM)
        outs.append(t.transpose(2, 1, 0, 3).reshape(HEADS, s, HEAD_DIM))
    return jnp.stack(outs, 0)


def _streams_to_tok(o):
    s = o.shape[2]
    outs = []
    for g, (_, dil) in enumerate(ATT_PATTERNS):
        t = o[g].reshape(HEADS, dil, s // dil, HEAD_DIM)
        outs.append(t.transpose(2, 1, 0, 3).reshape(s, WIDTH_A))
    return outs


def _t5_bucket(dist):
    max_exact = N_REL_BUCKETS // 2
    d = np.maximum(dist, 1).astype(np.float32)
    scale = (N_REL_BUCKETS - max_exact) / math.log(REL_MAX_DIST / max_exact)
    large = max_exact + (np.log(d / max_exact) * scale).astype(np.int32)
    large = np.minimum(large, N_REL_BUCKETS - 1)
    return np.where(dist < max_exact, dist, large).astype(np.int32)


def _bucket_maps():
    i = np.arange(ATT_BLOCK)[:, None]
    kk = np.arange(2 * ATT_BLOCK)[None, :]
    steps = np.maximum(ATT_BLOCK + i - kk, 0)
    return np.stack([_t5_bucket(steps * dil) for _, dil in ATT_PATTERNS], 0)


def _band_bias(rel_bias):
    q = ATT_BLOCK

    def body(rel_ref, m_ref, o_ref):
        g = pl.program_id(0)
        bm = m_ref[...]
        for h in range(HEADS):
            acc = jnp.zeros((q, 2 * q), F32)
            for bk in range(N_REL_BUCKETS):
                acc = jnp.where(bm == bk, rel_ref[bk, g * HEADS + h], acc)
            o_ref[h] = acc

    return pl.pallas_call(body, out_shape=jax.ShapeDtypeStruct((N_GROUPS_A, HEADS, q, 2 * q), F32), grid=(N_GROUPS_A,),
                          in_specs=[pl.BlockSpec(memory_space=pltpu.SMEM),
                                    pl.BlockSpec((None, q, 2 * q), lambda g: (g, 0, 0))],
                          out_specs=pl.BlockSpec((None, HEADS, q, 2 * q), lambda g: (g, 0, 0, 0)),
                          compiler_params=_cparams(1), name="rel_bias_band")(rel_bias, jnp.asarray(_bucket_maps()))


def _attn_masks(nbs):
    q = ATT_BLOCK
    g, b = pl.program_id(0), pl.program_id(1)
    nb = jnp.where(g == 0, nbs[0], jnp.where(g == 1, nbs[1], nbs[2]))
    shift = jnp.where(lax.rem(b, nb) != 0, 0, q)
    ii = lax.broadcasted_iota(jnp.int32, (q, q), 0)
    kk = lax.broadcasted_iota(jnp.int32, (q, q), 1)
    return kk >= ii + shift, kk <= ii


def _attn_logits(q, kp, kc, bias_h, mask_p, mask_c):
    scale = HEAD_DIM ** -0.5
    sp = lax.dot_general(q, kp, _DIMS["nt"], preferred_element_type=F32) * scale + bias_h[:, :ATT_BLOCK]
    sc = lax.dot_general(q, kc, _DIMS["nt"], preferred_element_type=F32) * scale + bias_h[:, ATT_BLOCK:]
    return jnp.where(mask_p, sp, NEG_INF), jnp.where(mask_c, sc, NEG_INF)


def _attn_specs(s):
    q, h, e = ATT_BLOCK, HEADS, HEAD_DIM
    blk = (None, None, h, q, e)
    prev = lambda b: jnp.maximum(b - 1, 0)
    qkv_specs = [pl.BlockSpec(blk, lambda g, b: (g, 0, 0, b, 0)),
                 pl.BlockSpec(blk, lambda g, b: (g, 1, 0, prev(b), 0)),
                 pl.BlockSpec(blk, lambda g, b: (g, 1, 0, b, 0)),
                 pl.BlockSpec(blk, lambda g, b: (g, 2, 0, prev(b), 0)),
                 pl.BlockSpec(blk, lambda g, b: (g, 2, 0, b, 0))]
    bias_spec = pl.BlockSpec((None, h, q, 2 * q), lambda g, b: (g, 0, 0, 0))
    row_spec = pl.BlockSpec((None, h, q, e), lambda g, b: (g, 0, b, 0))
    return qkv_specs, bias_spec, row_spec


def _attn_fwd(qkv_s, bias):
    s = qkv_s.shape[3]
    nbs = tuple(s // dil // ATT_BLOCK for _, dil in ATT_PATTERNS)
    qkv_specs, bias_spec, row_spec = _attn_specs(s)

    def body(q_ref, kp_ref, kc_ref, vp_ref, vc_ref, b_ref, o_ref, l_ref):
        mask_p, mask_c = _attn_masks(nbs)
        for h in range(HEADS):
            sp, sc = _attn_logits(q_ref[h], kp_ref[h], kc_ref[h], b_ref[h], mask_p, mask_c)
            m = jnp.maximum(jnp.max(sp, axis=1, keepdims=True), jnp.max(sc, axis=1, keepdims=True))
            pp, pc = jnp.exp(sp - m), jnp.exp(sc - m)
            den = jnp.sum(pp, axis=1, keepdims=True) + jnp.sum(pc, axis=1, keepdims=True)
            o = (lax.dot_general(pp.astype(BF16), vp_ref[h], _DIMS["nn"], preferred_element_type=F32)
                 + lax.dot_general(pc.astype(BF16), vc_ref[h], _DIMS["nn"], preferred_element_type=F32))
            o_ref[h] = o / den
            l_ref[h] = jnp.broadcast_to(m + jnp.log(den), (ATT_BLOCK, HEAD_DIM))

    sds = jax.ShapeDtypeStruct((N_GROUPS_A, HEADS, s, HEAD_DIM), F32)
    return pl.pallas_call(body, out_shape=(sds, sds), grid=(N_GROUPS_A, s // ATT_BLOCK),
                          in_specs=qkv_specs + [bias_spec], out_specs=(row_spec, row_spec),
                          compiler_params=_cparams(2), name="attn_fwd")(qkv_s, qkv_s, qkv_s, qkv_s, qkv_s, bias)


def _attn_bwd(qkv_s, bias, do_s, ya_s, lse_s):
    s = qkv_s.shape[3]
    nbs = tuple(s // dil // ATT_BLOCK for _, dil in ATT_PATTERNS)
    qkv_specs, bias_spec, row_spec = _attn_specs(s)
    scale = HEAD_DIM ** -0.5

    def body(q_ref, kp_ref, kc_ref, vp_ref, vc_ref, b_ref, do_ref, ya_ref, l_ref,
             dq_ref, dkc_ref, dkp_ref, dvc_ref, dvp_ref, db_ref):
        mask_p, mask_c = _attn_masks(nbs)

        @pl.when(pl.program_id(1) == 0)
        def _():
            db_ref[...] = jnp.zeros_like(db_ref)

        for h in range(HEADS):
            q, kp, kc, vp, vc = q_ref[h], kp_ref[h], kc_ref[h], vp_ref[h], vc_ref[h]
            sp, sc = _attn_logits(q, kp, kc, b_ref[h], mask_p, mask_c)
            lse = l_ref[h][:, 0:1]
            pp, pc = jnp.exp(sp - lse), jnp.exp(sc - lse)
            do = do_ref[h]
            dsum = jnp.sum(do * ya_ref[h], axis=1, keepdims=True)
            dob = do.astype(BF16)
            dsp = pp * (lax.dot_general(dob, vp, _DIMS["nt"], preferred_element_type=F32) - dsum)
            dsc = pc * (lax.dot_general(dob, vc, _DIMS["nt"], preferred_element_type=F32) - dsum)
            db_ref[h, :, :ATT_BLOCK] += dsp
            db_ref[h, :, ATT_BLOCK:] += dsc
            dspb, dscb = dsp.astype(BF16), dsc.astype(BF16)
            dq_ref[h] = scale * (lax.dot_general(dspb, kp, _DIMS["nn"], preferred_element_type=F32)
                                 + lax.dot_general(dscb, kc, _DIMS["nn"], preferred_element_type=F32))
            dkp_ref[h] = scale * lax.dot_general(dspb, q, _DIMS["tn"], preferred_element_type=F32)
            dkc_ref[h] = scale * lax.dot_general(dscb, q, _DIMS["tn"], preferred_element_type=F32)
            dvp_ref[h] = lax.dot_general(pp.astype(BF16), dob, _DIMS["tn"], preferred_element_type=F32)
            dvc_ref[h] = lax.dot_general(pc.astype(BF16), dob, _DIMS["tn"], preferred_element_type=F32)

    sds = jax.ShapeDtypeStruct((N_GROUPS_A, HEADS, s, HEAD_DIM), F32)
    return pl.pallas_call(body, out_shape=(sds,) * 5 + (jax.ShapeDtypeStruct(bias.shape, F32),),
                          grid=(N_GROUPS_A, s // ATT_BLOCK),
                          in_specs=qkv_specs + [bias_spec, row_spec, row_spec, row_spec],
                          out_specs=(row_spec,) * 5 + (bias_spec,), compiler_params=_cparams(2), name="attn_bwd")(
        qkv_s, qkv_s, qkv_s, qkv_s, qkv_s, bias, do_s, ya_s, lse_s)


def _shift_add(cur, prv):
    s = cur.shape[2]
    nb = s // ATT_BLOCK
    blk = (None, HEADS, ATT_BLOCK, HEAD_DIM)

    def body(c_ref, p_ref, o_ref):
        live = (pl.program_id(1) < nb - 1).astype(F32)
        o_ref[...] = c_ref[...] + live * p_ref[...]

    return pl.pallas_call(body, out_shape=jax.ShapeDtypeStruct(cur.shape, F32), grid=(N_GROUPS_A, nb),
                          in_specs=[pl.BlockSpec(blk, lambda g, b: (g, 0, b, 0)),
                                    pl.BlockSpec(blk, lambda g, b: (g, 0, jnp.minimum(b + 1, nb - 1), 0))],
                          out_specs=pl.BlockSpec(blk, lambda g, b: (g, 0, b, 0)),
                          compiler_params=_cparams(2), name="attn_shift_add")(cur, prv)


def _bias_to_buckets(dbias):
    bmap = jnp.asarray(_bucket_maps())
    q = ATT_BLOCK

    def body(db_ref, m_ref, o_ref):
        lane = lax.broadcasted_iota(jnp.int32, (HEADS, LANES), 1)
        row = lax.broadcasted_iota(jnp.int32, (HEADS, LANES), 0)
        acc = jnp.zeros((HEADS, LANES), F32)
        bm = m_ref[...]
        for h in range(HEADS):
            dbh = db_ref[h]
            for bk in range(N_REL_BUCKETS):
                sv = jnp.sum(jnp.sum(jnp.where(bm == bk, dbh, 0.0), axis=1, keepdims=True), axis=0, keepdims=True)
                acc = acc + jnp.where((lane == bk) & (row == h), sv, 0.0)
        o_ref[...] = acc

    out = pl.pallas_call(body, out_shape=jax.ShapeDtypeStruct((N_GROUPS_A, HEADS, LANES), F32), grid=(N_GROUPS_A,),
                         in_specs=[pl.BlockSpec((None, HEADS, q, 2 * q), lambda g: (g, 0, 0, 0)),
                                   pl.BlockSpec((None, q, 2 * q), lambda g: (g, 0, 0))],
                         out_specs=pl.BlockSpec((None, HEADS, LANES), lambda g: (g, 0, 0)),
                         compiler_params=_cparams(1), name="rel_bias_grad")(dbias, bmap)
    return out[:, :, :N_REL_BUCKETS].reshape(N_GROUPS_A * HEADS, N_REL_BUCKETS).T


def _sgu_specs():
    c, w = CHUNK, WIDTH_B
    return [pl.BlockSpec((c, w), lambda i: (i, 0)), pl.BlockSpec((c, w), lambda i: (i, 1)),
            pl.BlockSpec((1, w), lambda i: (0, 0)), pl.BlockSpec((1, w), lambda i: (0, 0)),
            pl.BlockSpec((N_GROUPS_B, c, c), lambda i: (0, 0, 0)), pl.BlockSpec((N_GROUPS_B, c, 1), lambda i: (0, 0, 0))]


def _sgu_fwd(rest, ln_g, ln_b, w_tril, b_col):
    s = rest.shape[0]

    def body(zu_ref, zv_ref, g_ref, b_ref, w_ref, bs_ref, y_ref):
        u, vn = _f_sgu_pre(zu_ref[...], zv_ref[...], g_ref[...], b_ref[...])
        for gi in range(N_GROUPS_B):
            sl = slice(gi * GROUP_B, (gi + 1) * GROUP_B)
            mixed = lax.dot_general(w_ref[gi], vn[:, sl].astype(BF16), _DIMS["nn"], preferred_element_type=F32)
            y_ref[:, sl] = u[:, sl] * (mixed + bs_ref[gi])

    return pl.pallas_call(body, out_shape=jax.ShapeDtypeStruct((s, WIDTH_B), F32), grid=(s // CHUNK,),
                          in_specs=_sgu_specs(), out_specs=pl.BlockSpec((CHUNK, WIDTH_B), lambda i: (i, 0)),
                          compiler_params=_cparams(1), name="sgu_fwd")(rest, rest, ln_g, ln_b, w_tril, b_col)


def _sgu_bwd(rest, ln_g, ln_b, w_tril, b_col, dyb):
    s = rest.shape[0]
    c, w, ng = CHUNK, WIDTH_B, N_GROUPS_B

    def body(zu_ref, zv_ref, g_ref, b_ref, w_ref, bs_ref, dy_ref, dzu_ref, dzv_ref, dw_ref, dbs_ref, dg_ref, db_ref):
        (u, vn), vjp = jax.vjp(_f_sgu_pre, zu_ref[...], zv_ref[...], g_ref[...], b_ref[...])
        first = pl.program_id(0) == 0
        du, dvn = [], []
        for gi in range(ng):
            sl = slice(gi * GROUP_B, (gi + 1) * GROUP_B)
            vg = vn[:, sl].astype(BF16)
            mixed = lax.dot_general(w_ref[gi], vg, _DIMS["nn"], preferred_element_type=F32) + bs_ref[gi]
            dy = dy_ref[:, sl]
            dmix = dy * u[:, sl]
            du.append(dy * mixed)
            dmb = dmix.astype(BF16)
            dvn.append(lax.dot_general(w_ref[gi], dmb, _DIMS["tn"], preferred_element_type=F32))
            dwg = lax.dot_general(dmb, vg, _DIMS["nt"], preferred_element_type=F32)
            dbg = jnp.sum(dmix, axis=1, keepdims=True)

            @pl.when(first)
            def _(gi=gi, dwg=dwg, dbg=dbg):
                dw_ref[gi] = dwg
                dbs_ref[gi] = dbg

            @pl.when(jnp.logical_not(first))
            def _(gi=gi, dwg=dwg, dbg=dbg):
                dw_ref[gi] += dwg
                dbs_ref[gi] += dbg

        dzu, dzv, dg, db = vjp((jnp.concatenate(du, axis=1), jnp.concatenate(dvn, axis=1)))
        dzu_ref[...] = dzu.astype(dzu_ref.dtype)
        dzv_ref[...] = dzv.astype(dzv_ref.dtype)

        @pl.when(first)
        def _():
            dg_ref[...] = dg
            db_ref[...] = db

        @pl.when(jnp.logical_not(first))
        def _():
            dg_ref[...] += dg
            db_ref[...] += db

    row = pl.BlockSpec((c, w), lambda i: (i, 0))
    par = pl.BlockSpec((1, w), lambda i: (0, 0))
    return pl.pallas_call(
        body, grid=(s // c,),
        out_shape=(jax.ShapeDtypeStruct((s, w), BF16), jax.ShapeDtypeStruct((s, w), BF16),
                   jax.ShapeDtypeStruct((ng, c, c), F32), jax.ShapeDtypeStruct((ng, c, 1), F32),
                   jax.ShapeDtypeStruct((1, w), F32), jax.ShapeDtypeStruct((1, w), F32)),
        in_specs=_sgu_specs() + [row],
        out_specs=(row, row, pl.BlockSpec((ng, c, c), lambda i: (0, 0, 0)), pl.BlockSpec((ng, c, 1), lambda i: (0, 0, 0)),
                   par, par),
        compiler_params=_cparams(1), name="sgu_bwd")(rest, rest, ln_g, ln_b, w_tril, b_col, dyb)


SCAN_T = 128


def _to_time_major(dst, src_ref, lead):
    for r in range(SSM_CH // LANES):
        dst[:, r, :] = src_ref[lead, :, r * LANES:(r + 1) * LANES]


def _from_time_major(dst_ref, lead, src):
    for r in range(SSM_CH // LANES):
        dst_ref[lead, :, r * LANES:(r + 1) * LANES] = src[:, r, :]


def _scan_fwd(bu, a3):
    s = bu.shape[1]
    t_blk = _tile(s, SCAN_T, 8)
    rows = SSM_CH // LANES

    def body(bu_ref, a_ref, x_ref, b3r, b3i, x3r, x3i, carry):
        @pl.when(pl.program_id(0) == 0)
        def _():
            carry[...] = jnp.zeros_like(carry)

        _to_time_major(b3r, bu_ref, 0)
        _to_time_major(b3i, bu_ref, 1)
        ar, ai = a_ref[0], a_ref[1]

        def step(t, c):
            xr, xi = c
            nr = ar * xr - ai * xi + b3r[t]
            ni = ar * xi + ai * xr + b3i[t]
            x3r[t] = nr
            x3i[t] = ni
            return nr, ni

        xr, xi = lax.fori_loop(0, t_blk, step, (carry[0], carry[1]), unroll=8)
        carry[0] = xr
        carry[1] = xi
        _from_time_major(x_ref, 0, x3r)
        _from_time_major(x_ref, 1, x3i)

    blk = pl.BlockSpec((2, t_blk, SSM_CH), lambda i: (0, i, 0))
    tm = pltpu.VMEM((t_blk, rows, LANES), F32)
    return pl.pallas_call(body, out_shape=jax.ShapeDtypeStruct(bu.shape, F32), grid=(s // t_blk,),
                          in_specs=[blk, pl.BlockSpec((2, rows, LANES), lambda i: (0, 0, 0))], out_specs=blk,
                          scratch_shapes=[tm, tm, tm, tm, pltpu.VMEM((2, rows, LANES), F32)],
                          compiler_params=_cparams(1), name="ssm_scan_fwd")(bu, a3)


def _scan_bwd(dx, x, a3):
    s = dx.shape[1]
    t_blk = _tile(s, SCAN_T, 8)
    nb = s // t_blk
    rows = SSM_CH // LANES

    def body(dx_ref, x_ref, a_ref, g_ref, da_ref, d3r, d3i, x3r, x3i, g3r, g3i, carry):
        first = pl.program_id(0) == 0

        @pl.when(first)
        def _():
            carry[...] = jnp.zeros_like(carry)

        _to_time_major(d3r, dx_ref, 0)
        _to_time_major(d3i, dx_ref, 1)
        _to_time_major(x3r, x_ref, 0)
        _to_time_major(x3i, x_ref, 1)
        ar, ai = a_ref[0], a_ref[1]

        def step(k, c):
            t = t_blk - 1 - k
            gr, gi, dar, dai = c
            xr, xi = x3r[t], x3i[t]
            dar = dar + gr * xr + gi * xi
            dai = dai + gi * xr - gr * xi
            ngr = d3r[t] + ar * gr + ai * gi
            ngi = d3i[t] + ar * gi - ai * gr
            g3r[t] = ngr
            g3i[t] = ngi
            return ngr, ngi, dar, dai

        zero = jnp.zeros((rows, LANES), F32)
        gr, gi, dar, dai = lax.fori_loop(0, t_blk, step, (carry[0], carry[1], zero, zero), unroll=8)
        carry[0] = gr
        carry[1] = gi

        @pl.when(first)
        def _():
            da_ref[0] = dar
            da_ref[1] = dai

        @pl.when(jnp.logical_not(first))
        def _():
            da_ref[0] += dar
            da_ref[1] += dai

        _from_time_major(g_ref, 0, g3r)
        _from_time_major(g_ref, 1, g3i)

    blk = pl.BlockSpec((2, t_blk, SSM_CH), lambda i: (0, nb - 1 - i, 0))
    par = pl.BlockSpec((2, rows, LANES), lambda i: (0, 0, 0))
    tm = pltpu.VMEM((t_blk, rows, LANES), F32)
    return pl.pallas_call(body, out_shape=(jax.ShapeDtypeStruct(dx.shape, F32), jax.ShapeDtypeStruct((2, rows, LANES), F32)),
                          grid=(nb,), in_specs=[blk, blk, par], out_specs=(blk, par),
                          scratch_shapes=[tm] * 6 + [pltpu.VMEM((2, rows, LANES), F32)],
                          compiler_params=_cparams(1), name="ssm_scan_bwd")(dx, x, a3)


SSM_TILES = WIDTH_C // LANES
SSM_TILE_W = SSM_CH // SSM_TILES


def _block_diag(m):
    gpt = N_GROUPS_C // SSM_TILES
    t = m.reshape(SSM_TILES, gpt, SSM_GROUP, SSM_STATE)
    eye = jnp.eye(gpt, dtype=m.dtype)
    return (t[:, :, :, None, :] * eye[None, :, None, :, None]).reshape(SSM_TILES, LANES, SSM_TILE_W)


def _block_diag_extract(t):
    gpt = N_GROUPS_C // SSM_TILES
    t = t.reshape(SSM_TILES, gpt, SSM_GROUP, gpt, SSM_STATE)
    eye = jnp.eye(gpt, dtype=t.dtype)
    return jnp.sum(t * eye[None, :, None, :, None], axis=3).reshape(WIDTH_C, SSM_STATE)


def _ssm_in(uc_src, uc_col0, bd):
    s = uc_src.shape[0]
    tm = _tile(s, 512, 8)
    j0 = uc_col0 // LANES
    return _mm("ssm_in", "nn", (s // tm, 2 * SSM_TILES, 1),
               uc_src, pl.BlockSpec((tm, LANES), lambda i, j, k: (i, j0 + j % SSM_TILES)),
               bd, pl.BlockSpec((None, None, LANES, SSM_TILE_W), lambda i, j, k: (j // SSM_TILES, j % SSM_TILES, 0, 0)),
               jax.ShapeDtypeStruct((2, s, SSM_CH), F32),
               pl.BlockSpec((None, tm, SSM_TILE_W), lambda i, j, k: (j // SSM_TILES, i, j % SSM_TILES)), None, exact=True)


def _ssm_in_dgrad(g, bd):
    s = g.shape[1]
    tm = _tile(s, 512, 8)
    return _mm("ssm_in_dgrad", "nt", (s // tm, SSM_TILES, 2),
               g, pl.BlockSpec((None, tm, SSM_TILE_W), lambda i, j, k: (k, i, j)),
               bd, pl.BlockSpec((None, None, LANES, SSM_TILE_W), lambda i, j, k: (k, j, 0, 0)),
               jax.ShapeDtypeStruct((s, WIDTH_C), F32), pl.BlockSpec((tm, LANES), lambda i, j, k: (i, j)),
               (tm, LANES), exact=True)


def _ssm_in_wgrad(uc_src, uc_col0, g):
    s = g.shape[1]
    tk = _tile(s, 512, 8)
    j0 = uc_col0 // LANES
    return _mm("ssm_in_wgrad", "tn", (2, SSM_TILES, s // tk),
               uc_src, pl.BlockSpec((tk, LANES), lambda i, j, k: (k, j0 + j)),
               g, pl.BlockSpec((None, tk, SSM_TILE_W), lambda i, j, k: (i, k, j)),
               jax.ShapeDtypeStruct((2, SSM_TILES, LANES, SSM_TILE_W), F32),
               pl.BlockSpec((None, None, LANES, SSM_TILE_W), lambda i, j, k: (i, j, 0, 0)), (LANES, SSM_TILE_W), exact=True)


def _ssm_out(x, cd):
    s = x.shape[1]
    tm = _tile(s, 512, 8)
    return _mm("ssm_out", "nn", (s // tm, SSM_TILES, 2),
               x, pl.BlockSpec((None, tm, SSM_TILE_W), lambda i, j, k: (k, i, j)),
               cd, pl.BlockSpec((None, None, SSM_TILE_W, LANES), lambda i, j, k: (k, j, 0, 0)),
               jax.ShapeDtypeStruct((s, WIDTH_C), F32), pl.BlockSpec((tm, LANES), lambda i, j, k: (i, j)),
               (tm, LANES), exact=True)


def _ssm_out_dgrad(dy, cd):
    s = dy.shape[0]
    tm = _tile(s, 512, 8)
    return _mm("ssm_out_dgrad", "nt", (s // tm, 2 * SSM_TILES, 1),
               dy, pl.BlockSpec((tm, LANES), lambda i, j, k: (i, j % SSM_TILES)),
               cd, pl.BlockSpec((None, None, SSM_TILE_W, LANES), lambda i, j, k: (j // SSM_TILES, j % SSM_TILES, 0, 0)),
               jax.ShapeDtypeStruct((2, s, SSM_CH), F32),
               pl.BlockSpec((None, tm, SSM_TILE_W), lambda i, j, k: (j // SSM_TILES, i, j % SSM_TILES)), None, exact=True)


def _ssm_out_wgrad(x, dy):
    s = dy.shape[0]
    tk = _tile(s, 512, 8)
    return _mm("ssm_out_wgrad", "tn", (2, SSM_TILES, s // tk),
               x, pl.BlockSpec((None, tk, SSM_TILE_W), lambda i, j, k: (i, k, j)),
               dy, pl.BlockSpec((tk, LANES), lambda i, j, k: (k, j)),
               jax.ShapeDtypeStruct((2, SSM_TILES, SSM_TILE_W, LANES), F32),
               pl.BlockSpec((None, None, SSM_TILE_W, LANES), lambda i, j, k: (i, j, 0, 0)), (SSM_TILE_W, LANES), exact=True)


def _ssm_param_inputs(w, l):
    rep = lambda a: jnp.repeat(a, SSM_GROUP, axis=0)
    lr, li, ld = w["lam_re"][l], w["lam_im"][l], w["log_dt"][l][:, None]
    br_t = jnp.transpose(w["b_re"][l], (0, 2, 1)).reshape(WIDTH_C, SSM_STATE)
    bi_t = jnp.transpose(w["b_im"][l], (0, 2, 1)).reshape(WIDTH_C, SSM_STATE)
    return lr, li, ld, rep(lr), rep(li), rep(ld), br_t, bi_t


def _ssm_params_fwd(pin):
    def body(*refs):
        res = _f_ssm_params(*[r[...] for r in refs[:8]])
        for r, v in zip(refs[8:], res):
            r[...] = v

    g, p = N_GROUPS_C, SSM_STATE
    return pl.pallas_call(body, out_shape=(jax.ShapeDtypeStruct((g, p), F32),) * 2
                          + (jax.ShapeDtypeStruct((WIDTH_C, p), F32),) * 2, name="ssm_params_fwd")(*pin)


def _ssm_params_bwd(pin, d_abr, d_abi, d_bbr, d_bbi):
    g, p = N_GROUPS_C, SSM_STATE
    group_sum = jnp.asarray(np.kron(np.eye(g, dtype=np.float32), np.ones((1, SSM_GROUP), np.float32)))

    def body(*refs):
        ins = [r[...] for r in refs[:8]]
        cts = tuple(r[...] for r in refs[8:12])
        gs = refs[12][...]
        d_lr_ref, d_li_ref, d_ld_ref, d_br_ref, d_bi_ref = refs[13:]
        _, vjp = jax.vjp(_f_ssm_params, *ins)
        d = vjp(cts)
        fold = lambda v: lax.dot_general(gs, v, _DIMS["nn"], preferred_element_type=F32, precision=lax.Precision.HIGHEST)
        d_lr_ref[...] = d[0] + fold(d[3])
        d_li_ref[...] = d[1] + fold(d[4])
        d_ld_ref[...] = d[2] + fold(jnp.broadcast_to(d[5], (WIDTH_C, p)))[:, 0:1]
        d_br_ref[...] = d[6]
        d_bi_ref[...] = d[7]

    return pl.pallas_call(body, out_shape=(jax.ShapeDtypeStruct((g, p), F32), jax.ShapeDtypeStruct((g, p), F32),
                                           jax.ShapeDtypeStruct((g, 1), F32), jax.ShapeDtypeStruct((WIDTH_C, p), F32),
                                           jax.ShapeDtypeStruct((WIDTH_C, p), F32)), name="ssm_params_bwd")(
        *pin, d_abr, d_abi, d_bbr, d_bbi, group_sum)


def _c_block_diag(w, l):
    def one(c):
        return jnp.transpose(_block_diag(c.reshape(WIDTH_C, SSM_STATE)), (0, 2, 1))
    return jnp.stack([one(w["c_re"][l]), -one(w["c_im"][l])], 0)


ANY = pl.BlockSpec(memory_space=pl.ANY)


def _ag8(name, xb):
    def body(x_ref, out_ref, send_sems, recv_sems, local_sem):
        x, y, c = lax.axis_index("x"), lax.axis_index("y"), lax.axis_index("c")
        me, sibling = (x, y, c), (x, y, 1 - c)
        chips = [(1 - x, y), (x, 1 - y), (1 - x, 1 - y)]

        def rows(px, py, pc):
            return out_ref.at[4 * px + 2 * py + pc]

        def copy(k, block, to, src=None):
            return pltpu.make_async_remote_copy(src_ref=rows(*block) if src is None else src, dst_ref=rows(*block),
                                                send_sem=send_sems.at[k], recv_sem=recv_sems.at[k],
                                                device_id=to, device_id_type=MESH)

        mine = pltpu.make_async_copy(x_ref, rows(*me), local_sem)
        mine.start()
        first = [copy(0, me, sibling, src=x_ref)]
        first += [copy(1 + j, me, (*chip, c), src=x_ref) for j, chip in enumerate(chips)]
        for cp in first:
            cp.start()
        passed = [copy(4 + j, (*chip, c), sibling) for j, chip in enumerate(chips)]
        for j, chip in enumerate(chips):
            copy(1 + j, (*chip, c), me).wait_recv()
            passed[j].start()
        copy(0, sibling, me).wait_recv()
        for j, chip in enumerate(chips):
            copy(4 + j, (*chip, 1 - c), me).wait_recv()
        for cp in first + passed:
            cp.wait_send()
        mine.wait()

    return pl.pallas_call(body, out_shape=jax.ShapeDtypeStruct((8,) + xb.shape, xb.dtype), in_specs=[ANY], out_specs=ANY,
                          scratch_shapes=[pltpu.SemaphoreType.DMA((7,)), pltpu.SemaphoreType.DMA((7,)),
                                          pltpu.SemaphoreType.DMA(())], name=name)(xb)


def _mesh_place():
    x, y, c = lax.axis_index("x"), lax.axis_index("y"), lax.axis_index("c")
    return x, y, c, [(1 - x, y), (x, 1 - y), (1 - x, 1 - y)]


ICI_CHUNK_BYTES = 2 << 20
D2D_CHUNK_BYTES = 1 << 20


def _split_rows(rows, row_bytes, chunk_bytes):
    k = 1
    for cand in range(1, max(1, (rows * row_bytes) // chunk_bytes) + 1):
        if rows % cand == 0 and (rows // cand) % SUBLANES_BF16 == 0:
            k = cand
    return [(j * (rows // k), rows // k) for j in range(k)]


def _gather_weights(wb, l):
    nw = len(wb)
    halves = [a.shape[1] // 2 for a in wb]
    pieces = [_split_rows(h, a.shape[2] * a.dtype.itemsize, ICI_CHUNK_BYTES) for a, h in zip(wb, halves)]

    def body(*refs):
        ins, outs = refs[:nw], refs[nw:2 * nw]
        send_sems, recv_sems, local_sems = refs[2 * nw:]
        x, y, c, chips = _mesh_place()
        me = 2 * x + y

        def rows(ref, i, hc, r0=0, n=None):
            return ref.at[pl.ds(hc * halves[i] + r0, halves[i] if n is None else n)]

        def copy(i, k, src, dst, to):
            return pltpu.make_async_remote_copy(src_ref=src, dst_ref=dst, send_sem=send_sems.at[6 * i + k],
                                                recv_sem=recv_sems.at[6 * i + k], device_id=to, device_id_type=MESH)

        for i in range(nw):
            for hc in range(2):
                for r0, n in pieces[i]:
                    pltpu.make_async_copy(rows(ins[i].at[l], i, hc, r0, n), rows(outs[i].at[me], i, hc, r0, n),
                                          local_sems.at[i]).start()
        for i in range(nw):
            for k, (px, py) in enumerate(chips):
                for r0, n in pieces[i]:
                    copy(i, k, rows(ins[i].at[l], i, c, r0, n), rows(outs[i].at[me], i, c, r0, n), (px, py, c)).start()
        for k, (px, py) in enumerate(chips):
            for i in range(nw):
                src = outs[i].at[2 * px + py]
                copy(i, k, rows(src, i, c), rows(src, i, c), (px, py, c)).wait_recv()
                for r0, n in pieces[i]:
                    copy(i, 3 + k, rows(src, i, c, r0, n), rows(src, i, c, r0, n), (x, y, 1 - c)).start()
        for k, (px, py) in enumerate(chips):
            for i in range(nw):
                other = rows(outs[i].at[2 * px + py], i, 1 - c)
                copy(i, 3 + k, other, other, (x, y, 1 - c)).wait_recv()
        for i in range(nw):
            whole = rows(outs[i].at[me], i, c)
            for k in range(6):
                copy(i, k, whole, whole, (x, y, 1 - c)).wait_send()
            pltpu.make_async_copy(ins[i].at[l], outs[i].at[me], local_sems.at[i]).wait()

    return pl.pallas_call(body, out_shape=[jax.ShapeDtypeStruct((4,) + a.shape[1:], a.dtype) for a in wb],
                          in_specs=[ANY] * nw, out_specs=[ANY] * nw,
                          scratch_shapes=[pltpu.SemaphoreType.DMA((6 * nw,)), pltpu.SemaphoreType.DMA((6 * nw,)),
                                          pltpu.SemaphoreType.DMA((nw,))], name="gather_weights")(*wb)


def _grad_core_swap(g4):
    nw = len(g4)
    halves = [a.shape[1] // 2 for a in g4]
    pieces = [_split_rows(h, a.shape[2] * a.dtype.itemsize, D2D_CHUNK_BYTES) for a, h in zip(g4, halves)]

    def body(*refs):
        ins, theirs = refs[:nw], refs[nw:2 * nw]
        send_sems, recv_sems = refs[2 * nw:]
        x, y, c, _ = _mesh_place()

        def copy(i, src, dst):
            return pltpu.make_async_remote_copy(src_ref=src, dst_ref=dst, send_sem=send_sems.at[i], recv_sem=recv_sems.at[i],
                                                device_id=(x, y, 1 - c), device_id_type=MESH)

        for i in range(nw):
            for j in range(4):
                for r0, n in pieces[i]:
                    copy(i, ins[i].at[j, pl.ds((1 - c) * halves[i] + r0, n)], theirs[i].at[j, pl.ds(r0, n)]).start()
        for i in range(nw):
            copy(i, ins[i].at[:, pl.ds((1 - c) * halves[i], halves[i])], theirs[i]).wait()

    sds = [jax.ShapeDtypeStruct((4, h) + a.shape[2:], a.dtype) for a, h in zip(g4, halves)]
    return pl.pallas_call(body, out_shape=sds, in_specs=[ANY] * nw, out_specs=[ANY] * nw,
                          scratch_shapes=[pltpu.SemaphoreType.DMA((nw,)), pltpu.SemaphoreType.DMA((nw,))],
                          name="grad_core_swap")(*g4)


def _grad_chip_exchange(t4):
    nw = len(t4)
    pieces = [_split_rows(a.shape[1], a.shape[2] * a.dtype.itemsize, ICI_CHUNK_BYTES) for a in t4]

    def body(*refs):
        ins, outs = refs[:nw], refs[nw:2 * nw]
        send_sems, recv_sems, local_sems = refs[2 * nw:]
        x, y, c, chips = _mesh_place()
        me = 2 * x + y

        def copy(i, k, src, dst, px, py):
            return pltpu.make_async_remote_copy(src_ref=src, dst_ref=dst, send_sem=send_sems.at[3 * i + k],
                                                recv_sem=recv_sems.at[3 * i + k], device_id=(px, py, c),
                                                device_id_type=MESH)

        for i in range(nw):
            for r0, n in pieces[i]:
                for k, (px, py) in enumerate(chips):
                    copy(i, k, ins[i].at[2 * px + py, pl.ds(r0, n)], outs[i].at[me, pl.ds(r0, n)], px, py).start()
                pltpu.make_async_copy(ins[i].at[me, pl.ds(r0, n)], outs[i].at[me, pl.ds(r0, n)], local_sems.at[i]).start()
        for i in range(nw):
            for k, (px, py) in enumerate(chips):
                copy(i, k, ins[i].at[me], outs[i].at[2 * px + py], px, py).wait_recv()
        for i in range(nw):
            for k, (px, py) in enumerate(chips):
                copy(i, k, ins[i].at[me], outs[i].at[me], px, py).wait_send()
            pltpu.make_async_copy(ins[i].at[me], outs[i].at[me], local_sems.at[i]).wait()

    return pl.pallas_call(body, out_shape=[jax.ShapeDtypeStruct(a.shape, a.dtype) for a in t4],
                          in_specs=[ANY] * nw, out_specs=[ANY] * nw,
                          scratch_shapes=[pltpu.SemaphoreType.DMA((3 * nw,)), pltpu.SemaphoreType.DMA((3 * nw,)),
                                          pltpu.SemaphoreType.DMA((nw,))], name="grad_chip_exchange")(*t4)


def _grad_half_swap(full):
    nw = len(full)
    halves = [a.shape[0] // 2 for a in full]
    pieces = [_split_rows(h, a.shape[1] * a.dtype.itemsize, D2D_CHUNK_BYTES) for a, h in zip(full, halves)]

    def body(*refs):
        bufs = refs[nw:2 * nw]
        send_sems, recv_sems = refs[2 * nw:]
        x, y, c, _ = _mesh_place()

        def copy(i, hc, r0, n):
            view = bufs[i].at[pl.ds(hc * halves[i] + r0, n)]
            return pltpu.make_async_remote_copy(src_ref=view, dst_ref=view, send_sem=send_sems.at[i], recv_sem=recv_sems.at[i],
                                                device_id=(x, y, 1 - c), device_id_type=MESH)

        for i in range(nw):
            for r0, n in pieces[i]:
                copy(i, c, r0, n).start()
        for i in range(nw):
            copy(i, c, 0, halves[i]).wait_send()
            copy(i, 1 - c, 0, halves[i]).wait_recv()

    return pl.pallas_call(body, out_shape=[jax.ShapeDtypeStruct(a.shape, a.dtype) for a in full],
                          in_specs=[ANY] * nw, out_specs=[ANY] * nw, input_output_aliases={i: i for i in range(nw)},
                          scratch_shapes=[pltpu.SemaphoreType.DMA((nw,)), pltpu.SemaphoreType.DMA((nw,))],
                          name="grad_half_swap")(*full)


def _core_add(g, theirs, core):
    _, rh, cc = theirs.shape
    tr = _tile(rh, max(SUBLANES_BF16, (1 << 19) // cc), SUBLANES_BF16)
    nt = rh // tr

    def body(c_ref, a_ref, b_ref, o_ref):
        o_ref[...] = (a_ref[...].astype(F32) + b_ref[...].astype(F32)).astype(o_ref.dtype)

    blk = (None, tr, cc)
    return pl.pallas_call(
        body, out_shape=jax.ShapeDtypeStruct(theirs.shape, theirs.dtype),
        grid_spec=pltpu.PrefetchScalarGridSpec(
            num_scalar_prefetch=1, grid=(4, nt),
            in_specs=[pl.BlockSpec(blk, lambda j, t, c_ref: (j, c_ref[0] * nt + t, 0)),
                      pl.BlockSpec(blk, lambda j, t, c_ref: (j, t, 0))],
            out_specs=pl.BlockSpec(blk, lambda j, t, c_ref: (j, t, 0))),
        compiler_params=_cparams(2), name="grad_core_add")(core, g, theirs)


def _sum_to_half(xb, core):
    n, rh, cc = xb.shape
    tr = _tile(rh, max(SUBLANES_BF16, (1 << 18) // cc), SUBLANES_BF16)
    nt = rh // tr

    def body(c_ref, x_ref, o_ref):
        acc = x_ref[0].astype(F32)
        for k in range(1, n):
            acc = acc + x_ref[k].astype(F32)
        o_ref[...] = acc

    return pl.pallas_call(
        body, out_shape=jax.ShapeDtypeStruct((2 * rh, cc), F32),
        grid_spec=pltpu.PrefetchScalarGridSpec(
            num_scalar_prefetch=1, grid=(nt,),
            in_specs=[pl.BlockSpec((n, tr, cc), lambda t, c_ref: (0, t, 0))],
            out_specs=pl.BlockSpec((tr, cc), lambda t, c_ref: (c_ref[0] * nt + t, 0))),
        compiler_params=_cparams(1), name="grad_chip_sum")(core, xb)


def _sum_lead(name, xb, out_dtype=F32):
    n, rows, w = xb.shape
    tr = _tile(rows, max(SUBLANES_BF16, (1 << 18) // w), SUBLANES_BF16)

    def body(x_ref, o_ref):
        acc = x_ref[0].astype(F32)
        for k in range(1, n):
            acc = acc + x_ref[k].astype(F32)
        o_ref[...] = acc.astype(o_ref.dtype)

    return pl.pallas_call(body, out_shape=jax.ShapeDtypeStruct((rows, w), out_dtype), grid=(rows // tr,),
                          in_specs=[pl.BlockSpec((n, tr, w), lambda i: (0, i, 0))],
                          out_specs=pl.BlockSpec((tr, w), lambda i: (i, 0)), compiler_params=_cparams(1), name=name)(xb)


def _pad_to(v, mult):
    n = v.shape[-1]
    pad = (-n) % mult
    return v if pad == 0 else jnp.pad(v, [(0, 0)] * (v.ndim - 1) + [(0, pad)])


RELAYOUT_ROWS = 256


def _cols_from_shards(a):
    _, k, cc = a.shape
    tr = _tile(k, RELAYOUT_ROWS, SUBLANES_BF16)

    def body(i_ref, o_ref):
        for j in range(4):
            o_ref[:, j * cc:(j + 1) * cc] = i_ref[j]

    return pl.pallas_call(body, out_shape=jax.ShapeDtypeStruct((k, 4 * cc), a.dtype), grid=(k // tr,),
                          in_specs=[pl.BlockSpec((4, tr, cc), lambda i: (0, i, 0))],
                          out_specs=pl.BlockSpec((tr, 4 * cc), lambda i: (i, 0)),
                          compiler_params=_cparams(1), name="cols_from_shards")(a)


def _cols_to_shards(a):
    k, n = a.shape
    cc = n // 4
    tr = _tile(k, RELAYOUT_ROWS, SUBLANES_BF16)

    def body(i_ref, o_ref):
        for j in range(4):
            o_ref[j] = i_ref[:, j * cc:(j + 1) * cc]

    return pl.pallas_call(body, out_shape=jax.ShapeDtypeStruct((4, k, cc), a.dtype), grid=(k // tr,),
                          in_specs=[pl.BlockSpec((tr, n), lambda i: (i, 0))],
                          out_specs=pl.BlockSpec((4, tr, cc), lambda i: (0, i, 0)),
                          compiler_params=_cparams(1), name="cols_to_shards")(a)


def _gather_layer_weights(wb, l):
    got = _gather_weights([wb[n] for n in BIG_NAMES], l)
    out = {}
    for (n, kind), a in zip(BIG, got):
        if kind == "row":
            out[n] = a.reshape(4 * a.shape[1], a.shape[2])
        else:
            out[n] = _cols_from_shards(a) if n == "w_in" else a
    return out


def _reduce_layer_grads(g):
    g4 = []
    for n, kind in BIG:
        a = g[n]
        if kind == "row":
            a = a.reshape(4, a.shape[0] // 4, a.shape[1])
        elif a.ndim == 2:
            a = _cols_to_shards(a)
        g4.append(a)
    core = lax.axis_index("c").astype(jnp.int32).reshape(1)
    theirs = _grad_core_swap(g4)
    arrived = _grad_chip_exchange([_core_add(a, b, core) for a, b in zip(g4, theirs)])
    full = _grad_half_swap([_sum_to_half(a, core) for a in arrived])
    return dict(zip(BIG_NAMES, full))


def _adam_fn(w, g, m, v):
    m = ADAM_B1 * m + (1.0 - ADAM_B1) * g
    v = ADAM_B2 * v + (1.0 - ADAM_B2) * (g * g)
    m_hat = m / (1.0 - ADAM_B1 ** ADAM_STEP)
    v_hat = v / (1.0 - ADAM_B2 ** ADAM_STEP)
    return -ADAM_LR * (m_hat / (jnp.sqrt(v_hat) + ADAM_EPS) + ADAM_WD * w), m, v


def _adamw(name, w, g, m, v):
    shape = w.shape
    cols = shape[-1]
    f = lambda a: a.reshape(-1, cols)
    rows = f(w).shape[0]
    tr = max(8, (1 << 19) // cols)
    d, nm, nv = _ew(name, _adam_fn, rows, [R(f(w)), R(f(g)), R(f(m)), R(f(v))], [(cols, F32)] * 3, tr=tr)
    return d.reshape(shape), nm.reshape(shape), nv.reshape(shape)


def _layer_fwd(x, wl, bias, alpha):
    s, d = x.shape
    d_ff = wl["w_ffn_out"].shape[0]
    b_in = wl["b_in"][None, :]
    qkv = _mm_nn("proj_qkv", x, wl["w_in"], n0=0, n=3 * QKV_W, bias=b_in, out_dtype=BF16, tn=768)
    rest = _mm_nn("proj_rest", x, wl["w_in"], n0=3 * QKV_W, bias=b_in, tn=768)
    qkv_s = _to_streams(qkv)
    o_s, l_s = _attn_fwd(qkv_s, bias)
    o_t, l_t = _streams_to_tok(o_s), _streams_to_tok(l_s)
    ya, lse = _ew("attn_combine", _f_combine, s, [R(a) for a in o_t + l_t], [(WIDTH_A, F32), (WIDTH_A, F32)])
    yb = _sgu_fwd(rest, wl["sgu_ln_g"][None], wl["sgu_ln_b"][None], wl["w_tril"], wl["b_col"])
    bu = _ssm_in(rest, REST_UC, wl["bd"])
    xs = _scan_fwd(bu, wl["a3"])
    y_lin = _ssm_out(xs, wl["cd"])
    uc_blk = REST_UC // WIDTH_C
    yc0 = _ew("ssm_skip_gelu", _f_ssm_out, s, [R(y_lin), R(rest, WIDTH_C, uc_blk), P(wl["d_skip"][None])], [(WIDTH_C, F32)])
    t_glu = _mm_nn("glu_proj", yc0, wl["w_glu"], bias=wl["b_glu"][None], tn=768)
    yc = _ew("glu", _f_glu, s, [R(yc0), R(t_glu)], [(WIDTH_C, F32)])
    pa = _mm_nn("proj_a", ya, wl["w_pa"])
    pb = _mm_nn("proj_b", yb, wl["w_pb"])
    pc = _mm_nn("proj_c", yc, wl["w_pc"])
    gw = _tile(math.gcd(d, REST_GL), 256, LANES)
    gl_ins = [R(rest, gw, (REST_GL + i * d) // gw, True) for i in range(N_BRANCH)]
    merged = _ew("merge", _f_merge, s, gl_ins + [R(pa, gw, 0, True), R(pb, gw, 0, True), R(pc, gw, 0, True)],
                 [(gw, F32)], nj=d // gw, tr=512)
    h1 = _mm_nn("proj_o", merged, wl["w_o"], add=x, add_scale=alpha)
    x1 = _ew("ln1", _f_ln, s, [R(h1), P(wl["ln1_g"][None]), P(wl["ln1_b"][None])], [(d, F32)])
    ff = _mm_nn("ffn_in", x1, wl["w_ffn_in"], tn=1408)
    fw = _tile(d_ff, 512, LANES)
    act = _ew("swiglu", _f_swiglu, s, [R(ff, fw, 0, True), R(ff, fw, d_ff // fw, True)], [(fw, F32)], nj=d_ff // fw, tr=512)
    h2 = _mm_nn("ffn_out", act, wl["w_ffn_out"], add=x1, add_scale=alpha, tk=1408)
    x2 = _ew("ln2", _f_ln, s, [R(h2), P(wl["ln2_g"][None]), P(wl["ln2_b"][None])], [(d, F32)])
    saved = dict(x=x, qkv_s=qkv_s, rest=rest, ya=ya, lse=lse, yb=yb, xs=xs, y_lin=y_lin, yc0=yc0, t_glu=t_glu, yc=yc,
                 pa=pa, pb=pb, pc=pc, merged=merged, h1=h1, x1=x1, ff=ff, act=act, h2=h2)
    return x2, saved


def _vjp_rows(f, n_primal):
    def fn(*args):
        n_ct = len(args) - n_primal
        cts, primals = args[:n_ct], args[n_ct:]
        out, vjp = jax.vjp(f, *primals)
        ct = tuple(c.astype(F32) for c in cts)
        return vjp(ct if isinstance(out, (tuple, list)) else ct[0])
    return fn


def _layer_bwd(dx2, sv, wl, bias, alpha):
    s, d = dx2.shape
    d_ff = wl["w_ffn_out"].shape[0]
    g = {}
    row = lambda a: a[None]
    dh2, g["ln2_g"], g["ln2_b"] = _ew("ln2_bwd", _vjp_rows(_f_ln, 3), s,
                                      [R(dx2), R(sv["h2"]), P(row(wl["ln2_g"])), P(row(wl["ln2_b"]))],
                                      [(d, F32)], accs=[d, d])
    g["w_ffn_out"] = _mm_tn("ffn_out_wgrad", sv["act"], dh2)
    dact = _mm_nt("ffn_out_dgrad", dh2, wl["w_ffn_out"], tn=1408, tk=2048)
    fw = _tile(d_ff, 512, LANES)
    nf = d_ff // fw
    dgf, dup = _ew("swiglu_bwd", _vjp_rows(_f_swiglu, 2), s,
                   [R(dact, fw, 0, True), R(sv["ff"], fw, 0, True), R(sv["ff"], fw, nf, True)],
                   [(fw, BF16), (fw, BF16)], nj=nf, tr=512)
    dff = jnp.concatenate([dgf, dup], axis=1)
    g["w_ffn_in"] = _mm_tn("ffn_in_wgrad", sv["x1"], dff, tn=1408, col_shards=True)
    dx1 = _mm_nt("ffn_in_dgrad", dff, wl["w_ffn_in"], add=dh2, add_scale=alpha, tk=1408)
    dh1, g["ln1_g"], g["ln1_b"] = _ew("ln1_bwd", _vjp_rows(_f_ln, 3), s,
                                      [R(dx1), R(sv["h1"]), P(row(wl["ln1_g"])), P(row(wl["ln1_b"]))],
                                      [(d, F32)], accs=[d, d])
    g["w_o"] = _mm_tn("proj_o_wgrad", sv["merged"], dh1)
    dmerged = _mm_nt("proj_o_dgrad", dh1, wl["w_o"])
    rest = sv["rest"]
    gw = _tile(math.gcd(d, REST_GL), 256, LANES)
    gl_ins = [R(rest, gw, (REST_GL + i * d) // gw, True) for i in range(N_BRANCH)]
    dg0, dg1, dg2, dpa, dpb, dpc = _ew(
        "merge_bwd", _vjp_rows(_f_merge, 6), s,
        [R(dmerged, gw, 0, True)] + gl_ins + [R(sv[k], gw, 0, True) for k in ("pa", "pb", "pc")],
        [(gw, BF16)] * 6, nj=d // gw, tr=512)
    g["w_pa"] = _mm_tn("proj_a_wgrad", sv["ya"], dpa, col_shards=True)
    g["w_pb"] = _mm_tn("proj_b_wgrad", sv["yb"], dpb, col_shards=True)
    g["w_pc"] = _mm_tn("proj_c_wgrad", sv["yc"], dpc, col_shards=True)
    dya = _mm_nt("proj_a_dgrad", dpa, wl["w_pa"], tn=512)
    dyb = _mm_nt("proj_b_dgrad", dpb, wl["w_pb"], tn=768)
    dyc = _mm_nt("proj_c_dgrad", dpc, wl["w_pc"], tn=768)
    dyc0_a, dt_glu, g["b_glu"] = _ew("glu_bwd", lambda ct, a, t: (lambda r: (r[0], r[1], _rowsum(r[1])))(_vjp_rows(_f_glu, 2)(ct, a, t)),
                                     s, [R(dyc), R(sv["yc0"]), R(sv["t_glu"])], [(WIDTH_C, F32), (WIDTH_C, F32)], accs=[WIDTH_C])
    g["w_glu"] = _mm_tn("glu_wgrad", sv["yc0"], dt_glu)
    dyc0 = _mm_nt("glu_dgrad", dt_glu, wl["w_glu"], add=dyc0_a, tn=768)
    uc_blk = REST_UC // WIDTH_C
    dy_lin, duc_skip, g["d_skip"] = _ew("ssm_skip_gelu_bwd", _vjp_rows(_f_ssm_out, 3), s,
                                        [R(dyc0), R(sv["y_lin"]), R(rest, WIDTH_C, uc_blk), P(row(wl["d_skip"]))],
                                        [(WIDTH_C, F32), (WIDTH_C, F32)], accs=[WIDTH_C])
    d_cd = _ssm_out_wgrad(sv["xs"], dy_lin)
    dxs = _ssm_out_dgrad(dy_lin, wl["cd"])
    gs, da = _scan_bwd(dxs, sv["xs"], wl["a3"])
    d_bd = _ssm_in_wgrad(rest, REST_UC, gs)
    duc_lin = _ssm_in_dgrad(gs, wl["bd"])
    duc = _ew("ssm_duc", lambda a, b: a + b, s, [R(duc_lin), R(duc_skip)], [(WIDTH_C, BF16)])
    d_abr = da[0].reshape(N_GROUPS_C, SSM_STATE)
    d_abi = da[1].reshape(N_GROUPS_C, SSM_STATE)
    d_lr, d_li, d_ld, d_br_t, d_bi_t = _ssm_params_bwd(wl["ssm_pin"], d_abr, d_abi,
                                                       _block_diag_extract(d_bd[0]), _block_diag_extract(d_bd[1]))
    g["lam_re"], g["lam_im"], g["log_dt"] = d_lr, d_li, d_ld[:, 0]
    un_t = lambda a: jnp.transpose(a.reshape(N_GROUPS_C, SSM_GROUP, SSM_STATE), (0, 2, 1))
    g["b_re"], g["b_im"] = un_t(d_br_t), un_t(d_bi_t)
    cd_ex = lambda a: _block_diag_extract(jnp.transpose(a, (0, 2, 1))).reshape(N_GROUPS_C, SSM_GROUP, SSM_STATE)
    g["c_re"], g["c_im"] = cd_ex(d_cd[0]), -cd_ex(d_cd[1])
    dzu, dzv, dws, dbs, g["sgu_ln_g"], g["sgu_ln_b"] = _sgu_bwd(rest, row(wl["sgu_ln_g"]), row(wl["sgu_ln_b"]),
                                                                  wl["w_tril"], wl["b_col"], dyb)
    g["w_s"] = jnp.tril(dws)
    g["b_s"] = dbs[:, :, 0]
    do_s, ya_s, lse_s = _tok_to_streams(dya), _tok_to_streams(sv["ya"]), _tok_to_streams(sv["lse"])
    dq, dkc, dkp, dvc, dvp, dbias = _attn_bwd(sv["qkv_s"], bias, do_s, ya_s, lse_s)
    dk, dv = _shift_add(dkc, dkp), _shift_add(dvc, dvp)
    dqkv = jnp.stack([jnp.stack(_streams_to_tok(t), 1) for t in (dq, dk, dv)], 1)
    dproj = jnp.concatenate([dqkv.reshape(s, 3 * QKV_W).astype(BF16), dzu, dzv, duc, dg0, dg1, dg2], axis=1)
    n_in = dproj.shape[1]
    cw = _tile(n_in, 1024, LANES)
    g["b_in"] = _ew("b_in_grad", lambda a: _rowsum(a.astype(F32)), s, [R(dproj, cw, 0, True)], [], accs=[cw],
                    nj=n_in // cw, tr=512)
    g["w_in"] = _mm_tn("proj_in_wgrad", sv["x"], dproj, tn=768)
    dx = _mm_nt("proj_in_dgrad", dproj, wl["w_in"], add=dh1, add_scale=alpha, tk=768)
    for k in ("ln2_g", "ln2_b", "ln1_g", "ln1_b", "b_glu", "d_skip", "sgu_ln_g", "sgu_ln_b", "b_in"):
        g[k] = g[k][0]
    return dx, g, dbias


def _loss_head(y, target):
    s, d = y.shape

    def fn(yb, tb):
        err = yb - tb
        return err / d, _rowsum(err * err)

    dy, sq = _ew("loss_head", fn, s, [R(y), R(target)], [(d, F32)], accs=[d])
    return dy, 0.5 * jnp.sum(sq) / d


def _step(x, target, w, m, v):
    depth = w["w_in"].shape[0]
    alpha = (2 * depth) ** 0.25
    bias = _band_bias(w["rel_bias"])
    wb = {n: w[n].astype(BF16) for n in BIG_NAMES}
    xl = x[0]
    layers, saved = [], []
    for l in range(depth):
        wl = _gather_layer_weights(wb, l)
        for n in SMALL:
            if n != "rel_bias":
                wl[n] = w[n][l]
        wl["w_tril"] = jnp.tril(w["w_s"][l]).astype(BF16)
        wl["b_col"] = w["b_s"][l][:, :, None]
        pin = _ssm_param_inputs(w, l)
        abr, abi, bbr, bbi = _ssm_params_fwd(pin)
        wl["ssm_pin"] = pin
        wl["a3"] = jnp.stack([abr.reshape(-1, LANES), abi.reshape(-1, LANES)], 0)
        wl["bd"] = jnp.stack([_block_diag(bbr), _block_diag(bbi)], 0)
        wl["cd"] = _c_block_diag(w, l)
        xl, sv = _layer_fwd(xl, wl, bias, alpha)
        layers.append(wl)
        saved.append(sv)
    dx, loss = _loss_head(xl, target[0])
    loss = lax.psum(loss, ("x", "y", "c"))
    grads = {n: [None] * depth for n in WEIGHTS if n != "rel_bias"}
    dbias_sum = None
    for l in reversed(range(depth)):
        dx, g, dbias = _layer_bwd(dx, saved[l], layers[l], bias, alpha)
        dbias_sum = dbias if dbias_sum is None else dbias_sum + dbias
        red = _reduce_layer_grads({n: g[n] for n in BIG_NAMES})
        for n in BIG_NAMES:
            grads[n][l] = red[n]
        for n in SMALL:
            if n != "rel_bias":
                grads[n][l] = g[n]
    grads = {n: jnp.stack(gl, 0) for n, gl in grads.items()}
    grads["rel_bias"] = _bias_to_buckets(dbias_sum)
    small_vec = _pad_to(jnp.concatenate([grads[n].reshape(-1) for n in SMALL]), 8 * LANES).reshape(-1, LANES)
    small_sum = _sum_lead("small_grad_sum", _ag8("small_grad_gather", small_vec)).reshape(-1)
    off = 0
    for n in SMALL:
        size = math.prod(w[n].shape)
        grads[n] = small_sum[off:off + size].reshape(w[n].shape)
        off += size
    pack = lambda t: _pad_to(jnp.concatenate([t[n].reshape(-1) for n in SMALL]), 8 * LANES).reshape(-1, LANES)
    sd, sm, sv_ = _adamw("adamw_small", pack(w), small_sum.reshape(-1, LANES), pack(m), pack(v))
    delta, new_m, new_v = {}, {}, {}
    off = 0
    for n in SMALL:
        size = math.prod(w[n].shape)
        take = lambda t: t.reshape(-1)[off:off + size].reshape(w[n].shape)
        delta[n], new_m[n], new_v[n] = take(sd), take(sm), take(sv_)
        off += size
    for n in BIG_NAMES:
        delta[n], new_m[n], new_v[n] = _adamw("adamw_" + n, w[n], grads[n], m[n], v[n])
    return loss, dx[None], grads, delta, new_m, new_v


def kernel(x, w_in, b_in, rel_bias, sgu_ln_g, sgu_ln_b, w_s, b_s, lam_re, lam_im, log_dt, b_re, b_im, c_re, c_im, d_skip, w_glu, b_glu, w_pa, w_pb, w_pc, w_o, ln1_g, ln1_b, w_ffn_in, w_ffn_out, ln2_g, ln2_b, loss_target, m_w_in, m_b_in, m_rel_bias, m_sgu_ln_g, m_sgu_ln_b, m_w_s, m_b_s, m_lam_re, m_lam_im, m_log_dt, m_b_re, m_b_im, m_c_re, m_c_im, m_d_skip, m_w_glu, m_b_glu, m_w_pa, m_w_pb, m_w_pc, m_w_o, m_ln1_g, m_ln1_b, m_w_ffn_in, m_w_ffn_out, m_ln2_g, m_ln2_b, v_w_in, v_b_in, v_rel_bias, v_sgu_ln_g, v_sgu_ln_b, v_w_s, v_b_s, v_lam_re, v_lam_im, v_log_dt, v_b_re, v_b_im, v_c_re, v_c_im, v_d_skip, v_w_glu, v_b_glu, v_w_pa, v_w_pb, v_w_pc, v_w_o, v_ln1_g, v_ln1_b, v_w_ffn_in, v_w_ffn_out, v_ln2_g, v_ln2_b):
    w = dict(w_in=w_in, b_in=b_in, rel_bias=rel_bias, sgu_ln_g=sgu_ln_g, sgu_ln_b=sgu_ln_b, w_s=w_s, b_s=b_s,
             lam_re=lam_re, lam_im=lam_im, log_dt=log_dt, b_re=b_re, b_im=b_im, c_re=c_re, c_im=c_im, d_skip=d_skip,
             w_glu=w_glu, b_glu=b_glu, w_pa=w_pa, w_pb=w_pb, w_pc=w_pc, w_o=w_o, ln1_g=ln1_g, ln1_b=ln1_b,
             w_ffn_in=w_ffn_in, w_ffn_out=w_ffn_out, ln2_g=ln2_g, ln2_b=ln2_b)
    m = dict(w_in=m_w_in, b_in=m_b_in, rel_bias=m_rel_bias, sgu_ln_g=m_sgu_ln_g, sgu_ln_b=m_sgu_ln_b, w_s=m_w_s,
             b_s=m_b_s, lam_re=m_lam_re, lam_im=m_lam_im, log_dt=m_log_dt, b_re=m_b_re, b_im=m_b_im, c_re=m_c_re,
             c_im=m_c_im, d_skip=m_d_skip, w_glu=m_w_glu, b_glu=m_b_glu, w_pa=m_w_pa, w_pb=m_w_pb, w_pc=m_w_pc,
             w_o=m_w_o, ln1_g=m_ln1_g, ln1_b=m_ln1_b, w_ffn_in=m_w_ffn_in, w_ffn_out=m_w_ffn_out, ln2_g=m_ln2_g,
             ln2_b=m_ln2_b)
    v = dict(w_in=v_w_in, b_in=v_b_in, rel_bias=v_rel_bias, sgu_ln_g=v_sgu_ln_g, sgu_ln_b=v_sgu_ln_b, w_s=v_w_s,
             b_s=v_b_s, lam_re=v_lam_re, lam_im=v_lam_im, log_dt=v_log_dt, b_re=v_b_re, b_im=v_b_im, c_re=v_c_re,
             c_im=v_c_im, d_skip=v_d_skip, w_glu=v_w_glu, b_glu=v_b_glu, w_pa=v_w_pa, w_pb=v_w_pb, w_pc=v_w_pc,
             w_o=v_w_o, ln1_g=v_ln1_g, ln1_b=v_ln1_b, w_ffn_in=v_w_ffn_in, w_ffn_out=v_w_ffn_out, ln2_g=v_ln2_g,
             ln2_b=v_ln2_b)
    loss, grad_x, grads, delta, new_m, new_v = _step(x, loss_target, w, m, v)
    return (loss, grad_x, *[grads[n] for n in WEIGHTS], *[delta[n] for n in WEIGHTS],
            *[new_m[n] for n in WEIGHTS], *[new_v[n] for n in WEIGHTS])
```

```python
import functools
import math

import numpy as np
import jax
import jax.numpy as jnp
from jax import lax
from jax.experimental import pallas as pl
from jax.experimental.pallas import tpu as pltpu

F32 = jnp.float32
BF16 = jnp.bfloat16
MESH = pl.DeviceIdType.MESH

ATT_PATTERNS = ((128, 1), (512, 4), (2048, 16))
N_GROUPS_A = 3
HEADS = 8
HEAD_DIM = 64
WIDTH_A = HEADS * HEAD_DIM
QKV_W = N_GROUPS_A * WIDTH_A
ATT_BLOCK = 128
N_REL_BUCKETS = 32
REL_MAX_DIST = 2048
NEG_INF = -1e30
CHUNK = 128
WIDTH_B = 768
N_GROUPS_B = 6
GROUP_B = 128
WIDTH_C = 768
SSM_GROUP = 16
N_GROUPS_C = 48
SSM_STATE = 64
SSM_CH = N_GROUPS_C * SSM_STATE
N_BRANCH = 3
LN_EPS = 1e-5
ADAM_LR = 0.001
ADAM_B1 = 0.9
ADAM_B2 = 0.999
ADAM_EPS = 1e-08
ADAM_WD = 0.01
ADAM_STEP = 10

LANES = 128
SUBLANES_BF16 = 16
V7X_VMEM_BYTES = 64 * 1024 * 1024
VMEM_LIMIT = V7X_VMEM_BYTES * 7 // 8

OFF_ZB = 3 * QKV_W
REST_ZB = 0
REST_UC = 2 * WIDTH_B
REST_GL = 2 * WIDTH_B + WIDTH_C

BIG = (("w_in", "col"), ("w_glu", "row"), ("w_pa", "col"), ("w_pb", "col"), ("w_pc", "col"),
       ("w_o", "row"), ("w_ffn_in", "col"), ("w_ffn_out", "row"))
WEIGHTS = ("w_in", "b_in", "rel_bias", "sgu_ln_g", "sgu_ln_b", "w_s", "b_s", "lam_re", "lam_im", "log_dt",
           "b_re", "b_im", "c_re", "c_im", "d_skip", "w_glu", "b_glu", "w_pa", "w_pb", "w_pc", "w_o",
           "ln1_g", "ln1_b", "w_ffn_in", "w_ffn_out", "ln2_g", "ln2_b")
BIG_NAMES = tuple(n for n, _ in BIG)
SMALL = tuple(n for n in WEIGHTS if n not in BIG_NAMES)


def _tile(n, target, mult):
    t = (min(target, n) // mult) * mult
    while t >= mult:
        if n % t == 0:
            return t
        t -= mult
    return n


def _cparams(n_axes):
    return pltpu.CompilerParams(dimension_semantics=("arbitrary",) * n_axes, vmem_limit_bytes=VMEM_LIMIT)


_DIMS = {"nn": (((1,), (0,)), ((), ())), "nt": (((1,), (1,)), ((), ())), "tn": (((0,), (0,)), ((), ()))}


def _mm(name, mode, grid, a, a_spec, b, b_spec, out_sds, o_spec, acc_shape, *, bias=None, bias_spec=None,
        add=None, add_spec=None, add_scale=1.0, exact=False):
    nk = grid[2]
    has_bias = bias is not None
    has_add = add is not None

    def body(*refs):
        a_ref, b_ref = refs[0], refs[1]
        pos = 2
        bias_ref = add_ref = None
        if has_bias:
            bias_ref = refs[pos]
            pos += 1
        if has_add:
            add_ref = refs[pos]
            pos += 1
        o_ref = refs[pos]
        if exact:
            part = lax.dot_general(a_ref[...].astype(F32), b_ref[...].astype(F32), _DIMS[mode],
                                   preferred_element_type=F32, precision=lax.Precision.HIGHEST)
        else:
            part = lax.dot_general(a_ref[...].astype(BF16), b_ref[...].astype(BF16), _DIMS[mode],
                                   preferred_element_type=F32)

        def finish(r):
            if has_bias:
                r = r + bias_ref[...]
            if has_add:
                r = r + add_scale * add_ref[...].astype(F32)
            o_ref[...] = r.astype(o_ref.dtype)

        if nk == 1:
            finish(part)
        else:
            acc_ref = refs[pos + 1]
            k = pl.program_id(2)

            @pl.when(k == 0)
            def _():
                acc_ref[...] = part

            @pl.when(k > 0)
            def _():
                acc_ref[...] += part

            @pl.when(k == nk - 1)
            def _():
                finish(acc_ref[...])

    ins, specs = [a, b], [a_spec, b_spec]
    if has_bias:
        ins.append(bias)
        specs.append(bias_spec)
    if has_add:
        ins.append(add)
        specs.append(add_spec)
    scratch = [pltpu.VMEM(acc_shape, F32)] if nk > 1 else []
    return pl.pallas_call(body, out_shape=out_sds, grid=grid, in_specs=specs, out_specs=o_spec,
                          scratch_shapes=scratch, compiler_params=_cparams(3), name=name)(*ins)


def _mm_nn(name, a, b, *, n0=0, n=None, a0=0, bias=None, add=None, add_scale=1.0, out_dtype=F32,
           tm=1024, tn=512, tk=None):
    m = a.shape[0]
    if b.ndim == 3:
        k, cc = b.shape[1:]
        n = 4 * cc
        tn = _tile(cc, tn, LANES)
        per = cc // tn
        tk = _tile(k, k if tk is None else tk, LANES)
        b_spec = pl.BlockSpec((None, tk, tn), lambda i, j, kk: (j // per, kk, j % per))
    else:
        k = b.shape[0]
        n = b.shape[1] - n0 if n is None else n
        tn = _tile(math.gcd(n, n0) if n0 else n, tn, LANES)
        tk = _tile(math.gcd(k, a0) if a0 else k, k if tk is None else tk, LANES)
        b_spec = pl.BlockSpec((tk, tn), lambda i, j, kk: (kk, n0 // tn + j))
    tm = _tile(m, tm, SUBLANES_BF16)
    jn0, ka0 = n0 // tn, a0 // tk
    o_spec = pl.BlockSpec((tm, tn), lambda i, j, kk: (i, j))
    return _mm(name, "nn", (m // tm, n // tn, k // tk),
               a, pl.BlockSpec((tm, tk), lambda i, j, kk: (i, ka0 + kk)), b, b_spec,
               jax.ShapeDtypeStruct((m, n), out_dtype), o_spec, (tm, tn),
               bias=bias, bias_spec=pl.BlockSpec((1, tn), lambda i, j, kk: (0, jn0 + j)),
               add=add, add_spec=o_spec, add_scale=add_scale)


def _mm_nt(name, a, b, *, n0=0, add=None, add_scale=1.0, out_dtype=F32, tm=512, tn=2048, tk=512):
    m, n = a.shape
    if b.ndim == 3:
        k, cc = b.shape[1:]
        tn = _tile(k, tn, LANES)
        tk = _tile(cc, tk, LANES)
        per = cc // tk
        b_spec = pl.BlockSpec((None, tn, tk), lambda i, j, kk: (kk // per, j, kk % per))
    else:
        k = b.shape[0]
        tn = _tile(k, tn, LANES)
        tk = _tile(math.gcd(n, n0) if n0 else n, tk, LANES)
        b_spec = pl.BlockSpec((tn, tk), lambda i, j, kk: (j, n0 // tk + kk))
    tm = _tile(m, tm, SUBLANES_BF16)
    o_spec = pl.BlockSpec((tm, tn), lambda i, j, kk: (i, j))
    return _mm(name, "nt", (m // tm, k // tn, n // tk),
               a, pl.BlockSpec((tm, tk), lambda i, j, kk: (i, kk)), b, b_spec,
               jax.ShapeDtypeStruct((m, k), out_dtype), o_spec, (tm, tn),
               add=add, add_spec=o_spec, add_scale=add_scale)


def _mm_tn(name, a, b, *, a0=0, ka=None, out_dtype=BF16, tm=2048, tn=1024, tk=1024, col_shards=False):
    s = a.shape[0]
    ka = a.shape[1] - a0 if ka is None else ka
    n = b.shape[1]
    tm = _tile(math.gcd(ka, a0) if a0 else ka, tm, LANES)
    tk = _tile(s, tk, SUBLANES_BF16)
    ia0 = a0 // tm
    if col_shards:
        cc = n // 4
        tn = _tile(cc, tn, LANES)
        per = cc // tn
        out_sds = jax.ShapeDtypeStruct((4, ka, cc), out_dtype)
        o_spec = pl.BlockSpec((None, tm, tn), lambda i, j, kk: (j // per, i, j % per))
    else:
        tn = _tile(n, tn, LANES)
        out_sds = jax.ShapeDtypeStruct((ka, n), out_dtype)
        o_spec = pl.BlockSpec((tm, tn), lambda i, j, kk: (i, j))
    return _mm(name, "tn", (ka // tm, n // tn, s // tk),
               a, pl.BlockSpec((tk, tm), lambda i, j, kk: (kk, ia0 + i)),
               b, pl.BlockSpec((tk, tn), lambda i, j, kk: (kk, j)), out_sds, o_spec, (tm, tn))


def R(arr, bw=None, c0=0, j=False):
    return ("r", arr, arr.shape[1] if bw is None else bw, c0, j)


def P(arr, bw=None, c0=0, j=False):
    return ("p", arr, arr.shape[1] if bw is None else bw, c0, j)


def _ew(name, fn, rows, ins, outs, accs=(), *, tr=256, nj=1):
    tr = _tile(rows, tr, SUBLANES_BF16)
    ni = rows // tr
    n_in, n_out, n_acc = len(ins), len(outs), len(accs)

    def spec(kind, bw, c0, follows):
        rows_b = tr if kind == "r" else 1
        if kind == "r":
            return pl.BlockSpec((rows_b, bw), (lambda j, i: (i, c0 + j)) if follows else (lambda j, i: (i, c0)))
        return pl.BlockSpec((rows_b, bw), (lambda j, i: (0, c0 + j)) if follows else (lambda j, i: (0, c0)))

    def body(*refs):
        res = fn(*[r[...] for r in refs[:n_in]])
        res = tuple(res) if isinstance(res, (tuple, list)) else (res,)
        for r, v in zip(refs[n_in:n_in + n_out], res[:n_out]):
            r[...] = v.astype(r.dtype)
        if n_acc:
            i = pl.program_id(1)
            for r, v in zip(refs[n_in + n_out:], res[n_out:]):
                @pl.when(i == 0)
                def _(r=r, v=v):
                    r[...] = v

                @pl.when(i > 0)
                def _(r=r, v=v):
                    r[...] += v

    out_shape = [jax.ShapeDtypeStruct((rows, bw * nj), dt) for bw, dt in outs]
    out_shape += [jax.ShapeDtypeStruct((1, bw * nj), F32) for bw in accs]
    out_specs = [pl.BlockSpec((tr, bw), lambda j, i: (i, j)) for bw, _ in outs]
    out_specs += [pl.BlockSpec((1, bw), lambda j, i: (0, j)) for bw in accs]
    res = pl.pallas_call(body, out_shape=out_shape, grid=(nj, ni),
                         in_specs=[spec(k, bw, c0, f) for k, _, bw, c0, f in ins], out_specs=out_specs,
                         compiler_params=_cparams(2), name=name)(*[a for _, a, _, _, _ in ins])
    return res if len(res) > 1 else res[0]


def _gelu(x):
    c = math.sqrt(2.0 / math.pi)
    return 0.5 * x * (1.0 + jnp.tanh(c * (x + 0.044715 * (x * x * x))))


def _sigmoid(x):
    return lax.logistic(x)


def _f_ln(h, g, b):
    mu = jnp.mean(h, axis=-1, keepdims=True)
    xc = h - mu
    var = jnp.mean(xc * xc, axis=-1, keepdims=True)
    return xc * lax.rsqrt(var + LN_EPS) * g + b


def _f_combine(o1, o2, o3, l1, l2, l3):
    m = jnp.maximum(jnp.maximum(l1, l2), l3)
    e1, e2, e3 = jnp.exp(l1 - m), jnp.exp(l2 - m), jnp.exp(l3 - m)
    den = e1 + e2 + e3
    return (e1 * o1 + e2 * o2 + e3 * o3) / den, m + jnp.log(den)


def _f_ssm_out(y_lin, uc, d_skip):
    return _gelu(y_lin + d_skip * uc)


def _f_glu(yc0, t):
    return yc0 * _sigmoid(t)


def _f_merge(g0, g1, g2, pa, pb, pc):
    return _sigmoid(g0) * pa + _sigmoid(g1) * pb + _sigmoid(g2) * pc


def _f_swiglu(gf, up):
    return gf * _sigmoid(gf) * up


def _f_sgu_pre(zu, zv, g, b):
    return _gelu(zu), _f_ln(_gelu(zv), g, b)


def _f_ssm_params(lr, li, ld, lr_rep, li_rep, ld_rep, br_t, bi_t):
    def disc(lr, li, ld):
        dt = jnp.exp(ld)
        mag = jnp.exp(lr * dt)
        th = li * dt
        abr, abi = mag * jnp.cos(th), mag * jnp.sin(th)
        nrm = lr * lr + li * li
        cr = ((abr - 1.0) * lr + abi * li) / nrm
        ci = (abi * lr - (abr - 1.0) * li) / nrm
        return abr, abi, cr, ci

    abr, abi, _, _ = disc(lr, li, ld)
    _, _, cr, ci = disc(lr_rep, li_rep, ld_rep)
    return abr, abi, cr * br_t - ci * bi_t, cr * bi_t + ci * br_t


def _rowsum(x):
    return jnp.sum(x, axis=0, keepdims=True)


def _to_streams(qkv):
    s = qkv.shape[0]
    t = qkv.reshape(s, 3, N_GROUPS_A, HEADS, HEAD_DIM)
    outs = []
    for g, (_, dil) in enumerate(ATT_PATTERNS):
        tg = t[:, :, g].reshape(s // dil, dil, 3, HEADS, HEAD_DIM)
        outs.append(tg.transpose(2, 3, 1, 0, 4).reshape(3, HEADS, s, HEAD_DIM))
    return jnp.stack(outs, 0)


def _tok_to_streams(a):
    s = a.shape[0]
    outs = []
    for _, dil in ATT_PATTERNS:
        t = a.reshape(s // dil, dil, HEADS, HEAD_DIM)
        outs.append(t.transpose(2, 1, 0, 3).reshape(HEADS, s, HEAD_DIM))
    return jnp.stack(outs, 0)


def _streams_to_tok(o):
    s = o.shape[2]
    outs = []
    for g, (_, dil) in enumerate(ATT_PATTERNS):
        t = o[g].reshape(HEADS, dil, s // dil, HEAD_DIM)
        outs.append(t.transpose(2, 1, 0, 3).reshape(s, WIDTH_A))
    return outs


def _t5_bucket(dist):
    max_exact = N_REL_BUCKETS // 2
    d = np.maximum(dist, 1).astype(np.float32)
    scale = (N_REL_BUCKETS - max_exact) / math.log(REL_MAX_DIST / max_exact)
    large = max_exact + (np.log(d / max_exact) * scale).astype(np.int32)
    large = np.minimum(large, N_REL_BUCKETS - 1)
    return np.where(dist < max_exact, dist, large).astype(np.int32)


def _bucket_maps():
    i = np.arange(ATT_BLOCK)[:, None]
    kk = np.arange(2 * ATT_BLOCK)[None, :]
    steps = np.maximum(ATT_BLOCK + i - kk, 0)
    return np.stack([_t5_bucket(steps * dil) for _, dil in ATT_PATTERNS], 0)


def _band_bias(rel_bias):
    q = ATT_BLOCK

    def body(rel_ref, m_ref, o_ref):
        g = pl.program_id(0)
        bm = m_ref[...]
        for h in range(HEADS):
            acc = jnp.zeros((q, 2 * q), F32)
            for bk in range(N_REL_BUCKETS):
                acc = jnp.where(bm == bk, rel_ref[bk, g * HEADS + h], acc)
            o_ref[h] = acc

    return pl.pallas_call(body, out_shape=jax.ShapeDtypeStruct((N_GROUPS_A, HEADS, q, 2 * q), F32), grid=(N_GROUPS_A,),
                          in_specs=[pl.BlockSpec(memory_space=pltpu.SMEM),
                                    pl.BlockSpec((None, q, 2 * q), lambda g: (g, 0, 0))],
                          out_specs=pl.BlockSpec((None, HEADS, q, 2 * q), lambda g: (g, 0, 0, 0)),
                          compiler_params=_cparams(1), name="rel_bias_band")(rel_bias, jnp.asarray(_bucket_maps()))


def _attn_masks(nbs):
    q = ATT_BLOCK
    g, b = pl.program_id(0), pl.program_id(1)
    nb = jnp.where(g == 0, nbs[0], jnp.where(g == 1, nbs[1], nbs[2]))
    shift = jnp.where(lax.rem(b, nb) != 0, 0, q)
    ii = lax.broadcasted_iota(jnp.int32, (q, q), 0)
    kk = lax.broadcasted_iota(jnp.int32, (q, q), 1)
    return kk >= ii + shift, kk <= ii


def _attn_logits(q, kp, kc, bias_h, mask_p, mask_c):
    scale = HEAD_DIM ** -0.5
    sp = lax.dot_general(q, kp, _DIMS["nt"], preferred_element_type=F32) * scale + bias_h[:, :ATT_BLOCK]
    sc = lax.dot_general(q, kc, _DIMS["nt"], preferred_element_type=F32) * scale + bias_h[:, ATT_BLOCK:]
    return jnp.where(mask_p, sp, NEG_INF), jnp.where(mask_c, sc, NEG_INF)


def _attn_specs(s):
    q, h, e = ATT_BLOCK, HEADS, HEAD_DIM
    blk = (None, None, h, q, e)
    prev = lambda b: jnp.maximum(b - 1, 0)
    qkv_specs = [pl.BlockSpec(blk, lambda g, b: (g, 0, 0, b, 0)),
                 pl.BlockSpec(blk, lambda g, b: (g, 1, 0, prev(b), 0)),
                 pl.BlockSpec(blk, lambda g, b: (g, 1, 0, b, 0)),
                 pl.BlockSpec(blk, lambda g, b: (g, 2, 0, prev(b), 0)),
                 pl.BlockSpec(blk, lambda g, b: (g, 2, 0, b, 0))]
    bias_spec = pl.BlockSpec((None, h, q, 2 * q), lambda g, b: (g, 0, 0, 0))
    row_spec = pl.BlockSpec((None, h, q, e), lambda g, b: (g, 0, b, 0))
    return qkv_specs, bias_spec, row_spec


def _attn_fwd(qkv_s, bias):
    s = qkv_s.shape[3]
    nbs = tuple(s // dil // ATT_BLOCK for _, dil in ATT_PATTERNS)
    qkv_specs, bias_spec, row_spec = _attn_specs(s)

    def body(q_ref, kp_ref, kc_ref, vp_ref, vc_ref, b_ref, o_ref, l_ref):
        mask_p, mask_c = _attn_masks(nbs)
        for h in range(HEADS):
            sp, sc = _attn_logits(q_ref[h], kp_ref[h], kc_ref[h], b_ref[h], mask_p, mask_c)
            m = jnp.maximum(jnp.max(sp, axis=1, keepdims=True), jnp.max(sc, axis=1, keepdims=True))
            pp, pc = jnp.exp(sp - m), jnp.exp(sc - m)
            den = jnp.sum(pp, axis=1, keepdims=True) + jnp.sum(pc, axis=1, keepdims=True)
            o = (lax.dot_general(pp.astype(BF16), vp_ref[h], _DIMS["nn"], preferred_element_type=F32)
                 + lax.dot_general(pc.astype(BF16), vc_ref[h], _DIMS["nn"], preferred_element_type=F32))
            o_ref[h] = o / den
            l_ref[h] = jnp.broadcast_to(m + jnp.log(den), (ATT_BLOCK, HEAD_DIM))

    sds = jax.ShapeDtypeStruct((N_GROUPS_A, HEADS, s, HEAD_DIM), F32)
    return pl.pallas_call(body, out_shape=(sds, sds), grid=(N_GROUPS_A, s // ATT_BLOCK),
                          in_specs=qkv_specs + [bias_spec], out_specs=(row_spec, row_spec),
                          compiler_params=_cparams(2), name="attn_fwd")(qkv_s, qkv_s, qkv_s, qkv_s, qkv_s, bias)


def _attn_bwd(qkv_s, bias, do_s, ya_s, lse_s):
    s = qkv_s.shape[3]
    nblk = s // ATT_BLOCK
    nbs = tuple(s // dil // ATT_BLOCK for _, dil in ATT_PATTERNS)
    scale = HEAD_DIM ** -0.5
    q, h, e = ATT_BLOCK, HEADS, HEAD_DIM

    def body(q_ref, kp_ref, kc_ref, vp_ref, vc_ref, b_ref, do_ref, ya_ref, l_ref,
             dq_ref, dk_ref, dv_ref, db_ref, dk_own, dv_own):
        b = pl.program_id(1)
        mask_p, mask_c = _attn_masks(nbs)

        @pl.when(b == 0)
        def _():
            db_ref[...] = jnp.zeros_like(db_ref)

        @pl.when(b < nblk)
        def _():
            for hh in range(h):
                qh, kp, kc, vp, vc = q_ref[hh], kp_ref[hh], kc_ref[hh], vp_ref[hh], vc_ref[hh]
                sp, sc = _attn_logits(qh, kp, kc, b_ref[hh], mask_p, mask_c)
                lse = l_ref[hh][:, 0:1]
                pp, pc = jnp.exp(sp - lse), jnp.exp(sc - lse)
                do = do_ref[hh]
                dsum = jnp.sum(do * ya_ref[hh], axis=1, keepdims=True)
                dob = do.astype(BF16)
                dsp = pp * (lax.dot_general(dob, vp, _DIMS["nt"], preferred_element_type=F32) - dsum)
                dsc = pc * (lax.dot_general(dob, vc, _DIMS["nt"], preferred_element_type=F32) - dsum)
                db_ref[hh, :, :q] += dsp
                db_ref[hh, :, q:] += dsc
                dspb, dscb = dsp.astype(BF16), dsc.astype(BF16)
                dq_ref[hh] = scale * (lax.dot_general(dspb, kp, _DIMS["nn"], preferred_element_type=F32)
                                      + lax.dot_general(dscb, kc, _DIMS["nn"], preferred_element_type=F32))
                dkp = scale * lax.dot_general(dspb, qh, _DIMS["tn"], preferred_element_type=F32)
                dvp = lax.dot_general(pp.astype(BF16), dob, _DIMS["tn"], preferred_element_type=F32)

                @pl.when(b == 0)
                def _(hh=hh, dkp=dkp, dvp=dvp):
                    dk_ref[hh] = dkp
                    dv_ref[hh] = dvp

                @pl.when(b > 0)
                def _(hh=hh, dkp=dkp, dvp=dvp):
                    dk_ref[hh] = dk_own[hh] + dkp
                    dv_ref[hh] = dv_own[hh] + dvp

                dk_own[hh] = scale * lax.dot_general(dscb, qh, _DIMS["tn"], preferred_element_type=F32)
                dv_own[hh] = lax.dot_general(pc.astype(BF16), dob, _DIMS["tn"], preferred_element_type=F32)

        @pl.when(b == nblk)
        def _():
            dk_ref[...] = dk_own[...]
            dv_ref[...] = dv_own[...]

    blk5 = (None, None, h, q, e)
    cur = lambda b: jnp.minimum(b, nblk - 1)
    prev = lambda b: jnp.maximum(cur(b) - 1, 0)
    qkv_specs = [pl.BlockSpec(blk5, lambda g, b: (g, 0, 0, cur(b), 0)),
                 pl.BlockSpec(blk5, lambda g, b: (g, 1, 0, prev(b), 0)),
                 pl.BlockSpec(blk5, lambda g, b: (g, 1, 0, cur(b), 0)),
                 pl.BlockSpec(blk5, lambda g, b: (g, 2, 0, prev(b), 0)),
                 pl.BlockSpec(blk5, lambda g, b: (g, 2, 0, cur(b), 0))]
    bias_spec = pl.BlockSpec((None, h, q, 2 * q), lambda g, b: (g, 0, 0, 0))
    row_spec = pl.BlockSpec((None, h, q, e), lambda g, b: (g, 0, cur(b), 0))
    lag_spec = pl.BlockSpec((None, h, q, e), lambda g, b: (g, 0, jnp.maximum(b - 1, 0), 0))
    sds = jax.ShapeDtypeStruct((N_GROUPS_A, h, s, e), F32)
    own = pltpu.VMEM((h, q, e), F32)
    return pl.pallas_call(body, out_shape=(sds,) * 3 + (jax.ShapeDtypeStruct(bias.shape, F32),),
                          grid=(N_GROUPS_A, nblk + 1),
                          in_specs=qkv_specs + [bias_spec, row_spec, row_spec, row_spec],
                          out_specs=(row_spec, lag_spec, lag_spec, bias_spec), scratch_shapes=[own, own],
                          compiler_params=_cparams(2), name="attn_bwd")(
        qkv_s, qkv_s, qkv_s, qkv_s, qkv_s, bias, do_s, ya_s, lse_s)


def _bias_to_buckets(dbias):
    bmap = jnp.asarray(_bucket_maps())
    q = ATT_BLOCK

    def body(db_ref, m_ref, o_ref):
        lane = lax.broadcasted_iota(jnp.int32, (HEADS, LANES), 1)
        row = lax.broadcasted_iota(jnp.int32, (HEADS, LANES), 0)
        acc = jnp.zeros((HEADS, LANES), F32)
        bm = m_ref[...]
        for h in range(HEADS):
            dbh = db_ref[h]
            for bk in range(N_REL_BUCKETS):
                sv = jnp.sum(jnp.sum(jnp.where(bm == bk, dbh, 0.0), axis=1, keepdims=True), axis=0, keepdims=True)
                acc = acc + jnp.where((lane == bk) & (row == h), sv, 0.0)
        o_ref[...] = acc

    out = pl.pallas_call(body, out_shape=jax.ShapeDtypeStruct((N_GROUPS_A, HEADS, LANES), F32), grid=(N_GROUPS_A,),
                         in_specs=[pl.BlockSpec((None, HEADS, q, 2 * q), lambda g: (g, 0, 0, 0)),
                                   pl.BlockSpec((None, q, 2 * q), lambda g: (g, 0, 0))],
                         out_specs=pl.BlockSpec((None, HEADS, LANES), lambda g: (g, 0, 0)),
                         compiler_params=_cparams(1), name="rel_bias_grad")(dbias, bmap)
    return out[:, :, :N_REL_BUCKETS].reshape(N_GROUPS_A * HEADS, N_REL_BUCKETS).T


def _sgu_specs():
    c, w = CHUNK, WIDTH_B
    return [pl.BlockSpec((c, w), lambda i: (i, 0)), pl.BlockSpec((c, w), lambda i: (i, 1)),
            pl.BlockSpec((1, w), lambda i: (0, 0)), pl.BlockSpec((1, w), lambda i: (0, 0)),
            pl.BlockSpec((N_GROUPS_B, c, c), lambda i: (0, 0, 0)), pl.BlockSpec((N_GROUPS_B, c, 1), lambda i: (0, 0, 0))]


def _sgu_fwd(rest, ln_g, ln_b, w_tril, b_col):
    s = rest.shape[0]

    def body(zu_ref, zv_ref, g_ref, b_ref, w_ref, bs_ref, y_ref):
        u, vn = _f_sgu_pre(zu_ref[...], zv_ref[...], g_ref[...], b_ref[...])
        for gi in range(N_GROUPS_B):
            sl = slice(gi * GROUP_B, (gi + 1) * GROUP_B)
            mixed = lax.dot_general(w_ref[gi], vn[:, sl].astype(BF16), _DIMS["nn"], preferred_element_type=F32)
            y_ref[:, sl] = u[:, sl] * (mixed + bs_ref[gi])

    return pl.pallas_call(body, out_shape=jax.ShapeDtypeStruct((s, WIDTH_B), F32), grid=(s // CHUNK,),
                          in_specs=_sgu_specs(), out_specs=pl.BlockSpec((CHUNK, WIDTH_B), lambda i: (i, 0)),
                          compiler_params=_cparams(1), name="sgu_fwd")(rest, rest, ln_g, ln_b, w_tril, b_col)


def _sgu_bwd(rest, ln_g, ln_b, w_tril, b_col, dyb):
    s = rest.shape[0]
    c, w, ng = CHUNK, WIDTH_B, N_GROUPS_B

    def body(zu_ref, zv_ref, g_ref, b_ref, w_ref, bs_ref, dy_ref, dzu_ref, dzv_ref, dw_ref, dbs_ref, dg_ref, db_ref):
        (u, vn), vjp = jax.vjp(_f_sgu_pre, zu_ref[...], zv_ref[...], g_ref[...], b_ref[...])
        first = pl.program_id(0) == 0
        du, dvn = [], []
        for gi in range(ng):
            sl = slice(gi * GROUP_B, (gi + 1) * GROUP_B)
            vg = vn[:, sl].astype(BF16)
            mixed = lax.dot_general(w_ref[gi], vg, _DIMS["nn"], preferred_element_type=F32) + bs_ref[gi]
            dy = dy_ref[:, sl]
            dmix = dy * u[:, sl]
            du.append(dy * mixed)
            dmb = dmix.astype(BF16)
            dvn.append(lax.dot_general(w_ref[gi], dmb, _DIMS["tn"], preferred_element_type=F32))
            dwg = lax.dot_general(dmb, vg, _DIMS["nt"], preferred_element_type=F32)
            dbg = jnp.sum(dmix, axis=1, keepdims=True)

            @pl.when(first)
            def _(gi=gi, dwg=dwg, dbg=dbg):
                dw_ref[gi] = dwg
                dbs_ref[gi] = dbg

            @pl.when(jnp.logical_not(first))
            def _(gi=gi, dwg=dwg, dbg=dbg):
                dw_ref[gi] += dwg
                dbs_ref[gi] += dbg

        dzu, dzv, dg, db = vjp((jnp.concatenate(du, axis=1), jnp.concatenate(dvn, axis=1)))
        dzu_ref[...] = dzu.astype(dzu_ref.dtype)
        dzv_ref[...] = dzv.astype(dzv_ref.dtype)

        @pl.when(first)
        def _():
            dg_ref[...] = dg
            db_ref[...] = db

        @pl.when(jnp.logical_not(first))
        def _():
            dg_ref[...] += dg
            db_ref[...] += db

    row = pl.BlockSpec((c, w), lambda i: (i, 0))
    par = pl.BlockSpec((1, w), lambda i: (0, 0))
    return pl.pallas_call(
        body, grid=(s // c,),
        out_shape=(jax.ShapeDtypeStruct((s, w), BF16), jax.ShapeDtypeStruct((s, w), BF16),
                   jax.ShapeDtypeStruct((ng, c, c), F32), jax.ShapeDtypeStruct((ng, c, 1), F32),
                   jax.ShapeDtypeStruct((1, w), F32), jax.ShapeDtypeStruct((1, w), F32)),
        in_specs=_sgu_specs() + [row],
        out_specs=(row, row, pl.BlockSpec((ng, c, c), lambda i: (0, 0, 0)), pl.BlockSpec((ng, c, 1), lambda i: (0, 0, 0)),
                   par, par),
        compiler_params=_cparams(1), name="sgu_bwd")(rest, rest, ln_g, ln_b, w_tril, b_col, dyb)


SCAN_T = 128


def _to_time_major(dst, src_ref, lead):
    for r in range(SSM_CH // LANES):
        dst[:, r, :] = src_ref[lead, :, r * LANES:(r + 1) * LANES]


def _from_time_major(dst_ref, lead, src):
    for r in range(SSM_CH // LANES):
        dst_ref[lead, :, r * LANES:(r + 1) * LANES] = src[:, r, :]


def _scan_fwd(bu, a3):
    s = bu.shape[1]
    t_blk = _tile(s, SCAN_T, 8)
    rows = SSM_CH // LANES

    def body(bu_ref, a_ref, x_ref, b3r, b3i, x3r, x3i, carry):
        @pl.when(pl.program_id(0) == 0)
        def _():
            carry[...] = jnp.zeros_like(carry)

        _to_time_major(b3r, bu_ref, 0)
        _to_time_major(b3i, bu_ref, 1)
        ar, ai = a_ref[0], a_ref[1]

        def step(t, c):
            xr, xi = c
            nr = ar * xr - ai * xi + b3r[t]
            ni = ar * xi + ai * xr + b3i[t]
            x3r[t] = nr
            x3i[t] = ni
            return nr, ni

        xr, xi = lax.fori_loop(0, t_blk, step, (carry[0], carry[1]), unroll=8)
        carry[0] = xr
        carry[1] = xi
        _from_time_major(x_ref, 0, x3r)
        _from_time_major(x_ref, 1, x3i)

    blk = pl.BlockSpec((2, t_blk, SSM_CH), lambda i: (0, i, 0))
    tm = pltpu.VMEM((t_blk, rows, LANES), F32)
    return pl.pallas_call(body, out_shape=jax.ShapeDtypeStruct(bu.shape, F32), grid=(s // t_blk,),
                          in_specs=[blk, pl.BlockSpec((2, rows, LANES), lambda i: (0, 0, 0))], out_specs=blk,
                          scratch_shapes=[tm, tm, tm, tm, pltpu.VMEM((2, rows, LANES), F32)],
                          compiler_params=_cparams(1), name="ssm_scan_fwd")(bu, a3)


def _scan_bwd(dx, x, a3):
    s = dx.shape[1]
    t_blk = _tile(s, SCAN_T, 8)
    nb = s // t_blk
    rows = SSM_CH // LANES

    def body(dx_ref, x_ref, a_ref, g_ref, da_ref, d3r, d3i, x3r, x3i, g3r, g3i, carry):
        first = pl.program_id(0) == 0

        @pl.when(first)
        def _():
            carry[...] = jnp.zeros_like(carry)

        _to_time_major(d3r, dx_ref, 0)
        _to_time_major(d3i, dx_ref, 1)
        _to_time_major(x3r, x_ref, 0)
        _to_time_major(x3i, x_ref, 1)
        ar, ai = a_ref[0], a_ref[1]

        def step(k, c):
            t = t_blk - 1 - k
            gr, gi, dar, dai = c
            xr, xi = x3r[t], x3i[t]
            dar = dar + gr * xr + gi * xi
            dai = dai + gi * xr - gr * xi
            ngr = d3r[t] + ar * gr + ai * gi
            ngi = d3i[t] + ar * gi - ai * gr
            g3r[t] = ngr
            g3i[t] = ngi
            return ngr, ngi, dar, dai

        zero = jnp.zeros((rows, LANES), F32)
        gr, gi, dar, dai = lax.fori_loop(0, t_blk, step, (carry[0], carry[1], zero, zero), unroll=8)
        carry[0] = gr
        carry[1] = gi

        @pl.when(first)
        def _():
            da_ref[0] = dar
            da_ref[1] = dai

        @pl.when(jnp.logical_not(first))
        def _():
            da_ref[0] += dar
            da_ref[1] += dai

        _from_time_major(g_ref, 0, g3r)
        _from_time_major(g_ref, 1, g3i)

    blk = pl.BlockSpec((2, t_blk, SSM_CH), lambda i: (0, nb - 1 - i, 0))
    par = pl.BlockSpec((2, rows, LANES), lambda i: (0, 0, 0))
    tm = pltpu.VMEM((t_blk, rows, LANES), F32)
    return pl.pallas_call(body, out_shape=(jax.ShapeDtypeStruct(dx.shape, F32), jax.ShapeDtypeStruct((2, rows, LANES), F32)),
                          grid=(nb,), in_specs=[blk, blk, par], out_specs=(blk, par),
                          scratch_shapes=[tm] * 6 + [pltpu.VMEM((2, rows, LANES), F32)],
                          compiler_params=_cparams(1), name="ssm_scan_bwd")(dx, x, a3)


SSM_TILES = WIDTH_C // LANES
SSM_TILE_W = SSM_CH // SSM_TILES


def _block_diag(m):
    gpt = N_GROUPS_C // SSM_TILES
    t = m.reshape(SSM_TILES, gpt, SSM_GROUP, SSM_STATE)
    eye = jnp.eye(gpt, dtype=m.dtype)
    return (t[:, :, :, None, :] * eye[None, :, None, :, None]).reshape(SSM_TILES, LANES, SSM_TILE_W)


def _block_diag_extract(t):
    gpt = N_GROUPS_C // SSM_TILES
    t = t.reshape(SSM_TILES, gpt, SSM_GROUP, gpt, SSM_STATE)
    eye = jnp.eye(gpt, dtype=t.dtype)
    return jnp.sum(t * eye[None, :, None, :, None], axis=3).reshape(WIDTH_C, SSM_STATE)


def _ssm_in(uc_src, uc_col0, bd):
    s = uc_src.shape[0]
    tm = _tile(s, 2048, 8)
    j0 = uc_col0 // LANES
    return _mm("ssm_in", "nn", (s // tm, 2 * SSM_TILES, 1),
               uc_src, pl.BlockSpec((tm, LANES), lambda i, j, k: (i, j0 + j % SSM_TILES)),
               bd, pl.BlockSpec((None, None, LANES, SSM_TILE_W), lambda i, j, k: (j // SSM_TILES, j % SSM_TILES, 0, 0)),
               jax.ShapeDtypeStruct((2, s, SSM_CH), F32),
               pl.BlockSpec((None, tm, SSM_TILE_W), lambda i, j, k: (j // SSM_TILES, i, j % SSM_TILES)), None)


def _ssm_in_dgrad(g, bd):
    s = g.shape[1]
    tm = _tile(s, 2048, 8)
    return _mm("ssm_in_dgrad", "nt", (s // tm, SSM_TILES, 2),
               g, pl.BlockSpec((None, tm, SSM_TILE_W), lambda i, j, k: (k, i, j)),
               bd, pl.BlockSpec((None, None, LANES, SSM_TILE_W), lambda i, j, k: (k, j, 0, 0)),
               jax.ShapeDtypeStruct((s, WIDTH_C), F32), pl.BlockSpec((tm, LANES), lambda i, j, k: (i, j)),
               (tm, LANES))


def _ssm_in_wgrad(uc_src, uc_col0, g):
    s = g.shape[1]
    tk = _tile(s, 2048, 8)
    j0 = uc_col0 // LANES
    return _mm("ssm_in_wgrad", "tn", (2, SSM_TILES, s // tk),
               uc_src, pl.BlockSpec((tk, LANES), lambda i, j, k: (k, j0 + j)),
               g, pl.BlockSpec((None, tk, SSM_TILE_W), lambda i, j, k: (i, k, j)),
               jax.ShapeDtypeStruct((2, SSM_TILES, LANES, SSM_TILE_W), F32),
               pl.BlockSpec((None, None, LANES, SSM_TILE_W), lambda i, j, k: (i, j, 0, 0)), (LANES, SSM_TILE_W))


def _ssm_out(x, cd):
    s = x.shape[1]
    tm = _tile(s, 2048, 8)
    return _mm("ssm_out", "nn", (s // tm, SSM_TILES, 2),
               x, pl.BlockSpec((None, tm, SSM_TILE_W), lambda i, j, k: (k, i, j)),
               cd, pl.BlockSpec((None, None, SSM_TILE_W, LANES), lambda i, j, k: (k, j, 0, 0)),
               jax.ShapeDtypeStruct((s, WIDTH_C), F32), pl.BlockSpec((tm, LANES), lambda i, j, k: (i, j)),
               (tm, LANES))


def _ssm_out_dgrad(dy, cd):
    s = dy.shape[0]
    tm = _tile(s, 2048, 8)
    return _mm("ssm_out_dgrad", "nt", (s // tm, 2 * SSM_TILES, 1),
               dy, pl.BlockSpec((tm, LANES), lambda i, j, k: (i, j % SSM_TILES)),
               cd, pl.BlockSpec((None, None, SSM_TILE_W, LANES), lambda i, j, k: (j // SSM_TILES, j % SSM_TILES, 0, 0)),
               jax.ShapeDtypeStruct((2, s, SSM_CH), F32),
               pl.BlockSpec((None, tm, SSM_TILE_W), lambda i, j, k: (j // SSM_TILES, i, j % SSM_TILES)), None)


def _ssm_out_wgrad(x, dy):
    s = dy.shape[0]
    tk = _tile(s, 2048, 8)
    return _mm("ssm_out_wgrad", "tn", (2, SSM_TILES, s // tk),
               x, pl.BlockSpec((None, tk, SSM_TILE_W), lambda i, j, k: (i, k, j)),
               dy, pl.BlockSpec((tk, LANES), lambda i, j, k: (k, j)),
               jax.ShapeDtypeStruct((2, SSM_TILES, SSM_TILE_W, LANES), F32),
               pl.BlockSpec((None, None, SSM_TILE_W, LANES), lambda i, j, k: (i, j, 0, 0)), (SSM_TILE_W, LANES))


def _ssm_param_inputs(w, l):
    rep = lambda a: jnp.repeat(a, SSM_GROUP, axis=0)
    lr, li, ld = w["lam_re"][l], w["lam_im"][l], w["log_dt"][l][:, None]
    br_t = jnp.transpose(w["b_re"][l], (0, 2, 1)).reshape(WIDTH_C, SSM_STATE)
    bi_t = jnp.transpose(w["b_im"][l], (0, 2, 1)).reshape(WIDTH_C, SSM_STATE)
    return lr, li, ld, rep(lr), rep(li), rep(ld), br_t, bi_t


def _ssm_params_fwd(pin):
    def body(*refs):
        res = _f_ssm_params(*[r[...] for r in refs[:8]])
        for r, v in zip(refs[8:], res):
            r[...] = v

    g, p = N_GROUPS_C, SSM_STATE
    return pl.pallas_call(body, out_shape=(jax.ShapeDtypeStruct((g, p), F32),) * 2
                          + (jax.ShapeDtypeStruct((WIDTH_C, p), F32),) * 2, name="ssm_params_fwd")(*pin)


def _ssm_params_bwd(pin, d_abr, d_abi, d_bbr, d_bbi):
    g, p = N_GROUPS_C, SSM_STATE
    group_sum = jnp.asarray(np.kron(np.eye(g, dtype=np.float32), np.ones((1, SSM_GROUP), np.float32)))

    def body(*refs):
        ins = [r[...] for r in refs[:8]]
        cts = tuple(r[...] for r in refs[8:12])
        gs = refs[12][...]
        d_lr_ref, d_li_ref, d_ld_ref, d_br_ref, d_bi_ref = refs[13:]
        _, vjp = jax.vjp(_f_ssm_params, *ins)
        d = vjp(cts)
        fold = lambda v: lax.dot_general(gs, v, _DIMS["nn"], preferred_element_type=F32, precision=lax.Precision.HIGHEST)
        d_lr_ref[...] = d[0] + fold(d[3])
        d_li_ref[...] = d[1] + fold(d[4])
        d_ld_ref[...] = d[2] + fold(jnp.broadcast_to(d[5], (WIDTH_C, p)))[:, 0:1]
        d_br_ref[...] = d[6]
        d_bi_ref[...] = d[7]

    return pl.pallas_call(body, out_shape=(jax.ShapeDtypeStruct((g, p), F32), jax.ShapeDtypeStruct((g, p), F32),
                                           jax.ShapeDtypeStruct((g, 1), F32), jax.ShapeDtypeStruct((WIDTH_C, p), F32),
                                           jax.ShapeDtypeStruct((WIDTH_C, p), F32)), name="ssm_params_bwd")(
        *pin, d_abr, d_abi, d_bbr, d_bbi, group_sum)


def _c_block_diag(w, l):
    def one(c):
        return jnp.transpose(_block_diag(c.reshape(WIDTH_C, SSM_STATE)), (0, 2, 1))
    return jnp.stack([one(w["c_re"][l]), -one(w["c_im"][l])], 0)


ANY = pl.BlockSpec(memory_space=pl.ANY)


def _ag8(name, xb):
    def body(x_ref, out_ref, send_sems, recv_sems, local_sem):
        x, y, c = lax.axis_index("x"), lax.axis_index("y"), lax.axis_index("c")
        me, sibling = (x, y, c), (x, y, 1 - c)
        chips = [(1 - x, y), (x, 1 - y), (1 - x, 1 - y)]

        def rows(px, py, pc):
            return out_ref.at[4 * px + 2 * py + pc]

        def copy(k, block, to, src=None):
            return pltpu.make_async_remote_copy(src_ref=rows(*block) if src is None else src, dst_ref=rows(*block),
                                                send_sem=send_sems.at[k], recv_sem=recv_sems.at[k],
                                                device_id=to, device_id_type=MESH)

        mine = pltpu.make_async_copy(x_ref, rows(*me), local_sem)
        mine.start()
        first = [copy(0, me, sibling, src=x_ref)]
        first += [copy(1 + j, me, (*chip, c), src=x_ref) for j, chip in enumerate(chips)]
        for cp in first:
            cp.start()
        passed = [copy(4 + j, (*chip, c), sibling) for j, chip in enumerate(chips)]
        for j, chip in enumerate(chips):
            copy(1 + j, (*chip, c), me).wait_recv()
            passed[j].start()
        copy(0, sibling, me).wait_recv()
        for j, chip in enumerate(chips):
            copy(4 + j, (*chip, 1 - c), me).wait_recv()
        for cp in first + passed:
            cp.wait_send()
        mine.wait()

    return pl.pallas_call(body, out_shape=jax.ShapeDtypeStruct((8,) + xb.shape, xb.dtype), in_specs=[ANY], out_specs=ANY,
                          scratch_shapes=[pltpu.SemaphoreType.DMA((7,)), pltpu.SemaphoreType.DMA((7,)),
                                          pltpu.SemaphoreType.DMA(())], name=name)(xb)


def _mesh_place():
    x, y, c = lax.axis_index("x"), lax.axis_index("y"), lax.axis_index("c")
    return x, y, c, [(1 - x, y), (x, 1 - y), (1 - x, 1 - y)]


ICI_CHUNK_BYTES = 2 << 20
D2D_CHUNK_BYTES = 1 << 20


def _split_rows(rows, row_bytes, chunk_bytes):
    k = 1
    for cand in range(1, max(1, (rows * row_bytes) // chunk_bytes) + 1):
        if rows % cand == 0 and (rows // cand) % SUBLANES_BF16 == 0:
            k = cand
    return [(j * (rows // k), rows // k) for j in range(k)]


def _gather_weights(wb, l):
    nw = len(wb)
    halves = [a.shape[1] // 2 for a in wb]
    pieces = [_split_rows(h, a.shape[2] * a.dtype.itemsize, ICI_CHUNK_BYTES) for a, h in zip(wb, halves)]

    def body(*refs):
        ins, outs = refs[:nw], refs[nw:2 * nw]
        send_sems, recv_sems, local_sems = refs[2 * nw:]
        x, y, c, chips = _mesh_place()
        me = 2 * x + y

        def rows(ref, i, hc, r0=0, n=None):
            return ref.at[pl.ds(hc * halves[i] + r0, halves[i] if n is None else n)]

        def copy(i, k, src, dst, to):
            return pltpu.make_async_remote_copy(src_ref=src, dst_ref=dst, send_sem=send_sems.at[6 * i + k],
                                                recv_sem=recv_sems.at[6 * i + k], device_id=to, device_id_type=MESH)

        for i in range(nw):
            for hc in range(2):
                for r0, n in pieces[i]:
                    pltpu.make_async_copy(rows(ins[i].at[l], i, hc, r0, n), rows(outs[i].at[me], i, hc, r0, n),
                                          local_sems.at[i]).start()
        for i in range(nw):
            for k, (px, py) in enumerate(chips):
                for r0, n in pieces[i]:
                    copy(i, k, rows(ins[i].at[l], i, c, r0, n), rows(outs[i].at[me], i, c, r0, n), (px, py, c)).start()
        for k, (px, py) in enumerate(chips):
            for i in range(nw):
                src = outs[i].at[2 * px + py]
                copy(i, k, rows(src, i, c), rows(src, i, c), (px, py, c)).wait_recv()
                for r0, n in pieces[i]:
                    copy(i, 3 + k, rows(src, i, c, r0, n), rows(src, i, c, r0, n), (x, y, 1 - c)).start()
        for k, (px, py) in enumerate(chips):
            for i in range(nw):
                other = rows(outs[i].at[2 * px + py], i, 1 - c)
                copy(i, 3 + k, other, other, (x, y, 1 - c)).wait_recv()
        for i in range(nw):
            whole = rows(outs[i].at[me], i, c)
            for k in range(6):
                copy(i, k, whole, whole, (x, y, 1 - c)).wait_send()
            pltpu.make_async_copy(ins[i].at[l], outs[i].at[me], local_sems.at[i]).wait()

    return pl.pallas_call(body, out_shape=[jax.ShapeDtypeStruct((4,) + a.shape[1:], a.dtype) for a in wb],
                          in_specs=[ANY] * nw, out_specs=[ANY] * nw,
                          scratch_shapes=[pltpu.SemaphoreType.DMA((6 * nw,)), pltpu.SemaphoreType.DMA((6 * nw,)),
                                          pltpu.SemaphoreType.DMA((nw,))], name="gather_weights")(*wb)


def _grad_core_swap(g4):
    nw = len(g4)
    halves = [a.shape[1] // 2 for a in g4]
    pieces = [_split_rows(h, a.shape[2] * a.dtype.itemsize, D2D_CHUNK_BYTES) for a, h in zip(g4, halves)]

    def body(*refs):
        ins, theirs = refs[:nw], refs[nw:2 * nw]
        send_sems, recv_sems = refs[2 * nw:]
        x, y, c, _ = _mesh_place()

        def copy(i, src, dst):
            return pltpu.make_async_remote_copy(src_ref=src, dst_ref=dst, send_sem=send_sems.at[i], recv_sem=recv_sems.at[i],
                                                device_id=(x, y, 1 - c), device_id_type=MESH)

        for i in range(nw):
            for j in range(4):
                for r0, n in pieces[i]:
                    copy(i, ins[i].at[j, pl.ds((1 - c) * halves[i] + r0, n)], theirs[i].at[j, pl.ds(r0, n)]).start()
        for i in range(nw):
            copy(i, ins[i].at[:, pl.ds((1 - c) * halves[i], halves[i])], theirs[i]).wait()

    sds = [jax.ShapeDtypeStruct((4, h) + a.shape[2:], a.dtype) for a, h in zip(g4, halves)]
    return pl.pallas_call(body, out_shape=sds, in_specs=[ANY] * nw, out_specs=[ANY] * nw,
                          scratch_shapes=[pltpu.SemaphoreType.DMA((nw,)), pltpu.SemaphoreType.DMA((nw,))],
                          name="grad_core_swap")(*g4)


def _grad_chip_exchange(t4):
    nw = len(t4)
    pieces = [_split_rows(a.shape[1], a.shape[2] * a.dtype.itemsize, ICI_CHUNK_BYTES) for a in t4]

    def body(*refs):
        ins, outs = refs[:nw], refs[nw:2 * nw]
        send_sems, recv_sems, local_sems = refs[2 * nw:]
        x, y, c, chips = _mesh_place()
        me = 2 * x + y

        def copy(i, k, src, dst, px, py):
            return pltpu.make_async_remote_copy(src_ref=src, dst_ref=dst, send_sem=send_sems.at[3 * i + k],
                                                recv_sem=recv_sems.at[3 * i + k], device_id=(px, py, c),
                                                device_id_type=MESH)

        for i in range(nw):
            for r0, n in pieces[i]:
                for k, (px, py) in enumerate(chips):
                    copy(i, k, ins[i].at[2 * px + py, pl.ds(r0, n)], outs[i].at[me, pl.ds(r0, n)], px, py).start()
                pltpu.make_async_copy(ins[i].at[me, pl.ds(r0, n)], outs[i].at[me, pl.ds(r0, n)], local_sems.at[i]).start()
        for i in range(nw):
            for k, (px, py) in enumerate(chips):
                copy(i, k, ins[i].at[me], outs[i].at[2 * px + py], px, py).wait_recv()
        for i in range(nw):
            for k, (px, py) in enumerate(chips):
                copy(i, k, ins[i].at[me], outs[i].at[me], px, py).wait_send()
            pltpu.make_async_copy(ins[i].at[me], outs[i].at[me], local_sems.at[i]).wait()

    return pl.pallas_call(body, out_shape=[jax.ShapeDtypeStruct(a.shape, a.dtype) for a in t4],
                          in_specs=[ANY] * nw, out_specs=[ANY] * nw,
                          scratch_shapes=[pltpu.SemaphoreType.DMA((3 * nw,)), pltpu.SemaphoreType.DMA((3 * nw,)),
                                          pltpu.SemaphoreType.DMA((nw,))], name="grad_chip_exchange")(*t4)


def _grad_half_swap(full):
    nw = len(full)
    halves = [a.shape[0] // 2 for a in full]
    pieces = [_split_rows(h, a.shape[1] * a.dtype.itemsize, D2D_CHUNK_BYTES) for a, h in zip(full, halves)]

    def body(*refs):
        bufs = refs[nw:2 * nw]
        send_sems, recv_sems = refs[2 * nw:]
        x, y, c, _ = _mesh_place()

        def copy(i, hc, r0, n):
            view = bufs[i].at[pl.ds(hc * halves[i] + r0, n)]
            return pltpu.make_async_remote_copy(src_ref=view, dst_ref=view, send_sem=send_sems.at[i], recv_sem=recv_sems.at[i],
                                                device_id=(x, y, 1 - c), device_id_type=MESH)

        for i in range(nw):
            for r0, n in pieces[i]:
                copy(i, c, r0, n).start()
        for i in range(nw):
            copy(i, c, 0, halves[i]).wait_send()
            copy(i, 1 - c, 0, halves[i]).wait_recv()

    return pl.pallas_call(body, out_shape=[jax.ShapeDtypeStruct(a.shape, a.dtype) for a in full],
                          in_specs=[ANY] * nw, out_specs=[ANY] * nw, input_output_aliases={i: i for i in range(nw)},
                          scratch_shapes=[pltpu.SemaphoreType.DMA((nw,)), pltpu.SemaphoreType.DMA((nw,))],
                          name="grad_half_swap")(*full)


def _core_add(g, theirs, core):
    _, rh, cc = theirs.shape
    tr = _tile(rh, max(SUBLANES_BF16, (1 << 19) // cc), SUBLANES_BF16)
    nt = rh // tr

    def body(c_ref, a_ref, b_ref, o_ref):
        o_ref[...] = (a_ref[...].astype(F32) + b_ref[...].astype(F32)).astype(o_ref.dtype)

    blk = (None, tr, cc)
    return pl.pallas_call(
        body, out_shape=jax.ShapeDtypeStruct(theirs.shape, theirs.dtype),
        grid_spec=pltpu.PrefetchScalarGridSpec(
            num_scalar_prefetch=1, grid=(4, nt),
            in_specs=[pl.BlockSpec(blk, lambda j, t, c_ref: (j, c_ref[0] * nt + t, 0)),
                      pl.BlockSpec(blk, lambda j, t, c_ref: (j, t, 0))],
            out_specs=pl.BlockSpec(blk, lambda j, t, c_ref: (j, t, 0))),
        compiler_params=_cparams(2), name="grad_core_add")(core, g, theirs)


def _sum_to_half(xb, core):
    n, rh, cc = xb.shape
    tr = _tile(rh, max(SUBLANES_BF16, (1 << 18) // cc), SUBLANES_BF16)
    nt = rh // tr

    def body(c_ref, x_ref, o_ref):
        acc = x_ref[0].astype(F32)
        for k in range(1, n):
            acc = acc + x_ref[k].astype(F32)
        o_ref[...] = acc

    return pl.pallas_call(
        body, out_shape=jax.ShapeDtypeStruct((2 * rh, cc), F32),
        grid_spec=pltpu.PrefetchScalarGridSpec(
            num_scalar_prefetch=1, grid=(nt,),
            in_specs=[pl.BlockSpec((n, tr, cc), lambda t, c_ref: (0, t, 0))],
            out_specs=pl.BlockSpec((tr, cc), lambda t, c_ref: (c_ref[0] * nt + t, 0))),
        compiler_params=_cparams(1), name="grad_chip_sum")(core, xb)


def _sum_lead(name, xb, out_dtype=F32):
    n, rows, w = xb.shape
    tr = _tile(rows, max(SUBLANES_BF16, (1 << 18) // w), SUBLANES_BF16)

    def body(x_ref, o_ref):
        acc = x_ref[0].astype(F32)
        for k in range(1, n):
            acc = acc + x_ref[k].astype(F32)
        o_ref[...] = acc.astype(o_ref.dtype)

    return pl.pallas_call(body, out_shape=jax.ShapeDtypeStruct((rows, w), out_dtype), grid=(rows // tr,),
                          in_specs=[pl.BlockSpec((n, tr, w), lambda i: (0, i, 0))],
                          out_specs=pl.BlockSpec((tr, w), lambda i: (i, 0)), compiler_params=_cparams(1), name=name)(xb)


def _pad_to(v, mult):
    n = v.shape[-1]
    pad = (-n) % mult
    return v if pad == 0 else jnp.pad(v, [(0, 0)] * (v.ndim - 1) + [(0, pad)])


RELAYOUT_ROWS = 256


def _cols_from_shards(a):
    _, k, cc = a.shape
    tr = _tile(k, RELAYOUT_ROWS, SUBLANES_BF16)

    def body(i_ref, o_ref):
        for j in range(4):
            o_ref[:, j * cc:(j + 1) * cc] = i_ref[j]

    return pl.pallas_call(body, out_shape=jax.ShapeDtypeStruct((k, 4 * cc), a.dtype), grid=(k // tr,),
                          in_specs=[pl.BlockSpec((4, tr, cc), lambda i: (0, i, 0))],
                          out_specs=pl.BlockSpec((tr, 4 * cc), lambda i: (i, 0)),
                          compiler_params=_cparams(1), name="cols_from_shards")(a)


def _cols_to_shards(a):
    k, n = a.shape
    cc = n // 4
    tr = _tile(k, RELAYOUT_ROWS, SUBLANES_BF16)

    def body(i_ref, o_ref):
        for j in range(4):
            o_ref[j] = i_ref[:, j * cc:(j + 1) * cc]

    return pl.pallas_call(body, out_shape=jax.ShapeDtypeStruct((4, k, cc), a.dtype), grid=(k // tr,),
                          in_specs=[pl.BlockSpec((tr, n), lambda i: (i, 0))],
                          out_specs=pl.BlockSpec((4, tr, cc), lambda i: (0, i, 0)),
                          compiler_params=_cparams(1), name="cols_to_shards")(a)


def _gather_layer_weights(wb, l):
    got = _gather_weights([wb[n] for n in BIG_NAMES], l)
    out = {}
    for (n, kind), a in zip(BIG, got):
        if kind == "row":
            out[n] = a.reshape(4 * a.shape[1], a.shape[2])
        else:
            out[n] = _cols_from_shards(a) if n == "w_in" else a
    return out


def _reduce_layer_grads(g):
    g4 = []
    for n, kind in BIG:
        a = g[n]
        if kind == "row":
            a = a.reshape(4, a.shape[0] // 4, a.shape[1])
        elif a.ndim == 2:
            a = _cols_to_shards(a)
        g4.append(a)
    core = lax.axis_index("c").astype(jnp.int32).reshape(1)
    theirs = _grad_core_swap(g4)
    arrived = _grad_chip_exchange([_core_add(a, b, core) for a, b in zip(g4, theirs)])
    full = _grad_half_swap([_sum_to_half(a, core) for a in arrived])
    return dict(zip(BIG_NAMES, full))


def _adam_fn(w, g, m, v):
    m = ADAM_B1 * m + (1.0 - ADAM_B1) * g
    v = ADAM_B2 * v + (1.0 - ADAM_B2) * (g * g)
    m_hat = m / (1.0 - ADAM_B1 ** ADAM_STEP)
    v_hat = v / (1.0 - ADAM_B2 ** ADAM_STEP)
    return -ADAM_LR * (m_hat / (jnp.sqrt(v_hat) + ADAM_EPS) + ADAM_WD * w), m, v


def _adamw(name, w, g, m, v):
    shape = w.shape
    cols = shape[-1]
    f = lambda a: a.reshape(-1, cols)
    rows = f(w).shape[0]
    tr = max(8, (1 << 19) // cols)
    d, nm, nv = _ew(name, _adam_fn, rows, [R(f(w)), R(f(g)), R(f(m)), R(f(v))], [(cols, F32)] * 3, tr=tr)
    return d.reshape(shape), nm.reshape(shape), nv.reshape(shape)


def _layer_fwd(x, wl, bias, alpha):
    s, d = x.shape
    d_ff = wl["w_ffn_out"].shape[0]
    b_in = wl["b_in"][None, :]
    qkv = _mm_nn("proj_qkv", x, wl["w_in"], n0=0, n=3 * QKV_W, bias=b_in, out_dtype=BF16, tn=768)
    rest = _mm_nn("proj_rest", x, wl["w_in"], n0=3 * QKV_W, bias=b_in, tn=768)
    qkv_s = _to_streams(qkv)
    o_s, l_s = _attn_fwd(qkv_s, bias)
    o_t, l_t = _streams_to_tok(o_s), _streams_to_tok(l_s)
    ya, lse = _ew("attn_combine", _f_combine, s, [R(a) for a in o_t + l_t], [(WIDTH_A, F32), (WIDTH_A, F32)])
    yb = _sgu_fwd(rest, wl["sgu_ln_g"][None], wl["sgu_ln_b"][None], wl["w_tril"], wl["b_col"])
    bu = _ssm_in(rest, REST_UC, wl["bd"])
    xs = _scan_fwd(bu, wl["a3"])
    y_lin = _ssm_out(xs, wl["cd"])
    uc_blk = REST_UC // WIDTH_C
    yc0 = _ew("ssm_skip_gelu", _f_ssm_out, s, [R(y_lin), R(rest, WIDTH_C, uc_blk), P(wl["d_skip"][None])], [(WIDTH_C, F32)])
    t_glu = _mm_nn("glu_proj", yc0, wl["w_glu"], bias=wl["b_glu"][None], tn=768)
    yc = _ew("glu", _f_glu, s, [R(yc0), R(t_glu)], [(WIDTH_C, F32)])
    pa = _mm_nn("proj_a", ya, wl["w_pa"])
    pb = _mm_nn("proj_b", yb, wl["w_pb"])
    pc = _mm_nn("proj_c", yc, wl["w_pc"])
    gw = _tile(math.gcd(d, REST_GL), 256, LANES)
    gl_ins = [R(rest, gw, (REST_GL + i * d) // gw, True) for i in range(N_BRANCH)]
    merged = _ew("merge", _f_merge, s, gl_ins + [R(pa, gw, 0, True), R(pb, gw, 0, True), R(pc, gw, 0, True)],
                 [(gw, F32)], nj=d // gw, tr=512)
    h1 = _mm_nn("proj_o", merged, wl["w_o"], add=x, add_scale=alpha)
    x1 = _ew("ln1", _f_ln, s, [R(h1), P(wl["ln1_g"][None]), P(wl["ln1_b"][None])], [(d, F32)])
    ff = _mm_nn("ffn_in", x1, wl["w_ffn_in"], tn=1408)
    fw = _tile(d_ff, 512, LANES)
    act = _ew("swiglu", _f_swiglu, s, [R(ff, fw, 0, True), R(ff, fw, d_ff // fw, True)], [(fw, F32)], nj=d_ff // fw, tr=512)
    h2 = _mm_nn("ffn_out", act, wl["w_ffn_out"], add=x1, add_scale=alpha, tk=1408)
    x2 = _ew("ln2", _f_ln, s, [R(h2), P(wl["ln2_g"][None]), P(wl["ln2_b"][None])], [(d, F32)])
    saved = dict(x=x, qkv_s=qkv_s, rest=rest, ya=ya, lse=lse, yb=yb, xs=xs, y_lin=y_lin, yc0=yc0, t_glu=t_glu, yc=yc,
                 pa=pa, pb=pb, pc=pc, merged=merged, h1=h1, x1=x1, ff=ff, act=act, h2=h2)
    return x2, saved


def _vjp_rows(f, n_primal):
    def fn(*args):
        n_ct = len(args) - n_primal
        cts, primals = args[:n_ct], args[n_ct:]
        out, vjp = jax.vjp(f, *primals)
        ct = tuple(c.astype(F32) for c in cts)
        return vjp(ct if isinstance(out, (tuple, list)) else ct[0])
    return fn


def _layer_bwd(dx2, sv, wl, bias, alpha):
    s, d = dx2.shape
    d_ff = wl["w_ffn_out"].shape[0]
    g = {}
    row = lambda a: a[None]
    dh2, g["ln2_g"], g["ln2_b"] = _ew("ln2_bwd", _vjp_rows(_f_ln, 3), s,
                                      [R(dx2), R(sv["h2"]), P(row(wl["ln2_g"])), P(row(wl["ln2_b"]))],
                                      [(d, F32)], accs=[d, d])
    g["w_ffn_out"] = _mm_tn("ffn_out_wgrad", sv["act"], dh2)
    dact = _mm_nt("ffn_out_dgrad", dh2, wl["w_ffn_out"], tn=1408, tk=2048)
    fw = _tile(d_ff, 512, LANES)
    nf = d_ff // fw
    dgf, dup = _ew("swiglu_bwd", _vjp_rows(_f_swiglu, 2), s,
                   [R(dact, fw, 0, True), R(sv["ff"], fw, 0, True), R(sv["ff"], fw, nf, True)],
                   [(fw, BF16), (fw, BF16)], nj=nf, tr=512)
    dff = jnp.concatenate([dgf, dup], axis=1)
    g["w_ffn_in"] = _mm_tn("ffn_in_wgrad", sv["x1"], dff, tn=1408, col_shards=True)
    dx1 = _mm_nt("ffn_in_dgrad", dff, wl["w_ffn_in"], add=dh2, add_scale=alpha, tk=1408)
    dh1, g["ln1_g"], g["ln1_b"] = _ew("ln1_bwd", _vjp_rows(_f_ln, 3), s,
                                      [R(dx1), R(sv["h1"]), P(row(wl["ln1_g"])), P(row(wl["ln1_b"]))],
                                      [(d, F32)], accs=[d, d])
    g["w_o"] = _mm_tn("proj_o_wgrad", sv["merged"], dh1)
    dmerged = _mm_nt("proj_o_dgrad", dh1, wl["w_o"])
    rest = sv["rest"]
    gw = _tile(math.gcd(d, REST_GL), 256, LANES)
    gl_ins = [R(rest, gw, (REST_GL + i * d) // gw, True) for i in range(N_BRANCH)]
    dg0, dg1, dg2, dpa, dpb, dpc = _ew(
        "merge_bwd", _vjp_rows(_f_merge, 6), s,
        [R(dmerged, gw, 0, True)] + gl_ins + [R(sv[k], gw, 0, True) for k in ("pa", "pb", "pc")],
        [(gw, BF16)] * 6, nj=d // gw, tr=512)
    g["w_pa"] = _mm_tn("proj_a_wgrad", sv["ya"], dpa, col_shards=True)
    g["w_pb"] = _mm_tn("proj_b_wgrad", sv["yb"], dpb, col_shards=True)
    g["w_pc"] = _mm_tn("proj_c_wgrad", sv["yc"], dpc, col_shards=True)
    dya = _mm_nt("proj_a_dgrad", dpa, wl["w_pa"], tn=512)
    dyb = _mm_nt("proj_b_dgrad", dpb, wl["w_pb"], tn=768)
    dyc = _mm_nt("proj_c_dgrad", dpc, wl["w_pc"], tn=768)
    dyc0_a, dt_glu, g["b_glu"] = _ew("glu_bwd", lambda ct, a, t: (lambda r: (r[0], r[1], _rowsum(r[1])))(_vjp_rows(_f_glu, 2)(ct, a, t)),
                                     s, [R(dyc), R(sv["yc0"]), R(sv["t_glu"])], [(WIDTH_C, F32), (WIDTH_C, F32)], accs=[WIDTH_C])
    g["w_glu"] = _mm_tn("glu_wgrad", sv["yc0"], dt_glu)
    dyc0 = _mm_nt("glu_dgrad", dt_glu, wl["w_glu"], add=dyc0_a, tn=768)
    uc_blk = REST_UC // WIDTH_C
    dy_lin, duc_skip, g["d_skip"] = _ew("ssm_skip_gelu_bwd", _vjp_rows(_f_ssm_out, 3), s,
                                        [R(dyc0), R(sv["y_lin"]), R(rest, WIDTH_C, uc_blk), P(row(wl["d_skip"]))],
                                        [(WIDTH_C, F32), (WIDTH_C, F32)], accs=[WIDTH_C])
    d_cd = _ssm_out_wgrad(sv["xs"], dy_lin)
    dxs = _ssm_out_dgrad(dy_lin, wl["cd"])
    gs, da = _scan_bwd(dxs, sv["xs"], wl["a3"])
    d_bd = _ssm_in_wgrad(rest, REST_UC, gs)
    duc_lin = _ssm_in_dgrad(gs, wl["bd"])
    duc = _ew("ssm_duc", lambda a, b: a + b, s, [R(duc_lin), R(duc_skip)], [(WIDTH_C, BF16)])
    d_abr = da[0].reshape(N_GROUPS_C, SSM_STATE)
    d_abi = da[1].reshape(N_GROUPS_C, SSM_STATE)
    d_lr, d_li, d_ld, d_br_t, d_bi_t = _ssm_params_bwd(wl["ssm_pin"], d_abr, d_abi,
                                                       _block_diag_extract(d_bd[0]), _block_diag_extract(d_bd[1]))
    g["lam_re"], g["lam_im"], g["log_dt"] = d_lr, d_li, d_ld[:, 0]
    un_t = lambda a: jnp.transpose(a.reshape(N_GROUPS_C, SSM_GROUP, SSM_STATE), (0, 2, 1))
    g["b_re"], g["b_im"] = un_t(d_br_t), un_t(d_bi_t)
    cd_ex = lambda a: _block_diag_extract(jnp.transpose(a, (0, 2, 1))).reshape(N_GROUPS_C, SSM_GROUP, SSM_STATE)
    g["c_re"], g["c_im"] = cd_ex(d_cd[0]), -cd_ex(d_cd[1])
    dzu, dzv, dws, dbs, g["sgu_ln_g"], g["sgu_ln_b"] = _sgu_bwd(rest, row(wl["sgu_ln_g"]), row(wl["sgu_ln_b"]),
                                                                  wl["w_tril"], wl["b_col"], dyb)
    g["w_s"] = jnp.tril(dws)
    g["b_s"] = dbs[:, :, 0]
    do_s, ya_s, lse_s = _tok_to_streams(dya), _tok_to_streams(sv["ya"]), _tok_to_streams(sv["lse"])
    dq, dk, dv, dbias = _attn_bwd(sv["qkv_s"], bias, do_s, ya_s, lse_s)
    dqkv = jnp.stack([jnp.stack(_streams_to_tok(t), 1) for t in (dq, dk, dv)], 1)
    dproj = jnp.concatenate([dqkv.reshape(s, 3 * QKV_W).astype(BF16), dzu, dzv, duc, dg0, dg1, dg2], axis=1)
    n_in = dproj.shape[1]
    cw = _tile(n_in, 1024, LANES)
    g["b_in"] = _ew("b_in_grad", lambda a: _rowsum(a.astype(F32)), s, [R(dproj, cw, 0, True)], [], accs=[cw],
                    nj=n_in // cw, tr=512)
    g["w_in"] = _mm_tn("proj_in_wgrad", sv["x"], dproj, tn=768)
    dx = _mm_nt("proj_in_dgrad", dproj, wl["w_in"], add=dh1, add_scale=alpha, tk=768)
    for k in ("ln2_g", "ln2_b", "ln1_g", "ln1_b", "b_glu", "d_skip", "sgu_ln_g", "sgu_ln_b", "b_in"):
        g[k] = g[k][0]
    return dx, g, dbias


def _loss_head(y, target):
    s, d = y.shape

    def fn(yb, tb):
        err = yb - tb
        return err / d, _rowsum(err * err)

    dy, sq = _ew("loss_head", fn, s, [R(y), R(target)], [(d, F32)], accs=[d])
    return dy, 0.5 * jnp.sum(sq) / d


def _step(x, target, w, m, v):
    depth = w["w_in"].shape[0]
    alpha = (2 * depth) ** 0.25
    bias = _band_bias(w["rel_bias"])
    wb = {n: w[n].astype(BF16) for n in BIG_NAMES}
    xl = x[0]
    layers, saved = [], []
    for l in range(depth):
        wl = _gather_layer_weights(wb, l)
        for n in SMALL:
            if n != "rel_bias":
                wl[n] = w[n][l]
        wl["w_tril"] = jnp.tril(w["w_s"][l]).astype(BF16)
        wl["b_col"] = w["b_s"][l][:, :, None]
        pin = _ssm_param_inputs(w, l)
        abr, abi, bbr, bbi = _ssm_params_fwd(pin)
        wl["ssm_pin"] = pin
        wl["a3"] = jnp.stack([abr.reshape(-1, LANES), abi.reshape(-1, LANES)], 0)
        wl["bd"] = jnp.stack([_block_diag(bbr), _block_diag(bbi)], 0)
        wl["cd"] = _c_block_diag(w, l)
        xl, sv = _layer_fwd(xl, wl, bias, alpha)
        layers.append(wl)
        saved.append(sv)
    dx, loss = _loss_head(xl, target[0])
    loss = lax.psum(loss, ("x", "y", "c"))
    grads = {n: [None] * depth for n in WEIGHTS if n != "rel_bias"}
    dbias_sum = None
    for l in reversed(range(depth)):
        dx, g, dbias = _layer_bwd(dx, saved[l], layers[l], bias, alpha)
        dbias_sum = dbias if dbias_sum is None else dbias_sum + dbias
        red = _reduce_layer_grads({n: g[n] for n in BIG_NAMES})
        for n in BIG_NAMES:
            grads[n][l] = red[n]
        for n in SMALL:
            if n != "rel_bias":
                grads[n][l] = g[n]
    grads = {n: jnp.stack(gl, 0) for n, gl in grads.items()}
    grads["rel_bias"] = _bias_to_buckets(dbias_sum)
    small_vec = _pad_to(jnp.concatenate([grads[n].reshape(-1) for n in SMALL]), 8 * LANES).reshape(-1, LANES)
    small_sum = _sum_lead("small_grad_sum", _ag8("small_grad_gather", small_vec)).reshape(-1)
    off = 0
    for n in SMALL:
        size = math.prod(w[n].shape)
        grads[n] = small_sum[off:off + size].reshape(w[n].shape)
        off += size
    pack = lambda t: _pad_to(jnp.concatenate([t[n].reshape(-1) for n in SMALL]), 8 * LANES).reshape(-1, LANES)
    sd, sm, sv_ = _adamw("adamw_small", pack(w), small_sum.reshape(-1, LANES), pack(m), pack(v))
    delta, new_m, new_v = {}, {}, {}
    off = 0
    for n in SMALL:
        size = math.prod(w[n].shape)
        take = lambda t: t.reshape(-1)[off:off + size].reshape(w[n].shape)
        delta[n], new_m[n], new_v[n] = take(sd), take(sm), take(sv_)
        off += size
    for n in BIG_NAMES:
        delta[n], new_m[n], new_v[n] = _adamw("adamw_" + n, w[n], grads[n], m[n], v[n])
    return loss, dx[None], grads, delta, new_m, new_v


def kernel(x, w_in, b_in, rel_bias, sgu_ln_g, sgu_ln_b, w_s, b_s, lam_re, lam_im, log_dt, b_re, b_im, c_re, c_im, d_skip, w_glu, b_glu, w_pa, w_pb, w_pc, w_o, ln1_g, ln1_b, w_ffn_in, w_ffn_out, ln2_g, ln2_b, loss_target, m_w_in, m_b_in, m_rel_bias, m_sgu_ln_g, m_sgu_ln_b, m_w_s, m_b_s, m_lam_re, m_lam_im, m_log_dt, m_b_re, m_b_im, m_c_re, m_c_im, m_d_skip, m_w_glu, m_b_glu, m_w_pa, m_w_pb, m_w_pc, m_w_o, m_ln1_g, m_ln1_b, m_w_ffn_in, m_w_ffn_out, m_ln2_g, m_ln2_b, v_w_in, v_b_in, v_rel_bias, v_sgu_ln_g, v_sgu_ln_b, v_w_s, v_b_s, v_lam_re, v_lam_im, v_log_dt, v_b_re, v_b_im, v_c_re, v_c_im, v_d_skip, v_w_glu, v_b_glu, v_w_pa, v_w_pb, v_w_pc, v_w_o, v_ln1_g, v_ln1_b, v_w_ffn_in, v_w_ffn_out, v_ln2_g, v_ln2_b):
    w = dict(w_in=w_in, b_in=b_in, rel_bias=rel_bias, sgu_ln_g=sgu_ln_g, sgu_ln_b=sgu_ln_b, w_s=w_s, b_s=b_s,
             lam_re=lam_re, lam_im=lam_im, log_dt=log_dt, b_re=b_re, b_im=b_im, c_re=c_re, c_im=c_im, d_skip=d_skip,
             w_glu=w_glu, b_glu=b_glu, w_pa=w_pa, w_pb=w_pb, w_pc=w_pc, w_o=w_o, ln1_g=ln1_g, ln1_b=ln1_b,
             w_ffn_in=w_ffn_in, w_ffn_out=w_ffn_out, ln2_g=ln2_g, ln2_b=ln2_b)
    m = dict(w_in=m_w_in, b_in=m_b_in, rel_bias=m_rel_bias, sgu_ln_g=m_sgu_ln_g, sgu_ln_b=m_sgu_ln_b, w_s=m_w_s,
             b_s=m_b_s, lam_re=m_lam_re, lam_im=m_lam_im, log_dt=m_log_dt, b_re=m_b_re, b_im=m_b_im, c_re=m_c_re,
             c_im=m_c_im, d_skip=m_d_skip, w_glu=m_w_glu, b_glu=m_b_glu, w_pa=m_w_pa, w_pb=m_w_pb, w_pc=m_w_pc,
             w_o=m_w_o, ln1_g=m_ln1_g, ln1_b=m_ln1_b, w_ffn_in=m_w_ffn_in, w_ffn_out=m_w_ffn_out, ln2_g=m_ln2_g,
             ln2_b=m_ln2_b)
    v = dict(w_in=v_w_in, b_in=v_b_in, rel_bias=v_rel_bias, sgu_ln_g=v_sgu_ln_g, sgu_ln_b=v_sgu_ln_b, w_s=v_w_s,
             b_s=v_b_s, lam_re=v_lam_re, lam_im=v_lam_im, log_dt=v_log_dt, b_re=v_b_re, b_im=v_b_im, c_re=v_c_re,
             c_im=v_c_im, d_skip=v_d_skip, w_glu=v_w_glu, b_glu=v_b_glu, w_pa=v_w_pa, w_pb=v_w_pb, w_pc=v_w_pc,
             w_o=v_w_o, ln1_g=v_ln1_g, ln1_b=v_ln1_b, w_ffn_in=v_w_ffn_in, w_ffn_out=v_w_ffn_out, ln2_g=v_ln2_g,
             ln2_b=v_ln2_b)
    loss, grad_x, grads, delta, new_m, new_v = _step(x, loss_target, w, m, v)
    return (loss, grad_x, *[grads[n] for n in WEIGHTS], *[delta[n] for n in WEIGHTS],
            *[new_m[n] for n in WEIGHTS], *[new_v[n] for n in WEIGHTS])
```

```python
import functools
import math

import numpy as np
import jax
import jax.numpy as jnp
from jax import lax
from jax.experimental import pallas as pl
from jax.experimental.pallas import tpu as pltpu

F32 = jnp.float32
BF16 = jnp.bfloat16
MESH = pl.DeviceIdType.MESH

ATT_PATTERNS = ((128, 1), (512, 4), (2048, 16))
N_GROUPS_A = 3
HEADS = 8
HEAD_DIM = 64
WIDTH_A = HEADS * HEAD_DIM
QKV_W = N_GROUPS_A * WIDTH_A
ATT_BLOCK = 128
N_REL_BUCKETS = 32
REL_MAX_DIST = 2048
NEG_INF = -1e30
CHUNK = 128
WIDTH_B = 768
N_GROUPS_B = 6
GROUP_B = 128
WIDTH_C = 768
SSM_GROUP = 16
N_GROUPS_C = 48
SSM_STATE = 64
SSM_CH = N_GROUPS_C * SSM_STATE
N_BRANCH = 3
LN_EPS = 1e-5
ADAM_LR = 0.001
ADAM_B1 = 0.9
ADAM_B2 = 0.999
ADAM_EPS = 1e-08
ADAM_WD = 0.01
ADAM_STEP = 10

LANES = 128
SUBLANES_BF16 = 16
V7X_VMEM_BYTES = 64 * 1024 * 1024
VMEM_LIMIT = V7X_VMEM_BYTES * 7 // 8

OFF_ZB = 3 * QKV_W
REST_ZB = 0
REST_UC = 2 * WIDTH_B
REST_GL = 2 * WIDTH_B + WIDTH_C

BIG = (("w_in", "col"), ("w_glu", "row"), ("w_pa", "col"), ("w_pb", "col"), ("w_pc", "col"),
       ("w_o", "row"), ("w_ffn_in", "col"), ("w_ffn_out", "row"))
WEIGHTS = ("w_in", "b_in", "rel_bias", "sgu_ln_g", "sgu_ln_b", "w_s", "b_s", "lam_re", "lam_im", "log_dt",
           "b_re", "b_im", "c_re", "c_im", "d_skip", "w_glu", "b_glu", "w_pa", "w_pb", "w_pc", "w_o",
           "ln1_g", "ln1_b", "w_ffn_in", "w_ffn_out", "ln2_g", "ln2_b")
BIG_NAMES = tuple(n for n, _ in BIG)
SMALL = tuple(n for n in WEIGHTS if n not in BIG_NAMES)


def _tile(n, target, mult):
    t = (min(target, n) // mult) * mult
    while t >= mult:
        if n % t == 0:
            return t
        t -= mult
    return n


def _cparams(n_axes):
    return pltpu.CompilerParams(dimension_semantics=("arbitrary",) * n_axes, vmem_limit_bytes=VMEM_LIMIT)


_DIMS = {"nn": (((1,), (0,)), ((), ())), "nt": (((1,), (1,)), ((), ())), "tn": (((0,), (0,)), ((), ()))}


def _mm(name, mode, grid, a, a_spec, b, b_spec, out_sds, o_spec, acc_shape, *, bias=None, bias_spec=None,
        add=None, add_spec=None, add_scale=1.0, exact=False, side=None):
    nk = grid[2]
    has_bias = bias is not None
    has_add = add is not None
    n_side_in = len(side["ins"]) if side else 0
    n_side_out = len(side["out_sds"]) if side else 0
    n_side_sem = len(side["sems"]) if side else 0

    def body(*refs):
        a_ref, b_ref = refs[0], refs[1]
        pos = 2
        bias_ref = add_ref = None
        if has_bias:
            bias_ref = refs[pos]
            pos += 1
        if has_add:
            add_ref = refs[pos]
            pos += 1
        side_ins = refs[pos:pos + n_side_in]
        pos += n_side_in
        o_ref = refs[pos]
        side_outs = refs[pos + 1:pos + 1 + n_side_out]
        side_sems = refs[len(refs) - n_side_sem:] if n_side_sem else ()
        pos += n_side_out
        if side:
            at = [pl.program_id(d) for d in range(3)]

            @pl.when((at[0] == 0) & (at[1] == 0) & (at[2] == 0))
            def _():
                side["start"](side_ins, side_outs, side_sems)
        if exact:
            part = lax.dot_general(a_ref[...].astype(F32), b_ref[...].astype(F32), _DIMS[mode],
                                   preferred_element_type=F32, precision=lax.Precision.HIGHEST)
        else:
            part = lax.dot_general(a_ref[...].astype(BF16), b_ref[...].astype(BF16), _DIMS[mode],
                                   preferred_element_type=F32)

        def finish(r):
            if has_bias:
                r = r + bias_ref[...]
            if has_add:
                r = r + add_scale * add_ref[...].astype(F32)
            o_ref[...] = r.astype(o_ref.dtype)

        if nk == 1:
            finish(part)
        else:
            acc_ref = refs[pos + 1]
            k = pl.program_id(2)

            @pl.when(k == 0)
            def _():
                acc_ref[...] = part

            @pl.when(k > 0)
            def _():
                acc_ref[...] += part

            @pl.when(k == nk - 1)
            def _():
                finish(acc_ref[...])

        if side:
            @pl.when((at[0] == grid[0] - 1) & (at[1] == grid[1] - 1) & (at[2] == grid[2] - 1))
            def _():
                side["finish"](side_ins, side_outs, side_sems)

    ins, specs = [a, b], [a_spec, b_spec]
    if has_bias:
        ins.append(bias)
        specs.append(bias_spec)
    if has_add:
        ins.append(add)
        specs.append(add_spec)
    scratch = [pltpu.VMEM(acc_shape, F32)] if nk > 1 else []
    if not side:
        return pl.pallas_call(body, out_shape=out_sds, grid=grid, in_specs=specs, out_specs=o_spec,
                              scratch_shapes=scratch, compiler_params=_cparams(3), name=name)(*ins)
    res = pl.pallas_call(body, out_shape=[out_sds] + list(side["out_sds"]), grid=grid,
                         in_specs=specs + [ANY] * n_side_in, out_specs=[o_spec] + [ANY] * n_side_out,
                         scratch_shapes=scratch + list(side["sems"]), compiler_params=_cparams(3),
                         name=name + "_" + side["name"])(*ins, *side["ins"])
    return res[0], res[1:]


def _mm_nn(name, a, b, *, n0=0, n=None, a0=0, bias=None, add=None, add_scale=1.0, out_dtype=F32,
           tm=1024, tn=512, tk=None, side=None):
    m = a.shape[0]
    if b.ndim == 3:
        k, cc = b.shape[1:]
        n = 4 * cc
        tn = _tile(cc, tn, LANES)
        per = cc // tn
        tk = _tile(k, k if tk is None else tk, LANES)
        b_spec = pl.BlockSpec((None, tk, tn), lambda i, j, kk: (j // per, kk, j % per))
    else:
        k = b.shape[0]
        n = b.shape[1] - n0 if n is None else n
        tn = _tile(math.gcd(n, n0) if n0 else n, tn, LANES)
        tk = _tile(math.gcd(k, a0) if a0 else k, k if tk is None else tk, LANES)
        b_spec = pl.BlockSpec((tk, tn), lambda i, j, kk: (kk, n0 // tn + j))
    tm = _tile(m, tm, SUBLANES_BF16)
    jn0, ka0 = n0 // tn, a0 // tk
    o_spec = pl.BlockSpec((tm, tn), lambda i, j, kk: (i, j))
    return _mm(name, "nn", (m // tm, n // tn, k // tk),
               a, pl.BlockSpec((tm, tk), lambda i, j, kk: (i, ka0 + kk)), b, b_spec,
               jax.ShapeDtypeStruct((m, n), out_dtype), o_spec, (tm, tn),
               bias=bias, bias_spec=pl.BlockSpec((1, tn), lambda i, j, kk: (0, jn0 + j)),
               add=add, add_spec=o_spec, add_scale=add_scale, side=side)


def _mm_nt(name, a, b, *, n0=0, add=None, add_scale=1.0, out_dtype=F32, tm=512, tn=2048, tk=512, side=None):
    m, n = a.shape
    if b.ndim == 3:
        k, cc = b.shape[1:]
        tn = _tile(k, tn, LANES)
        tk = _tile(cc, tk, LANES)
        per = cc // tk
        b_spec = pl.BlockSpec((None, tn, tk), lambda i, j, kk: (kk // per, j, kk % per))
    else:
        k = b.shape[0]
        tn = _tile(k, tn, LANES)
        tk = _tile(math.gcd(n, n0) if n0 else n, tk, LANES)
        b_spec = pl.BlockSpec((tn, tk), lambda i, j, kk: (j, n0 // tk + kk))
    tm = _tile(m, tm, SUBLANES_BF16)
    o_spec = pl.BlockSpec((tm, tn), lambda i, j, kk: (i, j))
    return _mm(name, "nt", (m // tm, k // tn, n // tk),
               a, pl.BlockSpec((tm, tk), lambda i, j, kk: (i, kk)), b, b_spec,
               jax.ShapeDtypeStruct((m, k), out_dtype), o_spec, (tm, tn),
               add=add, add_spec=o_spec, add_scale=add_scale, side=side)


def _mm_tn(name, a, b, *, a0=0, ka=None, out_dtype=BF16, tm=2048, tn=1024, tk=1024, col_shards=False, side=None):
    s = a.shape[0]
    ka = a.shape[1] - a0 if ka is None else ka
    n = b.shape[1]
    tm = _tile(math.gcd(ka, a0) if a0 else ka, tm, LANES)
    tk = _tile(s, tk, SUBLANES_BF16)
    ia0 = a0 // tm
    if col_shards:
        cc = n // 4
        tn = _tile(cc, tn, LANES)
        per = cc // tn
        out_sds = jax.ShapeDtypeStruct((4, ka, cc), out_dtype)
        o_spec = pl.BlockSpec((None, tm, tn), lambda i, j, kk: (j // per, i, j % per))
    else:
        tn = _tile(n, tn, LANES)
        out_sds = jax.ShapeDtypeStruct((ka, n), out_dtype)
        o_spec = pl.BlockSpec((tm, tn), lambda i, j, kk: (i, j))
    return _mm(name, "tn", (ka // tm, n // tn, s // tk),
               a, pl.BlockSpec((tk, tm), lambda i, j, kk: (kk, ia0 + i)),
               b, pl.BlockSpec((tk, tn), lambda i, j, kk: (kk, j)), out_sds, o_spec, (tm, tn), side=side)


def R(arr, bw=None, c0=0, j=False):
    return ("r", arr, arr.shape[1] if bw is None else bw, c0, j)


def P(arr, bw=None, c0=0, j=False):
    return ("p", arr, arr.shape[1] if bw is None else bw, c0, j)


def _ew(name, fn, rows, ins, outs, accs=(), *, tr=256, nj=1):
    tr = _tile(rows, tr, SUBLANES_BF16)
    ni = rows // tr
    n_in, n_out, n_acc = len(ins), len(outs), len(accs)

    def spec(kind, bw, c0, follows):
        rows_b = tr if kind == "r" else 1
        if kind == "r":
            return pl.BlockSpec((rows_b, bw), (lambda j, i: (i, c0 + j)) if follows else (lambda j, i: (i, c0)))
        return pl.BlockSpec((rows_b, bw), (lambda j, i: (0, c0 + j)) if follows else (lambda j, i: (0, c0)))

    def body(*refs):
        res = fn(*[r[...] for r in refs[:n_in]])
        res = tuple(res) if isinstance(res, (tuple, list)) else (res,)
        for r, v in zip(refs[n_in:n_in + n_out], res[:n_out]):
            r[...] = v.astype(r.dtype)
        if n_acc:
            i = pl.program_id(1)
            for r, v in zip(refs[n_in + n_out:], res[n_out:]):
                @pl.when(i == 0)
                def _(r=r, v=v):
                    r[...] = v

                @pl.when(i > 0)
                def _(r=r, v=v):
                    r[...] += v

    out_shape = [jax.ShapeDtypeStruct((rows, bw * nj), dt) for bw, dt in outs]
    out_shape += [jax.ShapeDtypeStruct((1, bw * nj), F32) for bw in accs]
    out_specs = [pl.BlockSpec((tr, bw), lambda j, i: (i, j)) for bw, _ in outs]
    out_specs += [pl.BlockSpec((1, bw), lambda j, i: (0, j)) for bw in accs]
    res = pl.pallas_call(body, out_shape=out_shape, grid=(nj, ni),
                         in_specs=[spec(k, bw, c0, f) for k, _, bw, c0, f in ins], out_specs=out_specs,
                         compiler_params=_cparams(2), name=name)(*[a for _, a, _, _, _ in ins])
    return res if len(res) > 1 else res[0]


def _gelu(x):
    c = math.sqrt(2.0 / math.pi)
    return 0.5 * x * (1.0 + jnp.tanh(c * (x + 0.044715 * (x * x * x))))


def _sigmoid(x):
    return lax.logistic(x)


def _f_ln(h, g, b):
    mu = jnp.mean(h, axis=-1, keepdims=True)
    xc = h - mu
    var = jnp.mean(xc * xc, axis=-1, keepdims=True)
    return xc * lax.rsqrt(var + LN_EPS) * g + b


def _f_combine(o1, o2, o3, l1, l2, l3):
    m = jnp.maximum(jnp.maximum(l1, l2), l3)
    e1, e2, e3 = jnp.exp(l1 - m), jnp.exp(l2 - m), jnp.exp(l3 - m)
    den = e1 + e2 + e3
    return (e1 * o1 + e2 * o2 + e3 * o3) / den, m + jnp.log(den)


def _f_ssm_out(y_lin, uc, d_skip):
    return _gelu(y_lin + d_skip * uc)


def _f_glu(yc0, t):
    return yc0 * _sigmoid(t)


def _f_merge(g0, g1, g2, pa, pb, pc):
    return _sigmoid(g0) * pa + _sigmoid(g1) * pb + _sigmoid(g2) * pc


def _f_swiglu(gf, up):
    return gf * _sigmoid(gf) * up


def _f_sgu_pre(zu, zv, g, b):
    return _gelu(zu), _f_ln(_gelu(zv), g, b)


def _f_ssm_params(lr, li, ld, lr_rep, li_rep, ld_rep, br_t, bi_t):
    def disc(lr, li, ld):
        dt = jnp.exp(ld)
        mag = jnp.exp(lr * dt)
        th = li * dt
        abr, abi = mag * jnp.cos(th), mag * jnp.sin(th)
        nrm = lr * lr + li * li
        cr = ((abr - 1.0) * lr + abi * li) / nrm
        ci = (abi * lr - (abr - 1.0) * li) / nrm
        return abr, abi, cr, ci

    abr, abi, _, _ = disc(lr, li, ld)
    _, _, cr, ci = disc(lr_rep, li_rep, ld_rep)
    return abr, abi, cr * br_t - ci * bi_t, cr * bi_t + ci * br_t


def _rowsum(x):
    return jnp.sum(x, axis=0, keepdims=True)


def _to_streams(qkv):
    s = qkv.shape[0]
    t = qkv.reshape(s, 3, N_GROUPS_A, HEADS, HEAD_DIM)
    outs = []
    for g, (_, dil) in enumerate(ATT_PATTERNS):
        tg = t[:, :, g].reshape(s // dil, dil, 3, HEADS, HEAD_DIM)
        outs.append(tg.transpose(2, 3, 1, 0, 4).reshape(3, HEADS, s, HEAD_DIM))
    return jnp.stack(outs, 0)


def _tok_to_streams(a):
    s = a.shape[0]
    outs = []
    for _, dil in ATT_PATTERNS:
        t = a.reshape(s // dil, dil, HEADS, HEAD_DIM)
        outs.append(t.transpose(2, 1, 0, 3).reshape(HEADS, s, HEAD_DIM))
    return jnp.stack(outs, 0)


def _streams_to_tok(o):
    s = o.shape[2]
    outs = []
    for g, (_, dil) in enumerate(ATT_PATTERNS):
        t = o[g].reshape(HEADS, dil, s // dil, HEAD_DIM)
        outs.append(t.transpose(2, 1, 0, 3).reshape(s, WIDTH_A))
    return outs


def _t5_bucket(dist):
    max_exact = N_REL_BUCKETS // 2
    d = np.maximum(dist, 1).astype(np.float32)
    scale = (N_REL_BUCKETS - max_exact) / math.log(REL_MAX_DIST / max_exact)
    large = max_exact + (np.log(d / max_exact) * scale).astype(np.int32)
    large = np.minimum(large, N_REL_BUCKETS - 1)
    return np.where(dist < max_exact, dist, large).astype(np.int32)


def _bucket_maps():
    i = np.arange(ATT_BLOCK)[:, None]
    kk = np.arange(2 * ATT_BLOCK)[None, :]
    steps = np.maximum(ATT_BLOCK + i - kk, 0)
    return np.stack([_t5_bucket(steps * dil) for _, dil in ATT_PATTERNS], 0)


def _band_bias(rel_bias):
    q = ATT_BLOCK

    def body(rel_ref, m_ref, o_ref):
        g = pl.program_id(0)
        bm = m_ref[...]
        for h in range(HEADS):
            acc = jnp.zeros((q, 2 * q), F32)
            for bk in range(N_REL_BUCKETS):
                acc = jnp.where(bm == bk, rel_ref[bk, g * HEADS + h], acc)
            o_ref[h] = acc

    return pl.pallas_call(body, out_shape=jax.ShapeDtypeStruct((N_GROUPS_A, HEADS, q, 2 * q), F32), grid=(N_GROUPS_A,),
                          in_specs=[pl.BlockSpec(memory_space=pltpu.SMEM),
                                    pl.BlockSpec((None, q, 2 * q), lambda g: (g, 0, 0))],
                          out_specs=pl.BlockSpec((None, HEADS, q, 2 * q), lambda g: (g, 0, 0, 0)),
                          compiler_params=_cparams(1), name="rel_bias_band")(rel_bias, jnp.asarray(_bucket_maps()))


def _attn_masks(nbs):
    q = ATT_BLOCK
    g, b = pl.program_id(0), pl.program_id(1)
    nb = jnp.where(g == 0, nbs[0], jnp.where(g == 1, nbs[1], nbs[2]))
    shift = jnp.where(lax.rem(b, nb) != 0, 0, q)
    ii = lax.broadcasted_iota(jnp.int32, (q, q), 0)
    kk = lax.broadcasted_iota(jnp.int32, (q, q), 1)
    return kk >= ii + shift, kk <= ii


def _attn_logits(q, kp, kc, bias_h, mask_p, mask_c):
    scale = HEAD_DIM ** -0.5
    sp = lax.dot_general(q, kp, _DIMS["nt"], preferred_element_type=F32) * scale + bias_h[:, :ATT_BLOCK]
    sc = lax.dot_general(q, kc, _DIMS["nt"], preferred_element_type=F32) * scale + bias_h[:, ATT_BLOCK:]
    return jnp.where(mask_p, sp, NEG_INF), jnp.where(mask_c, sc, NEG_INF)


def _attn_specs(s):
    q, h, e = ATT_BLOCK, HEADS, HEAD_DIM
    blk = (None, None, h, q, e)
    prev = lambda b: jnp.maximum(b - 1, 0)
    qkv_specs = [pl.BlockSpec(blk, lambda g, b: (g, 0, 0, b, 0)),
                 pl.BlockSpec(blk, lambda g, b: (g, 1, 0, prev(b), 0)),
                 pl.BlockSpec(blk, lambda g, b: (g, 1, 0, b, 0)),
                 pl.BlockSpec(blk, lambda g, b: (g, 2, 0, prev(b), 0)),
                 pl.BlockSpec(blk, lambda g, b: (g, 2, 0, b, 0))]
    bias_spec = pl.BlockSpec((None, h, q, 2 * q), lambda g, b: (g, 0, 0, 0))
    row_spec = pl.BlockSpec((None, h, q, e), lambda g, b: (g, 0, b, 0))
    return qkv_specs, bias_spec, row_spec


def _attn_fwd(qkv_s, bias):
    s = qkv_s.shape[3]
    nbs = tuple(s // dil // ATT_BLOCK for _, dil in ATT_PATTERNS)
    qkv_specs, bias_spec, row_spec = _attn_specs(s)

    def body(q_ref, kp_ref, kc_ref, vp_ref, vc_ref, b_ref, o_ref, l_ref):
        mask_p, mask_c = _attn_masks(nbs)
        for h in range(HEADS):
            sp, sc = _attn_logits(q_ref[h], kp_ref[h], kc_ref[h], b_ref[h], mask_p, mask_c)
            m = jnp.maximum(jnp.max(sp, axis=1, keepdims=True), jnp.max(sc, axis=1, keepdims=True))
            pp, pc = jnp.exp(sp - m), jnp.exp(sc - m)
            den = jnp.sum(pp, axis=1, keepdims=True) + jnp.sum(pc, axis=1, keepdims=True)
            o = (lax.dot_general(pp.astype(BF16), vp_ref[h], _DIMS["nn"], preferred_element_type=F32)
                 + lax.dot_general(pc.astype(BF16), vc_ref[h], _DIMS["nn"], preferred_element_type=F32))
            o_ref[h] = o / den
            l_ref[h] = jnp.broadcast_to(m + jnp.log(den), (ATT_BLOCK, HEAD_DIM))

    sds = jax.ShapeDtypeStruct((N_GROUPS_A, HEADS, s, HEAD_DIM), F32)
    return pl.pallas_call(body, out_shape=(sds, sds), grid=(N_GROUPS_A, s // ATT_BLOCK),
                          in_specs=qkv_specs + [bias_spec], out_specs=(row_spec, row_spec),
                          compiler_params=_cparams(2), name="attn_fwd")(qkv_s, qkv_s, qkv_s, qkv_s, qkv_s, bias)


def _attn_bwd(qkv_s, bias, do_s, ya_s, lse_s):
    s = qkv_s.shape[3]
    nblk = s // ATT_BLOCK
    nbs = tuple(s // dil // ATT_BLOCK for _, dil in ATT_PATTERNS)
    scale = HEAD_DIM ** -0.5
    q, h, e = ATT_BLOCK, HEADS, HEAD_DIM

    def body(q_ref, kp_ref, kc_ref, vp_ref, vc_ref, b_ref, do_ref, ya_ref, l_ref,
             dq_ref, dk_ref, dv_ref, db_ref, dk_own, dv_own):
        b = pl.program_id(1)
        mask_p, mask_c = _attn_masks(nbs)

        @pl.when(b == 0)
        def _():
            db_ref[...] = jnp.zeros_like(db_ref)
            dk_own[...] = jnp.zeros_like(dk_own)
            dv_own[...] = jnp.zeros_like(dv_own)

        @pl.when(b < nblk)
        def _():
            for hh in range(h):
                qh, kp, kc, vp, vc = q_ref[hh], kp_ref[hh], kc_ref[hh], vp_ref[hh], vc_ref[hh]
                sp, sc = _attn_logits(qh, kp, kc, b_ref[hh], mask_p, mask_c)
                lse = l_ref[hh][:, 0:1]
                pp, pc = jnp.exp(sp - lse), jnp.exp(sc - lse)
                do = do_ref[hh]
                dsum = jnp.sum(do * ya_ref[hh], axis=1, keepdims=True)
                dob = do.astype(BF16)
                dsp = pp * (lax.dot_general(dob, vp, _DIMS["nt"], preferred_element_type=F32) - dsum)
                dsc = pc * (lax.dot_general(dob, vc, _DIMS["nt"], preferred_element_type=F32) - dsum)
                db_ref[hh, :, :q] += dsp
                db_ref[hh, :, q:] += dsc
                dspb, dscb = dsp.astype(BF16), dsc.astype(BF16)
                dq_ref[hh] = scale * (lax.dot_general(dspb, kp, _DIMS["nn"], preferred_element_type=F32)
                                      + lax.dot_general(dscb, kc, _DIMS["nn"], preferred_element_type=F32))
                dkp = scale * lax.dot_general(dspb, qh, _DIMS["tn"], preferred_element_type=F32)
                dvp = lax.dot_general(pp.astype(BF16), dob, _DIMS["tn"], preferred_element_type=F32)

                dk_ref[hh] = dk_own[hh] + dkp
                dv_ref[hh] = dv_own[hh] + dvp
                dk_own[hh] = scale * lax.dot_general(dscb, qh, _DIMS["tn"], preferred_element_type=F32)
                dv_own[hh] = lax.dot_general(pc.astype(BF16), dob, _DIMS["tn"], preferred_element_type=F32)

        @pl.when(b == nblk)
        def _():
            dk_ref[...] = dk_own[...]
            dv_ref[...] = dv_own[...]

    blk5 = (None, None, h, q, e)
    cur = lambda b: jnp.minimum(b, nblk - 1)
    prev = lambda b: jnp.maximum(cur(b) - 1, 0)
    qkv_specs = [pl.BlockSpec(blk5, lambda g, b: (g, 0, 0, cur(b), 0)),
                 pl.BlockSpec(blk5, lambda g, b: (g, 1, 0, prev(b), 0)),
                 pl.BlockSpec(blk5, lambda g, b: (g, 1, 0, cur(b), 0)),
                 pl.BlockSpec(blk5, lambda g, b: (g, 2, 0, prev(b), 0)),
                 pl.BlockSpec(blk5, lambda g, b: (g, 2, 0, cur(b), 0))]
    bias_spec = pl.BlockSpec((None, h, q, 2 * q), lambda g, b: (g, 0, 0, 0))
    row_spec = pl.BlockSpec((None, h, q, e), lambda g, b: (g, 0, cur(b), 0))
    lag_spec = pl.BlockSpec((None, h, q, e), lambda g, b: (g, 0, jnp.maximum(b - 1, 0), 0))
    sds = jax.ShapeDtypeStruct((N_GROUPS_A, h, s, e), F32)
    own = pltpu.VMEM((h, q, e), F32)
    return pl.pallas_call(body, out_shape=(sds,) * 3 + (jax.ShapeDtypeStruct(bias.shape, F32),),
                          grid=(N_GROUPS_A, nblk + 1),
                          in_specs=qkv_specs + [bias_spec, row_spec, row_spec, row_spec],
                          out_specs=(row_spec, lag_spec, lag_spec, bias_spec), scratch_shapes=[own, own],
                          compiler_params=_cparams(2), name="attn_bwd")(
        qkv_s, qkv_s, qkv_s, qkv_s, qkv_s, bias, do_s, ya_s, lse_s)


def _bias_to_buckets(dbias):
    bmap = jnp.asarray(_bucket_maps())
    q = ATT_BLOCK

    def body(db_ref, m_ref, o_ref):
        lane = lax.broadcasted_iota(jnp.int32, (HEADS, LANES), 1)
        row = lax.broadcasted_iota(jnp.int32, (HEADS, LANES), 0)
        acc = jnp.zeros((HEADS, LANES), F32)
        bm = m_ref[...]
        for h in range(HEADS):
            dbh = db_ref[h]
            for bk in range(N_REL_BUCKETS):
                sv = jnp.sum(jnp.sum(jnp.where(bm == bk, dbh, 0.0), axis=1, keepdims=True), axis=0, keepdims=True)
                acc = acc + jnp.where((lane == bk) & (row == h), sv, 0.0)
        o_ref[...] = acc

    out = pl.pallas_call(body, out_shape=jax.ShapeDtypeStruct((N_GROUPS_A, HEADS, LANES), F32), grid=(N_GROUPS_A,),
                         in_specs=[pl.BlockSpec((None, HEADS, q, 2 * q), lambda g: (g, 0, 0, 0)),
                                   pl.BlockSpec((None, q, 2 * q), lambda g: (g, 0, 0))],
                         out_specs=pl.BlockSpec((None, HEADS, LANES), lambda g: (g, 0, 0)),
                         compiler_params=_cparams(1), name="rel_bias_grad")(dbias, bmap)
    return out[:, :, :N_REL_BUCKETS].reshape(N_GROUPS_A * HEADS, N_REL_BUCKETS).T


def _sgu_specs():
    c, w = CHUNK, WIDTH_B
    return [pl.BlockSpec((c, w), lambda i: (i, 0)), pl.BlockSpec((c, w), lambda i: (i, 1)),
            pl.BlockSpec((1, w), lambda i: (0, 0)), pl.BlockSpec((1, w), lambda i: (0, 0)),
            pl.BlockSpec((N_GROUPS_B, c, c), lambda i: (0, 0, 0)), pl.BlockSpec((N_GROUPS_B, c, 1), lambda i: (0, 0, 0))]


def _sgu_fwd(rest, ln_g, ln_b, w_tril, b_col):
    s = rest.shape[0]

    def body(zu_ref, zv_ref, g_ref, b_ref, w_ref, bs_ref, y_ref):
        u, vn = _f_sgu_pre(zu_ref[...], zv_ref[...], g_ref[...], b_ref[...])
        for gi in range(N_GROUPS_B):
            sl = slice(gi * GROUP_B, (gi + 1) * GROUP_B)
            mixed = lax.dot_general(w_ref[gi], vn[:, sl].astype(BF16), _DIMS["nn"], preferred_element_type=F32)
            y_ref[:, sl] = u[:, sl] * (mixed + bs_ref[gi])

    return pl.pallas_call(body, out_shape=jax.ShapeDtypeStruct((s, WIDTH_B), F32), grid=(s // CHUNK,),
                          in_specs=_sgu_specs(), out_specs=pl.BlockSpec((CHUNK, WIDTH_B), lambda i: (i, 0)),
                          compiler_params=_cparams(1), name="sgu_fwd")(rest, rest, ln_g, ln_b, w_tril, b_col)


def _sgu_bwd(rest, ln_g, ln_b, w_tril, b_col, dyb):
    s = rest.shape[0]
    c, w, ng = CHUNK, WIDTH_B, N_GROUPS_B

    def body(zu_ref, zv_ref, g_ref, b_ref, w_ref, bs_ref, dy_ref, dzu_ref, dzv_ref, dw_ref, dbs_ref, dg_ref, db_ref):
        (u, vn), vjp = jax.vjp(_f_sgu_pre, zu_ref[...], zv_ref[...], g_ref[...], b_ref[...])
        first = pl.program_id(0) == 0
        du, dvn = [], []
        for gi in range(ng):
            sl = slice(gi * GROUP_B, (gi + 1) * GROUP_B)
            vg = vn[:, sl].astype(BF16)
            mixed = lax.dot_general(w_ref[gi], vg, _DIMS["nn"], preferred_element_type=F32) + bs_ref[gi]
            dy = dy_ref[:, sl]
            dmix = dy * u[:, sl]
            du.append(dy * mixed)
            dmb = dmix.astype(BF16)
            dvn.append(lax.dot_general(w_ref[gi], dmb, _DIMS["tn"], preferred_element_type=F32))
            dwg = lax.dot_general(dmb, vg, _DIMS["nt"], preferred_element_type=F32)
            dbg = jnp.sum(dmix, axis=1, keepdims=True)

            @pl.when(first)
            def _(gi=gi, dwg=dwg, dbg=dbg):
                dw_ref[gi] = dwg
                dbs_ref[gi] = dbg

            @pl.when(jnp.logical_not(first))
            def _(gi=gi, dwg=dwg, dbg=dbg):
                dw_ref[gi] += dwg
                dbs_ref[gi] += dbg

        dzu, dzv, dg, db = vjp((jnp.concatenate(du, axis=1), jnp.concatenate(dvn, axis=1)))
        dzu_ref[...] = dzu.astype(dzu_ref.dtype)
        dzv_ref[...] = dzv.astype(dzv_ref.dtype)

        @pl.when(first)
        def _():
            dg_ref[...] = dg
            db_ref[...] = db

        @pl.when(jnp.logical_not(first))
        def _():
            dg_ref[...] += dg
            db_ref[...] += db

    row = pl.BlockSpec((c, w), lambda i: (i, 0))
    par = pl.BlockSpec((1, w), lambda i: (0, 0))
    return pl.pallas_call(
        body, grid=(s // c,),
        out_shape=(jax.ShapeDtypeStruct((s, w), BF16), jax.ShapeDtypeStruct((s, w), BF16),
                   jax.ShapeDtypeStruct((ng, c, c), F32), jax.ShapeDtypeStruct((ng, c, 1), F32),
                   jax.ShapeDtypeStruct((1, w), F32), jax.ShapeDtypeStruct((1, w), F32)),
        in_specs=_sgu_specs() + [row],
        out_specs=(row, row, pl.BlockSpec((ng, c, c), lambda i: (0, 0, 0)), pl.BlockSpec((ng, c, 1), lambda i: (0, 0, 0)),
                   par, par),
        compiler_params=_cparams(1), name="sgu_bwd")(rest, rest, ln_g, ln_b, w_tril, b_col, dyb)


SCAN_T = 128


def _to_time_major(dst, src_ref, lead):
    for r in range(SSM_CH // LANES):
        dst[:, r, :] = src_ref[lead, :, r * LANES:(r + 1) * LANES]


def _from_time_major(dst_ref, lead, src):
    for r in range(SSM_CH // LANES):
        dst_ref[lead, :, r * LANES:(r + 1) * LANES] = src[:, r, :]


def _scan_fwd(bu, a3):
    s = bu.shape[1]
    t_blk = _tile(s, SCAN_T, 8)
    rows = SSM_CH // LANES

    def body(bu_ref, a_ref, x_ref, b3r, b3i, x3r, x3i, carry):
        @pl.when(pl.program_id(0) == 0)
        def _():
            carry[...] = jnp.zeros_like(carry)

        _to_time_major(b3r, bu_ref, 0)
        _to_time_major(b3i, bu_ref, 1)
        ar, ai = a_ref[0], a_ref[1]

        def step(t, c):
            xr, xi = c
            nr = ar * xr - ai * xi + b3r[t]
            ni = ar * xi + ai * xr + b3i[t]
            x3r[t] = nr
            x3i[t] = ni
            return nr, ni

        xr, xi = lax.fori_loop(0, t_blk, step, (carry[0], carry[1]), unroll=8)
        carry[0] = xr
        carry[1] = xi
        _from_time_major(x_ref, 0, x3r)
        _from_time_major(x_ref, 1, x3i)

    blk = pl.BlockSpec((2, t_blk, SSM_CH), lambda i: (0, i, 0))
    tm = pltpu.VMEM((t_blk, rows, LANES), F32)
    return pl.pallas_call(body, out_shape=jax.ShapeDtypeStruct(bu.shape, F32), grid=(s // t_blk,),
                          in_specs=[blk, pl.BlockSpec((2, rows, LANES), lambda i: (0, 0, 0))], out_specs=blk,
                          scratch_shapes=[tm, tm, tm, tm, pltpu.VMEM((2, rows, LANES), F32)],
                          compiler_params=_cparams(1), name="ssm_scan_fwd")(bu, a3)


def _scan_bwd(dx, x, a3):
    s = dx.shape[1]
    t_blk = _tile(s, SCAN_T, 8)
    nb = s // t_blk
    rows = SSM_CH // LANES

    def body(dx_ref, x_ref, a_ref, g_ref, da_ref, d3r, d3i, x3r, x3i, g3r, g3i, carry):
        first = pl.program_id(0) == 0

        @pl.when(first)
        def _():
            carry[...] = jnp.zeros_like(carry)

        _to_time_major(d3r, dx_ref, 0)
        _to_time_major(d3i, dx_ref, 1)
        _to_time_major(x3r, x_ref, 0)
        _to_time_major(x3i, x_ref, 1)
        ar, ai = a_ref[0], a_ref[1]

        def step(k, c):
            t = t_blk - 1 - k
            gr, gi, dar, dai = c
            xr, xi = x3r[t], x3i[t]
            dar = dar + gr * xr + gi * xi
            dai = dai + gi * xr - gr * xi
            ngr = d3r[t] + ar * gr + ai * gi
            ngi = d3i[t] + ar * gi - ai * gr
            g3r[t] = ngr
            g3i[t] = ngi
            return ngr, ngi, dar, dai

        zero = jnp.zeros((rows, LANES), F32)
        gr, gi, dar, dai = lax.fori_loop(0, t_blk, step, (carry[0], carry[1], zero, zero), unroll=8)
        carry[0] = gr
        carry[1] = gi

        @pl.when(first)
        def _():
            da_ref[0] = dar
            da_ref[1] = dai

        @pl.when(jnp.logical_not(first))
        def _():
            da_ref[0] += dar
            da_ref[1] += dai

        _from_time_major(g_ref, 0, g3r)
        _from_time_major(g_ref, 1, g3i)

    blk = pl.BlockSpec((2, t_blk, SSM_CH), lambda i: (0, nb - 1 - i, 0))
    par = pl.BlockSpec((2, rows, LANES), lambda i: (0, 0, 0))
    tm = pltpu.VMEM((t_blk, rows, LANES), F32)
    return pl.pallas_call(body, out_shape=(jax.ShapeDtypeStruct(dx.shape, F32), jax.ShapeDtypeStruct((2, rows, LANES), F32)),
                          grid=(nb,), in_specs=[blk, blk, par], out_specs=(blk, par),
                          scratch_shapes=[tm] * 6 + [pltpu.VMEM((2, rows, LANES), F32)],
                          compiler_params=_cparams(1), name="ssm_scan_bwd")(dx, x, a3)


SSM_TILES = WIDTH_C // LANES
SSM_TILE_W = SSM_CH // SSM_TILES


def _block_diag(m):
    gpt = N_GROUPS_C // SSM_TILES
    t = m.reshape(SSM_TILES, gpt, SSM_GROUP, SSM_STATE)
    eye = jnp.eye(gpt, dtype=m.dtype)
    return (t[:, :, :, None, :] * eye[None, :, None, :, None]).reshape(SSM_TILES, LANES, SSM_TILE_W)


def _block_diag_extract(t):
    gpt = N_GROUPS_C // SSM_TILES
    t = t.reshape(SSM_TILES, gpt, SSM_GROUP, gpt, SSM_STATE)
    eye = jnp.eye(gpt, dtype=t.dtype)
    return jnp.sum(t * eye[None, :, None, :, None], axis=3).reshape(WIDTH_C, SSM_STATE)


def _ssm_in(uc_src, uc_col0, bd):
    s = uc_src.shape[0]
    tm = _tile(s, 2048, 8)
    j0 = uc_col0 // LANES
    return _mm("ssm_in", "nn", (s // tm, 2 * SSM_TILES, 1),
               uc_src, pl.BlockSpec((tm, LANES), lambda i, j, k: (i, j0 + j % SSM_TILES)),
               bd, pl.BlockSpec((None, None, LANES, SSM_TILE_W), lambda i, j, k: (j // SSM_TILES, j % SSM_TILES, 0, 0)),
               jax.ShapeDtypeStruct((2, s, SSM_CH), F32),
               pl.BlockSpec((None, tm, SSM_TILE_W), lambda i, j, k: (j // SSM_TILES, i, j % SSM_TILES)), None)


def _ssm_in_dgrad(g, bd):
    s = g.shape[1]
    tm = _tile(s, 2048, 8)
    return _mm("ssm_in_dgrad", "nt", (s // tm, SSM_TILES, 2),
               g, pl.BlockSpec((None, tm, SSM_TILE_W), lambda i, j, k: (k, i, j)),
               bd, pl.BlockSpec((None, None, LANES, SSM_TILE_W), lambda i, j, k: (k, j, 0, 0)),
               jax.ShapeDtypeStruct((s, WIDTH_C), F32), pl.BlockSpec((tm, LANES), lambda i, j, k: (i, j)),
               (tm, LANES))


def _ssm_in_wgrad(uc_src, uc_col0, g):
    s = g.shape[1]
    tk = _tile(s, 2048, 8)
    j0 = uc_col0 // LANES
    return _mm("ssm_in_wgrad", "tn", (2, SSM_TILES, s // tk),
               uc_src, pl.BlockSpec((tk, LANES), lambda i, j, k: (k, j0 + j)),
               g, pl.BlockSpec((None, tk, SSM_TILE_W), lambda i, j, k: (i, k, j)),
               jax.ShapeDtypeStruct((2, SSM_TILES, LANES, SSM_TILE_W), F32),
               pl.BlockSpec((None, None, LANES, SSM_TILE_W), lambda i, j, k: (i, j, 0, 0)), (LANES, SSM_TILE_W))


def _ssm_out(x, cd):
    s = x.shape[1]
    tm = _tile(s, 2048, 8)
    return _mm("ssm_out", "nn", (s // tm, SSM_TILES, 2),
               x, pl.BlockSpec((None, tm, SSM_TILE_W), lambda i, j, k: (k, i, j)),
               cd, pl.BlockSpec((None, None, SSM_TILE_W, LANES), lambda i, j, k: (k, j, 0, 0)),
               jax.ShapeDtypeStruct((s, WIDTH_C), F32), pl.BlockSpec((tm, LANES), lambda i, j, k: (i, j)),
               (tm, LANES))


def _ssm_out_dgrad(dy, cd):
    s = dy.shape[0]
    tm = _tile(s, 2048, 8)
    return _mm("ssm_out_dgrad", "nt", (s // tm, 2 * SSM_TILES, 1),
               dy, pl.BlockSpec((tm, LANES), lambda i, j, k: (i, j % SSM_TILES)),
               cd, pl.BlockSpec((None, None, SSM_TILE_W, LANES), lambda i, j, k: (j // SSM_TILES, j % SSM_TILES, 0, 0)),
               jax.ShapeDtypeStruct((2, s, SSM_CH), F32),
               pl.BlockSpec((None, tm, SSM_TILE_W), lambda i, j, k: (j // SSM_TILES, i, j % SSM_TILES)), None)


def _ssm_out_wgrad(x, dy):
    s = dy.shape[0]
    tk = _tile(s, 2048, 8)
    return _mm("ssm_out_wgrad", "tn", (2, SSM_TILES, s // tk),
               x, pl.BlockSpec((None, tk, SSM_TILE_W), lambda i, j, k: (i, k, j)),
               dy, pl.BlockSpec((tk, LANES), lambda i, j, k: (k, j)),
               jax.ShapeDtypeStruct((2, SSM_TILES, SSM_TILE_W, LANES), F32),
               pl.BlockSpec((None, None, SSM_TILE_W, LANES), lambda i, j, k: (i, j, 0, 0)), (SSM_TILE_W, LANES))


def _ssm_param_inputs(w, l):
    rep = lambda a: jnp.repeat(a, SSM_GROUP, axis=0)
    lr, li, ld = w["lam_re"][l], w["lam_im"][l], w["log_dt"][l][:, None]
    br_t = jnp.transpose(w["b_re"][l], (0, 2, 1)).reshape(WIDTH_C, SSM_STATE)
    bi_t = jnp.transpose(w["b_im"][l], (0, 2, 1)).reshape(WIDTH_C, SSM_STATE)
    return lr, li, ld, rep(lr), rep(li), rep(ld), br_t, bi_t


def _ssm_params_fwd(pin):
    def body(*refs):
        res = _f_ssm_params(*[r[...] for r in refs[:8]])
        for r, v in zip(refs[8:], res):
            r[...] = v

    g, p = N_GROUPS_C, SSM_STATE
    return pl.pallas_call(body, out_shape=(jax.ShapeDtypeStruct((g, p), F32),) * 2
                          + (jax.ShapeDtypeStruct((WIDTH_C, p), F32),) * 2, name="ssm_params_fwd")(*pin)


def _ssm_params_bwd(pin, d_abr, d_abi, d_bbr, d_bbi):
    g, p = N_GROUPS_C, SSM_STATE
    group_sum = jnp.asarray(np.kron(np.eye(g, dtype=np.float32), np.ones((1, SSM_GROUP), np.float32)))

    def body(*refs):
        ins = [r[...] for r in refs[:8]]
        cts = tuple(r[...] for r in refs[8:12])
        gs = refs[12][...]
        d_lr_ref, d_li_ref, d_ld_ref, d_br_ref, d_bi_ref = refs[13:]
        _, vjp = jax.vjp(_f_ssm_params, *ins)
        d = vjp(cts)
        fold = lambda v: lax.dot_general(gs, v, _DIMS["nn"], preferred_element_type=F32, precision=lax.Precision.HIGHEST)
        d_lr_ref[...] = d[0] + fold(d[3])
        d_li_ref[...] = d[1] + fold(d[4])
        d_ld_ref[...] = d[2] + fold(jnp.broadcast_to(d[5], (WIDTH_C, p)))[:, 0:1]
        d_br_ref[...] = d[6]
        d_bi_ref[...] = d[7]

    return pl.pallas_call(body, out_shape=(jax.ShapeDtypeStruct((g, p), F32), jax.ShapeDtypeStruct((g, p), F32),
                                           jax.ShapeDtypeStruct((g, 1), F32), jax.ShapeDtypeStruct((WIDTH_C, p), F32),
                                           jax.ShapeDtypeStruct((WIDTH_C, p), F32)), name="ssm_params_bwd")(
        *pin, d_abr, d_abi, d_bbr, d_bbi, group_sum)


def _c_block_diag(w, l):
    def one(c):
        return jnp.transpose(_block_diag(c.reshape(WIDTH_C, SSM_STATE)), (0, 2, 1))
    return jnp.stack([one(w["c_re"][l]), -one(w["c_im"][l])], 0)


ANY = pl.BlockSpec(memory_space=pl.ANY)


def _ag8(name, xb):
    def body(x_ref, out_ref, send_sems, recv_sems, local_sem):
        x, y, c = lax.axis_index("x"), lax.axis_index("y"), lax.axis_index("c")
        me, sibling = (x, y, c), (x, y, 1 - c)
        chips = [(1 - x, y), (x, 1 - y), (1 - x, 1 - y)]

        def rows(px, py, pc):
            return out_ref.at[4 * px + 2 * py + pc]

        def copy(k, block, to, src=None):
            return pltpu.make_async_remote_copy(src_ref=rows(*block) if src is None else src, dst_ref=rows(*block),
                                                send_sem=send_sems.at[k], recv_sem=recv_sems.at[k],
                                                device_id=to, device_id_type=MESH)

        mine = pltpu.make_async_copy(x_ref, rows(*me), local_sem)
        mine.start()
        first = [copy(0, me, sibling, src=x_ref)]
        first += [copy(1 + j, me, (*chip, c), src=x_ref) for j, chip in enumerate(chips)]
        for cp in first:
            cp.start()
        passed = [copy(4 + j, (*chip, c), sibling) for j, chip in enumerate(chips)]
        for j, chip in enumerate(chips):
            copy(1 + j, (*chip, c), me).wait_recv()
            passed[j].start()
        copy(0, sibling, me).wait_recv()
        for j, chip in enumerate(chips):
            copy(4 + j, (*chip, 1 - c), me).wait_recv()
        for cp in first + passed:
            cp.wait_send()
        mine.wait()

    return pl.pallas_call(body, out_shape=jax.ShapeDtypeStruct((8,) + xb.shape, xb.dtype), in_specs=[ANY], out_specs=ANY,
                          scratch_shapes=[pltpu.SemaphoreType.DMA((7,)), pltpu.SemaphoreType.DMA((7,)),
                                          pltpu.SemaphoreType.DMA(())], name=name)(xb)


def _mesh_place():
    x, y, c = lax.axis_index("x"), lax.axis_index("y"), lax.axis_index("c")
    return x, y, c, [(1 - x, y), (x, 1 - y), (1 - x, 1 - y)]


ICI_CHUNK_BYTES = 2 << 20
D2D_CHUNK_BYTES = 1 << 20


def _split_rows(rows, row_bytes, chunk_bytes):
    k = 1
    for cand in range(1, max(1, (rows * row_bytes) // chunk_bytes) + 1):
        if rows % cand == 0 and (rows // cand) % SUBLANES_BF16 == 0:
            k = cand
    return [(j * (rows // k), rows // k) for j in range(k)]


def _run_side(side):
    ni, no = len(side["ins"]), len(side["out_sds"])

    def body(*refs):
        ins, outs, sems = refs[:ni], refs[ni:ni + no], refs[ni + no:]
        side["start"](ins, outs, sems)
        side["finish"](ins, outs, sems)

    return pl.pallas_call(body, out_shape=list(side["out_sds"]), in_specs=[ANY] * ni, out_specs=[ANY] * no,
                          scratch_shapes=list(side["sems"]), name=side["name"])(*side["ins"])


def _gather_side(wb, l):
    nw = len(wb)
    halves = [a.shape[1] // 2 for a in wb]
    pieces = [_split_rows(h, a.shape[2] * a.dtype.itemsize, ICI_CHUNK_BYTES) for a, h in zip(wb, halves)]

    def rows(ref, i, hc, r0=0, n=None):
        return ref.at[pl.ds(hc * halves[i] + r0, halves[i] if n is None else n)]

    def copier(sems):
        def copy(i, k, src, dst, to):
            return pltpu.make_async_remote_copy(src_ref=src, dst_ref=dst, send_sem=sems[0].at[6 * i + k],
                                                recv_sem=sems[1].at[6 * i + k], device_id=to, device_id_type=MESH)
        return copy

    def start(ins, outs, sems):
        x, y, c, chips = _mesh_place()
        me = 2 * x + y
        copy = copier(sems)
        for i in range(nw):
            for hc in range(2):
                for r0, n in pieces[i]:
                    pltpu.make_async_copy(rows(ins[i].at[l], i, hc, r0, n), rows(outs[i].at[me], i, hc, r0, n),
                                          sems[2].at[i]).start()
        for i in range(nw):
            for k, (px, py) in enumerate(chips):
                for r0, n in pieces[i]:
                    copy(i, k, rows(ins[i].at[l], i, c, r0, n), rows(outs[i].at[me], i, c, r0, n), (px, py, c)).start()

    def finish(ins, outs, sems):
        x, y, c, chips = _mesh_place()
        me = 2 * x + y
        copy = copier(sems)
        for k, (px, py) in enumerate(chips):
            for i in range(nw):
                src = outs[i].at[2 * px + py]
                copy(i, k, rows(src, i, c), rows(src, i, c), (px, py, c)).wait_recv()
                for r0, n in pieces[i]:
                    copy(i, 3 + k, rows(src, i, c, r0, n), rows(src, i, c, r0, n), (x, y, 1 - c)).start()
        for k, (px, py) in enumerate(chips):
            for i in range(nw):
                other = rows(outs[i].at[2 * px + py], i, 1 - c)
                copy(i, 3 + k, other, other, (x, y, 1 - c)).wait_recv()
        for i in range(nw):
            whole = rows(outs[i].at[me], i, c)
            for k in range(6):
                copy(i, k, whole, whole, (x, y, 1 - c)).wait_send()
            pltpu.make_async_copy(ins[i].at[l], outs[i].at[me], sems[2].at[i]).wait()

    return dict(name="gather_weights", ins=list(wb), out_sds=[jax.ShapeDtypeStruct((4,) + a.shape[1:], a.dtype) for a in wb],
                sems=[pltpu.SemaphoreType.DMA((6 * nw,)), pltpu.SemaphoreType.DMA((6 * nw,)), pltpu.SemaphoreType.DMA((nw,))],
                start=start, finish=finish)


def _grad_core_swap(g4):
    nw = len(g4)
    halves = [a.shape[1] // 2 for a in g4]
    pieces = [_split_rows(h, a.shape[2] * a.dtype.itemsize, D2D_CHUNK_BYTES) for a, h in zip(g4, halves)]

    def body(*refs):
        ins, theirs = refs[:nw], refs[nw:2 * nw]
        send_sems, recv_sems = refs[2 * nw:]
        x, y, c, _ = _mesh_place()

        def copy(i, src, dst):
            return pltpu.make_async_remote_copy(src_ref=src, dst_ref=dst, send_sem=send_sems.at[i], recv_sem=recv_sems.at[i],
                                                device_id=(x, y, 1 - c), device_id_type=MESH)

        for i in range(nw):
            for j in range(4):
                for r0, n in pieces[i]:
                    copy(i, ins[i].at[j, pl.ds((1 - c) * halves[i] + r0, n)], theirs[i].at[j, pl.ds(r0, n)]).start()
        for i in range(nw):
            copy(i, ins[i].at[:, pl.ds((1 - c) * halves[i], halves[i])], theirs[i]).wait()

    sds = [jax.ShapeDtypeStruct((4, h) + a.shape[2:], a.dtype) for a, h in zip(g4, halves)]
    return pl.pallas_call(body, out_shape=sds, in_specs=[ANY] * nw, out_specs=[ANY] * nw,
                          scratch_shapes=[pltpu.SemaphoreType.DMA((nw,)), pltpu.SemaphoreType.DMA((nw,))],
                          name="grad_core_swap")(*g4)


def _exchange_side(t4):
    nw = len(t4)
    pieces = [_split_rows(a.shape[1], a.shape[2] * a.dtype.itemsize, ICI_CHUNK_BYTES) for a in t4]

    def copier(sems, c):
        def copy(i, k, src, dst, px, py):
            return pltpu.make_async_remote_copy(src_ref=src, dst_ref=dst, send_sem=sems[0].at[3 * i + k],
                                                recv_sem=sems[1].at[3 * i + k], device_id=(px, py, c),
                                                device_id_type=MESH)
        return copy

    def start(ins, outs, sems):
        x, y, c, chips = _mesh_place()
        me = 2 * x + y
        copy = copier(sems, c)
        for i in range(nw):
            for r0, n in pieces[i]:
                for k, (px, py) in enumerate(chips):
                    copy(i, k, ins[i].at[2 * px + py, pl.ds(r0, n)], outs[i].at[me, pl.ds(r0, n)], px, py).start()
                pltpu.make_async_copy(ins[i].at[me, pl.ds(r0, n)], outs[i].at[me, pl.ds(r0, n)], sems[2].at[i]).start()

    def finish(ins, outs, sems):
        x, y, c, chips = _mesh_place()
        me = 2 * x + y
        copy = copier(sems, c)
        for i in range(nw):
            for k, (px, py) in enumerate(chips):
                copy(i, k, ins[i].at[me], outs[i].at[2 * px + py], px, py).wait_recv()
        for i in range(nw):
            for k, (px, py) in enumerate(chips):
                copy(i, k, ins[i].at[me], outs[i].at[me], px, py).wait_send()
            pltpu.make_async_copy(ins[i].at[me], outs[i].at[me], sems[2].at[i]).wait()

    return dict(name="grad_chip_exchange", ins=list(t4), out_sds=[jax.ShapeDtypeStruct(a.shape, a.dtype) for a in t4],
                sems=[pltpu.SemaphoreType.DMA((3 * nw,)), pltpu.SemaphoreType.DMA((3 * nw,)), pltpu.SemaphoreType.DMA((nw,))],
                start=start, finish=finish)


def _grad_half_swap(full):
    nw = len(full)
    halves = [a.shape[0] // 2 for a in full]
    pieces = [_split_rows(h, a.shape[1] * a.dtype.itemsize, D2D_CHUNK_BYTES) for a, h in zip(full, halves)]

    def body(*refs):
        bufs = refs[nw:2 * nw]
        send_sems, recv_sems = refs[2 * nw:]
        x, y, c, _ = _mesh_place()

        def copy(i, hc, r0, n):
            view = bufs[i].at[pl.ds(hc * halves[i] + r0, n)]
            return pltpu.make_async_remote_copy(src_ref=view, dst_ref=view, send_sem=send_sems.at[i], recv_sem=recv_sems.at[i],
                                                device_id=(x, y, 1 - c), device_id_type=MESH)

        for i in range(nw):
            for r0, n in pieces[i]:
                copy(i, c, r0, n).start()
        for i in range(nw):
            copy(i, c, 0, halves[i]).wait_send()
            copy(i, 1 - c, 0, halves[i]).wait_recv()

    return pl.pallas_call(body, out_shape=[jax.ShapeDtypeStruct(a.shape, a.dtype) for a in full],
                          in_specs=[ANY] * nw, out_specs=[ANY] * nw, input_output_aliases={i: i for i in range(nw)},
                          scratch_shapes=[pltpu.SemaphoreType.DMA((nw,)), pltpu.SemaphoreType.DMA((nw,))],
                          name="grad_half_swap")(*full)


def _core_add(g, theirs, core):
    _, rh, cc = theirs.shape
    tr = _tile(rh, max(SUBLANES_BF16, (1 << 19) // cc), SUBLANES_BF16)
    nt = rh // tr

    def body(c_ref, a_ref, b_ref, o_ref):
        o_ref[...] = (a_ref[...].astype(F32) + b_ref[...].astype(F32)).astype(o_ref.dtype)

    blk = (None, tr, cc)
    return pl.pallas_call(
        body, out_shape=jax.ShapeDtypeStruct(theirs.shape, theirs.dtype),
        grid_spec=pltpu.PrefetchScalarGridSpec(
            num_scalar_prefetch=1, grid=(4, nt),
            in_specs=[pl.BlockSpec(blk, lambda j, t, c_ref: (j, c_ref[0] * nt + t, 0)),
                      pl.BlockSpec(blk, lambda j, t, c_ref: (j, t, 0))],
            out_specs=pl.BlockSpec(blk, lambda j, t, c_ref: (j, t, 0))),
        compiler_params=_cparams(2), name="grad_core_add")(core, g, theirs)


def _sum_to_half(xb, core):
    n, rh, cc = xb.shape
    tr = _tile(rh, max(SUBLANES_BF16, (1 << 18) // cc), SUBLANES_BF16)
    nt = rh // tr

    def body(c_ref, x_ref, o_ref):
        acc = x_ref[0].astype(F32)
        for k in range(1, n):
            acc = acc + x_ref[k].astype(F32)
        o_ref[...] = acc

    return pl.pallas_call(
        body, out_shape=jax.ShapeDtypeStruct((2 * rh, cc), F32),
        grid_spec=pltpu.PrefetchScalarGridSpec(
            num_scalar_prefetch=1, grid=(nt,),
            in_specs=[pl.BlockSpec((n, tr, cc), lambda t, c_ref: (0, t, 0))],
            out_specs=pl.BlockSpec((tr, cc), lambda t, c_ref: (c_ref[0] * nt + t, 0))),
        compiler_params=_cparams(1), name="grad_chip_sum")(core, xb)


def _sum_lead(name, xb, out_dtype=F32):
    n, rows, w = xb.shape
    tr = _tile(rows, max(SUBLANES_BF16, (1 << 18) // w), SUBLANES_BF16)

    def body(x_ref, o_ref):
        acc = x_ref[0].astype(F32)
        for k in range(1, n):
            acc = acc + x_ref[k].astype(F32)
        o_ref[...] = acc.astype(o_ref.dtype)

    return pl.pallas_call(body, out_shape=jax.ShapeDtypeStruct((rows, w), out_dtype), grid=(rows // tr,),
                          in_specs=[pl.BlockSpec((n, tr, w), lambda i: (0, i, 0))],
                          out_specs=pl.BlockSpec((tr, w), lambda i: (i, 0)), compiler_params=_cparams(1), name=name)(xb)


def _pad_to(v, mult):
    n = v.shape[-1]
    pad = (-n) % mult
    return v if pad == 0 else jnp.pad(v, [(0, 0)] * (v.ndim - 1) + [(0, pad)])


RELAYOUT_ROWS = 256


def _cols_from_shards(a):
    _, k, cc = a.shape
    tr = _tile(k, RELAYOUT_ROWS, SUBLANES_BF16)

    def body(i_ref, o_ref):
        for j in range(4):
            o_ref[:, j * cc:(j + 1) * cc] = i_ref[j]

    return pl.pallas_call(body, out_shape=jax.ShapeDtypeStruct((k, 4 * cc), a.dtype), grid=(k // tr,),
                          in_specs=[pl.BlockSpec((4, tr, cc), lambda i: (0, i, 0))],
                          out_specs=pl.BlockSpec((tr, 4 * cc), lambda i: (i, 0)),
                          compiler_params=_cparams(1), name="cols_from_shards")(a)


def _cols_to_shards(a):
    k, n = a.shape
    cc = n // 4
    tr = _tile(k, RELAYOUT_ROWS, SUBLANES_BF16)

    def body(i_ref, o_ref):
        for j in range(4):
            o_ref[j] = i_ref[:, j * cc:(j + 1) * cc]

    return pl.pallas_call(body, out_shape=jax.ShapeDtypeStruct((4, k, cc), a.dtype), grid=(k // tr,),
                          in_specs=[pl.BlockSpec((tr, n), lambda i: (i, 0))],
                          out_specs=pl.BlockSpec((4, tr, cc), lambda i: (0, i, 0)),
                          compiler_params=_cparams(1), name="cols_to_shards")(a)


COMM_GROUPS = (("w_in",), ("w_ffn_in",), ("w_glu", "w_pa", "w_pb", "w_pc", "w_o", "w_ffn_out"))


def _gather_sides(wb, l):
    return [_gather_side([wb[n] for n in grp], l) for grp in COMM_GROUPS]


def _first_weights(wb):
    return _next_weights(wb, 0, [_run_side(s) for s in _gather_sides(wb, 0)])


def _next_weights(wb, l, got):
    kinds = dict(BIG)
    out = {}
    for grp, arrs in zip(COMM_GROUPS, got):
        for n, a in zip(grp, arrs):
            if kinds[n] == "row":
                out[n] = a.reshape(4 * a.shape[1], a.shape[2])
            else:
                out[n] = _cols_from_shards(a) if n == "w_in" else a
    return out


def _start_reduce(g):
    g4 = []
    for n, kind in BIG:
        a = g[n]
        if kind == "row":
            a = a.reshape(4, a.shape[0] // 4, a.shape[1])
        elif a.ndim == 2:
            a = _cols_to_shards(a)
        g4.append(a)
    core = lax.axis_index("c").astype(jnp.int32).reshape(1)
    theirs = _grad_core_swap(g4)
    return {n: _core_add(a, b, core) for n, a, b in zip(BIG_NAMES, g4, theirs)}


def _exchange_sides(pending):
    return [_exchange_side([pending[n] for n in grp]) for grp in COMM_GROUPS]


def _run_exchange(pending):
    return [_run_side(s) for s in _exchange_sides(pending)]


def _finish_reduce(pending, arrived):
    core = lax.axis_index("c").astype(jnp.int32).reshape(1)
    names = [n for grp in COMM_GROUPS for n in grp]
    halves = [_sum_to_half(a, core) for arrs in arrived for a in arrs]
    return dict(zip(names, _grad_half_swap(halves)))


def _adam_fn(w, g, m, v):
    m = ADAM_B1 * m + (1.0 - ADAM_B1) * g
    v = ADAM_B2 * v + (1.0 - ADAM_B2) * (g * g)
    m_hat = m / (1.0 - ADAM_B1 ** ADAM_STEP)
    v_hat = v / (1.0 - ADAM_B2 ** ADAM_STEP)
    return -ADAM_LR * (m_hat / (jnp.sqrt(v_hat) + ADAM_EPS) + ADAM_WD * w), m, v


def _adamw(name, w, g, m, v):
    shape = w.shape
    cols = shape[-1]
    f = lambda a: a.reshape(-1, cols)
    rows = f(w).shape[0]
    tr = max(8, (1 << 19) // cols)
    d, nm, nv = _ew(name, _adam_fn, rows, [R(f(w)), R(f(g)), R(f(m)), R(f(v))], [(cols, F32)] * 3, tr=tr)
    return d.reshape(shape), nm.reshape(shape), nv.reshape(shape)


def _beside(sides, k, call):
    if sides is None:
        return call(side=None), None
    return call(side=sides[k])


def _layer_fwd(x, wl, bias, alpha, sides=None):
    s, d = x.shape
    d_ff = wl["w_ffn_out"].shape[0]
    b_in = wl["b_in"][None, :]
    got = [None] * len(COMM_GROUPS)
    qkv = _mm_nn("proj_qkv", x, wl["w_in"], n0=0, n=3 * QKV_W, bias=b_in, out_dtype=BF16, tn=768)
    rest, got[0] = _beside(sides, 0, functools.partial(_mm_nn, "proj_rest", x, wl["w_in"], n0=3 * QKV_W, bias=b_in, tn=768))
    qkv_s = _to_streams(qkv)
    o_s, l_s = _attn_fwd(qkv_s, bias)
    o_t, l_t = _streams_to_tok(o_s), _streams_to_tok(l_s)
    ya, lse = _ew("attn_combine", _f_combine, s, [R(a) for a in o_t + l_t], [(WIDTH_A, F32), (WIDTH_A, F32)])
    yb = _sgu_fwd(rest, wl["sgu_ln_g"][None], wl["sgu_ln_b"][None], wl["w_tril"], wl["b_col"])
    bu = _ssm_in(rest, REST_UC, wl["bd"])
    xs = _scan_fwd(bu, wl["a3"])
    y_lin = _ssm_out(xs, wl["cd"])
    uc_blk = REST_UC // WIDTH_C
    yc0 = _ew("ssm_skip_gelu", _f_ssm_out, s, [R(y_lin), R(rest, WIDTH_C, uc_blk), P(wl["d_skip"][None])], [(WIDTH_C, F32)])
    t_glu = _mm_nn("glu_proj", yc0, wl["w_glu"], bias=wl["b_glu"][None], tn=768)
    yc = _ew("glu", _f_glu, s, [R(yc0), R(t_glu)], [(WIDTH_C, F32)])
    pa = _mm_nn("proj_a", ya, wl["w_pa"])
    pb = _mm_nn("proj_b", yb, wl["w_pb"])
    pc = _mm_nn("proj_c", yc, wl["w_pc"])
    gw = _tile(math.gcd(d, REST_GL), 256, LANES)
    gl_ins = [R(rest, gw, (REST_GL + i * d) // gw, True) for i in range(N_BRANCH)]
    merged = _ew("merge", _f_merge, s, gl_ins + [R(pa, gw, 0, True), R(pb, gw, 0, True), R(pc, gw, 0, True)],
                 [(gw, F32)], nj=d // gw, tr=512)
    h1 = _mm_nn("proj_o", merged, wl["w_o"], add=x, add_scale=alpha)
    x1 = _ew("ln1", _f_ln, s, [R(h1), P(wl["ln1_g"][None]), P(wl["ln1_b"][None])], [(d, F32)])
    ff, got[1] = _beside(sides, 1, functools.partial(_mm_nn, "ffn_in", x1, wl["w_ffn_in"], tn=1408))
    fw = _tile(d_ff, 512, LANES)
    act = _ew("swiglu", _f_swiglu, s, [R(ff, fw, 0, True), R(ff, fw, d_ff // fw, True)], [(fw, F32)], nj=d_ff // fw, tr=512)
    h2, got[2] = _beside(sides, 2, functools.partial(_mm_nn, "ffn_out", act, wl["w_ffn_out"], add=x1, add_scale=alpha, tk=1408))
    x2 = _ew("ln2", _f_ln, s, [R(h2), P(wl["ln2_g"][None]), P(wl["ln2_b"][None])], [(d, F32)])
    saved = dict(x=x, qkv_s=qkv_s, rest=rest, ya=ya, lse=lse, yb=yb, xs=xs, y_lin=y_lin, yc0=yc0, t_glu=t_glu, yc=yc,
                 pa=pa, pb=pb, pc=pc, merged=merged, h1=h1, x1=x1, ff=ff, act=act, h2=h2)
    return x2, saved, got


def _vjp_rows(f, n_primal):
    def fn(*args):
        n_ct = len(args) - n_primal
        cts, primals = args[:n_ct], args[n_ct:]
        out, vjp = jax.vjp(f, *primals)
        ct = tuple(c.astype(F32) for c in cts)
        return vjp(ct if isinstance(out, (tuple, list)) else ct[0])
    return fn


def _layer_bwd(dx2, sv, wl, bias, alpha, sides=None):
    s, d = dx2.shape
    arrived = [None] * len(COMM_GROUPS)
    d_ff = wl["w_ffn_out"].shape[0]
    g = {}
    row = lambda a: a[None]
    dh2, g["ln2_g"], g["ln2_b"] = _ew("ln2_bwd", _vjp_rows(_f_ln, 3), s,
                                      [R(dx2), R(sv["h2"]), P(row(wl["ln2_g"])), P(row(wl["ln2_b"]))],
                                      [(d, F32)], accs=[d, d])
    g["w_ffn_out"] = _mm_tn("ffn_out_wgrad", sv["act"], dh2)
    dact = _mm_nt("ffn_out_dgrad", dh2, wl["w_ffn_out"], tn=1408, tk=2048)
    fw = _tile(d_ff, 512, LANES)
    nf = d_ff // fw
    dgf, dup = _ew("swiglu_bwd", _vjp_rows(_f_swiglu, 2), s,
                   [R(dact, fw, 0, True), R(sv["ff"], fw, 0, True), R(sv["ff"], fw, nf, True)],
                   [(fw, BF16), (fw, BF16)], nj=nf, tr=512)
    dff = jnp.concatenate([dgf, dup], axis=1)
    g["w_ffn_in"], arrived[0] = _beside(sides, 0, functools.partial(_mm_tn, "ffn_in_wgrad", sv["x1"], dff, tn=1408,
                                                                    col_shards=True))
    dx1, arrived[1] = _beside(sides, 1, functools.partial(_mm_nt, "ffn_in_dgrad", dff, wl["w_ffn_in"], add=dh2,
                                                          add_scale=alpha, tk=1408))
    dh1, g["ln1_g"], g["ln1_b"] = _ew("ln1_bwd", _vjp_rows(_f_ln, 3), s,
                                      [R(dx1), R(sv["h1"]), P(row(wl["ln1_g"])), P(row(wl["ln1_b"]))],
                                      [(d, F32)], accs=[d, d])
    g["w_o"] = _mm_tn("proj_o_wgrad", sv["merged"], dh1)
    dmerged = _mm_nt("proj_o_dgrad", dh1, wl["w_o"])
    rest = sv["rest"]
    gw = _tile(math.gcd(d, REST_GL), 256, LANES)
    gl_ins = [R(rest, gw, (REST_GL + i * d) // gw, True) for i in range(N_BRANCH)]
    dg0, dg1, dg2, dpa, dpb, dpc = _ew(
        "merge_bwd", _vjp_rows(_f_merge, 6), s,
        [R(dmerged, gw, 0, True)] + gl_ins + [R(sv[k], gw, 0, True) for k in ("pa", "pb", "pc")],
        [(gw, BF16)] * 6, nj=d // gw, tr=512)
    g["w_pa"] = _mm_tn("proj_a_wgrad", sv["ya"], dpa, col_shards=True)
    g["w_pb"] = _mm_tn("proj_b_wgrad", sv["yb"], dpb, col_shards=True)
    g["w_pc"] = _mm_tn("proj_c_wgrad", sv["yc"], dpc, col_shards=True)
    dya = _mm_nt("proj_a_dgrad", dpa, wl["w_pa"], tn=512)
    dyb = _mm_nt("proj_b_dgrad", dpb, wl["w_pb"], tn=768)
    dyc = _mm_nt("proj_c_dgrad", dpc, wl["w_pc"], tn=768)
    dyc0_a, dt_glu, g["b_glu"] = _ew("glu_bwd", lambda ct, a, t: (lambda r: (r[0], r[1], _rowsum(r[1])))(_vjp_rows(_f_glu, 2)(ct, a, t)),
                                     s, [R(dyc), R(sv["yc0"]), R(sv["t_glu"])], [(WIDTH_C, F32), (WIDTH_C, F32)], accs=[WIDTH_C])
    g["w_glu"] = _mm_tn("glu_wgrad", sv["yc0"], dt_glu)
    dyc0 = _mm_nt("glu_dgrad", dt_glu, wl["w_glu"], add=dyc0_a, tn=768)
    uc_blk = REST_UC // WIDTH_C
    dy_lin, duc_skip, g["d_skip"] = _ew("ssm_skip_gelu_bwd", _vjp_rows(_f_ssm_out, 3), s,
                                        [R(dyc0), R(sv["y_lin"]), R(rest, WIDTH_C, uc_blk), P(row(wl["d_skip"]))],
                                        [(WIDTH_C, F32), (WIDTH_C, F32)], accs=[WIDTH_C])
    d_cd = _ssm_out_wgrad(sv["xs"], dy_lin)
    dxs = _ssm_out_dgrad(dy_lin, wl["cd"])
    gs, da = _scan_bwd(dxs, sv["xs"], wl["a3"])
    d_bd = _ssm_in_wgrad(rest, REST_UC, gs)
    duc_lin = _ssm_in_dgrad(gs, wl["bd"])
    duc = _ew("ssm_duc", lambda a, b: a + b, s, [R(duc_lin), R(duc_skip)], [(WIDTH_C, BF16)])
    d_abr = da[0].reshape(N_GROUPS_C, SSM_STATE)
    d_abi = da[1].reshape(N_GROUPS_C, SSM_STATE)
    d_lr, d_li, d_ld, d_br_t, d_bi_t = _ssm_params_bwd(wl["ssm_pin"], d_abr, d_abi,
                                                       _block_diag_extract(d_bd[0]), _block_diag_extract(d_bd[1]))
    g["lam_re"], g["lam_im"], g["log_dt"] = d_lr, d_li, d_ld[:, 0]
    un_t = lambda a: jnp.transpose(a.reshape(N_GROUPS_C, SSM_GROUP, SSM_STATE), (0, 2, 1))
    g["b_re"], g["b_im"] = un_t(d_br_t), un_t(d_bi_t)
    cd_ex = lambda a: _block_diag_extract(jnp.transpose(a, (0, 2, 1))).reshape(N_GROUPS_C, SSM_GROUP, SSM_STATE)
    g["c_re"], g["c_im"] = cd_ex(d_cd[0]), -cd_ex(d_cd[1])
    dzu, dzv, dws, dbs, g["sgu_ln_g"], g["sgu_ln_b"] = _sgu_bwd(rest, row(wl["sgu_ln_g"]), row(wl["sgu_ln_b"]),
                                                                  wl["w_tril"], wl["b_col"], dyb)
    g["w_s"] = jnp.tril(dws)
    g["b_s"] = dbs[:, :, 0]
    do_s, ya_s, lse_s = _tok_to_streams(dya), _tok_to_streams(sv["ya"]), _tok_to_streams(sv["lse"])
    dq, dk, dv, dbias = _attn_bwd(sv["qkv_s"], bias, do_s, ya_s, lse_s)
    dqkv = jnp.stack([jnp.stack(_streams_to_tok(t), 1) for t in (dq, dk, dv)], 1)
    dproj = jnp.concatenate([dqkv.reshape(s, 3 * QKV_W).astype(BF16), dzu, dzv, duc, dg0, dg1, dg2], axis=1)
    n_in = dproj.shape[1]
    cw = _tile(n_in, 1024, LANES)
    g["b_in"] = _ew("b_in_grad", lambda a: _rowsum(a.astype(F32)), s, [R(dproj, cw, 0, True)], [], accs=[cw],
                    nj=n_in // cw, tr=512)
    g["w_in"], arrived[2] = _beside(sides, 2, functools.partial(_mm_tn, "proj_in_wgrad", sv["x"], dproj, tn=768))
    dx = _mm_nt("proj_in_dgrad", dproj, wl["w_in"], add=dh1, add_scale=alpha, tk=768)
    for k in ("ln2_g", "ln2_b", "ln1_g", "ln1_b", "b_glu", "d_skip", "sgu_ln_g", "sgu_ln_b", "b_in"):
        g[k] = g[k][0]
    return dx, g, dbias, arrived


def _loss_head(y, target):
    s, d = y.shape

    def fn(yb, tb):
        err = yb - tb
        return err / d, _rowsum(err * err)

    dy, sq = _ew("loss_head", fn, s, [R(y), R(target)], [(d, F32)], accs=[d])
    return dy, 0.5 * jnp.sum(sq) / d


def _step(x, target, w, m, v):
    depth = w["w_in"].shape[0]
    alpha = (2 * depth) ** 0.25
    bias = _band_bias(w["rel_bias"])
    wb = {n: w[n].astype(BF16) for n in BIG_NAMES}
    xl = x[0]
    layers, saved = [], []
    big = _first_weights(wb)
    for l in range(depth):
        wl = dict(big)
        for n in SMALL:
            if n != "rel_bias":
                wl[n] = w[n][l]
        wl["w_tril"] = jnp.tril(w["w_s"][l]).astype(BF16)
        wl["b_col"] = w["b_s"][l][:, :, None]
        pin = _ssm_param_inputs(w, l)
        abr, abi, bbr, bbi = _ssm_params_fwd(pin)
        wl["ssm_pin"] = pin
        wl["a3"] = jnp.stack([abr.reshape(-1, LANES), abi.reshape(-1, LANES)], 0)
        wl["bd"] = jnp.stack([_block_diag(bbr), _block_diag(bbi)], 0)
        wl["cd"] = _c_block_diag(w, l)
        xl, sv, got = _layer_fwd(xl, wl, bias, alpha, _gather_sides(wb, l + 1) if l + 1 < depth else None)
        if l + 1 < depth:
            big = _next_weights(wb, l + 1, got)
        layers.append(wl)
        saved.append(sv)
    dx, loss = _loss_head(xl, target[0])
    loss = lax.psum(loss, ("x", "y", "c"))
    grads = {n: [None] * depth for n in WEIGHTS if n != "rel_bias"}
    dbias_sum = None
    pending = None
    for l in reversed(range(depth)):
        dx, g, dbias, arrived = _layer_bwd(dx, saved[l], layers[l], bias, alpha,
                                           _exchange_sides(pending) if pending is not None else None)
        dbias_sum = dbias if dbias_sum is None else dbias_sum + dbias
        if pending is not None:
            red = _finish_reduce(pending, arrived)
            for n in BIG_NAMES:
                grads[n][l + 1] = red[n]
        pending = _start_reduce({n: g[n] for n in BIG_NAMES})
        for n in SMALL:
            if n != "rel_bias":
                grads[n][l] = g[n]
    red = _finish_reduce(pending, _run_exchange(pending))
    for n in BIG_NAMES:
        grads[n][0] = red[n]
    grads = {n: jnp.stack(gl, 0) for n, gl in grads.items()}
    grads["rel_bias"] = _bias_to_buckets(dbias_sum)
    small_vec = _pad_to(jnp.concatenate([grads[n].reshape(-1) for n in SMALL]), 8 * LANES).reshape(-1, LANES)
    small_sum = _sum_lead("small_grad_sum", _ag8("small_grad_gather", small_vec)).reshape(-1)
    off = 0
    for n in SMALL:
        size = math.prod(w[n].shape)
        grads[n] = small_sum[off:off + size].reshape(w[n].shape)
        off += size
    pack = lambda t: _pad_to(jnp.concatenate([t[n].reshape(-1) for n in SMALL]), 8 * LANES).reshape(-1, LANES)
    sd, sm, sv_ = _adamw("adamw_small", pack(w), small_sum.reshape(-1, LANES), pack(m), pack(v))
    delta, new_m, new_v = {}, {}, {}
    off = 0
    for n in SMALL:
        size = math.prod(w[n].shape)
        take = lambda t: t.reshape(-1)[off:off + size].reshape(w[n].shape)
        delta[n], new_m[n], new_v[n] = take(sd), take(sm), take(sv_)
        off += size
    for n in BIG_NAMES:
        delta[n], new_m[n], new_v[n] = _adamw("adamw_" + n, w[n], grads[n], m[n], v[n])
    return loss, dx[None], grads, delta, new_m, new_v


def kernel(x, w_in, b_in, rel_bias, sgu_ln_g, sgu_ln_b, w_s, b_s, lam_re, lam_im, log_dt, b_re, b_im, c_re, c_im, d_skip, w_glu, b_glu, w_pa, w_pb, w_pc, w_o, ln1_g, ln1_b, w_ffn_in, w_ffn_out, ln2_g, ln2_b, loss_target, m_w_in, m_b_in, m_rel_bias, m_sgu_ln_g, m_sgu_ln_b, m_w_s, m_b_s, m_lam_re, m_lam_im, m_log_dt, m_b_re, m_b_im, m_c_re, m_c_im, m_d_skip, m_w_glu, m_b_glu, m_w_pa, m_w_pb, m_w_pc, m_w_o, m_ln1_g, m_ln1_b, m_w_ffn_in, m_w_ffn_out, m_ln2_g, m_ln2_b, v_w_in, v_b_in, v_rel_bias, v_sgu_ln_g, v_sgu_ln_b, v_w_s, v_b_s, v_lam_re, v_lam_im, v_log_dt, v_b_re, v_b_im, v_c_re, v_c_im, v_d_skip, v_w_glu, v_b_glu, v_w_pa, v_w_pb, v_w_pc, v_w_o, v_ln1_g, v_ln1_b, v_w_ffn_in, v_w_ffn_out, v_ln2_g, v_ln2_b):
    w = dict(w_in=w_in, b_in=b_in, rel_bias=rel_bias, sgu_ln_g=sgu_ln_g, sgu_ln_b=sgu_ln_b, w_s=w_s, b_s=b_s,
             lam_re=lam_re, lam_im=lam_im, log_dt=log_dt, b_re=b_re, b_im=b_im, c_re=c_re, c_im=c_im, d_skip=d_skip,
             w_glu=w_glu, b_glu=b_glu, w_pa=w_pa, w_pb=w_pb, w_pc=w_pc, w_o=w_o, ln1_g=ln1_g, ln1_b=ln1_b,
             w_ffn_in=w_ffn_in, w_ffn_out=w_ffn_out, ln2_g=ln2_g, ln2_b=ln2_b)
    m = dict(w_in=m_w_in, b_in=m_b_in, rel_bias=m_rel_bias, sgu_ln_g=m_sgu_ln_g, sgu_ln_b=m_sgu_ln_b, w_s=m_w_s,
             b_s=m_b_s, lam_re=m_lam_re, lam_im=m_lam_im, log_dt=m_log_dt, b_re=m_b_re, b_im=m_b_im, c_re=m_c_re,
             c_im=m_c_im, d_skip=m_d_skip, w_glu=m_w_glu, b_glu=m_b_glu, w_pa=m_w_pa, w_pb=m_w_pb, w_pc=m_w_pc,
             w_o=m_w_o, ln1_g=m_ln1_g, ln1_b=m_ln1_b, w_ffn_in=m_w_ffn_in, w_ffn_out=m_w_ffn_out, ln2_g=m_ln2_g,
             ln2_b=m_ln2_b)
    v = dict(w_in=v_w_in, b_in=v_b_in, rel_bias=v_rel_bias, sgu_ln_g=v_sgu_ln_g, sgu_ln_b=v_sgu_ln_b, w_s=v_w_s,
             b_s=v_b_s, lam_re=v_lam_re, lam_im=v_lam_im, log_dt=v_log_dt, b_re=v_b_re, b_im=v_b_im, c_re=v_c_re,
             c_im=v_c_im, d_skip=v_d_skip, w_glu=v_w_glu, b_glu=v_b_glu, w_pa=v_w_pa, w_pb=v_w_pb, w_pc=v_w_pc,
             w_o=v_w_o, ln1_g=v_ln1_g, ln1_b=v_ln1_b, w_ffn_in=v_w_ffn_in, w_ffn_out=v_w_ffn_out, ln2_g=v_ln2_g,
             ln2_b=v_ln2_b)
    loss, grad_x, grads, delta, new_m, new_v = _step(x, loss_target, w, m, v)
    return (loss, grad_x, *[grads[n] for n in WEIGHTS], *[delta[n] for n in WEIGHTS],
            *[new_m[n] for n in WEIGHTS], *[new_v[n] for n in WEIGHTS])
```

```python
import functools
import math

import numpy as np
import jax
import jax.numpy as jnp
from jax import lax
from jax.experimental import pallas as pl
from jax.experimental.pallas import tpu as pltpu

F32 = jnp.float32
BF16 = jnp.bfloat16
MESH = pl.DeviceIdType.MESH

ATT_PATTERNS = ((128, 1), (512, 4), (2048, 16))
N_GROUPS_A = 3
HEADS = 8
HEAD_DIM = 64
WIDTH_A = HEADS * HEAD_DIM
QKV_W = N_GROUPS_A * WIDTH_A
ATT_BLOCK = 128
N_REL_BUCKETS = 32
REL_MAX_DIST = 2048
NEG_INF = -1e30
CHUNK = 128
WIDTH_B = 768
N_GROUPS_B = 6
GROUP_B = 128
WIDTH_C = 768
SSM_GROUP = 16
N_GROUPS_C = 48
SSM_STATE = 64
SSM_CH = N_GROUPS_C * SSM_STATE
N_BRANCH = 3
LN_EPS = 1e-5
ADAM_LR = 0.001
ADAM_B1 = 0.9
ADAM_B2 = 0.999
ADAM_EPS = 1e-08
ADAM_WD = 0.01
ADAM_STEP = 10

LANES = 128
SUBLANES_BF16 = 16
V7X_VMEM_BYTES = 64 * 1024 * 1024
VMEM_LIMIT = V7X_VMEM_BYTES * 7 // 8

OFF_ZB = 3 * QKV_W
REST_ZB = 0
REST_UC = 2 * WIDTH_B
REST_GL = 2 * WIDTH_B + WIDTH_C

BIG = (("w_in", "col"), ("w_glu", "row"), ("w_pa", "col"), ("w_pb", "col"), ("w_pc", "col"),
       ("w_o", "row"), ("w_ffn_in", "col"), ("w_ffn_out", "row"))
WEIGHTS = ("w_in", "b_in", "rel_bias", "sgu_ln_g", "sgu_ln_b", "w_s", "b_s", "lam_re", "lam_im", "log_dt",
           "b_re", "b_im", "c_re", "c_im", "d_skip", "w_glu", "b_glu", "w_pa", "w_pb", "w_pc", "w_o",
           "ln1_g", "ln1_b", "w_ffn_in", "w_ffn_out", "ln2_g", "ln2_b")
BIG_NAMES = tuple(n for n, _ in BIG)
SMALL = tuple(n for n in WEIGHTS if n not in BIG_NAMES)


def _tile(n, target, mult):
    t = (min(target, n) // mult) * mult
    while t >= mult:
        if n % t == 0:
            return t
        t -= mult
    return n


def _cparams(n_axes):
    return pltpu.CompilerParams(dimension_semantics=("arbitrary",) * n_axes, vmem_limit_bytes=VMEM_LIMIT)


_DIMS = {"nn": (((1,), (0,)), ((), ())), "nt": (((1,), (1,)), ((), ())), "tn": (((0,), (0,)), ((), ()))}


def _mm(name, mode, grid, a, a_spec, b, b_spec, out_sds, o_spec, acc_shape, *, bias=None, bias_spec=None,
        add=None, add_spec=None, add_scale=1.0, exact=False, side=None):
    nk = grid[2]
    has_bias = bias is not None
    has_add = add is not None
    n_side_in = len(side["ins"]) if side else 0
    n_side_out = len(side["out_sds"]) if side else 0
    n_side_sem = len(side["sems"]) if side else 0

    def body(*refs):
        a_ref, b_ref = refs[0], refs[1]
        pos = 2
        bias_ref = add_ref = None
        if has_bias:
            bias_ref = refs[pos]
            pos += 1
        if has_add:
            add_ref = refs[pos]
            pos += 1
        side_ins = refs[pos:pos + n_side_in]
        pos += n_side_in
        o_ref = refs[pos]
        side_outs = refs[pos + 1:pos + 1 + n_side_out]
        side_sems = refs[len(refs) - n_side_sem:] if n_side_sem else ()
        pos += n_side_out
        if side:
            at = [pl.program_id(d) for d in range(3)]

            @pl.when((at[0] == 0) & (at[1] == 0) & (at[2] == 0))
            def _():
                side["start"](side_ins, side_outs, side_sems)
        if exact:
            part = lax.dot_general(a_ref[...].astype(F32), b_ref[...].astype(F32), _DIMS[mode],
                                   preferred_element_type=F32, precision=lax.Precision.HIGHEST)
        else:
            part = lax.dot_general(a_ref[...].astype(BF16), b_ref[...].astype(BF16), _DIMS[mode],
                                   preferred_element_type=F32)

        def finish(r):
            if has_bias:
                r = r + bias_ref[...]
            if has_add:
                r = r + add_scale * add_ref[...].astype(F32)
            o_ref[...] = r.astype(o_ref.dtype)

        if nk == 1:
            finish(part)
        else:
            acc_ref = refs[pos + 1]
            k = pl.program_id(2)

            @pl.when(k == 0)
            def _():
                acc_ref[...] = part

            @pl.when(k > 0)
            def _():
                acc_ref[...] += part

            @pl.when(k == nk - 1)
            def _():
                finish(acc_ref[...])

        if side:
            @pl.when((at[0] == grid[0] - 1) & (at[1] == grid[1] - 1) & (at[2] == grid[2] - 1))
            def _():
                side["finish"](side_ins, side_outs, side_sems)

    ins, specs = [a, b], [a_spec, b_spec]
    if has_bias:
        ins.append(bias)
        specs.append(bias_spec)
    if has_add:
        ins.append(add)
        specs.append(add_spec)
    scratch = [pltpu.VMEM(acc_shape, F32)] if nk > 1 else []
    if not side:
        return pl.pallas_call(body, out_shape=out_sds, grid=grid, in_specs=specs, out_specs=o_spec,
                              scratch_shapes=scratch, compiler_params=_cparams(3), name=name)(*ins)
    res = pl.pallas_call(body, out_shape=[out_sds] + list(side["out_sds"]), grid=grid,
                         in_specs=specs + [ANY] * n_side_in, out_specs=[o_spec] + [ANY] * n_side_out,
                         scratch_shapes=scratch + list(side["sems"]), compiler_params=_cparams(3),
                         name=name + "_" + side["name"])(*ins, *side["ins"])
    return res[0], res[1:]


def _mm_nn(name, a, b, *, n0=0, n=None, a0=0, bias=None, add=None, add_scale=1.0, out_dtype=F32,
           tm=1024, tn=512, tk=None, side=None):
    m = a.shape[0]
    if b.ndim == 3:
        k, cc = b.shape[1:]
        n = 4 * cc
        tn = _tile(cc, tn, LANES)
        per = cc // tn
        tk = _tile(k, k if tk is None else tk, LANES)
        b_spec = pl.BlockSpec((None, tk, tn), lambda i, j, kk: (j // per, kk, j % per))
    else:
        k = b.shape[0]
        n = b.shape[1] - n0 if n is None else n
        tn = _tile(math.gcd(n, n0) if n0 else n, tn, LANES)
        tk = _tile(math.gcd(k, a0) if a0 else k, k if tk is None else tk, LANES)
        b_spec = pl.BlockSpec((tk, tn), lambda i, j, kk: (kk, n0 // tn + j))
    tm = _tile(m, tm, SUBLANES_BF16)
    jn0, ka0 = n0 // tn, a0 // tk
    o_spec = pl.BlockSpec((tm, tn), lambda i, j, kk: (i, j))
    return _mm(name, "nn", (m // tm, n // tn, k // tk),
               a, pl.BlockSpec((tm, tk), lambda i, j, kk: (i, ka0 + kk)), b, b_spec,
               jax.ShapeDtypeStruct((m, n), out_dtype), o_spec, (tm, tn),
               bias=bias, bias_spec=pl.BlockSpec((1, tn), lambda i, j, kk: (0, jn0 + j)),
               add=add, add_spec=o_spec, add_scale=add_scale, side=side)


def _mm_nt(name, a, b, *, n0=0, add=None, add_scale=1.0, out_dtype=F32, tm=512, tn=2048, tk=512, side=None):
    m, n = a.shape
    if b.ndim == 3:
        k, cc = b.shape[1:]
        tn = _tile(k, tn, LANES)
        tk = _tile(cc, tk, LANES)
        per = cc // tk
        b_spec = pl.BlockSpec((None, tn, tk), lambda i, j, kk: (kk // per, j, kk % per))
    else:
        k = b.shape[0]
        tn = _tile(k, tn, LANES)
        tk = _tile(math.gcd(n, n0) if n0 else n, tk, LANES)
        b_spec = pl.BlockSpec((tn, tk), lambda i, j, kk: (j, n0 // tk + kk))
    tm = _tile(m, tm, SUBLANES_BF16)
    o_spec = pl.BlockSpec((tm, tn), lambda i, j, kk: (i, j))
    return _mm(name, "nt", (m // tm, k // tn, n // tk),
               a, pl.BlockSpec((tm, tk), lambda i, j, kk: (i, kk)), b, b_spec,
               jax.ShapeDtypeStruct((m, k), out_dtype), o_spec, (tm, tn),
               add=add, add_spec=o_spec, add_scale=add_scale, side=side)


def _mm_tn(name, a, b, *, a0=0, ka=None, out_dtype=BF16, tm=2048, tn=1024, tk=1024, col_shards=False, side=None):
    s = a.shape[0]
    ka = a.shape[1] - a0 if ka is None else ka
    n = b.shape[1]
    tm = _tile(math.gcd(ka, a0) if a0 else ka, tm, LANES)
    tk = _tile(s, tk, SUBLANES_BF16)
    ia0 = a0 // tm
    if col_shards:
        cc = n // 4
        tn = _tile(cc, tn, LANES)
        per = cc // tn
        out_sds = jax.ShapeDtypeStruct((4, ka, cc), out_dtype)
        o_spec = pl.BlockSpec((None, tm, tn), lambda i, j, kk: (j // per, i, j % per))
    else:
        tn = _tile(n, tn, LANES)
        out_sds = jax.ShapeDtypeStruct((ka, n), out_dtype)
        o_spec = pl.BlockSpec((tm, tn), lambda i, j, kk: (i, j))
    return _mm(name, "tn", (ka // tm, n // tn, s // tk),
               a, pl.BlockSpec((tk, tm), lambda i, j, kk: (kk, ia0 + i)),
               b, pl.BlockSpec((tk, tn), lambda i, j, kk: (kk, j)), out_sds, o_spec, (tm, tn), side=side)


def R(arr, bw=None, c0=0, j=False):
    return ("r", arr, arr.shape[1] if bw is None else bw, c0, j)


def P(arr, bw=None, c0=0, j=False):
    return ("p", arr, arr.shape[1] if bw is None else bw, c0, j)


def _ew(name, fn, rows, ins, outs, accs=(), *, tr=256, nj=1):
    tr = _tile(rows, tr, SUBLANES_BF16)
    ni = rows // tr
    n_in, n_out, n_acc = len(ins), len(outs), len(accs)

    def spec(kind, bw, c0, follows):
        rows_b = tr if kind == "r" else 1
        if kind == "r":
            return pl.BlockSpec((rows_b, bw), (lambda j, i: (i, c0 + j)) if follows else (lambda j, i: (i, c0)))
        return pl.BlockSpec((rows_b, bw), (lambda j, i: (0, c0 + j)) if follows else (lambda j, i: (0, c0)))

    def body(*refs):
        res = fn(*[r[...] for r in refs[:n_in]])
        res = tuple(res) if isinstance(res, (tuple, list)) else (res,)
        for r, v in zip(refs[n_in:n_in + n_out], res[:n_out]):
            r[...] = v.astype(r.dtype)
        if n_acc:
            i = pl.program_id(1)
            for r, v in zip(refs[n_in + n_out:], res[n_out:]):
                @pl.when(i == 0)
                def _(r=r, v=v):
                    r[...] = v

                @pl.when(i > 0)
                def _(r=r, v=v):
                    r[...] += v

    out_shape = [jax.ShapeDtypeStruct((rows, bw * nj), dt) for bw, dt in outs]
    out_shape += [jax.ShapeDtypeStruct((1, bw * nj), F32) for bw in accs]
    out_specs = [pl.BlockSpec((tr, bw), lambda j, i: (i, j)) for bw, _ in outs]
    out_specs += [pl.BlockSpec((1, bw), lambda j, i: (0, j)) for bw in accs]
    res = pl.pallas_call(body, out_shape=out_shape, grid=(nj, ni),
                         in_specs=[spec(k, bw, c0, f) for k, _, bw, c0, f in ins], out_specs=out_specs,
                         compiler_params=_cparams(2), name=name)(*[a for _, a, _, _, _ in ins])
    return res if len(res) > 1 else res[0]


def _gelu(x):
    c = math.sqrt(2.0 / math.pi)
    return 0.5 * x * (1.0 + jnp.tanh(c * (x + 0.044715 * (x * x * x))))


def _sigmoid(x):
    return lax.logistic(x)


def _f_ln(h, g, b):
    mu = jnp.mean(h, axis=-1, keepdims=True)
    xc = h - mu
    var = jnp.mean(xc * xc, axis=-1, keepdims=True)
    return xc * lax.rsqrt(var + LN_EPS) * g + b


def _f_combine(o1, o2, o3, l1, l2, l3):
    m = jnp.maximum(jnp.maximum(l1, l2), l3)
    e1, e2, e3 = jnp.exp(l1 - m), jnp.exp(l2 - m), jnp.exp(l3 - m)
    den = e1 + e2 + e3
    return (e1 * o1 + e2 * o2 + e3 * o3) / den, m + jnp.log(den)


def _f_ssm_out(y_lin, uc, d_skip):
    return _gelu(y_lin + d_skip * uc)


def _f_glu(yc0, t):
    return yc0 * _sigmoid(t)


def _f_merge(g0, g1, g2, pa, pb, pc):
    return _sigmoid(g0) * pa + _sigmoid(g1) * pb + _sigmoid(g2) * pc


def _f_swiglu(gf, up):
    return gf * _sigmoid(gf) * up


def _f_sgu_pre(zu, zv, g, b):
    return _gelu(zu), _f_ln(_gelu(zv), g, b)


def _f_ssm_params(lr, li, ld, lr_rep, li_rep, ld_rep, br_t, bi_t):
    def disc(lr, li, ld):
        dt = jnp.exp(ld)
        mag = jnp.exp(lr * dt)
        th = li * dt
        abr, abi = mag * jnp.cos(th), mag * jnp.sin(th)
        nrm = lr * lr + li * li
        cr = ((abr - 1.0) * lr + abi * li) / nrm
        ci = (abi * lr - (abr - 1.0) * li) / nrm
        return abr, abi, cr, ci

    abr, abi, _, _ = disc(lr, li, ld)
    _, _, cr, ci = disc(lr_rep, li_rep, ld_rep)
    return abr, abi, cr * br_t - ci * bi_t, cr * bi_t + ci * br_t


def _rowsum(x):
    return jnp.sum(x, axis=0, keepdims=True)


def _to_streams(qkv):
    s = qkv.shape[0]
    t = qkv.reshape(s, 3, N_GROUPS_A, HEADS, HEAD_DIM)
    outs = []
    for g, (_, dil) in enumerate(ATT_PATTERNS):
        tg = t[:, :, g].reshape(s // dil, dil, 3, HEADS, HEAD_DIM)
        outs.append(tg.transpose(2, 3, 1, 0, 4).reshape(3, HEADS, s, HEAD_DIM))
    return jnp.stack(outs, 0)


def _tok_to_streams(a):
    s = a.shape[0]
    outs = []
    for _, dil in ATT_PATTERNS:
        t = a.reshape(s // dil, dil, HEADS, HEAD_DIM)
        outs.append(t.transpose(2, 1, 0, 3).reshape(HEADS, s, HEAD_DIM))
    return jnp.stack(outs, 0)


def _streams_to_tok(o):
    s = o.shape[2]
    outs = []
    for g, (_, dil) in enumerate(ATT_PATTERNS):
        t = o[g].reshape(HEADS, dil, s // dil, HEAD_DIM)
        outs.append(t.transpose(2, 1, 0, 3).reshape(s, WIDTH_A))
    return outs


def _t5_bucket(dist):
    max_exact = N_REL_BUCKETS // 2
    d = np.maximum(dist, 1).astype(np.float32)
    scale = (N_REL_BUCKETS - max_exact) / math.log(REL_MAX_DIST / max_exact)
    large = max_exact + (np.log(d / max_exact) * scale).astype(np.int32)
    large = np.minimum(large, N_REL_BUCKETS - 1)
    return np.where(dist < max_exact, dist, large).astype(np.int32)


def _bucket_maps():
    i = np.arange(ATT_BLOCK)[:, None]
    kk = np.arange(2 * ATT_BLOCK)[None, :]
    steps = np.maximum(ATT_BLOCK + i - kk, 0)
    return np.stack([_t5_bucket(steps * dil) for _, dil in ATT_PATTERNS], 0)


def _band_bias(rel_bias):
    q = ATT_BLOCK

    def body(rel_ref, m_ref, o_ref):
        g = pl.program_id(0)
        bm = m_ref[...]
        for h in range(HEADS):
            acc = jnp.zeros((q, 2 * q), F32)
            for bk in range(N_REL_BUCKETS):
                acc = jnp.where(bm == bk, rel_ref[bk, g * HEADS + h], acc)
            o_ref[h] = acc

    return pl.pallas_call(body, out_shape=jax.ShapeDtypeStruct((N_GROUPS_A, HEADS, q, 2 * q), F32), grid=(N_GROUPS_A,),
                          in_specs=[pl.BlockSpec(memory_space=pltpu.SMEM),
                                    pl.BlockSpec((None, q, 2 * q), lambda g: (g, 0, 0))],
                          out_specs=pl.BlockSpec((None, HEADS, q, 2 * q), lambda g: (g, 0, 0, 0)),
                          compiler_params=_cparams(1), name="rel_bias_band")(rel_bias, jnp.asarray(_bucket_maps()))


def _attn_masks(nbs):
    q = ATT_BLOCK
    g, b = pl.program_id(0), pl.program_id(1)
    nb = jnp.where(g == 0, nbs[0], jnp.where(g == 1, nbs[1], nbs[2]))
    shift = jnp.where(lax.rem(b, nb) != 0, 0, q)
    ii = lax.broadcasted_iota(jnp.int32, (q, q), 0)
    kk = lax.broadcasted_iota(jnp.int32, (q, q), 1)
    return kk >= ii + shift, kk <= ii


def _attn_logits(q, kp, kc, bias_h, mask_p, mask_c):
    scale = HEAD_DIM ** -0.5
    sp = lax.dot_general(q, kp, _DIMS["nt"], preferred_element_type=F32) * scale + bias_h[:, :ATT_BLOCK]
    sc = lax.dot_general(q, kc, _DIMS["nt"], preferred_element_type=F32) * scale + bias_h[:, ATT_BLOCK:]
    return jnp.where(mask_p, sp, NEG_INF), jnp.where(mask_c, sc, NEG_INF)


def _attn_specs(s):
    q, h, e = ATT_BLOCK, HEADS, HEAD_DIM
    blk = (None, None, h, q, e)
    prev = lambda b: jnp.maximum(b - 1, 0)
    qkv_specs = [pl.BlockSpec(blk, lambda g, b: (g, 0, 0, b, 0)),
                 pl.BlockSpec(blk, lambda g, b: (g, 1, 0, prev(b), 0)),
                 pl.BlockSpec(blk, lambda g, b: (g, 1, 0, b, 0)),
                 pl.BlockSpec(blk, lambda g, b: (g, 2, 0, prev(b), 0)),
                 pl.BlockSpec(blk, lambda g, b: (g, 2, 0, b, 0))]
    bias_spec = pl.BlockSpec((None, h, q, 2 * q), lambda g, b: (g, 0, 0, 0))
    row_spec = pl.BlockSpec((None, h, q, e), lambda g, b: (g, 0, b, 0))
    return qkv_specs, bias_spec, row_spec


def _attn_fwd(qkv_s, bias, side=None):
    s = qkv_s.shape[3]
    nblk = s // ATT_BLOCK
    nbs = tuple(s // dil // ATT_BLOCK for _, dil in ATT_PATTERNS)
    qkv_specs, bias_spec, row_spec = _attn_specs(s)
    n_side_in = len(side["ins"]) if side else 0
    n_side_out = len(side["out_sds"]) if side else 0

    def body(q_ref, kp_ref, kc_ref, vp_ref, vc_ref, b_ref, *rest):
        side_ins, (o_ref, l_ref) = rest[:n_side_in], rest[n_side_in:n_side_in + 2]
        side_outs = rest[n_side_in + 2:n_side_in + 2 + n_side_out]
        side_sems = rest[n_side_in + 2 + n_side_out:]
        if side:
            @pl.when((pl.program_id(0) == 0) & (pl.program_id(1) == 0))
            def _():
                side["start"](side_ins, side_outs, side_sems)

        mask_p, mask_c = _attn_masks(nbs)
        for h in range(HEADS):
            sp, sc = _attn_logits(q_ref[h], kp_ref[h], kc_ref[h], b_ref[h], mask_p, mask_c)
            m = jnp.maximum(jnp.max(sp, axis=1, keepdims=True), jnp.max(sc, axis=1, keepdims=True))
            pp, pc = jnp.exp(sp - m), jnp.exp(sc - m)
            den = jnp.sum(pp, axis=1, keepdims=True) + jnp.sum(pc, axis=1, keepdims=True)
            o = (lax.dot_general(pp.astype(BF16), vp_ref[h], _DIMS["nn"], preferred_element_type=F32)
                 + lax.dot_general(pc.astype(BF16), vc_ref[h], _DIMS["nn"], preferred_element_type=F32))
            o_ref[h] = o / den
            l_ref[h] = jnp.broadcast_to(m + jnp.log(den), (ATT_BLOCK, HEAD_DIM))

        if side:
            @pl.when((pl.program_id(0) == N_GROUPS_A - 1) & (pl.program_id(1) == nblk - 1))
            def _():
                side["finish"](side_ins, side_outs, side_sems)

    sds = jax.ShapeDtypeStruct((N_GROUPS_A, HEADS, s, HEAD_DIM), F32)
    if not side:
        return pl.pallas_call(body, out_shape=(sds, sds), grid=(N_GROUPS_A, nblk),
                              in_specs=qkv_specs + [bias_spec], out_specs=(row_spec, row_spec),
                              compiler_params=_cparams(2), name="attn_fwd")(qkv_s, qkv_s, qkv_s, qkv_s, qkv_s, bias), None
    res = pl.pallas_call(body, out_shape=[sds, sds] + list(side["out_sds"]), grid=(N_GROUPS_A, nblk),
                         in_specs=qkv_specs + [bias_spec] + [ANY] * n_side_in,
                         out_specs=[row_spec, row_spec] + [ANY] * n_side_out, scratch_shapes=list(side["sems"]),
                         compiler_params=_cparams(2), name="attn_fwd_" + side["name"])(
        qkv_s, qkv_s, qkv_s, qkv_s, qkv_s, bias, *side["ins"])
    return (res[0], res[1]), res[2:]


def _attn_bwd(qkv_s, bias, do_s, ya_s, lse_s):
    s = qkv_s.shape[3]
    nblk = s // ATT_BLOCK
    nbs = tuple(s // dil // ATT_BLOCK for _, dil in ATT_PATTERNS)
    scale = HEAD_DIM ** -0.5
    q, h, e = ATT_BLOCK, HEADS, HEAD_DIM

    def body(q_ref, kp_ref, kc_ref, vp_ref, vc_ref, b_ref, do_ref, ya_ref, l_ref,
             dq_ref, dk_ref, dv_ref, db_ref, dk_own, dv_own):
        b = pl.program_id(1)
        mask_p, mask_c = _attn_masks(nbs)

        @pl.when(b == 0)
        def _():
            db_ref[...] = jnp.zeros_like(db_ref)
            dk_own[...] = jnp.zeros_like(dk_own)
            dv_own[...] = jnp.zeros_like(dv_own)

        @pl.when(b < nblk)
        def _():
            for hh in range(h):
                qh, kp, kc, vp, vc = q_ref[hh], kp_ref[hh], kc_ref[hh], vp_ref[hh], vc_ref[hh]
                sp, sc = _attn_logits(qh, kp, kc, b_ref[hh], mask_p, mask_c)
                lse = l_ref[hh][:, 0:1]
                pp, pc = jnp.exp(sp - lse), jnp.exp(sc - lse)
                do = do_ref[hh]
                dsum = jnp.sum(do * ya_ref[hh], axis=1, keepdims=True)
                dob = do.astype(BF16)
                dsp = pp * (lax.dot_general(dob, vp, _DIMS["nt"], preferred_element_type=F32) - dsum)
                dsc = pc * (lax.dot_general(dob, vc, _DIMS["nt"], preferred_element_type=F32) - dsum)
                db_ref[hh, :, :q] += dsp
                db_ref[hh, :, q:] += dsc
                dspb, dscb = dsp.astype(BF16), dsc.astype(BF16)
                dq_ref[hh] = scale * (lax.dot_general(dspb, kp, _DIMS["nn"], preferred_element_type=F32)
                                      + lax.dot_general(dscb, kc, _DIMS["nn"], preferred_element_type=F32))
                dkp = scale * lax.dot_general(dspb, qh, _DIMS["tn"], preferred_element_type=F32)
                dvp = lax.dot_general(pp.astype(BF16), dob, _DIMS["tn"], preferred_element_type=F32)

                dk_ref[hh] = dk_own[hh] + dkp
                dv_ref[hh] = dv_own[hh] + dvp
                dk_own[hh] = scale * lax.dot_general(dscb, qh, _DIMS["tn"], preferred_element_type=F32)
                dv_own[hh] = lax.dot_general(pc.astype(BF16), dob, _DIMS["tn"], preferred_element_type=F32)

        @pl.when(b == nblk)
        def _():
            dk_ref[...] = dk_own[...]
            dv_ref[...] = dv_own[...]

    blk5 = (None, None, h, q, e)
    cur = lambda b: jnp.minimum(b, nblk - 1)
    prev = lambda b: jnp.maximum(cur(b) - 1, 0)
    qkv_specs = [pl.BlockSpec(blk5, lambda g, b: (g, 0, 0, cur(b), 0)),
                 pl.BlockSpec(blk5, lambda g, b: (g, 1, 0, prev(b), 0)),
                 pl.BlockSpec(blk5, lambda g, b: (g, 1, 0, cur(b), 0)),
                 pl.BlockSpec(blk5, lambda g, b: (g, 2, 0, prev(b), 0)),
                 pl.BlockSpec(blk5, lambda g, b: (g, 2, 0, cur(b), 0))]
    bias_spec = pl.BlockSpec((None, h, q, 2 * q), lambda g, b: (g, 0, 0, 0))
    row_spec = pl.BlockSpec((None, h, q, e), lambda g, b: (g, 0, cur(b), 0))
    lag_spec = pl.BlockSpec((None, h, q, e), lambda g, b: (g, 0, jnp.maximum(b - 1, 0), 0))
    sds = jax.ShapeDtypeStruct((N_GROUPS_A, h, s, e), F32)
    own = pltpu.VMEM((h, q, e), F32)
    return pl.pallas_call(body, out_shape=(sds,) * 3 + (jax.ShapeDtypeStruct(bias.shape, F32),),
                          grid=(N_GROUPS_A, nblk + 1),
                          in_specs=qkv_specs + [bias_spec, row_spec, row_spec, row_spec],
                          out_specs=(row_spec, lag_spec, lag_spec, bias_spec), scratch_shapes=[own, own],
                          compiler_params=_cparams(2), name="attn_bwd")(
        qkv_s, qkv_s, qkv_s, qkv_s, qkv_s, bias, do_s, ya_s, lse_s)


def _bias_to_buckets(dbias):
    bmap = jnp.asarray(_bucket_maps())
    q = ATT_BLOCK

    def body(db_ref, m_ref, o_ref):
        lane = lax.broadcasted_iota(jnp.int32, (HEADS, LANES), 1)
        row = lax.broadcasted_iota(jnp.int32, (HEADS, LANES), 0)
        acc = jnp.zeros((HEADS, LANES), F32)
        bm = m_ref[...]
        for h in range(HEADS):
            dbh = db_ref[h]
            for bk in range(N_REL_BUCKETS):
                sv = jnp.sum(jnp.sum(jnp.where(bm == bk, dbh, 0.0), axis=1, keepdims=True), axis=0, keepdims=True)
                acc = acc + jnp.where((lane == bk) & (row == h), sv, 0.0)
        o_ref[...] = acc

    out = pl.pallas_call(body, out_shape=jax.ShapeDtypeStruct((N_GROUPS_A, HEADS, LANES), F32), grid=(N_GROUPS_A,),
                         in_specs=[pl.BlockSpec((None, HEADS, q, 2 * q), lambda g: (g, 0, 0, 0)),
                                   pl.BlockSpec((None, q, 2 * q), lambda g: (g, 0, 0))],
                         out_specs=pl.BlockSpec((None, HEADS, LANES), lambda g: (g, 0, 0)),
                         compiler_params=_cparams(1), name="rel_bias_grad")(dbias, bmap)
    return out[:, :, :N_REL_BUCKETS].reshape(N_GROUPS_A * HEADS, N_REL_BUCKETS).T


def _sgu_specs():
    c, w = CHUNK, WIDTH_B
    return [pl.BlockSpec((c, w), lambda i: (i, 0)), pl.BlockSpec((c, w), lambda i: (i, 1)),
            pl.BlockSpec((1, w), lambda i: (0, 0)), pl.BlockSpec((1, w), lambda i: (0, 0)),
            pl.BlockSpec((N_GROUPS_B, c, c), lambda i: (0, 0, 0)), pl.BlockSpec((N_GROUPS_B, c, 1), lambda i: (0, 0, 0))]


def _sgu_fwd(rest, ln_g, ln_b, w_tril, b_col):
    s = rest.shape[0]

    def body(zu_ref, zv_ref, g_ref, b_ref, w_ref, bs_ref, y_ref):
        u, vn = _f_sgu_pre(zu_ref[...], zv_ref[...], g_ref[...], b_ref[...])
        for gi in range(N_GROUPS_B):
            sl = slice(gi * GROUP_B, (gi + 1) * GROUP_B)
            mixed = lax.dot_general(w_ref[gi], vn[:, sl].astype(BF16), _DIMS["nn"], preferred_element_type=F32)
            y_ref[:, sl] = u[:, sl] * (mixed + bs_ref[gi])

    return pl.pallas_call(body, out_shape=jax.ShapeDtypeStruct((s, WIDTH_B), F32), grid=(s // CHUNK,),
                          in_specs=_sgu_specs(), out_specs=pl.BlockSpec((CHUNK, WIDTH_B), lambda i: (i, 0)),
                          compiler_params=_cparams(1), name="sgu_fwd")(rest, rest, ln_g, ln_b, w_tril, b_col)


def _sgu_bwd(rest, ln_g, ln_b, w_tril, b_col, dyb):
    s = rest.shape[0]
    c, w, ng = CHUNK, WIDTH_B, N_GROUPS_B

    def body(zu_ref, zv_ref, g_ref, b_ref, w_ref, bs_ref, dy_ref, dzu_ref, dzv_ref, dw_ref, dbs_ref, dg_ref, db_ref):
        (u, vn), vjp = jax.vjp(_f_sgu_pre, zu_ref[...], zv_ref[...], g_ref[...], b_ref[...])
        first = pl.program_id(0) == 0
        du, dvn = [], []
        for gi in range(ng):
            sl = slice(gi * GROUP_B, (gi + 1) * GROUP_B)
            vg = vn[:, sl].astype(BF16)
            mixed = lax.dot_general(w_ref[gi], vg, _DIMS["nn"], preferred_element_type=F32) + bs_ref[gi]
            dy = dy_ref[:, sl]
            dmix = dy * u[:, sl]
            du.append(dy * mixed)
            dmb = dmix.astype(BF16)
            dvn.append(lax.dot_general(w_ref[gi], dmb, _DIMS["tn"], preferred_element_type=F32))
            dwg = lax.dot_general(dmb, vg, _DIMS["nt"], preferred_element_type=F32)
            dbg = jnp.sum(dmix, axis=1, keepdims=True)

            @pl.when(first)
            def _(gi=gi, dwg=dwg, dbg=dbg):
                dw_ref[gi] = dwg
                dbs_ref[gi] = dbg

            @pl.when(jnp.logical_not(first))
            def _(gi=gi, dwg=dwg, dbg=dbg):
                dw_ref[gi] += dwg
                dbs_ref[gi] += dbg

        dzu, dzv, dg, db = vjp((jnp.concatenate(du, axis=1), jnp.concatenate(dvn, axis=1)))
        dzu_ref[...] = dzu.astype(dzu_ref.dtype)
        dzv_ref[...] = dzv.astype(dzv_ref.dtype)

        @pl.when(first)
        def _():
            dg_ref[...] = dg
            db_ref[...] = db

        @pl.when(jnp.logical_not(first))
        def _():
            dg_ref[...] += dg
            db_ref[...] += db

    row = pl.BlockSpec((c, w), lambda i: (i, 0))
    par = pl.BlockSpec((1, w), lambda i: (0, 0))
    return pl.pallas_call(
        body, grid=(s // c,),
        out_shape=(jax.ShapeDtypeStruct((s, w), BF16), jax.ShapeDtypeStruct((s, w), BF16),
                   jax.ShapeDtypeStruct((ng, c, c), F32), jax.ShapeDtypeStruct((ng, c, 1), F32),
                   jax.ShapeDtypeStruct((1, w), F32), jax.ShapeDtypeStruct((1, w), F32)),
        in_specs=_sgu_specs() + [row],
        out_specs=(row, row, pl.BlockSpec((ng, c, c), lambda i: (0, 0, 0)), pl.BlockSpec((ng, c, 1), lambda i: (0, 0, 0)),
                   par, par),
        compiler_params=_cparams(1), name="sgu_bwd")(rest, rest, ln_g, ln_b, w_tril, b_col, dyb)


SCAN_T = 128


def _to_time_major(dst, src_ref, lead):
    for r in range(SSM_CH // LANES):
        dst[:, r, :] = src_ref[lead, :, r * LANES:(r + 1) * LANES]


def _from_time_major(dst_ref, lead, src):
    for r in range(SSM_CH // LANES):
        dst_ref[lead, :, r * LANES:(r + 1) * LANES] = src[:, r, :]


def _scan_fwd(bu, a3):
    s = bu.shape[1]
    t_blk = _tile(s, SCAN_T, 8)
    rows = SSM_CH // LANES

    def body(bu_ref, a_ref, x_ref, b3r, b3i, x3r, x3i, carry):
        @pl.when(pl.program_id(0) == 0)
        def _():
            carry[...] = jnp.zeros_like(carry)

        _to_time_major(b3r, bu_ref, 0)
        _to_time_major(b3i, bu_ref, 1)
        ar, ai = a_ref[0], a_ref[1]

        def step(t, c):
            xr, xi = c
            nr = ar * xr - ai * xi + b3r[t]
            ni = ar * xi + ai * xr + b3i[t]
            x3r[t] = nr
            x3i[t] = ni
            return nr, ni

        xr, xi = lax.fori_loop(0, t_blk, step, (carry[0], carry[1]), unroll=8)
        carry[0] = xr
        carry[1] = xi
        _from_time_major(x_ref, 0, x3r)
        _from_time_major(x_ref, 1, x3i)

    blk = pl.BlockSpec((2, t_blk, SSM_CH), lambda i: (0, i, 0))
    tm = pltpu.VMEM((t_blk, rows, LANES), F32)
    return pl.pallas_call(body, out_shape=jax.ShapeDtypeStruct(bu.shape, F32), grid=(s // t_blk,),
                          in_specs=[blk, pl.BlockSpec((2, rows, LANES), lambda i: (0, 0, 0))], out_specs=blk,
                          scratch_shapes=[tm, tm, tm, tm, pltpu.VMEM((2, rows, LANES), F32)],
                          compiler_params=_cparams(1), name="ssm_scan_fwd")(bu, a3)


def _scan_bwd(dx, x, a3):
    s = dx.shape[1]
    t_blk = _tile(s, SCAN_T, 8)
    nb = s // t_blk
    rows = SSM_CH // LANES

    def body(dx_ref, x_ref, a_ref, g_ref, da_ref, d3r, d3i, x3r, x3i, g3r, g3i, carry):
        first = pl.program_id(0) == 0

        @pl.when(first)
        def _():
            carry[...] = jnp.zeros_like(carry)

        _to_time_major(d3r, dx_ref, 0)
        _to_time_major(d3i, dx_ref, 1)
        _to_time_major(x3r, x_ref, 0)
        _to_time_major(x3i, x_ref, 1)
        ar, ai = a_ref[0], a_ref[1]

        def step(k, c):
            t = t_blk - 1 - k
            gr, gi, dar, dai = c
            xr, xi = x3r[t], x3i[t]
            dar = dar + gr * xr + gi * xi
            dai = dai + gi * xr - gr * xi
            ngr = d3r[t] + ar * gr + ai * gi
            ngi = d3i[t] + ar * gi - ai * gr
            g3r[t] = ngr
            g3i[t] = ngi
            return ngr, ngi, dar, dai

        zero = jnp.zeros((rows, LANES), F32)
        gr, gi, dar, dai = lax.fori_loop(0, t_blk, step, (carry[0], carry[1], zero, zero), unroll=8)
        carry[0] = gr
        carry[1] = gi

        @pl.when(first)
        def _():
            da_ref[0] = dar
            da_ref[1] = dai

        @pl.when(jnp.logical_not(first))
        def _():
            da_ref[0] += dar
            da_ref[1] += dai

        _from_time_major(g_ref, 0, g3r)
        _from_time_major(g_ref, 1, g3i)

    blk = pl.BlockSpec((2, t_blk, SSM_CH), lambda i: (0, nb - 1 - i, 0))
    par = pl.BlockSpec((2, rows, LANES), lambda i: (0, 0, 0))
    tm = pltpu.VMEM((t_blk, rows, LANES), F32)
    return pl.pallas_call(body, out_shape=(jax.ShapeDtypeStruct(dx.shape, F32), jax.ShapeDtypeStruct((2, rows, LANES), F32)),
                          grid=(nb,), in_specs=[blk, blk, par], out_specs=(blk, par),
                          scratch_shapes=[tm] * 6 + [pltpu.VMEM((2, rows, LANES), F32)],
                          compiler_params=_cparams(1), name="ssm_scan_bwd")(dx, x, a3)


SSM_TILES = WIDTH_C // LANES
SSM_TILE_W = SSM_CH // SSM_TILES


def _block_diag(m):
    gpt = N_GROUPS_C // SSM_TILES
    t = m.reshape(SSM_TILES, gpt, SSM_GROUP, SSM_STATE)
    eye = jnp.eye(gpt, dtype=m.dtype)
    return (t[:, :, :, None, :] * eye[None, :, None, :, None]).reshape(SSM_TILES, LANES, SSM_TILE_W)


def _block_diag_extract(t):
    gpt = N_GROUPS_C // SSM_TILES
    t = t.reshape(SSM_TILES, gpt, SSM_GROUP, gpt, SSM_STATE)
    eye = jnp.eye(gpt, dtype=t.dtype)
    return jnp.sum(t * eye[None, :, None, :, None], axis=3).reshape(WIDTH_C, SSM_STATE)


def _ssm_in(uc_src, uc_col0, bd):
    s = uc_src.shape[0]
    tm = _tile(s, 2048, 8)
    j0 = uc_col0 // LANES
    return _mm("ssm_in", "nn", (s // tm, 2 * SSM_TILES, 1),
               uc_src, pl.BlockSpec((tm, LANES), lambda i, j, k: (i, j0 + j % SSM_TILES)),
               bd, pl.BlockSpec((None, None, LANES, SSM_TILE_W), lambda i, j, k: (j // SSM_TILES, j % SSM_TILES, 0, 0)),
               jax.ShapeDtypeStruct((2, s, SSM_CH), F32),
               pl.BlockSpec((None, tm, SSM_TILE_W), lambda i, j, k: (j // SSM_TILES, i, j % SSM_TILES)), None)


def _ssm_in_dgrad(g, bd):
    s = g.shape[1]
    tm = _tile(s, 2048, 8)
    return _mm("ssm_in_dgrad", "nt", (s // tm, SSM_TILES, 2),
               g, pl.BlockSpec((None, tm, SSM_TILE_W), lambda i, j, k: (k, i, j)),
               bd, pl.BlockSpec((None, None, LANES, SSM_TILE_W), lambda i, j, k: (k, j, 0, 0)),
               jax.ShapeDtypeStruct((s, WIDTH_C), F32), pl.BlockSpec((tm, LANES), lambda i, j, k: (i, j)),
               (tm, LANES))


def _ssm_in_wgrad(uc_src, uc_col0, g):
    s = g.shape[1]
    tk = _tile(s, 2048, 8)
    j0 = uc_col0 // LANES
    return _mm("ssm_in_wgrad", "tn", (2, SSM_TILES, s // tk),
               uc_src, pl.BlockSpec((tk, LANES), lambda i, j, k: (k, j0 + j)),
               g, pl.BlockSpec((None, tk, SSM_TILE_W), lambda i, j, k: (i, k, j)),
               jax.ShapeDtypeStruct((2, SSM_TILES, LANES, SSM_TILE_W), F32),
               pl.BlockSpec((None, None, LANES, SSM_TILE_W), lambda i, j, k: (i, j, 0, 0)), (LANES, SSM_TILE_W))


def _ssm_out(x, cd):
    s = x.shape[1]
    tm = _tile(s, 2048, 8)
    return _mm("ssm_out", "nn", (s // tm, SSM_TILES, 2),
               x, pl.BlockSpec((None, tm, SSM_TILE_W), lambda i, j, k: (k, i, j)),
               cd, pl.BlockSpec((None, None, SSM_TILE_W, LANES), lambda i, j, k: (k, j, 0, 0)),
               jax.ShapeDtypeStruct((s, WIDTH_C), F32), pl.BlockSpec((tm, LANES), lambda i, j, k: (i, j)),
               (tm, LANES))


def _ssm_out_dgrad(dy, cd):
    s = dy.shape[0]
    tm = _tile(s, 2048, 8)
    return _mm("ssm_out_dgrad", "nt", (s // tm, 2 * SSM_TILES, 1),
               dy, pl.BlockSpec((tm, LANES), lambda i, j, k: (i, j % SSM_TILES)),
               cd, pl.BlockSpec((None, None, SSM_TILE_W, LANES), lambda i, j, k: (j // SSM_TILES, j % SSM_TILES, 0, 0)),
               jax.ShapeDtypeStruct((2, s, SSM_CH), F32),
               pl.BlockSpec((None, tm, SSM_TILE_W), lambda i, j, k: (j // SSM_TILES, i, j % SSM_TILES)), None)


def _ssm_out_wgrad(x, dy):
    s = dy.shape[0]
    tk = _tile(s, 2048, 8)
    return _mm("ssm_out_wgrad", "tn", (2, SSM_TILES, s // tk),
               x, pl.BlockSpec((None, tk, SSM_TILE_W), lambda i, j, k: (i, k, j)),
               dy, pl.BlockSpec((tk, LANES), lambda i, j, k: (k, j)),
               jax.ShapeDtypeStruct((2, SSM_TILES, SSM_TILE_W, LANES), F32),
               pl.BlockSpec((None, None, SSM_TILE_W, LANES), lambda i, j, k: (i, j, 0, 0)), (SSM_TILE_W, LANES))


def _ssm_param_inputs(w, l):
    rep = lambda a: jnp.repeat(a, SSM_GROUP, axis=0)
    lr, li, ld = w["lam_re"][l], w["lam_im"][l], w["log_dt"][l][:, None]
    br_t = jnp.transpose(w["b_re"][l], (0, 2, 1)).reshape(WIDTH_C, SSM_STATE)
    bi_t = jnp.transpose(w["b_im"][l], (0, 2, 1)).reshape(WIDTH_C, SSM_STATE)
    return lr, li, ld, rep(lr), rep(li), rep(ld), br_t, bi_t


def _ssm_params_fwd(pin):
    def body(*refs):
        res = _f_ssm_params(*[r[...] for r in refs[:8]])
        for r, v in zip(refs[8:], res):
            r[...] = v

    g, p = N_GROUPS_C, SSM_STATE
    return pl.pallas_call(body, out_shape=(jax.ShapeDtypeStruct((g, p), F32),) * 2
                          + (jax.ShapeDtypeStruct((WIDTH_C, p), F32),) * 2, name="ssm_params_fwd")(*pin)


def _ssm_params_bwd(pin, d_abr, d_abi, d_bbr, d_bbi):
    g, p = N_GROUPS_C, SSM_STATE
    group_sum = jnp.asarray(np.kron(np.eye(g, dtype=np.float32), np.ones((1, SSM_GROUP), np.float32)))

    def body(*refs):
        ins = [r[...] for r in refs[:8]]
        cts = tuple(r[...] for r in refs[8:12])
        gs = refs[12][...]
        d_lr_ref, d_li_ref, d_ld_ref, d_br_ref, d_bi_ref = refs[13:]
        _, vjp = jax.vjp(_f_ssm_params, *ins)
        d = vjp(cts)
        fold = lambda v: lax.dot_general(gs, v, _DIMS["nn"], preferred_element_type=F32, precision=lax.Precision.HIGHEST)
        d_lr_ref[...] = d[0] + fold(d[3])
        d_li_ref[...] = d[1] + fold(d[4])
        d_ld_ref[...] = d[2] + fold(jnp.broadcast_to(d[5], (WIDTH_C, p)))[:, 0:1]
        d_br_ref[...] = d[6]
        d_bi_ref[...] = d[7]

    return pl.pallas_call(body, out_shape=(jax.ShapeDtypeStruct((g, p), F32), jax.ShapeDtypeStruct((g, p), F32),
                                           jax.ShapeDtypeStruct((g, 1), F32), jax.ShapeDtypeStruct((WIDTH_C, p), F32),
                                           jax.ShapeDtypeStruct((WIDTH_C, p), F32)), name="ssm_params_bwd")(
        *pin, d_abr, d_abi, d_bbr, d_bbi, group_sum)


def _c_block_diag(w, l):
    def one(c):
        return jnp.transpose(_block_diag(c.reshape(WIDTH_C, SSM_STATE)), (0, 2, 1))
    return jnp.stack([one(w["c_re"][l]), -one(w["c_im"][l])], 0)


ANY = pl.BlockSpec(memory_space=pl.ANY)


def _ag8(name, xb):
    def body(x_ref, out_ref, send_sems, recv_sems, local_sem):
        x, y, c = lax.axis_index("x"), lax.axis_index("y"), lax.axis_index("c")
        me, sibling = (x, y, c), (x, y, 1 - c)
        chips = [(1 - x, y), (x, 1 - y), (1 - x, 1 - y)]

        def rows(px, py, pc):
            return out_ref.at[4 * px + 2 * py + pc]

        def copy(k, block, to, src=None):
            return pltpu.make_async_remote_copy(src_ref=rows(*block) if src is None else src, dst_ref=rows(*block),
                                                send_sem=send_sems.at[k], recv_sem=recv_sems.at[k],
                                                device_id=to, device_id_type=MESH)

        mine = pltpu.make_async_copy(x_ref, rows(*me), local_sem)
        mine.start()
        first = [copy(0, me, sibling, src=x_ref)]
        first += [copy(1 + j, me, (*chip, c), src=x_ref) for j, chip in enumerate(chips)]
        for cp in first:
            cp.start()
        passed = [copy(4 + j, (*chip, c), sibling) for j, chip in enumerate(chips)]
        for j, chip in enumerate(chips):
            copy(1 + j, (*chip, c), me).wait_recv()
            passed[j].start()
        copy(0, sibling, me).wait_recv()
        for j, chip in enumerate(chips):
            copy(4 + j, (*chip, 1 - c), me).wait_recv()
        for cp in first + passed:
            cp.wait_send()
        mine.wait()

    return pl.pallas_call(body, out_shape=jax.ShapeDtypeStruct((8,) + xb.shape, xb.dtype), in_specs=[ANY], out_specs=ANY,
                          scratch_shapes=[pltpu.SemaphoreType.DMA((7,)), pltpu.SemaphoreType.DMA((7,)),
                                          pltpu.SemaphoreType.DMA(())], name=name)(xb)


def _mesh_place():
    x, y, c = lax.axis_index("x"), lax.axis_index("y"), lax.axis_index("c")
    return x, y, c, [(1 - x, y), (x, 1 - y), (1 - x, 1 - y)]


ICI_CHUNK_BYTES = 2 << 20
D2D_CHUNK_BYTES = 1 << 20


def _split_rows(rows, row_bytes, chunk_bytes):
    k = 1
    for cand in range(1, max(1, (rows * row_bytes) // chunk_bytes) + 1):
        if rows % cand == 0 and (rows // cand) % SUBLANES_BF16 == 0:
            k = cand
    return [(j * (rows // k), rows // k) for j in range(k)]


def _run_side(side):
    ni, no = len(side["ins"]), len(side["out_sds"])

    def body(*refs):
        ins, outs, sems = refs[:ni], refs[ni:ni + no], refs[ni + no:]
        side["start"](ins, outs, sems)
        side["finish"](ins, outs, sems)

    return pl.pallas_call(body, out_shape=list(side["out_sds"]), in_specs=[ANY] * ni, out_specs=[ANY] * no,
                          scratch_shapes=list(side["sems"]), name=side["name"])(*side["ins"])


def _gather_side(wb, l):
    nw = len(wb)
    halves = [a.shape[1] // 2 for a in wb]
    pieces = [_split_rows(h, a.shape[2] * a.dtype.itemsize, ICI_CHUNK_BYTES) for a, h in zip(wb, halves)]

    def rows(ref, i, hc, r0=0, n=None):
        return ref.at[pl.ds(hc * halves[i] + r0, halves[i] if n is None else n)]

    def copier(sems):
        def copy(i, k, src, dst, to):
            return pltpu.make_async_remote_copy(src_ref=src, dst_ref=dst, send_sem=sems[0].at[6 * i + k],
                                                recv_sem=sems[1].at[6 * i + k], device_id=to, device_id_type=MESH)
        return copy

    def start(ins, outs, sems):
        x, y, c, chips = _mesh_place()
        me = 2 * x + y
        copy = copier(sems)
        for i in range(nw):
            for hc in range(2):
                for r0, n in pieces[i]:
                    pltpu.make_async_copy(rows(ins[i].at[l], i, hc, r0, n), rows(outs[i].at[me], i, hc, r0, n),
                                          sems[2].at[i]).start()
        for i in range(nw):
            for k, (px, py) in enumerate(chips):
                for r0, n in pieces[i]:
                    copy(i, k, rows(ins[i].at[l], i, c, r0, n), rows(outs[i].at[me], i, c, r0, n), (px, py, c)).start()

    def finish(ins, outs, sems):
        x, y, c, chips = _mesh_place()
        me = 2 * x + y
        copy = copier(sems)
        for k, (px, py) in enumerate(chips):
            for i in range(nw):
                src = outs[i].at[2 * px + py]
                copy(i, k, rows(src, i, c), rows(src, i, c), (px, py, c)).wait_recv()
                for r0, n in pieces[i]:
                    copy(i, 3 + k, rows(src, i, c, r0, n), rows(src, i, c, r0, n), (x, y, 1 - c)).start()
        for k, (px, py) in enumerate(chips):
            for i in range(nw):
                other = rows(outs[i].at[2 * px + py], i, 1 - c)
                copy(i, 3 + k, other, other, (x, y, 1 - c)).wait_recv()
        for i in range(nw):
            whole = rows(outs[i].at[me], i, c)
            for k in range(6):
                copy(i, k, whole, whole, (x, y, 1 - c)).wait_send()
            pltpu.make_async_copy(ins[i].at[l], outs[i].at[me], sems[2].at[i]).wait()

    return dict(name="gather_weights", ins=list(wb), out_sds=[jax.ShapeDtypeStruct((4,) + a.shape[1:], a.dtype) for a in wb],
                sems=[pltpu.SemaphoreType.DMA((6 * nw,)), pltpu.SemaphoreType.DMA((6 * nw,)), pltpu.SemaphoreType.DMA((nw,))],
                start=start, finish=finish)


def _grad_core_swap(g4):
    nw = len(g4)
    halves = [a.shape[1] // 2 for a in g4]
    pieces = [_split_rows(h, a.shape[2] * a.dtype.itemsize, D2D_CHUNK_BYTES) for a, h in zip(g4, halves)]

    def body(*refs):
        ins, theirs = refs[:nw], refs[nw:2 * nw]
        send_sems, recv_sems = refs[2 * nw:]
        x, y, c, _ = _mesh_place()

        def copy(i, src, dst):
            return pltpu.make_async_remote_copy(src_ref=src, dst_ref=dst, send_sem=send_sems.at[i], recv_sem=recv_sems.at[i],
                                                device_id=(x, y, 1 - c), device_id_type=MESH)

        for i in range(nw):
            for j in range(4):
                for r0, n in pieces[i]:
                    copy(i, ins[i].at[j, pl.ds((1 - c) * halves[i] + r0, n)], theirs[i].at[j, pl.ds(r0, n)]).start()
        for i in range(nw):
            copy(i, ins[i].at[:, pl.ds((1 - c) * halves[i], halves[i])], theirs[i]).wait()

    sds = [jax.ShapeDtypeStruct((4, h) + a.shape[2:], a.dtype) for a, h in zip(g4, halves)]
    return pl.pallas_call(body, out_shape=sds, in_specs=[ANY] * nw, out_specs=[ANY] * nw,
                          scratch_shapes=[pltpu.SemaphoreType.DMA((nw,)), pltpu.SemaphoreType.DMA((nw,))],
                          name="grad_core_swap")(*g4)


def _exchange_side(t4):
    nw = len(t4)
    pieces = [_split_rows(a.shape[1], a.shape[2] * a.dtype.itemsize, ICI_CHUNK_BYTES) for a in t4]

    def copier(sems, c):
        def copy(i, k, src, dst, px, py):
            return pltpu.make_async_remote_copy(src_ref=src, dst_ref=dst, send_sem=sems[0].at[3 * i + k],
                                                recv_sem=sems[1].at[3 * i + k], device_id=(px, py, c),
                                                device_id_type=MESH)
        return copy

    def start(ins, outs, sems):
        x, y, c, chips = _mesh_place()
        me = 2 * x + y
        copy = copier(sems, c)
        for i in range(nw):
            for r0, n in pieces[i]:
                for k, (px, py) in enumerate(chips):
                    copy(i, k, ins[i].at[2 * px + py, pl.ds(r0, n)], outs[i].at[me, pl.ds(r0, n)], px, py).start()
                pltpu.make_async_copy(ins[i].at[me, pl.ds(r0, n)], outs[i].at[me, pl.ds(r0, n)], sems[2].at[i]).start()

    def finish(ins, outs, sems):
        x, y, c, chips = _mesh_place()
        me = 2 * x + y
        copy = copier(sems, c)
        for i in range(nw):
            for k, (px, py) in enumerate(chips):
                copy(i, k, ins[i].at[me], outs[i].at[2 * px + py], px, py).wait_recv()
        for i in range(nw):
            for k, (px, py) in enumerate(chips):
                copy(i, k, ins[i].at[me], outs[i].at[me], px, py).wait_send()
            pltpu.make_async_copy(ins[i].at[me], outs[i].at[me], sems[2].at[i]).wait()

    return dict(name="grad_chip_exchange", ins=list(t4), out_sds=[jax.ShapeDtypeStruct(a.shape, a.dtype) for a in t4],
                sems=[pltpu.SemaphoreType.DMA((3 * nw,)), pltpu.SemaphoreType.DMA((3 * nw,)), pltpu.SemaphoreType.DMA((nw,))],
                start=start, finish=finish)


def _grad_half_swap(full, l):
    nw = len(full)
    halves = [a.shape[1] // 2 for a in full]
    pieces = [_split_rows(h, a.shape[2] * a.dtype.itemsize, D2D_CHUNK_BYTES) for a, h in zip(full, halves)]

    def body(*refs):
        bufs = refs[nw:2 * nw]
        send_sems, recv_sems = refs[2 * nw:]
        x, y, c, _ = _mesh_place()

        def copy(i, hc, r0, n):
            view = bufs[i].at[l, pl.ds(hc * halves[i] + r0, n)]
            return pltpu.make_async_remote_copy(src_ref=view, dst_ref=view, send_sem=send_sems.at[i], recv_sem=recv_sems.at[i],
                                                device_id=(x, y, 1 - c), device_id_type=MESH)

        for i in range(nw):
            for r0, n in pieces[i]:
                copy(i, c, r0, n).start()
        for i in range(nw):
            copy(i, c, 0, halves[i]).wait_send()
            copy(i, 1 - c, 0, halves[i]).wait_recv()

    return pl.pallas_call(body, out_shape=[jax.ShapeDtypeStruct(a.shape, a.dtype) for a in full],
                          in_specs=[ANY] * nw, out_specs=[ANY] * nw, input_output_aliases={i: i for i in range(nw)},
                          scratch_shapes=[pltpu.SemaphoreType.DMA((nw,)), pltpu.SemaphoreType.DMA((nw,))],
                          name="grad_half_swap")(*full)


def _core_add(g, theirs, core):
    _, rh, cc = theirs.shape
    tr = _tile(rh, max(SUBLANES_BF16, (1 << 19) // cc), SUBLANES_BF16)
    nt = rh // tr

    def body(c_ref, a_ref, b_ref, o_ref):
        o_ref[...] = (a_ref[...].astype(F32) + b_ref[...].astype(F32)).astype(o_ref.dtype)

    blk = (None, tr, cc)
    return pl.pallas_call(
        body, out_shape=jax.ShapeDtypeStruct(theirs.shape, theirs.dtype),
        grid_spec=pltpu.PrefetchScalarGridSpec(
            num_scalar_prefetch=1, grid=(4, nt),
            in_specs=[pl.BlockSpec(blk, lambda j, t, c_ref: (j, c_ref[0] * nt + t, 0)),
                      pl.BlockSpec(blk, lambda j, t, c_ref: (j, t, 0))],
            out_specs=pl.BlockSpec(blk, lambda j, t, c_ref: (j, t, 0))),
        compiler_params=_cparams(2), name="grad_core_add")(core, g, theirs)


def _sum_to_half(xb, core, l, depth, stacked=None):
    n, rh, cc = xb.shape
    tr = _tile(rh, max(SUBLANES_BF16, (1 << 18) // cc), SUBLANES_BF16)
    nt = rh // tr

    def body(c_ref, x_ref, *rest):
        acc = x_ref[0].astype(F32)
        for k in range(1, n):
            acc = acc + x_ref[k].astype(F32)
        rest[-1][...] = acc

    ins = (core, xb) if stacked is None else (core, xb, stacked)
    return pl.pallas_call(
        body, out_shape=jax.ShapeDtypeStruct((depth, 2 * rh, cc), F32),
        grid_spec=pltpu.PrefetchScalarGridSpec(
            num_scalar_prefetch=1, grid=(nt,),
            in_specs=[pl.BlockSpec((n, tr, cc), lambda t, c_ref: (0, t, 0))] + ([] if stacked is None else [ANY]),
            out_specs=pl.BlockSpec((None, tr, cc), lambda t, c_ref: (l, c_ref[0] * nt + t, 0))),
        input_output_aliases={} if stacked is None else {2: 0},
        compiler_params=_cparams(1), name="grad_chip_sum")(*ins)


def _sum_lead(name, xb, out_dtype=F32):
    n, rows, w = xb.shape
    tr = _tile(rows, max(SUBLANES_BF16, (1 << 18) // w), SUBLANES_BF16)

    def body(x_ref, o_ref):
        acc = x_ref[0].astype(F32)
        for k in range(1, n):
            acc = acc + x_ref[k].astype(F32)
        o_ref[...] = acc.astype(o_ref.dtype)

    return pl.pallas_call(body, out_shape=jax.ShapeDtypeStruct((rows, w), out_dtype), grid=(rows // tr,),
                          in_specs=[pl.BlockSpec((n, tr, w), lambda i: (0, i, 0))],
                          out_specs=pl.BlockSpec((tr, w), lambda i: (i, 0)), compiler_params=_cparams(1), name=name)(xb)


def _pad_to(v, mult):
    n = v.shape[-1]
    pad = (-n) % mult
    return v if pad == 0 else jnp.pad(v, [(0, 0)] * (v.ndim - 1) + [(0, pad)])


RELAYOUT_ROWS = 256


def _cols_from_shards(a):
    _, k, cc = a.shape
    tr = _tile(k, RELAYOUT_ROWS, SUBLANES_BF16)

    def body(i_ref, o_ref):
        for j in range(4):
            o_ref[:, j * cc:(j + 1) * cc] = i_ref[j]

    return pl.pallas_call(body, out_shape=jax.ShapeDtypeStruct((k, 4 * cc), a.dtype), grid=(k // tr,),
                          in_specs=[pl.BlockSpec((4, tr, cc), lambda i: (0, i, 0))],
                          out_specs=pl.BlockSpec((tr, 4 * cc), lambda i: (i, 0)),
                          compiler_params=_cparams(1), name="cols_from_shards")(a)


def _cols_to_shards(a):
    k, n = a.shape
    cc = n // 4
    tr = _tile(k, RELAYOUT_ROWS, SUBLANES_BF16)

    def body(i_ref, o_ref):
        for j in range(4):
            o_ref[j] = i_ref[:, j * cc:(j + 1) * cc]

    return pl.pallas_call(body, out_shape=jax.ShapeDtypeStruct((4, k, cc), a.dtype), grid=(k // tr,),
                          in_specs=[pl.BlockSpec((tr, n), lambda i: (i, 0))],
                          out_specs=pl.BlockSpec((4, tr, cc), lambda i: (0, i, 0)),
                          compiler_params=_cparams(1), name="cols_to_shards")(a)


COMM_GROUPS = (("w_in",), ("w_ffn_in",), ("w_glu", "w_pa", "w_pb", "w_pc", "w_o", "w_ffn_out"))


def _gather_sides(wb, l):
    return [_gather_side([wb[n] for n in grp], l) for grp in COMM_GROUPS]


def _first_weights(wb):
    sides = _gather_sides(wb, 0)
    return _next_weights(wb, 0, [_run_side(sides[0]), None, None]), sides


def _next_weights(wb, l, got):
    kinds = dict(BIG)
    out = {}
    for grp, arrs in zip(COMM_GROUPS, got):
        for n, a in zip(grp, arrs if arrs is not None else ()):
            if kinds[n] == "row":
                out[n] = a.reshape(4 * a.shape[1], a.shape[2])
            else:
                out[n] = _cols_from_shards(a) if n == "w_in" else a
    return out


def _start_reduce(g):
    g4 = []
    for n, kind in BIG:
        a = g[n]
        if kind == "row":
            a = a.reshape(4, a.shape[0] // 4, a.shape[1])
        elif a.ndim == 2:
            a = _cols_to_shards(a)
        g4.append(a)
    core = lax.axis_index("c").astype(jnp.int32).reshape(1)
    theirs = _grad_core_swap(g4)
    return {n: _core_add(a, b, core) for n, a, b in zip(BIG_NAMES, g4, theirs)}


def _exchange_sides(pending):
    return [_exchange_side([pending[n] for n in grp]) for grp in COMM_GROUPS]


def _run_exchange(pending):
    return [_run_side(s) for s in _exchange_sides(pending)]


def _finish_reduce(pending, arrived, l, depth, stacked):
    core = lax.axis_index("c").astype(jnp.int32).reshape(1)
    names = [n for grp in COMM_GROUPS for n in grp]
    arrs = [a for grp in arrived for a in grp]
    halves = [_sum_to_half(a, core, l, depth, None if stacked is None else stacked[n]) for n, a in zip(names, arrs)]
    return dict(zip(names, _grad_half_swap(halves, l)))


def _adam_fn(w, g, m, v):
    m = ADAM_B1 * m + (1.0 - ADAM_B1) * g
    v = ADAM_B2 * v + (1.0 - ADAM_B2) * (g * g)
    m_hat = m / (1.0 - ADAM_B1 ** ADAM_STEP)
    v_hat = v / (1.0 - ADAM_B2 ** ADAM_STEP)
    return -ADAM_LR * (m_hat / (jnp.sqrt(v_hat) + ADAM_EPS) + ADAM_WD * w), m, v


def _adamw(name, w, g, m, v):
    shape = w.shape
    cols = shape[-1]
    f = lambda a: a.reshape(-1, cols)
    rows = f(w).shape[0]
    tr = max(8, (1 << 19) // cols)
    d, nm, nv = _ew(name, _adam_fn, rows, [R(f(w)), R(f(g)), R(f(m)), R(f(v))], [(cols, F32)] * 3, tr=tr)
    return d.reshape(shape), nm.reshape(shape), nv.reshape(shape)


def _beside(sides, k, call):
    if sides is None:
        return call(side=None), None
    return call(side=sides[k])


def _layer_fwd(x, wl, bias, alpha, sides=None, own=None, wb=None):
    s, d = x.shape
    b_in = wl["b_in"][None, :]
    got = [None] * len(COMM_GROUPS)
    mine = [None] * len(COMM_GROUPS)
    qkv, mine[1] = _beside(own, 1, functools.partial(_mm_nn, "proj_qkv", x, wl["w_in"], n0=0, n=3 * QKV_W, bias=b_in,
                                                     out_dtype=BF16, tn=768))
    rest, mine[2] = _beside(own, 2, functools.partial(_mm_nn, "proj_rest", x, wl["w_in"], n0=3 * QKV_W, bias=b_in, tn=768))
    if own is not None:
        wl.update(_next_weights(wb, 0, mine))
    d_ff = wl["w_ffn_out"].shape[0]
    qkv_s = _to_streams(qkv)
    (o_s, l_s), got[1] = _attn_fwd(qkv_s, bias, side=sides[1] if sides else None)
    o_t, l_t = _streams_to_tok(o_s), _streams_to_tok(l_s)
    ya, lse = _ew("attn_combine", _f_combine, s, [R(a) for a in o_t + l_t], [(WIDTH_A, F32), (WIDTH_A, F32)])
    yb = _sgu_fwd(rest, wl["sgu_ln_g"][None], wl["sgu_ln_b"][None], wl["w_tril"], wl["b_col"])
    bu = _ssm_in(rest, REST_UC, wl["bd"])
    xs = _scan_fwd(bu, wl["a3"])
    y_lin = _ssm_out(xs, wl["cd"])
    uc_blk = REST_UC // WIDTH_C
    yc0 = _ew("ssm_skip_gelu", _f_ssm_out, s, [R(y_lin), R(rest, WIDTH_C, uc_blk), P(wl["d_skip"][None])], [(WIDTH_C, F32)])
    t_glu = _mm_nn("glu_proj", yc0, wl["w_glu"], bias=wl["b_glu"][None], tn=768)
    yc = _ew("glu", _f_glu, s, [R(yc0), R(t_glu)], [(WIDTH_C, F32)])
    pa = _mm_nn("proj_a", ya, wl["w_pa"])
    pb = _mm_nn("proj_b", yb, wl["w_pb"])
    pc = _mm_nn("proj_c", yc, wl["w_pc"])
    gw = _tile(math.gcd(d, REST_GL), 256, LANES)
    gl_ins = [R(rest, gw, (REST_GL + i * d) // gw, True) for i in range(N_BRANCH)]
    merged = _ew("merge", _f_merge, s, gl_ins + [R(pa, gw, 0, True), R(pb, gw, 0, True), R(pc, gw, 0, True)],
                 [(gw, F32)], nj=d // gw, tr=512)
    h1 = _mm_nn("proj_o", merged, wl["w_o"], add=x, add_scale=alpha)
    x1 = _ew("ln1", _f_ln, s, [R(h1), P(wl["ln1_g"][None]), P(wl["ln1_b"][None])], [(d, F32)])
    ff, got[0] = _beside(sides, 0, functools.partial(_mm_nn, "ffn_in", x1, wl["w_ffn_in"], tn=1408))
    fw = _tile(d_ff, 512, LANES)
    act = _ew("swiglu", _f_swiglu, s, [R(ff, fw, 0, True), R(ff, fw, d_ff // fw, True)], [(fw, F32)], nj=d_ff // fw, tr=512)
    h2, got[2] = _beside(sides, 2, functools.partial(_mm_nn, "ffn_out", act, wl["w_ffn_out"], add=x1, add_scale=alpha, tk=1408))
    x2 = _ew("ln2", _f_ln, s, [R(h2), P(wl["ln2_g"][None]), P(wl["ln2_b"][None])], [(d, F32)])
    saved = dict(x=x, qkv_s=qkv_s, rest=rest, ya=ya, lse=lse, yb=yb, xs=xs, y_lin=y_lin, yc0=yc0, t_glu=t_glu, yc=yc,
                 pa=pa, pb=pb, pc=pc, merged=merged, h1=h1, x1=x1, ff=ff, act=act, h2=h2)
    return x2, saved, got


def _vjp_rows(f, n_primal):
    def fn(*args):
        n_ct = len(args) - n_primal
        cts, primals = args[:n_ct], args[n_ct:]
        out, vjp = jax.vjp(f, *primals)
        ct = tuple(c.astype(F32) for c in cts)
        return vjp(ct if isinstance(out, (tuple, list)) else ct[0])
    return fn


def _layer_bwd(dx2, sv, wl, bias, alpha, sides=None):
    s, d = dx2.shape
    arrived = [None] * len(COMM_GROUPS)
    d_ff = wl["w_ffn_out"].shape[0]
    g = {}
    row = lambda a: a[None]
    dh2, g["ln2_g"], g["ln2_b"] = _ew("ln2_bwd", _vjp_rows(_f_ln, 3), s,
                                      [R(dx2), R(sv["h2"]), P(row(wl["ln2_g"])), P(row(wl["ln2_b"]))],
                                      [(d, F32)], accs=[d, d])
    g["w_ffn_out"] = _mm_tn("ffn_out_wgrad", sv["act"], dh2)
    dact = _mm_nt("ffn_out_dgrad", dh2, wl["w_ffn_out"], tn=1408, tk=2048)
    fw = _tile(d_ff, 512, LANES)
    nf = d_ff // fw
    dgf, dup = _ew("swiglu_bwd", _vjp_rows(_f_swiglu, 2), s,
                   [R(dact, fw, 0, True), R(sv["ff"], fw, 0, True), R(sv["ff"], fw, nf, True)],
                   [(fw, BF16), (fw, BF16)], nj=nf, tr=512)
    dff = jnp.concatenate([dgf, dup], axis=1)
    g["w_ffn_in"], arrived[0] = _beside(sides, 0, functools.partial(_mm_tn, "ffn_in_wgrad", sv["x1"], dff, tn=1408,
                                                                    col_shards=True))
    dx1, arrived[1] = _beside(sides, 1, functools.partial(_mm_nt, "ffn_in_dgrad", dff, wl["w_ffn_in"], add=dh2,
                                                          add_scale=alpha, tk=1408))
    dh1, g["ln1_g"], g["ln1_b"] = _ew("ln1_bwd", _vjp_rows(_f_ln, 3), s,
                                      [R(dx1), R(sv["h1"]), P(row(wl["ln1_g"])), P(row(wl["ln1_b"]))],
                                      [(d, F32)], accs=[d, d])
    g["w_o"] = _mm_tn("proj_o_wgrad", sv["merged"], dh1)
    dmerged = _mm_nt("proj_o_dgrad", dh1, wl["w_o"])
    rest = sv["rest"]
    gw = _tile(math.gcd(d, REST_GL), 256, LANES)
    gl_ins = [R(rest, gw, (REST_GL + i * d) // gw, True) for i in range(N_BRANCH)]
    dg0, dg1, dg2, dpa, dpb, dpc = _ew(
        "merge_bwd", _vjp_rows(_f_merge, 6), s,
        [R(dmerged, gw, 0, True)] + gl_ins + [R(sv[k], gw, 0, True) for k in ("pa", "pb", "pc")],
        [(gw, BF16)] * 6, nj=d // gw, tr=512)
    g["w_pa"] = _mm_tn("proj_a_wgrad", sv["ya"], dpa, col_shards=True)
    g["w_pb"] = _mm_tn("proj_b_wgrad", sv["yb"], dpb, col_shards=True)
    g["w_pc"] = _mm_tn("proj_c_wgrad", sv["yc"], dpc, col_shards=True)
    dya = _mm_nt("proj_a_dgrad", dpa, wl["w_pa"], tn=512)
    dyb = _mm_nt("proj_b_dgrad", dpb, wl["w_pb"], tn=768)
    dyc = _mm_nt("proj_c_dgrad", dpc, wl["w_pc"], tn=768)
    dyc0_a, dt_glu, g["b_glu"] = _ew("glu_bwd", lambda ct, a, t: (lambda r: (r[0], r[1], _rowsum(r[1])))(_vjp_rows(_f_glu, 2)(ct, a, t)),
                                     s, [R(dyc), R(sv["yc0"]), R(sv["t_glu"])], [(WIDTH_C, F32), (WIDTH_C, F32)], accs=[WIDTH_C])
    g["w_glu"] = _mm_tn("glu_wgrad", sv["yc0"], dt_glu)
    dyc0 = _mm_nt("glu_dgrad", dt_glu, wl["w_glu"], add=dyc0_a, tn=768)
    uc_blk = REST_UC // WIDTH_C
    dy_lin, duc_skip, g["d_skip"] = _ew("ssm_skip_gelu_bwd", _vjp_rows(_f_ssm_out, 3), s,
                                        [R(dyc0), R(sv["y_lin"]), R(rest, WIDTH_C, uc_blk), P(row(wl["d_skip"]))],
                                        [(WIDTH_C, F32), (WIDTH_C, F32)], accs=[WIDTH_C])
    d_cd = _ssm_out_wgrad(sv["xs"], dy_lin)
    dxs = _ssm_out_dgrad(dy_lin, wl["cd"])
    gs, da = _scan_bwd(dxs, sv["xs"], wl["a3"])
    d_bd = _ssm_in_wgrad(rest, REST_UC, gs)
    duc_lin = _ssm_in_dgrad(gs, wl["bd"])
    duc = _ew("ssm_duc", lambda a, b: a + b, s, [R(duc_lin), R(duc_skip)], [(WIDTH_C, BF16)])
    d_abr = da[0].reshape(N_GROUPS_C, SSM_STATE)
    d_abi = da[1].reshape(N_GROUPS_C, SSM_STATE)
    d_lr, d_li, d_ld, d_br_t, d_bi_t = _ssm_params_bwd(wl["ssm_pin"], d_abr, d_abi,
                                                       _block_diag_extract(d_bd[0]), _block_diag_extract(d_bd[1]))
    g["lam_re"], g["lam_im"], g["log_dt"] = d_lr, d_li, d_ld[:, 0]
    un_t = lambda a: jnp.transpose(a.reshape(N_GROUPS_C, SSM_GROUP, SSM_STATE), (0, 2, 1))
    g["b_re"], g["b_im"] = un_t(d_br_t), un_t(d_bi_t)
    cd_ex = lambda a: _block_diag_extract(jnp.transpose(a, (0, 2, 1))).reshape(N_GROUPS_C, SSM_GROUP, SSM_STATE)
    g["c_re"], g["c_im"] = cd_ex(d_cd[0]), -cd_ex(d_cd[1])
    dzu, dzv, dws, dbs, g["sgu_ln_g"], g["sgu_ln_b"] = _sgu_bwd(rest, row(wl["sgu_ln_g"]), row(wl["sgu_ln_b"]),
                                                                  wl["w_tril"], wl["b_col"], dyb)
    g["w_s"] = jnp.tril(dws)
    g["b_s"] = dbs[:, :, 0]
    do_s, ya_s, lse_s = _tok_to_streams(dya), _tok_to_streams(sv["ya"]), _tok_to_streams(sv["lse"])
    dq, dk, dv, dbias = _attn_bwd(sv["qkv_s"], bias, do_s, ya_s, lse_s)
    dqkv = jnp.stack([jnp.stack(_streams_to_tok(t), 1) for t in (dq, dk, dv)], 1)
    dproj = jnp.concatenate([dqkv.reshape(s, 3 * QKV_W).astype(BF16), dzu, dzv, duc, dg0, dg1, dg2], axis=1)
    n_in = dproj.shape[1]
    cw = _tile(n_in, 1024, LANES)
    g["b_in"] = _ew("b_in_grad", lambda a: _rowsum(a.astype(F32)), s, [R(dproj, cw, 0, True)], [], accs=[cw],
                    nj=n_in // cw, tr=512)
    g["w_in"], arrived[2] = _beside(sides, 2, functools.partial(_mm_tn, "proj_in_wgrad", sv["x"], dproj, tn=768))
    dx = _mm_nt("proj_in_dgrad", dproj, wl["w_in"], add=dh1, add_scale=alpha, tk=768)
    for k in ("ln2_g", "ln2_b", "ln1_g", "ln1_b", "b_glu", "d_skip", "sgu_ln_g", "sgu_ln_b", "b_in"):
        g[k] = g[k][0]
    return dx, g, dbias, arrived


def _loss_head(y, target):
    s, d = y.shape

    def fn(yb, tb):
        err = yb - tb
        return err / d, _rowsum(err * err)

    dy, sq = _ew("loss_head", fn, s, [R(y), R(target)], [(d, F32)], accs=[d])
    return dy, 0.5 * jnp.sum(sq) / d


def _step(x, target, w, m, v):
    depth = w["w_in"].shape[0]
    alpha = (2 * depth) ** 0.25
    bias = _band_bias(w["rel_bias"])
    wb = {n: w[n].astype(BF16) for n in BIG_NAMES}
    xl = x[0]
    layers, saved = [], []
    big, own = _first_weights(wb)
    for l in range(depth):
        wl = dict(big)
        for n in SMALL:
            if n != "rel_bias":
                wl[n] = w[n][l]
        wl["w_tril"] = jnp.tril(w["w_s"][l]).astype(BF16)
        wl["b_col"] = w["b_s"][l][:, :, None]
        pin = _ssm_param_inputs(w, l)
        abr, abi, bbr, bbi = _ssm_params_fwd(pin)
        wl["ssm_pin"] = pin
        wl["a3"] = jnp.stack([abr.reshape(-1, LANES), abi.reshape(-1, LANES)], 0)
        wl["bd"] = jnp.stack([_block_diag(bbr), _block_diag(bbi)], 0)
        wl["cd"] = _c_block_diag(w, l)
        xl, sv, got = _layer_fwd(xl, wl, bias, alpha, _gather_sides(wb, l + 1) if l + 1 < depth else None,
                                 own if l == 0 else None, wb)
        if l + 1 < depth:
            big = _next_weights(wb, l + 1, got)
        layers.append(wl)
        saved.append(sv)
    dx, loss = _loss_head(xl, target[0])
    loss = lax.psum(loss, ("x", "y", "c"))
    grads = {n: [None] * depth for n in SMALL if n != "rel_bias"}
    dbias_sum = None
    pending = None
    big_grads = None
    for l in reversed(range(depth)):
        dx, g, dbias, arrived = _layer_bwd(dx, saved[l], layers[l], bias, alpha,
                                           _exchange_sides(pending) if pending is not None else None)
        dbias_sum = dbias if dbias_sum is None else dbias_sum + dbias
        if pending is not None:
            big_grads = _finish_reduce(pending, arrived, l + 1, depth, big_grads)
        pending = _start_reduce({n: g[n] for n in BIG_NAMES})
        for n in SMALL:
            if n != "rel_bias":
                grads[n][l] = g[n]
    big_grads = _finish_reduce(pending, _run_exchange(pending), 0, depth, big_grads)
    grads = {n: jnp.stack(gl, 0) for n, gl in grads.items()}
    grads.update(big_grads)
    grads["rel_bias"] = _bias_to_buckets(dbias_sum)
    small_vec = _pad_to(jnp.concatenate([grads[n].reshape(-1) for n in SMALL]), 8 * LANES).reshape(-1, LANES)
    small_sum = _sum_lead("small_grad_sum", _ag8("small_grad_gather", small_vec)).reshape(-1)
    off = 0
    for n in SMALL:
        size = math.prod(w[n].shape)
        grads[n] = small_sum[off:off + size].reshape(w[n].shape)
        off += size
    pack = lambda t: _pad_to(jnp.concatenate([t[n].reshape(-1) for n in SMALL]), 8 * LANES).reshape(-1, LANES)
    sd, sm, sv_ = _adamw("adamw_small", pack(w), small_sum.reshape(-1, LANES), pack(m), pack(v))
    delta, new_m, new_v = {}, {}, {}
    off = 0
    for n in SMALL:
        size = math.prod(w[n].shape)
        take = lambda t: t.reshape(-1)[off:off + size].reshape(w[n].shape)
        delta[n], new_m[n], new_v[n] = take(sd), take(sm), take(sv_)
        off += size
    for n in BIG_NAMES:
        delta[n], new_m[n], new_v[n] = _adamw("adamw_" + n, w[n], grads[n], m[n], v[n])
    return loss, dx[None], grads, delta, new_m, new_v


def kernel(x, w_in, b_in, rel_bias, sgu_ln_g, sgu_ln_b, w_s, b_s, lam_re, lam_im, log_dt, b_re, b_im, c_re, c_im, d_skip, w_glu, b_glu, w_pa, w_pb, w_pc, w_o, ln1_g, ln1_b, w_ffn_in, w_ffn_out, ln2_g, ln2_b, loss_target, m_w_in, m_b_in, m_rel_bias, m_sgu_ln_g, m_sgu_ln_b, m_w_s, m_b_s, m_lam_re, m_lam_im, m_log_dt, m_b_re, m_b_im, m_c_re, m_c_im, m_d_skip, m_w_glu, m_b_glu, m_w_pa, m_w_pb, m_w_pc, m_w_o, m_ln1_g, m_ln1_b, m_w_ffn_in, m_w_ffn_out, m_ln2_g, m_ln2_b, v_w_in, v_b_in, v_rel_bias, v_sgu_ln_g, v_sgu_ln_b, v_w_s, v_b_s, v_lam_re, v_lam_im, v_log_dt, v_b_re, v_b_im, v_c_re, v_c_im, v_d_skip, v_w_glu, v_b_glu, v_w_pa, v_w_pb, v_w_pc, v_w_o, v_ln1_g, v_ln1_b, v_w_ffn_in, v_w_ffn_out, v_ln2_g, v_ln2_b):
    w = dict(w_in=w_in, b_in=b_in, rel_bias=rel_bias, sgu_ln_g=sgu_ln_g, sgu_ln_b=sgu_ln_b, w_s=w_s, b_s=b_s,
             lam_re=lam_re, lam_im=lam_im, log_dt=log_dt, b_re=b_re, b_im=b_im, c_re=c_re, c_im=c_im, d_skip=d_skip,
             w_glu=w_glu, b_glu=b_glu, w_pa=w_pa, w_pb=w_pb, w_pc=w_pc, w_o=w_o, ln1_g=ln1_g, ln1_b=ln1_b,
             w_ffn_in=w_ffn_in, w_ffn_out=w_ffn_out, ln2_g=ln2_g, ln2_b=ln2_b)
    m = dict(w_in=m_w_in, b_in=m_b_in, rel_bias=m_rel_bias, sgu_ln_g=m_sgu_ln_g, sgu_ln_b=m_sgu_ln_b, w_s=m_w_s,
             b_s=m_b_s, lam_re=m_lam_re, lam_im=m_lam_im, log_dt=m_log_dt, b_re=m_b_re, b_im=m_b_im, c_re=m_c_re,
             c_im=m_c_im, d_skip=m_d_skip, w_glu=m_w_glu, b_glu=m_b_glu, w_pa=m_w_pa, w_pb=m_w_pb, w_pc=m_w_pc,
             w_o=m_w_o, ln1_g=m_ln1_g, ln1_b=m_ln1_b, w_ffn_in=m_w_ffn_in, w_ffn_out=m_w_ffn_out, ln2_g=m_ln2_g,
             ln2_b=m_ln2_b)
    v = dict(w_in=v_w_in, b_in=v_b_in, rel_bias=v_rel_bias, sgu_ln_g=v_sgu_ln_g, sgu_ln_b=v_sgu_ln_b, w_s=v_w_s,
             b_s=v_b_s, lam_re=v_lam_re, lam_im=v_lam_im, log_dt=v_log_dt, b_re=v_b_re, b_im=v_b_im, c_re=v_c_re,
             c_im=v_c_im, d_skip=v_d_skip, w_glu=v_w_glu, b_glu=v_b_glu, w_pa=v_w_pa, w_pb=v_w_pb, w_pc=v_w_pc,
             w_o=v_w_o, ln1_g=v_ln1_g, ln1_b=v_ln1_b, w_ffn_in=v_w_ffn_in, w_ffn_out=v_w_ffn_out, ln2_g=v_ln2_g,
             ln2_b=v_ln2_b)
    loss, grad_x, grads, delta, new_m, new_v = _step(x, loss_target, w, m, v)
    return (loss, grad_x, *[grads[n] for n in WEIGHTS], *[delta[n] for n in WEIGHTS],
            *[new_m[n] for n in WEIGHTS], *[new_v[n] for n in WEIGHTS])
```

```python
import functools
import math

import numpy as np
import jax
import jax.numpy as jnp
from jax import lax
from jax.experimental import pallas as pl
from jax.experimental.pallas import tpu as pltpu

F32 = jnp.float32
BF16 = jnp.bfloat16
MESH = pl.DeviceIdType.MESH

ATT_PATTERNS = ((128, 1), (512, 4), (2048, 16))
N_GROUPS_A = 3
HEADS = 8
HEAD_DIM = 64
WIDTH_A = HEADS * HEAD_DIM
QKV_W = N_GROUPS_A * WIDTH_A
ATT_BLOCK = 128
N_REL_BUCKETS = 32
REL_MAX_DIST = 2048
NEG_INF = -1e30
CHUNK = 128
WIDTH_B = 768
N_GROUPS_B = 6
GROUP_B = 128
WIDTH_C = 768
SSM_GROUP = 16
N_GROUPS_C = 48
SSM_STATE = 64
SSM_CH = N_GROUPS_C * SSM_STATE
N_BRANCH = 3
LN_EPS = 1e-5
ADAM_LR = 0.001
ADAM_B1 = 0.9
ADAM_B2 = 0.999
ADAM_EPS = 1e-08
ADAM_WD = 0.01
ADAM_STEP = 10

LANES = 128
SUBLANES_BF16 = 16
V7X_VMEM_BYTES = 64 * 1024 * 1024
VMEM_LIMIT = V7X_VMEM_BYTES * 7 // 8

OFF_ZB = 3 * QKV_W
REST_ZB = 0
REST_UC = 2 * WIDTH_B
REST_GL = 2 * WIDTH_B + WIDTH_C

BIG = (("w_in", "col"), ("w_glu", "row"), ("w_pa", "col"), ("w_pb", "col"), ("w_pc", "col"),
       ("w_o", "row"), ("w_ffn_in", "col"), ("w_ffn_out", "row"))
WEIGHTS = ("w_in", "b_in", "rel_bias", "sgu_ln_g", "sgu_ln_b", "w_s", "b_s", "lam_re", "lam_im", "log_dt",
           "b_re", "b_im", "c_re", "c_im", "d_skip", "w_glu", "b_glu", "w_pa", "w_pb", "w_pc", "w_o",
           "ln1_g", "ln1_b", "w_ffn_in", "w_ffn_out", "ln2_g", "ln2_b")
BIG_NAMES = tuple(n for n, _ in BIG)
SMALL = tuple(n for n in WEIGHTS if n not in BIG_NAMES)


def _tile(n, target, mult):
    t = (min(target, n) // mult) * mult
    while t >= mult:
        if n % t == 0:
            return t
        t -= mult
    return n


def _cparams(n_axes):
    return pltpu.CompilerParams(dimension_semantics=("arbitrary",) * n_axes, vmem_limit_bytes=VMEM_LIMIT)


_DIMS = {"nn": (((1,), (0,)), ((), ())), "nt": (((1,), (1,)), ((), ())), "tn": (((0,), (0,)), ((), ()))}


def _mm(name, mode, grid, a, a_spec, b, b_spec, out_sds, o_spec, acc_shape, *, bias=None, bias_spec=None,
        add=None, add_spec=None, add_scale=1.0, exact=False, side=None):
    nk = grid[2]
    has_bias = bias is not None
    has_add = add is not None
    n_side_in = len(side["ins"]) if side else 0
    n_side_out = len(side["out_sds"]) if side else 0
    n_side_sem = len(side["sems"]) if side else 0

    def body(*refs):
        a_ref, b_ref = refs[0], refs[1]
        pos = 2
        bias_ref = add_ref = None
        if has_bias:
            bias_ref = refs[pos]
            pos += 1
        if has_add:
            add_ref = refs[pos]
            pos += 1
        side_ins = refs[pos:pos + n_side_in]
        pos += n_side_in
        o_ref = refs[pos]
        side_outs = refs[pos + 1:pos + 1 + n_side_out]
        side_sems = refs[len(refs) - n_side_sem:] if n_side_sem else ()
        pos += n_side_out
        if side:
            at = [pl.program_id(d) for d in range(3)]

            @pl.when((at[0] == 0) & (at[1] == 0) & (at[2] == 0))
            def _():
                side["start"](side_ins, side_outs, side_sems)
        if exact:
            part = lax.dot_general(a_ref[...].astype(F32), b_ref[...].astype(F32), _DIMS[mode],
                                   preferred_element_type=F32, precision=lax.Precision.HIGHEST)
        else:
            part = lax.dot_general(a_ref[...].astype(BF16), b_ref[...].astype(BF16), _DIMS[mode],
                                   preferred_element_type=F32)

        def finish(r):
            if has_bias:
                r = r + bias_ref[...]
            if has_add:
                r = r + add_scale * add_ref[...].astype(F32)
            o_ref[...] = r.astype(o_ref.dtype)

        if nk == 1:
            finish(part)
        else:
            acc_ref = refs[pos + 1]
            k = pl.program_id(2)

            @pl.when(k == 0)
            def _():
                acc_ref[...] = part

            @pl.when(k > 0)
            def _():
                acc_ref[...] += part

            @pl.when(k == nk - 1)
            def _():
                finish(acc_ref[...])

        if side:
            @pl.when((at[0] == grid[0] - 1) & (at[1] == grid[1] - 1) & (at[2] == grid[2] - 1))
            def _():
                side["finish"](side_ins, side_outs, side_sems)

    ins, specs = [a, b], [a_spec, b_spec]
    if has_bias:
        ins.append(bias)
        specs.append(bias_spec)
    if has_add:
        ins.append(add)
        specs.append(add_spec)
    scratch = [pltpu.VMEM(acc_shape, F32)] if nk > 1 else []
    if not side:
        return pl.pallas_call(body, out_shape=out_sds, grid=grid, in_specs=specs, out_specs=o_spec,
                              scratch_shapes=scratch, compiler_params=_cparams(3), name=name)(*ins)
    res = pl.pallas_call(body, out_shape=[out_sds] + list(side["out_sds"]), grid=grid,
                         in_specs=specs + [ANY] * n_side_in, out_specs=[o_spec] + [ANY] * n_side_out,
                         scratch_shapes=scratch + list(side["sems"]), compiler_params=_cparams(3),
                         name=name + "_" + side["name"])(*ins, *side["ins"])
    return res[0], res[1:]


def _mm_nn(name, a, b, *, n0=0, n=None, a0=0, bias=None, add=None, add_scale=1.0, out_dtype=F32,
           tm=1024, tn=512, tk=None, side=None):
    m = a.shape[0]
    if b.ndim == 3:
        k, cc = b.shape[1:]
        n = 4 * cc
        tn = _tile(cc, tn, LANES)
        per = cc // tn
        tk = _tile(k, k if tk is None else tk, LANES)
        b_spec = pl.BlockSpec((None, tk, tn), lambda i, j, kk: (j // per, kk, j % per))
    else:
        k = b.shape[0]
        n = b.shape[1] - n0 if n is None else n
        tn = _tile(math.gcd(n, n0) if n0 else n, tn, LANES)
        tk = _tile(math.gcd(k, a0) if a0 else k, k if tk is None else tk, LANES)
        b_spec = pl.BlockSpec((tk, tn), lambda i, j, kk: (kk, n0 // tn + j))
    tm = _tile(m, tm, SUBLANES_BF16)
    jn0, ka0 = n0 // tn, a0 // tk
    o_spec = pl.BlockSpec((tm, tn), lambda i, j, kk: (i, j))
    return _mm(name, "nn", (m // tm, n // tn, k // tk),
               a, pl.BlockSpec((tm, tk), lambda i, j, kk: (i, ka0 + kk)), b, b_spec,
               jax.ShapeDtypeStruct((m, n), out_dtype), o_spec, (tm, tn),
               bias=bias, bias_spec=pl.BlockSpec((1, tn), lambda i, j, kk: (0, jn0 + j)),
               add=add, add_spec=o_spec, add_scale=add_scale, side=side)


def _mm_nt(name, a, b, *, n0=0, add=None, add_scale=1.0, out_dtype=F32, tm=512, tn=2048, tk=512, side=None):
    m, n = a.shape
    if b.ndim == 3:
        k, cc = b.shape[1:]
        tn = _tile(k, tn, LANES)
        tk = _tile(cc, tk, LANES)
        per = cc // tk
        b_spec = pl.BlockSpec((None, tn, tk), lambda i, j, kk: (kk // per, j, kk % per))
    else:
        k = b.shape[0]
        tn = _tile(k, tn, LANES)
        tk = _tile(math.gcd(n, n0) if n0 else n, tk, LANES)
        b_spec = pl.BlockSpec((tn, tk), lambda i, j, kk: (j, n0 // tk + kk))
    tm = _tile(m, tm, SUBLANES_BF16)
    o_spec = pl.BlockSpec((tm, tn), lambda i, j, kk: (i, j))
    return _mm(name, "nt", (m // tm, k // tn, n // tk),
               a, pl.BlockSpec((tm, tk), lambda i, j, kk: (i, kk)), b, b_spec,
               jax.ShapeDtypeStruct((m, k), out_dtype), o_spec, (tm, tn),
               add=add, add_spec=o_spec, add_scale=add_scale, side=side)


def _mm_tn(name, a, b, *, a0=0, ka=None, out_dtype=BF16, tm=2048, tn=1024, tk=1024, col_shards=False, side=None):
    s = a.shape[0]
    ka = a.shape[1] - a0 if ka is None else ka
    n = b.shape[1]
    tm = _tile(math.gcd(ka, a0) if a0 else ka, tm, LANES)
    tk = _tile(s, tk, SUBLANES_BF16)
    ia0 = a0 // tm
    if col_shards:
        cc = n // 4
        tn = _tile(cc, tn, LANES)
        per = cc // tn
        out_sds = jax.ShapeDtypeStruct((4, ka, cc), out_dtype)
        o_spec = pl.BlockSpec((None, tm, tn), lambda i, j, kk: (j // per, i, j % per))
    else:
        tn = _tile(n, tn, LANES)
        out_sds = jax.ShapeDtypeStruct((ka, n), out_dtype)
        o_spec = pl.BlockSpec((tm, tn), lambda i, j, kk: (i, j))
    return _mm(name, "tn", (ka // tm, n // tn, s // tk),
               a, pl.BlockSpec((tk, tm), lambda i, j, kk: (kk, ia0 + i)),
               b, pl.BlockSpec((tk, tn), lambda i, j, kk: (kk, j)), out_sds, o_spec, (tm, tn), side=side)


def R(arr, bw=None, c0=0, j=False):
    return ("r", arr, arr.shape[1] if bw is None else bw, c0, j)


def P(arr, bw=None, c0=0, j=False):
    return ("p", arr, arr.shape[1] if bw is None else bw, c0, j)


def _ew(name, fn, rows, ins, outs, accs=(), *, tr=256, nj=1):
    tr = _tile(rows, tr, SUBLANES_BF16)
    ni = rows // tr
    n_in, n_out, n_acc = len(ins), len(outs), len(accs)

    def spec(kind, bw, c0, follows):
        rows_b = tr if kind == "r" else 1
        if kind == "r":
            return pl.BlockSpec((rows_b, bw), (lambda j, i: (i, c0 + j)) if follows else (lambda j, i: (i, c0)))
        return pl.BlockSpec((rows_b, bw), (lambda j, i: (0, c0 + j)) if follows else (lambda j, i: (0, c0)))

    def body(*refs):
        res = fn(*[r[...] for r in refs[:n_in]])
        res = tuple(res) if isinstance(res, (tuple, list)) else (res,)
        for r, v in zip(refs[n_in:n_in + n_out], res[:n_out]):
            r[...] = v.astype(r.dtype)
        if n_acc:
            i = pl.program_id(1)
            for r, v in zip(refs[n_in + n_out:], res[n_out:]):
                @pl.when(i == 0)
                def _(r=r, v=v):
                    r[...] = v

                @pl.when(i > 0)
                def _(r=r, v=v):
                    r[...] += v

    out_shape = [jax.ShapeDtypeStruct((rows, bw * nj), dt) for bw, dt in outs]
    out_shape += [jax.ShapeDtypeStruct((1, bw * nj), F32) for bw in accs]
    out_specs = [pl.BlockSpec((tr, bw), lambda j, i: (i, j)) for bw, _ in outs]
    out_specs += [pl.BlockSpec((1, bw), lambda j, i: (0, j)) for bw in accs]
    res = pl.pallas_call(body, out_shape=out_shape, grid=(nj, ni),
                         in_specs=[spec(k, bw, c0, f) for k, _, bw, c0, f in ins], out_specs=out_specs,
                         compiler_params=_cparams(2), name=name)(*[a for _, a, _, _, _ in ins])
    return res if len(res) > 1 else res[0]


def _gelu(x):
    c = math.sqrt(2.0 / math.pi)
    return 0.5 * x * (1.0 + jnp.tanh(c * (x + 0.044715 * (x * x * x))))


def _sigmoid(x):
    return lax.logistic(x)


def _f_ln(h, g, b):
    mu = jnp.mean(h, axis=-1, keepdims=True)
    xc = h - mu
    var = jnp.mean(xc * xc, axis=-1, keepdims=True)
    return xc * lax.rsqrt(var + LN_EPS) * g + b


def _f_combine(o1, o2, o3, l1, l2, l3):
    m = jnp.maximum(jnp.maximum(l1, l2), l3)
    e1, e2, e3 = jnp.exp(l1 - m), jnp.exp(l2 - m), jnp.exp(l3 - m)
    den = e1 + e2 + e3
    return (e1 * o1 + e2 * o2 + e3 * o3) / den, m + jnp.log(den)


def _f_ssm_out(y_lin, uc, d_skip):
    return _gelu(y_lin + d_skip * uc)


def _f_glu(yc0, t):
    return yc0 * _sigmoid(t)


def _f_merge(g0, g1, g2, pa, pb, pc):
    return _sigmoid(g0) * pa + _sigmoid(g1) * pb + _sigmoid(g2) * pc


def _f_swiglu(gf, up):
    return gf * _sigmoid(gf) * up


def _f_sgu_pre(zu, zv, g, b):
    return _gelu(zu), _f_ln(_gelu(zv), g, b)


def _f_ssm_params(lr, li, ld, lr_rep, li_rep, ld_rep, br_t, bi_t):
    def disc(lr, li, ld):
        dt = jnp.exp(ld)
        mag = jnp.exp(lr * dt)
        th = li * dt
        abr, abi = mag * jnp.cos(th), mag * jnp.sin(th)
        nrm = lr * lr + li * li
        cr = ((abr - 1.0) * lr + abi * li) / nrm
        ci = (abi * lr - (abr - 1.0) * li) / nrm
        return abr, abi, cr, ci

    abr, abi, _, _ = disc(lr, li, ld)
    _, _, cr, ci = disc(lr_rep, li_rep, ld_rep)
    return abr, abi, cr * br_t - ci * bi_t, cr * bi_t + ci * br_t


def _rowsum(x):
    return jnp.sum(x, axis=0, keepdims=True)


def _to_streams(qkv):
    s = qkv.shape[0]
    t = qkv.reshape(s, 3, N_GROUPS_A, HEADS, HEAD_DIM)
    outs = []
    for g, (_, dil) in enumerate(ATT_PATTERNS):
        tg = t[:, :, g].reshape(s // dil, dil, 3, HEADS, HEAD_DIM)
        outs.append(tg.transpose(2, 3, 1, 0, 4).reshape(3, HEADS, s, HEAD_DIM))
    return jnp.stack(outs, 0)


def _tok_to_streams(a):
    s = a.shape[0]
    outs = []
    for _, dil in ATT_PATTERNS:
        t = a.reshape(s // dil, dil, HEADS, HEAD_DIM)
        outs.append(t.transpose(2, 1, 0, 3).reshape(HEADS, s, HEAD_DIM))
    return jnp.stack(outs, 0)


def _streams_to_tok(o):
    s = o.shape[2]
    outs = []
    for g, (_, dil) in enumerate(ATT_PATTERNS):
        t = o[g].reshape(HEADS, dil, s // dil, HEAD_DIM)
        outs.append(t.transpose(2, 1, 0, 3).reshape(s, WIDTH_A))
    return outs


def _t5_bucket(dist):
    max_exact = N_REL_BUCKETS // 2
    d = np.maximum(dist, 1).astype(np.float32)
    scale = (N_REL_BUCKETS - max_exact) / math.log(REL_MAX_DIST / max_exact)
    large = max_exact + (np.log(d / max_exact) * scale).astype(np.int32)
    large = np.minimum(large, N_REL_BUCKETS - 1)
    return np.where(dist < max_exact, dist, large).astype(np.int32)


def _bucket_maps():
    i = np.arange(ATT_BLOCK)[:, None]
    kk = np.arange(2 * ATT_BLOCK)[None, :]
    steps = np.maximum(ATT_BLOCK + i - kk, 0)
    return np.stack([_t5_bucket(steps * dil) for _, dil in ATT_PATTERNS], 0)


def _band_bias(rel_bias):
    q = ATT_BLOCK

    def body(rel_ref, m_ref, o_ref):
        g = pl.program_id(0)
        bm = m_ref[...]
        for h in range(HEADS):
            acc = jnp.zeros((q, 2 * q), F32)
            for bk in range(N_REL_BUCKETS):
                acc = jnp.where(bm == bk, rel_ref[bk, g * HEADS + h], acc)
            o_ref[h] = acc

    return pl.pallas_call(body, out_shape=jax.ShapeDtypeStruct((N_GROUPS_A, HEADS, q, 2 * q), F32), grid=(N_GROUPS_A,),
                          in_specs=[pl.BlockSpec(memory_space=pltpu.SMEM),
                                    pl.BlockSpec((None, q, 2 * q), lambda g: (g, 0, 0))],
                          out_specs=pl.BlockSpec((None, HEADS, q, 2 * q), lambda g: (g, 0, 0, 0)),
                          compiler_params=_cparams(1), name="rel_bias_band")(rel_bias, jnp.asarray(_bucket_maps()))


def _attn_masks(nbs):
    q = ATT_BLOCK
    g, b = pl.program_id(0), pl.program_id(1)
    nb = jnp.where(g == 0, nbs[0], jnp.where(g == 1, nbs[1], nbs[2]))
    shift = jnp.where(lax.rem(b, nb) != 0, 0, q)
    ii = lax.broadcasted_iota(jnp.int32, (q, q), 0)
    kk = lax.broadcasted_iota(jnp.int32, (q, q), 1)
    return kk >= ii + shift, kk <= ii


def _attn_logits(q, kp, kc, bias_h, mask_p, mask_c):
    scale = HEAD_DIM ** -0.5
    sp = lax.dot_general(q, kp, _DIMS["nt"], preferred_element_type=F32) * scale + bias_h[:, :ATT_BLOCK]
    sc = lax.dot_general(q, kc, _DIMS["nt"], preferred_element_type=F32) * scale + bias_h[:, ATT_BLOCK:]
    return jnp.where(mask_p, sp, NEG_INF), jnp.where(mask_c, sc, NEG_INF)


def _attn_specs(s):
    q, h, e = ATT_BLOCK, HEADS, HEAD_DIM
    blk = (None, None, h, q, e)
    prev = lambda b: jnp.maximum(b - 1, 0)
    qkv_specs = [pl.BlockSpec(blk, lambda g, b: (g, 0, 0, b, 0)),
                 pl.BlockSpec(blk, lambda g, b: (g, 1, 0, prev(b), 0)),
                 pl.BlockSpec(blk, lambda g, b: (g, 1, 0, b, 0)),
                 pl.BlockSpec(blk, lambda g, b: (g, 2, 0, prev(b), 0)),
                 pl.BlockSpec(blk, lambda g, b: (g, 2, 0, b, 0))]
    bias_spec = pl.BlockSpec((None, h, q, 2 * q), lambda g, b: (g, 0, 0, 0))
    row_spec = pl.BlockSpec((None, h, q, e), lambda g, b: (g, 0, b, 0))
    return qkv_specs, bias_spec, row_spec


def _attn_fwd(qkv_s, bias, side=None):
    s = qkv_s.shape[3]
    nblk = s // ATT_BLOCK
    nbs = tuple(s // dil // ATT_BLOCK for _, dil in ATT_PATTERNS)
    qkv_specs, bias_spec, row_spec = _attn_specs(s)
    n_side_in = len(side["ins"]) if side else 0
    n_side_out = len(side["out_sds"]) if side else 0

    def body(q_ref, kp_ref, kc_ref, vp_ref, vc_ref, b_ref, *rest):
        side_ins, (o_ref, l_ref) = rest[:n_side_in], rest[n_side_in:n_side_in + 2]
        side_outs = rest[n_side_in + 2:n_side_in + 2 + n_side_out]
        side_sems = rest[n_side_in + 2 + n_side_out:]
        if side:
            @pl.when((pl.program_id(0) == 0) & (pl.program_id(1) == 0))
            def _():
                side["start"](side_ins, side_outs, side_sems)

        mask_p, mask_c = _attn_masks(nbs)
        for h in range(HEADS):
            sp, sc = _attn_logits(q_ref[h], kp_ref[h], kc_ref[h], b_ref[h], mask_p, mask_c)
            m = jnp.maximum(jnp.max(sp, axis=1, keepdims=True), jnp.max(sc, axis=1, keepdims=True))
            pp, pc = jnp.exp(sp - m), jnp.exp(sc - m)
            den = jnp.sum(pp, axis=1, keepdims=True) + jnp.sum(pc, axis=1, keepdims=True)
            o = (lax.dot_general(pp.astype(BF16), vp_ref[h], _DIMS["nn"], preferred_element_type=F32)
                 + lax.dot_general(pc.astype(BF16), vc_ref[h], _DIMS["nn"], preferred_element_type=F32))
            o_ref[h] = o / den
            l_ref[h] = jnp.broadcast_to(m + jnp.log(den), (ATT_BLOCK, HEAD_DIM))

        if side:
            @pl.when((pl.program_id(0) == N_GROUPS_A - 1) & (pl.program_id(1) == nblk - 1))
            def _():
                side["finish"](side_ins, side_outs, side_sems)

    sds = jax.ShapeDtypeStruct((N_GROUPS_A, HEADS, s, HEAD_DIM), F32)
    if not side:
        return pl.pallas_call(body, out_shape=(sds, sds), grid=(N_GROUPS_A, nblk),
                              in_specs=qkv_specs + [bias_spec], out_specs=(row_spec, row_spec),
                              compiler_params=_cparams(2), name="attn_fwd")(qkv_s, qkv_s, qkv_s, qkv_s, qkv_s, bias), None
    res = pl.pallas_call(body, out_shape=[sds, sds] + list(side["out_sds"]), grid=(N_GROUPS_A, nblk),
                         in_specs=qkv_specs + [bias_spec] + [ANY] * n_side_in,
                         out_specs=[row_spec, row_spec] + [ANY] * n_side_out, scratch_shapes=list(side["sems"]),
                         compiler_params=_cparams(2), name="attn_fwd_" + side["name"])(
        qkv_s, qkv_s, qkv_s, qkv_s, qkv_s, bias, *side["ins"])
    return (res[0], res[1]), res[2:]


def _attn_bwd(qkv_s, bias, do_s, ya_s, lse_s):
    s = qkv_s.shape[3]
    nblk = s // ATT_BLOCK
    nbs = tuple(s // dil // ATT_BLOCK for _, dil in ATT_PATTERNS)
    scale = HEAD_DIM ** -0.5
    q, h, e = ATT_BLOCK, HEADS, HEAD_DIM

    def body(q_ref, kp_ref, kc_ref, vp_ref, vc_ref, b_ref, do_ref, ya_ref, l_ref,
             dq_ref, dk_ref, dv_ref, db_ref, dk_own, dv_own):
        b = pl.program_id(1)
        mask_p, mask_c = _attn_masks(nbs)

        @pl.when(b == 0)
        def _():
            db_ref[...] = jnp.zeros_like(db_ref)
            dk_own[...] = jnp.zeros_like(dk_own)
            dv_own[...] = jnp.zeros_like(dv_own)

        @pl.when(b < nblk)
        def _():
            for hh in range(h):
                qh, kp, kc, vp, vc = q_ref[hh], kp_ref[hh], kc_ref[hh], vp_ref[hh], vc_ref[hh]
                sp, sc = _attn_logits(qh, kp, kc, b_ref[hh], mask_p, mask_c)
                lse = l_ref[hh][:, 0:1]
                pp, pc = jnp.exp(sp - lse), jnp.exp(sc - lse)
                do = do_ref[hh]
                dsum = jnp.sum(do * ya_ref[hh], axis=1, keepdims=True)
                dob = do.astype(BF16)
                dsp = pp * (lax.dot_general(dob, vp, _DIMS["nt"], preferred_element_type=F32) - dsum)
                dsc = pc * (lax.dot_general(dob, vc, _DIMS["nt"], preferred_element_type=F32) - dsum)
                db_ref[hh, :, :q] += dsp
                db_ref[hh, :, q:] += dsc
                dspb, dscb = dsp.astype(BF16), dsc.astype(BF16)
                dq_ref[hh] = scale * (lax.dot_general(dspb, kp, _DIMS["nn"], preferred_element_type=F32)
                                      + lax.dot_general(dscb, kc, _DIMS["nn"], preferred_element_type=F32))
                dkp = scale * lax.dot_general(dspb, qh, _DIMS["tn"], preferred_element_type=F32)
                dvp = lax.dot_general(pp.astype(BF16), dob, _DIMS["tn"], preferred_element_type=F32)

                dk_ref[hh] = dk_own[hh] + dkp
                dv_ref[hh] = dv_own[hh] + dvp
                dk_own[hh] = scale * lax.dot_general(dscb, qh, _DIMS["tn"], preferred_element_type=F32)
                dv_own[hh] = lax.dot_general(pc.astype(BF16), dob, _DIMS["tn"], preferred_element_type=F32)

        @pl.when(b == nblk)
        def _():
            dk_ref[...] = dk_own[...]
            dv_ref[...] = dv_own[...]

    blk5 = (None, None, h, q, e)
    cur = lambda b: jnp.minimum(b, nblk - 1)
    prev = lambda b: jnp.maximum(cur(b) - 1, 0)
    qkv_specs = [pl.BlockSpec(blk5, lambda g, b: (g, 0, 0, cur(b), 0)),
                 pl.BlockSpec(blk5, lambda g, b: (g, 1, 0, prev(b), 0)),
                 pl.BlockSpec(blk5, lambda g, b: (g, 1, 0, cur(b), 0)),
                 pl.BlockSpec(blk5, lambda g, b: (g, 2, 0, prev(b), 0)),
                 pl.BlockSpec(blk5, lambda g, b: (g, 2, 0, cur(b), 0))]
    bias_spec = pl.BlockSpec((None, h, q, 2 * q), lambda g, b: (g, 0, 0, 0))
    row_spec = pl.BlockSpec((None, h, q, e), lambda g, b: (g, 0, cur(b), 0))
    lag_spec = pl.BlockSpec((None, h, q, e), lambda g, b: (g, 0, jnp.maximum(b - 1, 0), 0))
    sds = jax.ShapeDtypeStruct((N_GROUPS_A, h, s, e), F32)
    own = pltpu.VMEM((h, q, e), F32)
    return pl.pallas_call(body, out_shape=(sds,) * 3 + (jax.ShapeDtypeStruct(bias.shape, F32),),
                          grid=(N_GROUPS_A, nblk + 1),
                          in_specs=qkv_specs + [bias_spec, row_spec, row_spec, row_spec],
                          out_specs=(row_spec, lag_spec, lag_spec, bias_spec), scratch_shapes=[own, own],
                          compiler_params=_cparams(2), name="attn_bwd")(
        qkv_s, qkv_s, qkv_s, qkv_s, qkv_s, bias, do_s, ya_s, lse_s)


def _bias_to_buckets(dbias):
    bmap = jnp.asarray(_bucket_maps())
    q = ATT_BLOCK

    def body(db_ref, m_ref, o_ref):
        lane = lax.broadcasted_iota(jnp.int32, (HEADS, LANES), 1)
        row = lax.broadcasted_iota(jnp.int32, (HEADS, LANES), 0)
        acc = jnp.zeros((HEADS, LANES), F32)
        bm = m_ref[...]
        for h in range(HEADS):
            dbh = db_ref[h]
            for bk in range(N_REL_BUCKETS):
                sv = jnp.sum(jnp.sum(jnp.where(bm == bk, dbh, 0.0), axis=1, keepdims=True), axis=0, keepdims=True)
                acc = acc + jnp.where((lane == bk) & (row == h), sv, 0.0)
        o_ref[...] = acc

    out = pl.pallas_call(body, out_shape=jax.ShapeDtypeStruct((N_GROUPS_A, HEADS, LANES), F32), grid=(N_GROUPS_A,),
                         in_specs=[pl.BlockSpec((None, HEADS, q, 2 * q), lambda g: (g, 0, 0, 0)),
                                   pl.BlockSpec((None, q, 2 * q), lambda g: (g, 0, 0))],
                         out_specs=pl.BlockSpec((None, HEADS, LANES), lambda g: (g, 0, 0)),
                         compiler_params=_cparams(1), name="rel_bias_grad")(dbias, bmap)
    return out[:, :, :N_REL_BUCKETS].reshape(N_GROUPS_A * HEADS, N_REL_BUCKETS).T


def _sgu_specs():
    c, w = CHUNK, WIDTH_B
    return [pl.BlockSpec((c, w), lambda i: (i, 0)), pl.BlockSpec((c, w), lambda i: (i, 1)),
            pl.BlockSpec((1, w), lambda i: (0, 0)), pl.BlockSpec((1, w), lambda i: (0, 0)),
            pl.BlockSpec((N_GROUPS_B, c, c), lambda i: (0, 0, 0)), pl.BlockSpec((N_GROUPS_B, c, 1), lambda i: (0, 0, 0))]


def _sgu_fwd(rest, ln_g, ln_b, w_tril, b_col):
    s = rest.shape[0]

    def body(zu_ref, zv_ref, g_ref, b_ref, w_ref, bs_ref, y_ref):
        u, vn = _f_sgu_pre(zu_ref[...], zv_ref[...], g_ref[...], b_ref[...])
        for gi in range(N_GROUPS_B):
            sl = slice(gi * GROUP_B, (gi + 1) * GROUP_B)
            mixed = lax.dot_general(w_ref[gi], vn[:, sl].astype(BF16), _DIMS["nn"], preferred_element_type=F32)
            y_ref[:, sl] = (u[:, sl] * (mixed + bs_ref[gi])).astype(y_ref.dtype)

    return pl.pallas_call(body, out_shape=jax.ShapeDtypeStruct((s, WIDTH_B), BF16), grid=(s // CHUNK,),
                          in_specs=_sgu_specs(), out_specs=pl.BlockSpec((CHUNK, WIDTH_B), lambda i: (i, 0)),
                          compiler_params=_cparams(1), name="sgu_fwd")(rest, rest, ln_g, ln_b, w_tril, b_col)


def _sgu_bwd(rest, ln_g, ln_b, w_tril, b_col, dyb):
    s = rest.shape[0]
    c, w, ng = CHUNK, WIDTH_B, N_GROUPS_B

    def body(zu_ref, zv_ref, g_ref, b_ref, w_ref, bs_ref, dy_ref, dzu_ref, dzv_ref, dw_ref, dbs_ref, dg_ref, db_ref):
        (u, vn), vjp = jax.vjp(_f_sgu_pre, zu_ref[...], zv_ref[...], g_ref[...], b_ref[...])
        first = pl.program_id(0) == 0
        du, dvn = [], []
        for gi in range(ng):
            sl = slice(gi * GROUP_B, (gi + 1) * GROUP_B)
            vg = vn[:, sl].astype(BF16)
            mixed = lax.dot_general(w_ref[gi], vg, _DIMS["nn"], preferred_element_type=F32) + bs_ref[gi]
            dy = dy_ref[:, sl]
            dmix = dy * u[:, sl]
            du.append(dy * mixed)
            dmb = dmix.astype(BF16)
            dvn.append(lax.dot_general(w_ref[gi], dmb, _DIMS["tn"], preferred_element_type=F32))
            dwg = lax.dot_general(dmb, vg, _DIMS["nt"], preferred_element_type=F32)
            dbg = jnp.sum(dmix, axis=1, keepdims=True)

            @pl.when(first)
            def _(gi=gi, dwg=dwg, dbg=dbg):
                dw_ref[gi] = dwg
                dbs_ref[gi] = dbg

            @pl.when(jnp.logical_not(first))
            def _(gi=gi, dwg=dwg, dbg=dbg):
                dw_ref[gi] += dwg
                dbs_ref[gi] += dbg

        dzu, dzv, dg, db = vjp((jnp.concatenate(du, axis=1), jnp.concatenate(dvn, axis=1)))
        dzu_ref[...] = dzu.astype(dzu_ref.dtype)
        dzv_ref[...] = dzv.astype(dzv_ref.dtype)

        @pl.when(first)
        def _():
            dg_ref[...] = dg
            db_ref[...] = db

        @pl.when(jnp.logical_not(first))
        def _():
            dg_ref[...] += dg
            db_ref[...] += db

    row = pl.BlockSpec((c, w), lambda i: (i, 0))
    par = pl.BlockSpec((1, w), lambda i: (0, 0))
    return pl.pallas_call(
        body, grid=(s // c,),
        out_shape=(jax.ShapeDtypeStruct((s, w), BF16), jax.ShapeDtypeStruct((s, w), BF16),
                   jax.ShapeDtypeStruct((ng, c, c), F32), jax.ShapeDtypeStruct((ng, c, 1), F32),
                   jax.ShapeDtypeStruct((1, w), F32), jax.ShapeDtypeStruct((1, w), F32)),
        in_specs=_sgu_specs() + [row],
        out_specs=(row, row, pl.BlockSpec((ng, c, c), lambda i: (0, 0, 0)), pl.BlockSpec((ng, c, 1), lambda i: (0, 0, 0)),
                   par, par),
        compiler_params=_cparams(1), name="sgu_bwd")(rest, rest, ln_g, ln_b, w_tril, b_col, dyb)


SCAN_T = 128


def _to_time_major(dst, src_ref, lead):
    for r in range(SSM_CH // LANES):
        dst[:, r, :] = src_ref[lead, :, r * LANES:(r + 1) * LANES]


def _from_time_major(dst_ref, lead, src):
    for r in range(SSM_CH // LANES):
        dst_ref[lead, :, r * LANES:(r + 1) * LANES] = src[:, r, :]


def _scan_fwd(bu, a3):
    s = bu.shape[1]
    t_blk = _tile(s, SCAN_T, 8)
    rows = SSM_CH // LANES

    def body(bu_ref, a_ref, x_ref, b3r, b3i, x3r, x3i, carry):
        @pl.when(pl.program_id(0) == 0)
        def _():
            carry[...] = jnp.zeros_like(carry)

        _to_time_major(b3r, bu_ref, 0)
        _to_time_major(b3i, bu_ref, 1)
        ar, ai = a_ref[0], a_ref[1]

        def step(t, c):
            xr, xi = c
            nr = ar * xr - ai * xi + b3r[t]
            ni = ar * xi + ai * xr + b3i[t]
            x3r[t] = nr
            x3i[t] = ni
            return nr, ni

        xr, xi = lax.fori_loop(0, t_blk, step, (carry[0], carry[1]), unroll=8)
        carry[0] = xr
        carry[1] = xi
        _from_time_major(x_ref, 0, x3r)
        _from_time_major(x_ref, 1, x3i)

    blk = pl.BlockSpec((2, t_blk, SSM_CH), lambda i: (0, i, 0))
    tm = pltpu.VMEM((t_blk, rows, LANES), F32)
    return pl.pallas_call(body, out_shape=jax.ShapeDtypeStruct(bu.shape, F32), grid=(s // t_blk,),
                          in_specs=[blk, pl.BlockSpec((2, rows, LANES), lambda i: (0, 0, 0))], out_specs=blk,
                          scratch_shapes=[tm, tm, tm, tm, pltpu.VMEM((2, rows, LANES), F32)],
                          compiler_params=_cparams(1), name="ssm_scan_fwd")(bu, a3)


def _scan_bwd(dx, x, a3):
    s = dx.shape[1]
    t_blk = _tile(s, SCAN_T, 8)
    nb = s // t_blk
    rows = SSM_CH // LANES

    def body(dx_ref, x_ref, a_ref, g_ref, da_ref, d3r, d3i, x3r, x3i, g3r, g3i, carry):
        first = pl.program_id(0) == 0

        @pl.when(first)
        def _():
            carry[...] = jnp.zeros_like(carry)

        _to_time_major(d3r, dx_ref, 0)
        _to_time_major(d3i, dx_ref, 1)
        _to_time_major(x3r, x_ref, 0)
        _to_time_major(x3i, x_ref, 1)
        ar, ai = a_ref[0], a_ref[1]

        def step(k, c):
            t = t_blk - 1 - k
            gr, gi, dar, dai = c
            xr, xi = x3r[t], x3i[t]
            dar = dar + gr * xr + gi * xi
            dai = dai + gi * xr - gr * xi
            ngr = d3r[t] + ar * gr + ai * gi
            ngi = d3i[t] + ar * gi - ai * gr
            g3r[t] = ngr
            g3i[t] = ngi
            return ngr, ngi, dar, dai

        zero = jnp.zeros((rows, LANES), F32)
        gr, gi, dar, dai = lax.fori_loop(0, t_blk, step, (carry[0], carry[1], zero, zero), unroll=8)
        carry[0] = gr
        carry[1] = gi

        @pl.when(first)
        def _():
            da_ref[0] = dar
            da_ref[1] = dai

        @pl.when(jnp.logical_not(first))
        def _():
            da_ref[0] += dar
            da_ref[1] += dai

        _from_time_major(g_ref, 0, g3r)
        _from_time_major(g_ref, 1, g3i)

    blk = pl.BlockSpec((2, t_blk, SSM_CH), lambda i: (0, nb - 1 - i, 0))
    par = pl.BlockSpec((2, rows, LANES), lambda i: (0, 0, 0))
    tm = pltpu.VMEM((t_blk, rows, LANES), F32)
    return pl.pallas_call(body, out_shape=(jax.ShapeDtypeStruct(dx.shape, F32), jax.ShapeDtypeStruct((2, rows, LANES), F32)),
                          grid=(nb,), in_specs=[blk, blk, par], out_specs=(blk, par),
                          scratch_shapes=[tm] * 6 + [pltpu.VMEM((2, rows, LANES), F32)],
                          compiler_params=_cparams(1), name="ssm_scan_bwd")(dx, x, a3)


SSM_TILES = WIDTH_C // LANES
SSM_TILE_W = SSM_CH // SSM_TILES


def _block_diag(m):
    gpt = N_GROUPS_C // SSM_TILES
    t = m.reshape(SSM_TILES, gpt, SSM_GROUP, SSM_STATE)
    eye = jnp.eye(gpt, dtype=m.dtype)
    return (t[:, :, :, None, :] * eye[None, :, None, :, None]).reshape(SSM_TILES, LANES, SSM_TILE_W)


def _block_diag_extract(t):
    gpt = N_GROUPS_C // SSM_TILES
    t = t.reshape(SSM_TILES, gpt, SSM_GROUP, gpt, SSM_STATE)
    eye = jnp.eye(gpt, dtype=t.dtype)
    return jnp.sum(t * eye[None, :, None, :, None], axis=3).reshape(WIDTH_C, SSM_STATE)


def _ssm_in(uc_src, uc_col0, bd):
    s = uc_src.shape[0]
    tm = _tile(s, 2048, 8)
    j0 = uc_col0 // LANES
    return _mm("ssm_in", "nn", (s // tm, 2 * SSM_TILES, 1),
               uc_src, pl.BlockSpec((tm, LANES), lambda i, j, k: (i, j0 + j % SSM_TILES)),
               bd, pl.BlockSpec((None, None, LANES, SSM_TILE_W), lambda i, j, k: (j // SSM_TILES, j % SSM_TILES, 0, 0)),
               jax.ShapeDtypeStruct((2, s, SSM_CH), F32),
               pl.BlockSpec((None, tm, SSM_TILE_W), lambda i, j, k: (j // SSM_TILES, i, j % SSM_TILES)), None)


def _ssm_in_dgrad(g, bd):
    s = g.shape[1]
    tm = _tile(s, 2048, 8)
    return _mm("ssm_in_dgrad", "nt", (s // tm, SSM_TILES, 2),
               g, pl.BlockSpec((None, tm, SSM_TILE_W), lambda i, j, k: (k, i, j)),
               bd, pl.BlockSpec((None, None, LANES, SSM_TILE_W), lambda i, j, k: (k, j, 0, 0)),
               jax.ShapeDtypeStruct((s, WIDTH_C), F32), pl.BlockSpec((tm, LANES), lambda i, j, k: (i, j)),
               (tm, LANES))


def _ssm_in_wgrad(uc_src, uc_col0, g):
    s = g.shape[1]
    tk = _tile(s, 2048, 8)
    j0 = uc_col0 // LANES
    return _mm("ssm_in_wgrad", "tn", (2, SSM_TILES, s // tk),
               uc_src, pl.BlockSpec((tk, LANES), lambda i, j, k: (k, j0 + j)),
               g, pl.BlockSpec((None, tk, SSM_TILE_W), lambda i, j, k: (i, k, j)),
               jax.ShapeDtypeStruct((2, SSM_TILES, LANES, SSM_TILE_W), F32),
               pl.BlockSpec((None, None, LANES, SSM_TILE_W), lambda i, j, k: (i, j, 0, 0)), (LANES, SSM_TILE_W))


def _ssm_out(x, cd):
    s = x.shape[1]
    tm = _tile(s, 2048, 8)
    return _mm("ssm_out", "nn", (s // tm, SSM_TILES, 2),
               x, pl.BlockSpec((None, tm, SSM_TILE_W), lambda i, j, k: (k, i, j)),
               cd, pl.BlockSpec((None, None, SSM_TILE_W, LANES), lambda i, j, k: (k, j, 0, 0)),
               jax.ShapeDtypeStruct((s, WIDTH_C), F32), pl.BlockSpec((tm, LANES), lambda i, j, k: (i, j)),
               (tm, LANES))


def _ssm_out_dgrad(dy, cd):
    s = dy.shape[0]
    tm = _tile(s, 2048, 8)
    return _mm("ssm_out_dgrad", "nt", (s // tm, 2 * SSM_TILES, 1),
               dy, pl.BlockSpec((tm, LANES), lambda i, j, k: (i, j % SSM_TILES)),
               cd, pl.BlockSpec((None, None, SSM_TILE_W, LANES), lambda i, j, k: (j // SSM_TILES, j % SSM_TILES, 0, 0)),
               jax.ShapeDtypeStruct((2, s, SSM_CH), F32),
               pl.BlockSpec((None, tm, SSM_TILE_W), lambda i, j, k: (j // SSM_TILES, i, j % SSM_TILES)), None)


def _ssm_out_wgrad(x, dy):
    s = dy.shape[0]
    tk = _tile(s, 2048, 8)
    return _mm("ssm_out_wgrad", "tn", (2, SSM_TILES, s // tk),
               x, pl.BlockSpec((None, tk, SSM_TILE_W), lambda i, j, k: (i, k, j)),
               dy, pl.BlockSpec((tk, LANES), lambda i, j, k: (k, j)),
               jax.ShapeDtypeStruct((2, SSM_TILES, SSM_TILE_W, LANES), F32),
               pl.BlockSpec((None, None, SSM_TILE_W, LANES), lambda i, j, k: (i, j, 0, 0)), (SSM_TILE_W, LANES))


def _ssm_param_inputs(w, l):
    rep = lambda a: jnp.repeat(a, SSM_GROUP, axis=0)
    lr, li, ld = w["lam_re"][l], w["lam_im"][l], w["log_dt"][l][:, None]
    br_t = jnp.transpose(w["b_re"][l], (0, 2, 1)).reshape(WIDTH_C, SSM_STATE)
    bi_t = jnp.transpose(w["b_im"][l], (0, 2, 1)).reshape(WIDTH_C, SSM_STATE)
    return lr, li, ld, rep(lr), rep(li), rep(ld), br_t, bi_t


def _ssm_params_fwd(pin):
    def body(*refs):
        res = _f_ssm_params(*[r[...] for r in refs[:8]])
        for r, v in zip(refs[8:], res):
            r[...] = v

    g, p = N_GROUPS_C, SSM_STATE
    return pl.pallas_call(body, out_shape=(jax.ShapeDtypeStruct((g, p), F32),) * 2
                          + (jax.ShapeDtypeStruct((WIDTH_C, p), F32),) * 2, name="ssm_params_fwd")(*pin)


def _ssm_params_bwd(pin, d_abr, d_abi, d_bbr, d_bbi):
    g, p = N_GROUPS_C, SSM_STATE
    group_sum = jnp.asarray(np.kron(np.eye(g, dtype=np.float32), np.ones((1, SSM_GROUP), np.float32)))

    def body(*refs):
        ins = [r[...] for r in refs[:8]]
        cts = tuple(r[...] for r in refs[8:12])
        gs = refs[12][...]
        d_lr_ref, d_li_ref, d_ld_ref, d_br_ref, d_bi_ref = refs[13:]
        _, vjp = jax.vjp(_f_ssm_params, *ins)
        d = vjp(cts)
        fold = lambda v: lax.dot_general(gs, v, _DIMS["nn"], preferred_element_type=F32, precision=lax.Precision.HIGHEST)
        d_lr_ref[...] = d[0] + fold(d[3])
        d_li_ref[...] = d[1] + fold(d[4])
        d_ld_ref[...] = d[2] + fold(jnp.broadcast_to(d[5], (WIDTH_C, p)))[:, 0:1]
        d_br_ref[...] = d[6]
        d_bi_ref[...] = d[7]

    return pl.pallas_call(body, out_shape=(jax.ShapeDtypeStruct((g, p), F32), jax.ShapeDtypeStruct((g, p), F32),
                                           jax.ShapeDtypeStruct((g, 1), F32), jax.ShapeDtypeStruct((WIDTH_C, p), F32),
                                           jax.ShapeDtypeStruct((WIDTH_C, p), F32)), name="ssm_params_bwd")(
        *pin, d_abr, d_abi, d_bbr, d_bbi, group_sum)


def _c_block_diag(w, l):
    def one(c):
        return jnp.transpose(_block_diag(c.reshape(WIDTH_C, SSM_STATE)), (0, 2, 1))
    return jnp.stack([one(w["c_re"][l]), -one(w["c_im"][l])], 0)


ANY = pl.BlockSpec(memory_space=pl.ANY)


def _ag8(name, xb):
    def body(x_ref, out_ref, send_sems, recv_sems, local_sem):
        x, y, c = lax.axis_index("x"), lax.axis_index("y"), lax.axis_index("c")
        me, sibling = (x, y, c), (x, y, 1 - c)
        chips = [(1 - x, y), (x, 1 - y), (1 - x, 1 - y)]

        def rows(px, py, pc):
            return out_ref.at[4 * px + 2 * py + pc]

        def copy(k, block, to, src=None):
            return pltpu.make_async_remote_copy(src_ref=rows(*block) if src is None else src, dst_ref=rows(*block),
                                                send_sem=send_sems.at[k], recv_sem=recv_sems.at[k],
                                                device_id=to, device_id_type=MESH)

        mine = pltpu.make_async_copy(x_ref, rows(*me), local_sem)
        mine.start()
        first = [copy(0, me, sibling, src=x_ref)]
        first += [copy(1 + j, me, (*chip, c), src=x_ref) for j, chip in enumerate(chips)]
        for cp in first:
            cp.start()
        passed = [copy(4 + j, (*chip, c), sibling) for j, chip in enumerate(chips)]
        for j, chip in enumerate(chips):
            copy(1 + j, (*chip, c), me).wait_recv()
            passed[j].start()
        copy(0, sibling, me).wait_recv()
        for j, chip in enumerate(chips):
            copy(4 + j, (*chip, 1 - c), me).wait_recv()
        for cp in first + passed:
            cp.wait_send()
        mine.wait()

    return pl.pallas_call(body, out_shape=jax.ShapeDtypeStruct((8,) + xb.shape, xb.dtype), in_specs=[ANY], out_specs=ANY,
                          scratch_shapes=[pltpu.SemaphoreType.DMA((7,)), pltpu.SemaphoreType.DMA((7,)),
                                          pltpu.SemaphoreType.DMA(())], name=name)(xb)


def _mesh_place():
    x, y, c = lax.axis_index("x"), lax.axis_index("y"), lax.axis_index("c")
    return x, y, c, [(1 - x, y), (x, 1 - y), (1 - x, 1 - y)]


ICI_CHUNK_BYTES = 2 << 20
D2D_CHUNK_BYTES = 1 << 20


def _split_rows(rows, row_bytes, chunk_bytes):
    k = 1
    for cand in range(1, max(1, (rows * row_bytes) // chunk_bytes) + 1):
        if rows % cand == 0 and (rows // cand) % SUBLANES_BF16 == 0:
            k = cand
    return [(j * (rows // k), rows // k) for j in range(k)]


def _run_side(side):
    ni, no = len(side["ins"]), len(side["out_sds"])

    def body(*refs):
        ins, outs, sems = refs[:ni], refs[ni:ni + no], refs[ni + no:]
        side["start"](ins, outs, sems)
        side["finish"](ins, outs, sems)

    return pl.pallas_call(body, out_shape=list(side["out_sds"]), in_specs=[ANY] * ni, out_specs=[ANY] * no,
                          scratch_shapes=list(side["sems"]), name=side["name"])(*side["ins"])


def _gather_side(wb, l):
    nw = len(wb)
    halves = [a.shape[1] // 2 for a in wb]
    pieces = [_split_rows(h, a.shape[2] * a.dtype.itemsize, ICI_CHUNK_BYTES) for a, h in zip(wb, halves)]

    def rows(ref, i, hc, r0=0, n=None):
        return ref.at[pl.ds(hc * halves[i] + r0, halves[i] if n is None else n)]

    def copier(sems):
        def copy(i, k, src, dst, to):
            return pltpu.make_async_remote_copy(src_ref=src, dst_ref=dst, send_sem=sems[0].at[6 * i + k],
                                                recv_sem=sems[1].at[6 * i + k], device_id=to, device_id_type=MESH)
        return copy

    def start(ins, outs, sems):
        x, y, c, chips = _mesh_place()
        me = 2 * x + y
        copy = copier(sems)
        for i in range(nw):
            for hc in range(2):
                for r0, n in pieces[i]:
                    pltpu.make_async_copy(rows(ins[i].at[l], i, hc, r0, n), rows(outs[i].at[me], i, hc, r0, n),
                                          sems[2].at[i]).start()
        for i in range(nw):
            for k, (px, py) in enumerate(chips):
                for r0, n in pieces[i]:
                    copy(i, k, rows(ins[i].at[l], i, c, r0, n), rows(outs[i].at[me], i, c, r0, n), (px, py, c)).start()

    def finish(ins, outs, sems):
        x, y, c, chips = _mesh_place()
        me = 2 * x + y
        copy = copier(sems)
        for k, (px, py) in enumerate(chips):
            for i in range(nw):
                src = outs[i].at[2 * px + py]
                copy(i, k, rows(src, i, c), rows(src, i, c), (px, py, c)).wait_recv()
                for r0, n in pieces[i]:
                    copy(i, 3 + k, rows(src, i, c, r0, n), rows(src, i, c, r0, n), (x, y, 1 - c)).start()
        for k, (px, py) in enumerate(chips):
            for i in range(nw):
                other = rows(outs[i].at[2 * px + py], i, 1 - c)
                copy(i, 3 + k, other, other, (x, y, 1 - c)).wait_recv()
        for i in range(nw):
            whole = rows(outs[i].at[me], i, c)
            for k in range(6):
                copy(i, k, whole, whole, (x, y, 1 - c)).wait_send()
            pltpu.make_async_copy(ins[i].at[l], outs[i].at[me], sems[2].at[i]).wait()

    return dict(name="gather_weights", ins=list(wb), out_sds=[jax.ShapeDtypeStruct((4,) + a.shape[1:], a.dtype) for a in wb],
                sems=[pltpu.SemaphoreType.DMA((6 * nw,)), pltpu.SemaphoreType.DMA((6 * nw,)), pltpu.SemaphoreType.DMA((nw,))],
                start=start, finish=finish)


def _grad_core_swap(g4):
    nw = len(g4)
    halves = [a.shape[1] // 2 for a in g4]
    pieces = [_split_rows(h, a.shape[2] * a.dtype.itemsize, D2D_CHUNK_BYTES) for a, h in zip(g4, halves)]

    def body(*refs):
        ins, theirs = refs[:nw], refs[nw:2 * nw]
        send_sems, recv_sems = refs[2 * nw:]
        x, y, c, _ = _mesh_place()

        def copy(i, src, dst):
            return pltpu.make_async_remote_copy(src_ref=src, dst_ref=dst, send_sem=send_sems.at[i], recv_sem=recv_sems.at[i],
                                                device_id=(x, y, 1 - c), device_id_type=MESH)

        for i in range(nw):
            for j in range(4):
                for r0, n in pieces[i]:
                    copy(i, ins[i].at[j, pl.ds((1 - c) * halves[i] + r0, n)], theirs[i].at[j, pl.ds(r0, n)]).start()
        for i in range(nw):
            copy(i, ins[i].at[:, pl.ds((1 - c) * halves[i], halves[i])], theirs[i]).wait()

    sds = [jax.ShapeDtypeStruct((4, h) + a.shape[2:], a.dtype) for a, h in zip(g4, halves)]
    return pl.pallas_call(body, out_shape=sds, in_specs=[ANY] * nw, out_specs=[ANY] * nw,
                          scratch_shapes=[pltpu.SemaphoreType.DMA((nw,)), pltpu.SemaphoreType.DMA((nw,))],
                          name="grad_core_swap")(*g4)


def _exchange_side(t4):
    nw = len(t4)
    pieces = [_split_rows(a.shape[1], a.shape[2] * a.dtype.itemsize, ICI_CHUNK_BYTES) for a in t4]

    def copier(sems, c):
        def copy(i, k, src, dst, px, py):
            return pltpu.make_async_remote_copy(src_ref=src, dst_ref=dst, send_sem=sems[0].at[3 * i + k],
                                                recv_sem=sems[1].at[3 * i + k], device_id=(px, py, c),
                                                device_id_type=MESH)
        return copy

    def start(ins, outs, sems):
        x, y, c, chips = _mesh_place()
        me = 2 * x + y
        copy = copier(sems, c)
        for i in range(nw):
            for r0, n in pieces[i]:
                for k, (px, py) in enumerate(chips):
                    copy(i, k, ins[i].at[2 * px + py, pl.ds(r0, n)], outs[i].at[me, pl.ds(r0, n)], px, py).start()
                pltpu.make_async_copy(ins[i].at[me, pl.ds(r0, n)], outs[i].at[me, pl.ds(r0, n)], sems[2].at[i]).start()

    def finish(ins, outs, sems):
        x, y, c, chips = _mesh_place()
        me = 2 * x + y
        copy = copier(sems, c)
        for i in range(nw):
            for k, (px, py) in enumerate(chips):
                copy(i, k, ins[i].at[me], outs[i].at[2 * px + py], px, py).wait_recv()
        for i in range(nw):
            for k, (px, py) in enumerate(chips):
                copy(i, k, ins[i].at[me], outs[i].at[me], px, py).wait_send()
            pltpu.make_async_copy(ins[i].at[me], outs[i].at[me], sems[2].at[i]).wait()

    return dict(name="grad_chip_exchange", ins=list(t4), out_sds=[jax.ShapeDtypeStruct(a.shape, a.dtype) for a in t4],
                sems=[pltpu.SemaphoreType.DMA((3 * nw,)), pltpu.SemaphoreType.DMA((3 * nw,)), pltpu.SemaphoreType.DMA((nw,))],
                start=start, finish=finish)


def _grad_half_swap(full, l):
    nw = len(full)
    halves = [a.shape[1] // 2 for a in full]
    pieces = [_split_rows(h, a.shape[2] * a.dtype.itemsize, D2D_CHUNK_BYTES) for a, h in zip(full, halves)]

    def body(*refs):
        bufs = refs[nw:2 * nw]
        send_sems, recv_sems = refs[2 * nw:]
        x, y, c, _ = _mesh_place()

        def copy(i, hc, r0, n):
            view = bufs[i].at[l, pl.ds(hc * halves[i] + r0, n)]
            return pltpu.make_async_remote_copy(src_ref=view, dst_ref=view, send_sem=send_sems.at[i], recv_sem=recv_sems.at[i],
                                                device_id=(x, y, 1 - c), device_id_type=MESH)

        for i in range(nw):
            for r0, n in pieces[i]:
                copy(i, c, r0, n).start()
        for i in range(nw):
            copy(i, c, 0, halves[i]).wait_send()
            copy(i, 1 - c, 0, halves[i]).wait_recv()

    return pl.pallas_call(body, out_shape=[jax.ShapeDtypeStruct(a.shape, a.dtype) for a in full],
                          in_specs=[ANY] * nw, out_specs=[ANY] * nw, input_output_aliases={i: i for i in range(nw)},
                          scratch_shapes=[pltpu.SemaphoreType.DMA((nw,)), pltpu.SemaphoreType.DMA((nw,))],
                          name="grad_half_swap")(*full)


def _core_add(g, theirs, core):
    _, rh, cc = theirs.shape
    tr = _tile(rh, max(SUBLANES_BF16, (1 << 19) // cc), SUBLANES_BF16)
    nt = rh // tr

    def body(c_ref, a_ref, b_ref, o_ref):
        o_ref[...] = (a_ref[...].astype(F32) + b_ref[...].astype(F32)).astype(o_ref.dtype)

    blk = (None, tr, cc)
    return pl.pallas_call(
        body, out_shape=jax.ShapeDtypeStruct(theirs.shape, theirs.dtype),
        grid_spec=pltpu.PrefetchScalarGridSpec(
            num_scalar_prefetch=1, grid=(4, nt),
            in_specs=[pl.BlockSpec(blk, lambda j, t, c_ref: (j, c_ref[0] * nt + t, 0)),
                      pl.BlockSpec(blk, lambda j, t, c_ref: (j, t, 0))],
            out_specs=pl.BlockSpec(blk, lambda j, t, c_ref: (j, t, 0))),
        compiler_params=_cparams(2), name="grad_core_add")(core, g, theirs)


def _sum_to_half(xb, core, l, depth, stacked=None):
    n, rh, cc = xb.shape
    tr = _tile(rh, max(SUBLANES_BF16, (1 << 18) // cc), SUBLANES_BF16)
    nt = rh // tr

    def body(c_ref, x_ref, *rest):
        acc = x_ref[0].astype(F32)
        for k in range(1, n):
            acc = acc + x_ref[k].astype(F32)
        rest[-1][...] = acc

    ins = (core, xb) if stacked is None else (core, xb, stacked)
    return pl.pallas_call(
        body, out_shape=jax.ShapeDtypeStruct((depth, 2 * rh, cc), F32),
        grid_spec=pltpu.PrefetchScalarGridSpec(
            num_scalar_prefetch=1, grid=(nt,),
            in_specs=[pl.BlockSpec((n, tr, cc), lambda t, c_ref: (0, t, 0))] + ([] if stacked is None else [ANY]),
            out_specs=pl.BlockSpec((None, tr, cc), lambda t, c_ref: (l, c_ref[0] * nt + t, 0))),
        input_output_aliases={} if stacked is None else {2: 0},
        compiler_params=_cparams(1), name="grad_chip_sum")(*ins)


def _sum_lead(name, xb, out_dtype=F32):
    n, rows, w = xb.shape
    tr = _tile(rows, max(SUBLANES_BF16, (1 << 18) // w), SUBLANES_BF16)

    def body(x_ref, o_ref):
        acc = x_ref[0].astype(F32)
        for k in range(1, n):
            acc = acc + x_ref[k].astype(F32)
        o_ref[...] = acc.astype(o_ref.dtype)

    return pl.pallas_call(body, out_shape=jax.ShapeDtypeStruct((rows, w), out_dtype), grid=(rows // tr,),
                          in_specs=[pl.BlockSpec((n, tr, w), lambda i: (0, i, 0))],
                          out_specs=pl.BlockSpec((tr, w), lambda i: (i, 0)), compiler_params=_cparams(1), name=name)(xb)


def _pad_to(v, mult):
    n = v.shape[-1]
    pad = (-n) % mult
    return v if pad == 0 else jnp.pad(v, [(0, 0)] * (v.ndim - 1) + [(0, pad)])


RELAYOUT_ROWS = 256


def _cols_from_shards(a):
    _, k, cc = a.shape
    tr = _tile(k, RELAYOUT_ROWS, SUBLANES_BF16)

    def body(i_ref, o_ref):
        for j in range(4):
            o_ref[:, j * cc:(j + 1) * cc] = i_ref[j]

    return pl.pallas_call(body, out_shape=jax.ShapeDtypeStruct((k, 4 * cc), a.dtype), grid=(k // tr,),
                          in_specs=[pl.BlockSpec((4, tr, cc), lambda i: (0, i, 0))],
                          out_specs=pl.BlockSpec((tr, 4 * cc), lambda i: (i, 0)),
                          compiler_params=_cparams(1), name="cols_from_shards")(a)


def _cols_to_shards(a):
    k, n = a.shape
    cc = n // 4
    tr = _tile(k, RELAYOUT_ROWS, SUBLANES_BF16)

    def body(i_ref, o_ref):
        for j in range(4):
            o_ref[j] = i_ref[:, j * cc:(j + 1) * cc]

    return pl.pallas_call(body, out_shape=jax.ShapeDtypeStruct((4, k, cc), a.dtype), grid=(k // tr,),
                          in_specs=[pl.BlockSpec((tr, n), lambda i: (i, 0))],
                          out_specs=pl.BlockSpec((4, tr, cc), lambda i: (0, i, 0)),
                          compiler_params=_cparams(1), name="cols_to_shards")(a)


COMM_GROUPS = (("w_in",), ("w_ffn_in",), ("w_glu", "w_pa", "w_pb", "w_pc", "w_o", "w_ffn_out"))


def _gather_sides(wb, l):
    return [_gather_side([wb[n] for n in grp], l) for grp in COMM_GROUPS]


def _first_weights(wb):
    sides = _gather_sides(wb, 0)
    return _next_weights(wb, 0, [_run_side(sides[0]), None, None]), sides


def _next_weights(wb, l, got):
    kinds = dict(BIG)
    out = {}
    for grp, arrs in zip(COMM_GROUPS, got):
        for n, a in zip(grp, arrs if arrs is not None else ()):
            if kinds[n] == "row":
                out[n] = a.reshape(4 * a.shape[1], a.shape[2])
            else:
                out[n] = _cols_from_shards(a) if n == "w_in" else a
    return out


def _start_reduce(g):
    g4 = []
    for n, kind in BIG:
        a = g[n]
        if kind == "row":
            a = a.reshape(4, a.shape[0] // 4, a.shape[1])
        elif a.ndim == 2:
            a = _cols_to_shards(a)
        g4.append(a)
    core = lax.axis_index("c").astype(jnp.int32).reshape(1)
    theirs = _grad_core_swap(g4)
    return {n: _core_add(a, b, core) for n, a, b in zip(BIG_NAMES, g4, theirs)}


def _exchange_sides(pending):
    return [_exchange_side([pending[n] for n in grp]) for grp in COMM_GROUPS]


def _run_exchange(pending):
    return [_run_side(s) for s in _exchange_sides(pending)]


def _finish_reduce(pending, arrived, l, depth, stacked):
    core = lax.axis_index("c").astype(jnp.int32).reshape(1)
    names = [n for grp in COMM_GROUPS for n in grp]
    arrs = [a for grp in arrived for a in grp]
    halves = [_sum_to_half(a, core, l, depth, None if stacked is None else stacked[n]) for n, a in zip(names, arrs)]
    return dict(zip(names, _grad_half_swap(halves, l)))


def _adam_fn(w, g, m, v):
    m = ADAM_B1 * m + (1.0 - ADAM_B1) * g
    v = ADAM_B2 * v + (1.0 - ADAM_B2) * (g * g)
    m_hat = m / (1.0 - ADAM_B1 ** ADAM_STEP)
    v_hat = v / (1.0 - ADAM_B2 ** ADAM_STEP)
    return -ADAM_LR * (m_hat / (jnp.sqrt(v_hat) + ADAM_EPS) + ADAM_WD * w), m, v


def _adamw(name, w, g, m, v):
    shape = w.shape
    cols = shape[-1]
    f = lambda a: a.reshape(-1, cols)
    rows = f(w).shape[0]
    tr = max(8, (1 << 19) // cols)
    d, nm, nv = _ew(name, _adam_fn, rows, [R(f(w)), R(f(g)), R(f(m)), R(f(v))], [(cols, F32)] * 3, tr=tr)
    return d.reshape(shape), nm.reshape(shape), nv.reshape(shape)


def _beside(sides, k, call):
    if sides is None:
        return call(side=None), None
    return call(side=sides[k])


def _layer_fwd(x, wl, bias, alpha, sides=None, own=None, wb=None):
    s, d = x.shape
    b_in = wl["b_in"][None, :]
    got = [None] * len(COMM_GROUPS)
    mine = [None] * len(COMM_GROUPS)
    qkv, mine[1] = _beside(own, 1, functools.partial(_mm_nn, "proj_qkv", x, wl["w_in"], n0=0, n=3 * QKV_W, bias=b_in,
                                                     out_dtype=BF16, tn=768))
    rest, mine[2] = _beside(own, 2, functools.partial(_mm_nn, "proj_rest", x, wl["w_in"], n0=3 * QKV_W, bias=b_in, tn=768))
    if own is not None:
        wl.update(_next_weights(wb, 0, mine))
    d_ff = wl["w_ffn_out"].shape[0]
    qkv_s = _to_streams(qkv)
    (o_s, l_s), got[1] = _attn_fwd(qkv_s, bias, side=sides[1] if sides else None)
    o_t, l_t = _streams_to_tok(o_s), _streams_to_tok(l_s)
    ya, lse = _ew("attn_combine", _f_combine, s, [R(a) for a in o_t + l_t], [(WIDTH_A, F32), (WIDTH_A, F32)])
    yb = _sgu_fwd(rest, wl["sgu_ln_g"][None], wl["sgu_ln_b"][None], wl["w_tril"], wl["b_col"])
    bu = _ssm_in(rest, REST_UC, wl["bd"])
    xs = _scan_fwd(bu, wl["a3"])
    y_lin = _ssm_out(xs, wl["cd"])
    uc_blk = REST_UC // WIDTH_C
    yc0 = _ew("ssm_skip_gelu", _f_ssm_out, s, [R(y_lin), R(rest, WIDTH_C, uc_blk), P(wl["d_skip"][None])], [(WIDTH_C, F32)])
    t_glu = _mm_nn("glu_proj", yc0, wl["w_glu"], bias=wl["b_glu"][None], tn=768)
    yc = _ew("glu", _f_glu, s, [R(yc0), R(t_glu)], [(WIDTH_C, BF16)])
    pa = _mm_nn("proj_a", ya, wl["w_pa"])
    pb = _mm_nn("proj_b", yb, wl["w_pb"])
    pc = _mm_nn("proj_c", yc, wl["w_pc"])
    gw = _tile(math.gcd(d, REST_GL), 256, LANES)
    gl_ins = [R(rest, gw, (REST_GL + i * d) // gw, True) for i in range(N_BRANCH)]
    merged = _ew("merge", _f_merge, s, gl_ins + [R(pa, gw, 0, True), R(pb, gw, 0, True), R(pc, gw, 0, True)],
                 [(gw, BF16)], nj=d // gw, tr=1024)
    h1 = _mm_nn("proj_o", merged, wl["w_o"], add=x, add_scale=alpha)
    x1 = _ew("ln1", _f_ln, s, [R(h1), P(wl["ln1_g"][None]), P(wl["ln1_b"][None])], [(d, F32)])
    ff, got[0] = _beside(sides, 0, functools.partial(_mm_nn, "ffn_in", x1, wl["w_ffn_in"], tn=1408))
    fw = _tile(d_ff, 512, LANES)
    act = _ew("swiglu", _f_swiglu, s, [R(ff, fw, 0, True), R(ff, fw, d_ff // fw, True)], [(fw, BF16)], nj=d_ff // fw, tr=1024)
    h2, got[2] = _beside(sides, 2, functools.partial(_mm_nn, "ffn_out", act, wl["w_ffn_out"], add=x1, add_scale=alpha))
    x2 = _ew("ln2", _f_ln, s, [R(h2), P(wl["ln2_g"][None]), P(wl["ln2_b"][None])], [(d, F32)])
    saved = dict(x=x, qkv_s=qkv_s, rest=rest, ya=ya, lse=lse, yb=yb, xs=xs, y_lin=y_lin, yc0=yc0, t_glu=t_glu, yc=yc,
                 pa=pa, pb=pb, pc=pc, merged=merged, h1=h1, x1=x1, ff=ff, act=act, h2=h2)
    return x2, saved, got


def _vjp_rows(f, n_primal):
    def fn(*args):
        n_ct = len(args) - n_primal
        cts, primals = args[:n_ct], args[n_ct:]
        out, vjp = jax.vjp(f, *primals)
        ct = tuple(c.astype(F32) for c in cts)
        return vjp(ct if isinstance(out, (tuple, list)) else ct[0])
    return fn


def _layer_bwd(dx2, sv, wl, bias, alpha, sides=None):
    s, d = dx2.shape
    arrived = [None] * len(COMM_GROUPS)
    d_ff = wl["w_ffn_out"].shape[0]
    g = {}
    row = lambda a: a[None]
    dh2, g["ln2_g"], g["ln2_b"] = _ew("ln2_bwd", _vjp_rows(_f_ln, 3), s,
                                      [R(dx2), R(sv["h2"]), P(row(wl["ln2_g"])), P(row(wl["ln2_b"]))],
                                      [(d, F32)], accs=[d, d])
    g["w_ffn_out"] = _mm_tn("ffn_out_wgrad", sv["act"], dh2)
    dact = _mm_nt("ffn_out_dgrad", dh2, wl["w_ffn_out"], tn=1408, tk=2048)
    fw = _tile(d_ff, 512, LANES)
    nf = d_ff // fw
    dgf, dup = _ew("swiglu_bwd", _vjp_rows(_f_swiglu, 2), s,
                   [R(dact, fw, 0, True), R(sv["ff"], fw, 0, True), R(sv["ff"], fw, nf, True)],
                   [(fw, BF16), (fw, BF16)], nj=nf, tr=1024)
    dff = jnp.concatenate([dgf, dup], axis=1)
    g["w_ffn_in"], arrived[0] = _beside(sides, 0, functools.partial(_mm_tn, "ffn_in_wgrad", sv["x1"], dff, tn=1408,
                                                                    col_shards=True))
    dx1, arrived[1] = _beside(sides, 1, functools.partial(_mm_nt, "ffn_in_dgrad", dff, wl["w_ffn_in"], add=dh2,
                                                          add_scale=alpha, tk=1408))
    dh1, g["ln1_g"], g["ln1_b"] = _ew("ln1_bwd", _vjp_rows(_f_ln, 3), s,
                                      [R(dx1), R(sv["h1"]), P(row(wl["ln1_g"])), P(row(wl["ln1_b"]))],
                                      [(d, F32)], accs=[d, d])
    g["w_o"] = _mm_tn("proj_o_wgrad", sv["merged"], dh1)
    dmerged = _mm_nt("proj_o_dgrad", dh1, wl["w_o"])
    rest = sv["rest"]
    gw = _tile(math.gcd(d, REST_GL), 256, LANES)
    gl_ins = [R(rest, gw, (REST_GL + i * d) // gw, True) for i in range(N_BRANCH)]
    dg0, dg1, dg2, dpa, dpb, dpc = _ew(
        "merge_bwd", _vjp_rows(_f_merge, 6), s,
        [R(dmerged, gw, 0, True)] + gl_ins + [R(sv[k], gw, 0, True) for k in ("pa", "pb", "pc")],
        [(gw, BF16)] * 6, nj=d // gw, tr=1024)
    g["w_pa"] = _mm_tn("proj_a_wgrad", sv["ya"], dpa, col_shards=True)
    g["w_pb"] = _mm_tn("proj_b_wgrad", sv["yb"], dpb, col_shards=True)
    g["w_pc"] = _mm_tn("proj_c_wgrad", sv["yc"], dpc, col_shards=True)
    dya = _mm_nt("proj_a_dgrad", dpa, wl["w_pa"], tn=512)
    dyb = _mm_nt("proj_b_dgrad", dpb, wl["w_pb"], tn=768)
    dyc = _mm_nt("proj_c_dgrad", dpc, wl["w_pc"], tn=768)
    dyc0_a, dt_glu, g["b_glu"] = _ew("glu_bwd", lambda ct, a, t: (lambda r: (r[0], r[1], _rowsum(r[1])))(_vjp_rows(_f_glu, 2)(ct, a, t)),
                                     s, [R(dyc), R(sv["yc0"]), R(sv["t_glu"])], [(WIDTH_C, F32), (WIDTH_C, BF16)], accs=[WIDTH_C])
    g["w_glu"] = _mm_tn("glu_wgrad", sv["yc0"], dt_glu)
    dyc0 = _mm_nt("glu_dgrad", dt_glu, wl["w_glu"], add=dyc0_a, tn=768)
    uc_blk = REST_UC // WIDTH_C
    dy_lin, duc_skip, g["d_skip"] = _ew("ssm_skip_gelu_bwd", _vjp_rows(_f_ssm_out, 3), s,
                                        [R(dyc0), R(sv["y_lin"]), R(rest, WIDTH_C, uc_blk), P(row(wl["d_skip"]))],
                                        [(WIDTH_C, F32), (WIDTH_C, F32)], accs=[WIDTH_C])
    d_cd = _ssm_out_wgrad(sv["xs"], dy_lin)
    dxs = _ssm_out_dgrad(dy_lin, wl["cd"])
    gs, da = _scan_bwd(dxs, sv["xs"], wl["a3"])
    d_bd = _ssm_in_wgrad(rest, REST_UC, gs)
    duc_lin = _ssm_in_dgrad(gs, wl["bd"])
    duc = _ew("ssm_duc", lambda a, b: a + b, s, [R(duc_lin), R(duc_skip)], [(WIDTH_C, BF16)])
    d_abr = da[0].reshape(N_GROUPS_C, SSM_STATE)
    d_abi = da[1].reshape(N_GROUPS_C, SSM_STATE)
    d_lr, d_li, d_ld, d_br_t, d_bi_t = _ssm_params_bwd(wl["ssm_pin"], d_abr, d_abi,
                                                       _block_diag_extract(d_bd[0]), _block_diag_extract(d_bd[1]))
    g["lam_re"], g["lam_im"], g["log_dt"] = d_lr, d_li, d_ld[:, 0]
    un_t = lambda a: jnp.transpose(a.reshape(N_GROUPS_C, SSM_GROUP, SSM_STATE), (0, 2, 1))
    g["b_re"], g["b_im"] = un_t(d_br_t), un_t(d_bi_t)
    cd_ex = lambda a: _block_diag_extract(jnp.transpose(a, (0, 2, 1))).reshape(N_GROUPS_C, SSM_GROUP, SSM_STATE)
    g["c_re"], g["c_im"] = cd_ex(d_cd[0]), -cd_ex(d_cd[1])
    dzu, dzv, dws, dbs, g["sgu_ln_g"], g["sgu_ln_b"] = _sgu_bwd(rest, row(wl["sgu_ln_g"]), row(wl["sgu_ln_b"]),
                                                                  wl["w_tril"], wl["b_col"], dyb)
    g["w_s"] = jnp.tril(dws)
    g["b_s"] = dbs[:, :, 0]
    do_s, ya_s, lse_s = _tok_to_streams(dya), _tok_to_streams(sv["ya"]), _tok_to_streams(sv["lse"])
    dq, dk, dv, dbias = _attn_bwd(sv["qkv_s"], bias, do_s, ya_s, lse_s)
    dqkv = jnp.stack([jnp.stack(_streams_to_tok(t), 1) for t in (dq, dk, dv)], 1)
    dproj = jnp.concatenate([dqkv.reshape(s, 3 * QKV_W).astype(BF16), dzu, dzv, duc, dg0, dg1, dg2], axis=1)
    n_in = dproj.shape[1]
    cw = _tile(n_in, 1024, LANES)
    g["b_in"] = _ew("b_in_grad", lambda a: _rowsum(a.astype(F32)), s, [R(dproj, cw, 0, True)], [], accs=[cw],
                    nj=n_in // cw, tr=512)
    g["w_in"], arrived[2] = _beside(sides, 2, functools.partial(_mm_tn, "proj_in_wgrad", sv["x"], dproj, tn=768))
    dx = _mm_nt("proj_in_dgrad", dproj, wl["w_in"], add=dh1, add_scale=alpha, tk=768)
    for k in ("ln2_g", "ln2_b", "ln1_g", "ln1_b", "b_glu", "d_skip", "sgu_ln_g", "sgu_ln_b", "b_in"):
        g[k] = g[k][0]
    return dx, g, dbias, arrived


def _loss_head(y, target):
    s, d = y.shape

    def fn(yb, tb):
        err = yb - tb
        return err / d, _rowsum(err * err)

    dy, sq = _ew("loss_head", fn, s, [R(y), R(target)], [(d, F32)], accs=[d])
    return dy, 0.5 * jnp.sum(sq) / d


def _step(x, target, w, m, v):
    depth = w["w_in"].shape[0]
    alpha = (2 * depth) ** 0.25
    bias = _band_bias(w["rel_bias"])
    wb = {n: w[n].astype(BF16) for n in BIG_NAMES}
    xl = x[0]
    layers, saved = [], []
    big, own = _first_weights(wb)
    for l in range(depth):
        wl = dict(big)
        for n in SMALL:
            if n != "rel_bias":
                wl[n] = w[n][l]
        wl["w_tril"] = jnp.tril(w["w_s"][l]).astype(BF16)
        wl["b_col"] = w["b_s"][l][:, :, None]
        pin = _ssm_param_inputs(w, l)
        abr, abi, bbr, bbi = _ssm_params_fwd(pin)
        wl["ssm_pin"] = pin
        wl["a3"] = jnp.stack([abr.reshape(-1, LANES), abi.reshape(-1, LANES)], 0)
        wl["bd"] = jnp.stack([_block_diag(bbr), _block_diag(bbi)], 0)
        wl["cd"] = _c_block_diag(w, l)
        xl, sv, got = _layer_fwd(xl, wl, bias, alpha, _gather_sides(wb, l + 1) if l + 1 < depth else None,
                                 own if l == 0 else None, wb)
        if l + 1 < depth:
            big = _next_weights(wb, l + 1, got)
        layers.append(wl)
        saved.append(sv)
    dx, loss = _loss_head(xl, target[0])
    loss = lax.psum(loss, ("x", "y", "c"))
    grads = {n: [None] * depth for n in SMALL if n != "rel_bias"}
    dbias_sum = None
    pending = None
    big_grads = None
    for l in reversed(range(depth)):
        dx, g, dbias, arrived = _layer_bwd(dx, saved[l], layers[l], bias, alpha,
                                           _exchange_sides(pending) if pending is not None else None)
        dbias_sum = dbias if dbias_sum is None else dbias_sum + dbias
        if pending is not None:
            big_grads = _finish_reduce(pending, arrived, l + 1, depth, big_grads)
        pending = _start_reduce({n: g[n] for n in BIG_NAMES})
        for n in SMALL:
            if n != "rel_bias":
                grads[n][l] = g[n]
    big_grads = _finish_reduce(pending, _run_exchange(pending), 0, depth, big_grads)
    grads = {n: jnp.stack(gl, 0) for n, gl in grads.items()}
    grads.update(big_grads)
    grads["rel_bias"] = _bias_to_buckets(dbias_sum)
    small_vec = _pad_to(jnp.concatenate([grads[n].reshape(-1) for n in SMALL]), 8 * LANES).reshape(-1, LANES)
    small_sum = _sum_lead("small_grad_sum", _ag8("small_grad_gather", small_vec)).reshape(-1)
    off = 0
    for n in SMALL:
        size = math.prod(w[n].shape)
        grads[n] = small_sum[off:off + size].reshape(w[n].shape)
        off += size
    pack = lambda t: _pad_to(jnp.concatenate([t[n].reshape(-1) for n in SMALL]), 8 * LANES).reshape(-1, LANES)
    sd, sm, sv_ = _adamw("adamw_small", pack(w), small_sum.reshape(-1, LANES), pack(m), pack(v))
    delta, new_m, new_v = {}, {}, {}
    off = 0
    for n in SMALL:
        size = math.prod(w[n].shape)
        take = lambda t: t.reshape(-1)[off:off + size].reshape(w[n].shape)
        delta[n], new_m[n], new_v[n] = take(sd), take(sm), take(sv_)
        off += size
    for n in BIG_NAMES:
        delta[n], new_m[n], new_v[n] = _adamw("adamw_" + n, w[n], grads[n], m[n], v[n])
    return loss, dx[None], grads, delta, new_m, new_v


def kernel(x, w_in, b_in, rel_bias, sgu_ln_g, sgu_ln_b, w_s, b_s, lam_re, lam_im, log_dt, b_re, b_im, c_re, c_im, d_skip, w_glu, b_glu, w_pa, w_pb, w_pc, w_o, ln1_g, ln1_b, w_ffn_in, w_ffn_out, ln2_g, ln2_b, loss_target, m_w_in, m_b_in, m_rel_bias, m_sgu_ln_g, m_sgu_ln_b, m_w_s, m_b_s, m_lam_re, m_lam_im, m_log_dt, m_b_re, m_b_im, m_c_re, m_c_im, m_d_skip, m_w_glu, m_b_glu, m_w_pa, m_w_pb, m_w_pc, m_w_o, m_ln1_g, m_ln1_b, m_w_ffn_in, m_w_ffn_out, m_ln2_g, m_ln2_b, v_w_in, v_b_in, v_rel_bias, v_sgu_ln_g, v_sgu_ln_b, v_w_s, v_b_s, v_lam_re, v_lam_im, v_log_dt, v_b_re, v_b_im, v_c_re, v_c_im, v_d_skip, v_w_glu, v_b_glu, v_w_pa, v_w_pb, v_w_pc, v_w_o, v_ln1_g, v_ln1_b, v_w_ffn_in, v_w_ffn_out, v_ln2_g, v_ln2_b):
    w = dict(w_in=w_in, b_in=b_in, rel_bias=rel_bias, sgu_ln_g=sgu_ln_g, sgu_ln_b=sgu_ln_b, w_s=w_s, b_s=b_s,
             lam_re=lam_re, lam_im=lam_im, log_dt=log_dt, b_re=b_re, b_im=b_im, c_re=c_re, c_im=c_im, d_skip=d_skip,
             w_glu=w_glu, b_glu=b_glu, w_pa=w_pa, w_pb=w_pb, w_pc=w_pc, w_o=w_o, ln1_g=ln1_g, ln1_b=ln1_b,
             w_ffn_in=w_ffn_in, w_ffn_out=w_ffn_out, ln2_g=ln2_g, ln2_b=ln2_b)
    m = dict(w_in=m_w_in, b_in=m_b_in, rel_bias=m_rel_bias, sgu_ln_g=m_sgu_ln_g, sgu_ln_b=m_sgu_ln_b, w_s=m_w_s,
             b_s=m_b_s, lam_re=m_lam_re, lam_im=m_lam_im, log_dt=m_log_dt, b_re=m_b_re, b_im=m_b_im, c_re=m_c_re,
             c_im=m_c_im, d_skip=m_d_skip, w_glu=m_w_glu, b_glu=m_b_glu, w_pa=m_w_pa, w_pb=m_w_pb, w_pc=m_w_pc,
             w_o=m_w_o, ln1_g=m_ln1_g, ln1_b=m_ln1_b, w_ffn_in=m_w_ffn_in, w_ffn_out=m_w_ffn_out, ln2_g=m_ln2_g,
             ln2_b=m_ln2_b)
    v = dict(w_in=v_w_in, b_in=v_b_in, rel_bias=v_rel_bias, sgu_ln_g=v_sgu_ln_g, sgu_ln_b=v_sgu_ln_b, w_s=v_w_s,
             b_s=v_b_s, lam_re=v_lam_re, lam_im=v_lam_im, log_dt=v_log_dt, b_re=v_b_re, b_im=v_b_im, c_re=v_c_re,
             c_im=v_c_im, d_skip=v_d_skip, w_glu=v_w_glu, b_glu=v_b_glu, w_pa=v_w_pa, w_pb=v_w_pb, w_pc=v_w_pc,
             w_o=v_w_o, ln1_g=v_ln1_g, ln1_b=v_ln1_b, w_ffn_in=v_w_ffn_in, w_ffn_out=v_w_ffn_out, ln2_g=v_ln2_g,
             ln2_b=v_ln2_b)
    loss, grad_x, grads, delta, new_m, new_v = _step(x, loss_target, w, m, v)
    return (loss, grad_x, *[grads[n] for n in WEIGHTS], *[delta[n] for n in WEIGHTS],
            *[new_m[n] for n in WEIGHTS], *[new_v[n] for n in WEIGHTS])
```

```python
import functools
import math

import numpy as np
import jax
import jax.numpy as jnp
from jax import lax
from jax.experimental import pallas as pl
from jax.experimental.pallas import tpu as pltpu

F32 = jnp.float32
BF16 = jnp.bfloat16
MESH = pl.DeviceIdType.MESH

ATT_PATTERNS = ((128, 1), (512, 4), (2048, 16))
N_GROUPS_A = 3
HEADS = 8
HEAD_DIM = 64
WIDTH_A = HEADS * HEAD_DIM
QKV_W = N_GROUPS_A * WIDTH_A
ATT_BLOCK = 128
N_REL_BUCKETS = 32
REL_MAX_DIST = 2048
NEG_INF = -1e30
CHUNK = 128
WIDTH_B = 768
N_GROUPS_B = 6
GROUP_B = 128
WIDTH_C = 768
SSM_GROUP = 16
N_GROUPS_C = 48
SSM_STATE = 64
SSM_CH = N_GROUPS_C * SSM_STATE
N_BRANCH = 3
LN_EPS = 1e-5
ADAM_LR = 0.001
ADAM_B1 = 0.9
ADAM_B2 = 0.999
ADAM_EPS = 1e-08
ADAM_WD = 0.01
ADAM_STEP = 10

LANES = 128
SUBLANES_BF16 = 16
V7X_VMEM_BYTES = 64 * 1024 * 1024
VMEM_LIMIT = V7X_VMEM_BYTES * 7 // 8

OFF_ZB = 3 * QKV_W
REST_ZB = 0
REST_UC = 2 * WIDTH_B
REST_GL = 2 * WIDTH_B + WIDTH_C

BIG = (("w_in", "col"), ("w_glu", "row"), ("w_pa", "col"), ("w_pb", "col"), ("w_pc", "col"),
       ("w_o", "row"), ("w_ffn_in", "col"), ("w_ffn_out", "row"))
WEIGHTS = ("w_in", "b_in", "rel_bias", "sgu_ln_g", "sgu_ln_b", "w_s", "b_s", "lam_re", "lam_im", "log_dt",
           "b_re", "b_im", "c_re", "c_im", "d_skip", "w_glu", "b_glu", "w_pa", "w_pb", "w_pc", "w_o",
           "ln1_g", "ln1_b", "w_ffn_in", "w_ffn_out", "ln2_g", "ln2_b")
BIG_NAMES = tuple(n for n, _ in BIG)
SMALL = tuple(n for n in WEIGHTS if n not in BIG_NAMES)


def _tile(n, target, mult):
    t = (min(target, n) // mult) * mult
    while t >= mult:
        if n % t == 0:
            return t
        t -= mult
    return n


def _cparams(n_axes):
    return pltpu.CompilerParams(dimension_semantics=("arbitrary",) * n_axes, vmem_limit_bytes=VMEM_LIMIT)


_DIMS = {"nn": (((1,), (0,)), ((), ())), "nt": (((1,), (1,)), ((), ())), "tn": (((0,), (0,)), ((), ()))}


def _mm(name, mode, grid, a, a_spec, b, b_spec, out_sds, o_spec, acc_shape, *, bias=None, bias_spec=None,
        add=None, add_spec=None, add_scale=1.0, exact=False, side=None):
    nk = grid[2]
    has_bias = bias is not None
    has_add = add is not None
    n_side_in = len(side["ins"]) if side else 0
    n_side_out = len(side["out_sds"]) if side else 0
    n_side_sem = len(side["sems"]) if side else 0

    def body(*refs):
        a_ref, b_ref = refs[0], refs[1]
        pos = 2
        bias_ref = add_ref = None
        if has_bias:
            bias_ref = refs[pos]
            pos += 1
        if has_add:
            add_ref = refs[pos]
            pos += 1
        side_ins = refs[pos:pos + n_side_in]
        pos += n_side_in
        o_ref = refs[pos]
        side_outs = refs[pos + 1:pos + 1 + n_side_out]
        side_sems = refs[len(refs) - n_side_sem:] if n_side_sem else ()
        pos += n_side_out
        if side:
            at = [pl.program_id(d) for d in range(3)]

            @pl.when((at[0] == 0) & (at[1] == 0) & (at[2] == 0))
            def _():
                side["start"](side_ins, side_outs, side_sems)
        if exact:
            part = lax.dot_general(a_ref[...].astype(F32), b_ref[...].astype(F32), _DIMS[mode],
                                   preferred_element_type=F32, precision=lax.Precision.HIGHEST)
        else:
            part = lax.dot_general(a_ref[...].astype(BF16), b_ref[...].astype(BF16), _DIMS[mode],
                                   preferred_element_type=F32)

        def finish(r):
            if has_bias:
                r = r + bias_ref[...]
            if has_add:
                r = r + add_scale * add_ref[...].astype(F32)
            o_ref[...] = r.astype(o_ref.dtype)

        if nk == 1:
            finish(part)
        else:
            acc_ref = refs[pos + 1]
            k = pl.program_id(2)

            @pl.when(k == 0)
            def _():
                acc_ref[...] = part

            @pl.when(k > 0)
            def _():
                acc_ref[...] += part

            @pl.when(k == nk - 1)
            def _():
                finish(acc_ref[...])

        if side:
            @pl.when((at[0] == grid[0] - 1) & (at[1] == grid[1] - 1) & (at[2] == grid[2] - 1))
            def _():
                side["finish"](side_ins, side_outs, side_sems)

    ins, specs = [a, b], [a_spec, b_spec]
    if has_bias:
        ins.append(bias)
        specs.append(bias_spec)
    if has_add:
        ins.append(add)
        specs.append(add_spec)
    scratch = [pltpu.VMEM(acc_shape, F32)] if nk > 1 else []
    if not side:
        return pl.pallas_call(body, out_shape=out_sds, grid=grid, in_specs=specs, out_specs=o_spec,
                              scratch_shapes=scratch, compiler_params=_cparams(3), name=name)(*ins)
    res = pl.pallas_call(body, out_shape=[out_sds] + list(side["out_sds"]), grid=grid,
                         in_specs=specs + [ANY] * n_side_in, out_specs=[o_spec] + [ANY] * n_side_out,
                         scratch_shapes=scratch + list(side["sems"]), compiler_params=_cparams(3),
                         name=name + "_" + side["name"])(*ins, *side["ins"])
    return res[0], res[1:]


def _mm_nn(name, a, b, *, n0=0, n=None, a0=0, bias=None, add=None, add_scale=1.0, out_dtype=F32,
           tm=1024, tn=512, tk=None, side=None):
    m = a.shape[0]
    if b.ndim == 3:
        k, cc = b.shape[1:]
        n = 4 * cc
        tn = _tile(cc, tn, LANES)
        per = cc // tn
        tk = _tile(k, k if tk is None else tk, LANES)
        b_spec = pl.BlockSpec((None, tk, tn), lambda i, j, kk: (j // per, kk, j % per))
    else:
        k = b.shape[0]
        n = b.shape[1] - n0 if n is None else n
        tn = _tile(math.gcd(n, n0) if n0 else n, tn, LANES)
        tk = _tile(math.gcd(k, a0) if a0 else k, k if tk is None else tk, LANES)
        b_spec = pl.BlockSpec((tk, tn), lambda i, j, kk: (kk, n0 // tn + j))
    tm = _tile(m, tm, SUBLANES_BF16)
    jn0, ka0 = n0 // tn, a0 // tk
    o_spec = pl.BlockSpec((tm, tn), lambda i, j, kk: (i, j))
    return _mm(name, "nn", (m // tm, n // tn, k // tk),
               a, pl.BlockSpec((tm, tk), lambda i, j, kk: (i, ka0 + kk)), b, b_spec,
               jax.ShapeDtypeStruct((m, n), out_dtype), o_spec, (tm, tn),
               bias=bias, bias_spec=pl.BlockSpec((1, tn), lambda i, j, kk: (0, jn0 + j)),
               add=add, add_spec=o_spec, add_scale=add_scale, side=side)


def _mm_nt(name, a, b, *, n0=0, add=None, add_scale=1.0, out_dtype=F32, tm=512, tn=2048, tk=512, side=None):
    m, n = a.shape
    if b.ndim == 3:
        k, cc = b.shape[1:]
        tn = _tile(k, tn, LANES)
        tk = _tile(cc, tk, LANES)
        per = cc // tk
        b_spec = pl.BlockSpec((None, tn, tk), lambda i, j, kk: (kk // per, j, kk % per))
    else:
        k = b.shape[0]
        tn = _tile(k, tn, LANES)
        tk = _tile(math.gcd(n, n0) if n0 else n, tk, LANES)
        b_spec = pl.BlockSpec((tn, tk), lambda i, j, kk: (j, n0 // tk + kk))
    tm = _tile(m, tm, SUBLANES_BF16)
    o_spec = pl.BlockSpec((tm, tn), lambda i, j, kk: (i, j))
    return _mm(name, "nt", (m // tm, k // tn, n // tk),
               a, pl.BlockSpec((tm, tk), lambda i, j, kk: (i, kk)), b, b_spec,
               jax.ShapeDtypeStruct((m, k), out_dtype), o_spec, (tm, tn),
               add=add, add_spec=o_spec, add_scale=add_scale, side=side)


def _mm_tn(name, a, b, *, a0=0, ka=None, out_dtype=BF16, tm=2048, tn=1024, tk=1024, col_shards=False, side=None):
    s = a.shape[0]
    ka = a.shape[1] - a0 if ka is None else ka
    n = b.shape[1]
    tm = _tile(math.gcd(ka, a0) if a0 else ka, tm, LANES)
    tk = _tile(s, tk, SUBLANES_BF16)
    ia0 = a0 // tm
    if col_shards:
        cc = n // 4
        tn = _tile(cc, tn, LANES)
        per = cc // tn
        out_sds = jax.ShapeDtypeStruct((4, ka, cc), out_dtype)
        o_spec = pl.BlockSpec((None, tm, tn), lambda i, j, kk: (j // per, i, j % per))
    else:
        tn = _tile(n, tn, LANES)
        out_sds = jax.ShapeDtypeStruct((ka, n), out_dtype)
        o_spec = pl.BlockSpec((tm, tn), lambda i, j, kk: (i, j))
    return _mm(name, "tn", (ka // tm, n // tn, s // tk),
               a, pl.BlockSpec((tk, tm), lambda i, j, kk: (kk, ia0 + i)),
               b, pl.BlockSpec((tk, tn), lambda i, j, kk: (kk, j)), out_sds, o_spec, (tm, tn), side=side)


def R(arr, bw=None, c0=0, j=False):
    return ("r", arr, arr.shape[1] if bw is None else bw, c0, j)


def P(arr, bw=None, c0=0, j=False):
    return ("p", arr, arr.shape[1] if bw is None else bw, c0, j)


def _ew(name, fn, rows, ins, outs, accs=(), *, tr=256, nj=1):
    tr = _tile(rows, tr, SUBLANES_BF16)
    ni = rows // tr
    n_in, n_out, n_acc = len(ins), len(outs), len(accs)

    def spec(kind, bw, c0, follows):
        rows_b = tr if kind == "r" else 1
        if kind == "r":
            return pl.BlockSpec((rows_b, bw), (lambda j, i: (i, c0 + j)) if follows else (lambda j, i: (i, c0)))
        return pl.BlockSpec((rows_b, bw), (lambda j, i: (0, c0 + j)) if follows else (lambda j, i: (0, c0)))

    def body(*refs):
        res = fn(*[r[...] for r in refs[:n_in]])
        res = tuple(res) if isinstance(res, (tuple, list)) else (res,)
        for r, v in zip(refs[n_in:n_in + n_out], res[:n_out]):
            r[...] = v.astype(r.dtype)
        if n_acc:
            i = pl.program_id(1)
            for r, v in zip(refs[n_in + n_out:], res[n_out:]):
                @pl.when(i == 0)
                def _(r=r, v=v):
                    r[...] = v

                @pl.when(i > 0)
                def _(r=r, v=v):
                    r[...] += v

    out_shape = [jax.ShapeDtypeStruct((rows, bw * nj), dt) for bw, dt in outs]
    out_shape += [jax.ShapeDtypeStruct((1, bw * nj), F32) for bw in accs]
    out_specs = [pl.BlockSpec((tr, bw), lambda j, i: (i, j)) for bw, _ in outs]
    out_specs += [pl.BlockSpec((1, bw), lambda j, i: (0, j)) for bw in accs]
    res = pl.pallas_call(body, out_shape=out_shape, grid=(nj, ni),
                         in_specs=[spec(k, bw, c0, f) for k, _, bw, c0, f in ins], out_specs=out_specs,
                         compiler_params=_cparams(2), name=name)(*[a for _, a, _, _, _ in ins])
    return res if len(res) > 1 else res[0]


def _gelu(x):
    c = math.sqrt(2.0 / math.pi)
    return 0.5 * x * (1.0 + jnp.tanh(c * (x + 0.044715 * (x * x * x))))


def _sigmoid(x):
    return lax.logistic(x)


def _f_ln(h, g, b):
    mu = jnp.mean(h, axis=-1, keepdims=True)
    xc = h - mu
    var = jnp.mean(xc * xc, axis=-1, keepdims=True)
    return xc * lax.rsqrt(var + LN_EPS) * g + b


def _f_combine(o1, o2, o3, l1, l2, l3):
    m = jnp.maximum(jnp.maximum(l1, l2), l3)
    e1, e2, e3 = jnp.exp(l1 - m), jnp.exp(l2 - m), jnp.exp(l3 - m)
    den = e1 + e2 + e3
    return (e1 * o1 + e2 * o2 + e3 * o3) / den, m + jnp.log(den)


def _f_ssm_out(y_lin, uc, d_skip):
    return _gelu(y_lin + d_skip * uc)


def _f_glu(yc0, t):
    return yc0 * _sigmoid(t)


def _f_merge(g0, g1, g2, pa, pb, pc):
    return _sigmoid(g0) * pa + _sigmoid(g1) * pb + _sigmoid(g2) * pc


def _f_swiglu(gf, up):
    return gf * _sigmoid(gf) * up


def _f_sgu_pre(zu, zv, g, b):
    return _gelu(zu), _f_ln(_gelu(zv), g, b)


def _f_ssm_params(lr, li, ld, lr_rep, li_rep, ld_rep, br_t, bi_t):
    def disc(lr, li, ld):
        dt = jnp.exp(ld)
        mag = jnp.exp(lr * dt)
        th = li * dt
        abr, abi = mag * jnp.cos(th), mag * jnp.sin(th)
        nrm = lr * lr + li * li
        cr = ((abr - 1.0) * lr + abi * li) / nrm
        ci = (abi * lr - (abr - 1.0) * li) / nrm
        return abr, abi, cr, ci

    abr, abi, _, _ = disc(lr, li, ld)
    _, _, cr, ci = disc(lr_rep, li_rep, ld_rep)
    return abr, abi, cr * br_t - ci * bi_t, cr * bi_t + ci * br_t


def _rowsum(x):
    return jnp.sum(x, axis=0, keepdims=True)


def _to_streams(qkv):
    s = qkv.shape[0]
    t = qkv.reshape(s, 3, N_GROUPS_A, HEADS, HEAD_DIM)
    outs = []
    for g, (_, dil) in enumerate(ATT_PATTERNS):
        tg = t[:, :, g].reshape(s // dil, dil, 3, HEADS, HEAD_DIM)
        outs.append(tg.transpose(2, 3, 1, 0, 4).reshape(3, HEADS, s, HEAD_DIM))
    return jnp.stack(outs, 0)


def _tok_to_streams(a):
    s = a.shape[0]
    outs = []
    for _, dil in ATT_PATTERNS:
        t = a.reshape(s // dil, dil, HEADS, HEAD_DIM)
        outs.append(t.transpose(2, 1, 0, 3).reshape(HEADS, s, HEAD_DIM))
    return jnp.stack(outs, 0)


def _streams_to_tok(o):
    s = o.shape[2]
    outs = []
    for g, (_, dil) in enumerate(ATT_PATTERNS):
        t = o[g].reshape(HEADS, dil, s // dil, HEAD_DIM)
        outs.append(t.transpose(2, 1, 0, 3).reshape(s, WIDTH_A))
    return outs


def _t5_bucket(dist):
    max_exact = N_REL_BUCKETS // 2
    d = np.maximum(dist, 1).astype(np.float32)
    scale = (N_REL_BUCKETS - max_exact) / math.log(REL_MAX_DIST / max_exact)
    large = max_exact + (np.log(d / max_exact) * scale).astype(np.int32)
    large = np.minimum(large, N_REL_BUCKETS - 1)
    return np.where(dist < max_exact, dist, large).astype(np.int32)


def _bucket_maps():
    i = np.arange(ATT_BLOCK)[:, None]
    kk = np.arange(2 * ATT_BLOCK)[None, :]
    steps = np.maximum(ATT_BLOCK + i - kk, 0)
    return np.stack([_t5_bucket(steps * dil) for _, dil in ATT_PATTERNS], 0)


def _band_bias(rel_bias):
    q = ATT_BLOCK

    def body(rel_ref, m_ref, o_ref):
        g = pl.program_id(0)
        bm = m_ref[...]
        for h in range(HEADS):
            acc = jnp.zeros((q, 2 * q), F32)
            for bk in range(N_REL_BUCKETS):
                acc = jnp.where(bm == bk, rel_ref[bk, g * HEADS + h], acc)
            o_ref[h] = acc

    return pl.pallas_call(body, out_shape=jax.ShapeDtypeStruct((N_GROUPS_A, HEADS, q, 2 * q), F32), grid=(N_GROUPS_A,),
                          in_specs=[pl.BlockSpec(memory_space=pltpu.SMEM),
                                    pl.BlockSpec((None, q, 2 * q), lambda g: (g, 0, 0))],
                          out_specs=pl.BlockSpec((None, HEADS, q, 2 * q), lambda g: (g, 0, 0, 0)),
                          compiler_params=_cparams(1), name="rel_bias_band")(rel_bias, jnp.asarray(_bucket_maps()))


def _attn_masks(nbs):
    q = ATT_BLOCK
    g, b = pl.program_id(0), pl.program_id(1)
    nb = jnp.where(g == 0, nbs[0], jnp.where(g == 1, nbs[1], nbs[2]))
    shift = jnp.where(lax.rem(b, nb) != 0, 0, q)
    ii = lax.broadcasted_iota(jnp.int32, (q, q), 0)
    kk = lax.broadcasted_iota(jnp.int32, (q, q), 1)
    return kk >= ii + shift, kk <= ii


def _attn_logits(q, kp, kc, bias_h, mask_p, mask_c):
    scale = HEAD_DIM ** -0.5
    sp = lax.dot_general(q, kp, _DIMS["nt"], preferred_element_type=F32) * scale + bias_h[:, :ATT_BLOCK]
    sc = lax.dot_general(q, kc, _DIMS["nt"], preferred_element_type=F32) * scale + bias_h[:, ATT_BLOCK:]
    return jnp.where(mask_p, sp, NEG_INF), jnp.where(mask_c, sc, NEG_INF)


def _attn_specs(s):
    q, h, e = ATT_BLOCK, HEADS, HEAD_DIM
    blk = (None, None, h, q, e)
    prev = lambda b: jnp.maximum(b - 1, 0)
    qkv_specs = [pl.BlockSpec(blk, lambda g, b: (g, 0, 0, b, 0)),
                 pl.BlockSpec(blk, lambda g, b: (g, 1, 0, prev(b), 0)),
                 pl.BlockSpec(blk, lambda g, b: (g, 1, 0, b, 0)),
                 pl.BlockSpec(blk, lambda g, b: (g, 2, 0, prev(b), 0)),
                 pl.BlockSpec(blk, lambda g, b: (g, 2, 0, b, 0))]
    bias_spec = pl.BlockSpec((None, h, q, 2 * q), lambda g, b: (g, 0, 0, 0))
    row_spec = pl.BlockSpec((None, h, q, e), lambda g, b: (g, 0, b, 0))
    return qkv_specs, bias_spec, row_spec


def _attn_fwd(qkv_s, bias, side=None):
    s = qkv_s.shape[3]
    nblk = s // ATT_BLOCK
    nbs = tuple(s // dil // ATT_BLOCK for _, dil in ATT_PATTERNS)
    qkv_specs, bias_spec, row_spec = _attn_specs(s)
    n_side_in = len(side["ins"]) if side else 0
    n_side_out = len(side["out_sds"]) if side else 0

    def body(q_ref, kp_ref, kc_ref, vp_ref, vc_ref, b_ref, *rest):
        side_ins, (o_ref, l_ref) = rest[:n_side_in], rest[n_side_in:n_side_in + 2]
        side_outs = rest[n_side_in + 2:n_side_in + 2 + n_side_out]
        side_sems = rest[n_side_in + 2 + n_side_out:]
        if side:
            @pl.when((pl.program_id(0) == 0) & (pl.program_id(1) == 0))
            def _():
                side["start"](side_ins, side_outs, side_sems)

        mask_p, mask_c = _attn_masks(nbs)
        for h in range(HEADS):
            sp, sc = _attn_logits(q_ref[h], kp_ref[h], kc_ref[h], b_ref[h], mask_p, mask_c)
            m = jnp.maximum(jnp.max(sp, axis=1, keepdims=True), jnp.max(sc, axis=1, keepdims=True))
            pp, pc = jnp.exp(sp - m), jnp.exp(sc - m)
            den = jnp.sum(pp, axis=1, keepdims=True) + jnp.sum(pc, axis=1, keepdims=True)
            o = (lax.dot_general(pp.astype(BF16), vp_ref[h], _DIMS["nn"], preferred_element_type=F32)
                 + lax.dot_general(pc.astype(BF16), vc_ref[h], _DIMS["nn"], preferred_element_type=F32))
            o_ref[h] = o / den
            l_ref[h] = jnp.broadcast_to(m + jnp.log(den), (ATT_BLOCK, HEAD_DIM))

        if side:
            @pl.when((pl.program_id(0) == N_GROUPS_A - 1) & (pl.program_id(1) == nblk - 1))
            def _():
                side["finish"](side_ins, side_outs, side_sems)

    sds = jax.ShapeDtypeStruct((N_GROUPS_A, HEADS, s, HEAD_DIM), F32)
    if not side:
        return pl.pallas_call(body, out_shape=(sds, sds), grid=(N_GROUPS_A, nblk),
                              in_specs=qkv_specs + [bias_spec], out_specs=(row_spec, row_spec),
                              compiler_params=_cparams(2), name="attn_fwd")(qkv_s, qkv_s, qkv_s, qkv_s, qkv_s, bias), None
    res = pl.pallas_call(body, out_shape=[sds, sds] + list(side["out_sds"]), grid=(N_GROUPS_A, nblk),
                         in_specs=qkv_specs + [bias_spec] + [ANY] * n_side_in,
                         out_specs=[row_spec, row_spec] + [ANY] * n_side_out, scratch_shapes=list(side["sems"]),
                         compiler_params=_cparams(2), name="attn_fwd_" + side["name"])(
        qkv_s, qkv_s, qkv_s, qkv_s, qkv_s, bias, *side["ins"])
    return (res[0], res[1]), res[2:]


def _attn_bwd(qkv_s, bias, do_s, ya_s, lse_s):
    s = qkv_s.shape[3]
    nblk = s // ATT_BLOCK
    nbs = tuple(s // dil // ATT_BLOCK for _, dil in ATT_PATTERNS)
    scale = HEAD_DIM ** -0.5
    q, h, e = ATT_BLOCK, HEADS, HEAD_DIM

    def body(q_ref, kp_ref, kc_ref, vp_ref, vc_ref, b_ref, do_ref, ya_ref, l_ref,
             dq_ref, dk_ref, dv_ref, db_ref, dk_own, dv_own):
        b = pl.program_id(1)
        mask_p, mask_c = _attn_masks(nbs)

        @pl.when(b == 0)
        def _():
            db_ref[...] = jnp.zeros_like(db_ref)
            dk_own[...] = jnp.zeros_like(dk_own)
            dv_own[...] = jnp.zeros_like(dv_own)

        @pl.when(b < nblk)
        def _():
            for hh in range(h):
                qh, kp, kc, vp, vc = q_ref[hh], kp_ref[hh], kc_ref[hh], vp_ref[hh], vc_ref[hh]
                sp, sc = _attn_logits(qh, kp, kc, b_ref[hh], mask_p, mask_c)
                lse = l_ref[hh][:, 0:1]
                pp, pc = jnp.exp(sp - lse), jnp.exp(sc - lse)
                do = do_ref[hh]
                dsum = jnp.sum(do * ya_ref[hh], axis=1, keepdims=True)
                dob = do.astype(BF16)
                dsp = pp * (lax.dot_general(dob, vp, _DIMS["nt"], preferred_element_type=F32) - dsum)
                dsc = pc * (lax.dot_general(dob, vc, _DIMS["nt"], preferred_element_type=F32) - dsum)
                db_ref[hh, :, :q] += dsp
                db_ref[hh, :, q:] += dsc
                dspb, dscb = dsp.astype(BF16), dsc.astype(BF16)
                dq_ref[hh] = (scale * (lax.dot_general(dspb, kp, _DIMS["nn"], preferred_element_type=F32)
                                       + lax.dot_general(dscb, kc, _DIMS["nn"], preferred_element_type=F32))).astype(dq_ref.dtype)
                dkp = scale * lax.dot_general(dspb, qh, _DIMS["tn"], preferred_element_type=F32)
                dvp = lax.dot_general(pp.astype(BF16), dob, _DIMS["tn"], preferred_element_type=F32)

                dk_ref[hh] = (dk_own[hh] + dkp).astype(dk_ref.dtype)
                dv_ref[hh] = (dv_own[hh] + dvp).astype(dv_ref.dtype)
                dk_own[hh] = scale * lax.dot_general(dscb, qh, _DIMS["tn"], preferred_element_type=F32)
                dv_own[hh] = lax.dot_general(pc.astype(BF16), dob, _DIMS["tn"], preferred_element_type=F32)

        @pl.when(b == nblk)
        def _():
            dk_ref[...] = dk_own[...].astype(dk_ref.dtype)
            dv_ref[...] = dv_own[...].astype(dv_ref.dtype)

    blk5 = (None, None, h, q, e)
    cur = lambda b: jnp.minimum(b, nblk - 1)
    prev = lambda b: jnp.maximum(cur(b) - 1, 0)
    qkv_specs = [pl.BlockSpec(blk5, lambda g, b: (g, 0, 0, cur(b), 0)),
                 pl.BlockSpec(blk5, lambda g, b: (g, 1, 0, prev(b), 0)),
                 pl.BlockSpec(blk5, lambda g, b: (g, 1, 0, cur(b), 0)),
                 pl.BlockSpec(blk5, lambda g, b: (g, 2, 0, prev(b), 0)),
                 pl.BlockSpec(blk5, lambda g, b: (g, 2, 0, cur(b), 0))]
    bias_spec = pl.BlockSpec((None, h, q, 2 * q), lambda g, b: (g, 0, 0, 0))
    row_spec = pl.BlockSpec((None, h, q, e), lambda g, b: (g, 0, cur(b), 0))
    lag_spec = pl.BlockSpec((None, h, q, e), lambda g, b: (g, 0, jnp.maximum(b - 1, 0), 0))
    sds = jax.ShapeDtypeStruct((N_GROUPS_A, h, s, e), BF16)
    own = pltpu.VMEM((h, q, e), F32)
    return pl.pallas_call(body, out_shape=(sds,) * 3 + (jax.ShapeDtypeStruct(bias.shape, F32),),
                          grid=(N_GROUPS_A, nblk + 1),
                          in_specs=qkv_specs + [bias_spec, row_spec, row_spec, row_spec],
                          out_specs=(row_spec, lag_spec, lag_spec, bias_spec), scratch_shapes=[own, own],
                          compiler_params=_cparams(2), name="attn_bwd")(
        qkv_s, qkv_s, qkv_s, qkv_s, qkv_s, bias, do_s, ya_s, lse_s)


def _bias_to_buckets(dbias):
    bmap = jnp.asarray(_bucket_maps())
    q = ATT_BLOCK

    def body(db_ref, m_ref, o_ref):
        lane = lax.broadcasted_iota(jnp.int32, (HEADS, LANES), 1)
        row = lax.broadcasted_iota(jnp.int32, (HEADS, LANES), 0)
        acc = jnp.zeros((HEADS, LANES), F32)
        bm = m_ref[...]
        for h in range(HEADS):
            dbh = db_ref[h]
            for bk in range(N_REL_BUCKETS):
                sv = jnp.sum(jnp.sum(jnp.where(bm == bk, dbh, 0.0), axis=1, keepdims=True), axis=0, keepdims=True)
                acc = acc + jnp.where((lane == bk) & (row == h), sv, 0.0)
        o_ref[...] = acc

    out = pl.pallas_call(body, out_shape=jax.ShapeDtypeStruct((N_GROUPS_A, HEADS, LANES), F32), grid=(N_GROUPS_A,),
                         in_specs=[pl.BlockSpec((None, HEADS, q, 2 * q), lambda g: (g, 0, 0, 0)),
                                   pl.BlockSpec((None, q, 2 * q), lambda g: (g, 0, 0))],
                         out_specs=pl.BlockSpec((None, HEADS, LANES), lambda g: (g, 0, 0)),
                         compiler_params=_cparams(1), name="rel_bias_grad")(dbias, bmap)
    return out[:, :, :N_REL_BUCKETS].reshape(N_GROUPS_A * HEADS, N_REL_BUCKETS).T


def _sgu_specs():
    c, w = CHUNK, WIDTH_B
    return [pl.BlockSpec((c, w), lambda i: (i, 0)), pl.BlockSpec((c, w), lambda i: (i, 1)),
            pl.BlockSpec((1, w), lambda i: (0, 0)), pl.BlockSpec((1, w), lambda i: (0, 0)),
            pl.BlockSpec((N_GROUPS_B, c, c), lambda i: (0, 0, 0)), pl.BlockSpec((N_GROUPS_B, c, 1), lambda i: (0, 0, 0))]


def _sgu_fwd(rest, ln_g, ln_b, w_tril, b_col):
    s = rest.shape[0]

    def body(zu_ref, zv_ref, g_ref, b_ref, w_ref, bs_ref, y_ref):
        u, vn = _f_sgu_pre(zu_ref[...], zv_ref[...], g_ref[...], b_ref[...])
        for gi in range(N_GROUPS_B):
            sl = slice(gi * GROUP_B, (gi + 1) * GROUP_B)
            mixed = lax.dot_general(w_ref[gi], vn[:, sl].astype(BF16), _DIMS["nn"], preferred_element_type=F32)
            y_ref[:, sl] = (u[:, sl] * (mixed + bs_ref[gi])).astype(y_ref.dtype)

    return pl.pallas_call(body, out_shape=jax.ShapeDtypeStruct((s, WIDTH_B), BF16), grid=(s // CHUNK,),
                          in_specs=_sgu_specs(), out_specs=pl.BlockSpec((CHUNK, WIDTH_B), lambda i: (i, 0)),
                          compiler_params=_cparams(1), name="sgu_fwd")(rest, rest, ln_g, ln_b, w_tril, b_col)


def _sgu_bwd(rest, ln_g, ln_b, w_tril, b_col, dyb):
    s = rest.shape[0]
    c, w, ng = CHUNK, WIDTH_B, N_GROUPS_B

    def body(zu_ref, zv_ref, g_ref, b_ref, w_ref, bs_ref, dy_ref, dzu_ref, dzv_ref, dw_ref, dbs_ref, dg_ref, db_ref):
        (u, vn), vjp = jax.vjp(_f_sgu_pre, zu_ref[...], zv_ref[...], g_ref[...], b_ref[...])
        first = pl.program_id(0) == 0
        du, dvn = [], []
        for gi in range(ng):
            sl = slice(gi * GROUP_B, (gi + 1) * GROUP_B)
            vg = vn[:, sl].astype(BF16)
            mixed = lax.dot_general(w_ref[gi], vg, _DIMS["nn"], preferred_element_type=F32) + bs_ref[gi]
            dy = dy_ref[:, sl]
            dmix = dy * u[:, sl]
            du.append(dy * mixed)
            dmb = dmix.astype(BF16)
            dvn.append(lax.dot_general(w_ref[gi], dmb, _DIMS["tn"], preferred_element_type=F32))
            dwg = lax.dot_general(dmb, vg, _DIMS["nt"], preferred_element_type=F32)
            dbg = jnp.sum(dmix, axis=1, keepdims=True)

            @pl.when(first)
            def _(gi=gi, dwg=dwg, dbg=dbg):
                dw_ref[gi] = dwg
                dbs_ref[gi] = dbg

            @pl.when(jnp.logical_not(first))
            def _(gi=gi, dwg=dwg, dbg=dbg):
                dw_ref[gi] += dwg
                dbs_ref[gi] += dbg

        dzu, dzv, dg, db = vjp((jnp.concatenate(du, axis=1), jnp.concatenate(dvn, axis=1)))
        dzu_ref[...] = dzu.astype(dzu_ref.dtype)
        dzv_ref[...] = dzv.astype(dzv_ref.dtype)

        @pl.when(first)
        def _():
            dg_ref[...] = dg
            db_ref[...] = db

        @pl.when(jnp.logical_not(first))
        def _():
            dg_ref[...] += dg
            db_ref[...] += db

    row = pl.BlockSpec((c, w), lambda i: (i, 0))
    par = pl.BlockSpec((1, w), lambda i: (0, 0))
    return pl.pallas_call(
        body, grid=(s // c,),
        out_shape=(jax.ShapeDtypeStruct((s, w), BF16), jax.ShapeDtypeStruct((s, w), BF16),
                   jax.ShapeDtypeStruct((ng, c, c), F32), jax.ShapeDtypeStruct((ng, c, 1), F32),
                   jax.ShapeDtypeStruct((1, w), F32), jax.ShapeDtypeStruct((1, w), F32)),
        in_specs=_sgu_specs() + [row],
        out_specs=(row, row, pl.BlockSpec((ng, c, c), lambda i: (0, 0, 0)), pl.BlockSpec((ng, c, 1), lambda i: (0, 0, 0)),
                   par, par),
        compiler_params=_cparams(1), name="sgu_bwd")(rest, rest, ln_g, ln_b, w_tril, b_col, dyb)


SCAN_T = 128


def _to_time_major(dst, src_ref, lead):
    for r in range(SSM_CH // LANES):
        dst[:, r, :] = src_ref[lead, :, r * LANES:(r + 1) * LANES]


def _from_time_major(dst_ref, lead, src):
    for r in range(SSM_CH // LANES):
        dst_ref[lead, :, r * LANES:(r + 1) * LANES] = src[:, r, :]


def _scan_fwd(bu, a3):
    s = bu.shape[1]
    t_blk = _tile(s, SCAN_T, 8)
    rows = SSM_CH // LANES

    def body(bu_ref, a_ref, x_ref, b3r, b3i, x3r, x3i, carry):
        @pl.when(pl.program_id(0) == 0)
        def _():
            carry[...] = jnp.zeros_like(carry)

        _to_time_major(b3r, bu_ref, 0)
        _to_time_major(b3i, bu_ref, 1)
        ar, ai = a_ref[0], a_ref[1]

        def step(t, c):
            xr, xi = c
            nr = ar * xr - ai * xi + b3r[t]
            ni = ar * xi + ai * xr + b3i[t]
            x3r[t] = nr
            x3i[t] = ni
            return nr, ni

        xr, xi = lax.fori_loop(0, t_blk, step, (carry[0], carry[1]), unroll=8)
        carry[0] = xr
        carry[1] = xi
        _from_time_major(x_ref, 0, x3r)
        _from_time_major(x_ref, 1, x3i)

    blk = pl.BlockSpec((2, t_blk, SSM_CH), lambda i: (0, i, 0))
    tm = pltpu.VMEM((t_blk, rows, LANES), F32)
    return pl.pallas_call(body, out_shape=jax.ShapeDtypeStruct(bu.shape, F32), grid=(s // t_blk,),
                          in_specs=[blk, pl.BlockSpec((2, rows, LANES), lambda i: (0, 0, 0))], out_specs=blk,
                          scratch_shapes=[tm, tm, tm, tm, pltpu.VMEM((2, rows, LANES), F32)],
                          compiler_params=_cparams(1), name="ssm_scan_fwd")(bu, a3)


def _scan_bwd(dx, x, a3):
    s = dx.shape[1]
    t_blk = _tile(s, SCAN_T, 8)
    nb = s // t_blk
    rows = SSM_CH // LANES

    def body(dx_ref, x_ref, a_ref, g_ref, da_ref, d3r, d3i, x3r, x3i, g3r, g3i, carry):
        first = pl.program_id(0) == 0

        @pl.when(first)
        def _():
            carry[...] = jnp.zeros_like(carry)

        _to_time_major(d3r, dx_ref, 0)
        _to_time_major(d3i, dx_ref, 1)
        _to_time_major(x3r, x_ref, 0)
        _to_time_major(x3i, x_ref, 1)
        ar, ai = a_ref[0], a_ref[1]

        def step(k, c):
            t = t_blk - 1 - k
            gr, gi, dar, dai = c
            xr, xi = x3r[t], x3i[t]
            dar = dar + gr * xr + gi * xi
            dai = dai + gi * xr - gr * xi
            ngr = d3r[t] + ar * gr + ai * gi
            ngi = d3i[t] + ar * gi - ai * gr
            g3r[t] = ngr
            g3i[t] = ngi
            return ngr, ngi, dar, dai

        zero = jnp.zeros((rows, LANES), F32)
        gr, gi, dar, dai = lax.fori_loop(0, t_blk, step, (carry[0], carry[1], zero, zero), unroll=8)
        carry[0] = gr
        carry[1] = gi

        @pl.when(first)
        def _():
            da_ref[0] = dar
            da_ref[1] = dai

        @pl.when(jnp.logical_not(first))
        def _():
            da_ref[0] += dar
            da_ref[1] += dai

        _from_time_major(g_ref, 0, g3r)
        _from_time_major(g_ref, 1, g3i)

    blk = pl.BlockSpec((2, t_blk, SSM_CH), lambda i: (0, nb - 1 - i, 0))
    par = pl.BlockSpec((2, rows, LANES), lambda i: (0, 0, 0))
    tm = pltpu.VMEM((t_blk, rows, LANES), F32)
    return pl.pallas_call(body, out_shape=(jax.ShapeDtypeStruct(dx.shape, F32), jax.ShapeDtypeStruct((2, rows, LANES), F32)),
                          grid=(nb,), in_specs=[blk, blk, par], out_specs=(blk, par),
                          scratch_shapes=[tm] * 6 + [pltpu.VMEM((2, rows, LANES), F32)],
                          compiler_params=_cparams(1), name="ssm_scan_bwd")(dx, x, a3)


SSM_TILES = WIDTH_C // LANES
SSM_TILE_W = SSM_CH // SSM_TILES


def _block_diag(m):
    gpt = N_GROUPS_C // SSM_TILES
    t = m.reshape(SSM_TILES, gpt, SSM_GROUP, SSM_STATE)
    eye = jnp.eye(gpt, dtype=m.dtype)
    return (t[:, :, :, None, :] * eye[None, :, None, :, None]).reshape(SSM_TILES, LANES, SSM_TILE_W)


def _block_diag_extract(t):
    gpt = N_GROUPS_C // SSM_TILES
    t = t.reshape(SSM_TILES, gpt, SSM_GROUP, gpt, SSM_STATE)
    eye = jnp.eye(gpt, dtype=t.dtype)
    return jnp.sum(t * eye[None, :, None, :, None], axis=3).reshape(WIDTH_C, SSM_STATE)


def _ssm_in(uc_src, uc_col0, bd):
    s = uc_src.shape[0]
    tm = _tile(s, 2048, 8)
    j0 = uc_col0 // LANES
    return _mm("ssm_in", "nn", (s // tm, 2 * SSM_TILES, 1),
               uc_src, pl.BlockSpec((tm, LANES), lambda i, j, k: (i, j0 + j % SSM_TILES)),
               bd, pl.BlockSpec((None, None, LANES, SSM_TILE_W), lambda i, j, k: (j // SSM_TILES, j % SSM_TILES, 0, 0)),
               jax.ShapeDtypeStruct((2, s, SSM_CH), F32),
               pl.BlockSpec((None, tm, SSM_TILE_W), lambda i, j, k: (j // SSM_TILES, i, j % SSM_TILES)), None)


def _ssm_in_dgrad(g, bd):
    s = g.shape[1]
    tm = _tile(s, 2048, 8)
    return _mm("ssm_in_dgrad", "nt", (s // tm, SSM_TILES, 2),
               g, pl.BlockSpec((None, tm, SSM_TILE_W), lambda i, j, k: (k, i, j)),
               bd, pl.BlockSpec((None, None, LANES, SSM_TILE_W), lambda i, j, k: (k, j, 0, 0)),
               jax.ShapeDtypeStruct((s, WIDTH_C), F32), pl.BlockSpec((tm, LANES), lambda i, j, k: (i, j)),
               (tm, LANES))


def _ssm_in_wgrad(uc_src, uc_col0, g):
    s = g.shape[1]
    tk = _tile(s, 2048, 8)
    j0 = uc_col0 // LANES
    return _mm("ssm_in_wgrad", "tn", (2, SSM_TILES, s // tk),
               uc_src, pl.BlockSpec((tk, LANES), lambda i, j, k: (k, j0 + j)),
               g, pl.BlockSpec((None, tk, SSM_TILE_W), lambda i, j, k: (i, k, j)),
               jax.ShapeDtypeStruct((2, SSM_TILES, LANES, SSM_TILE_W), F32),
               pl.BlockSpec((None, None, LANES, SSM_TILE_W), lambda i, j, k: (i, j, 0, 0)), (LANES, SSM_TILE_W))


def _ssm_out(x, cd):
    s = x.shape[1]
    tm = _tile(s, 2048, 8)
    return _mm("ssm_out", "nn", (s // tm, SSM_TILES, 2),
               x, pl.BlockSpec((None, tm, SSM_TILE_W), lambda i, j, k: (k, i, j)),
               cd, pl.BlockSpec((None, None, SSM_TILE_W, LANES), lambda i, j, k: (k, j, 0, 0)),
               jax.ShapeDtypeStruct((s, WIDTH_C), F32), pl.BlockSpec((tm, LANES), lambda i, j, k: (i, j)),
               (tm, LANES))


def _ssm_out_dgrad(dy, cd):
    s = dy.shape[0]
    tm = _tile(s, 2048, 8)
    return _mm("ssm_out_dgrad", "nt", (s // tm, 2 * SSM_TILES, 1),
               dy, pl.BlockSpec((tm, LANES), lambda i, j, k: (i, j % SSM_TILES)),
               cd, pl.BlockSpec((None, None, SSM_TILE_W, LANES), lambda i, j, k: (j // SSM_TILES, j % SSM_TILES, 0, 0)),
               jax.ShapeDtypeStruct((2, s, SSM_CH), F32),
               pl.BlockSpec((None, tm, SSM_TILE_W), lambda i, j, k: (j // SSM_TILES, i, j % SSM_TILES)), None)


def _ssm_out_wgrad(x, dy):
    s = dy.shape[0]
    tk = _tile(s, 2048, 8)
    return _mm("ssm_out_wgrad", "tn", (2, SSM_TILES, s // tk),
               x, pl.BlockSpec((None, tk, SSM_TILE_W), lambda i, j, k: (i, k, j)),
               dy, pl.BlockSpec((tk, LANES), lambda i, j, k: (k, j)),
               jax.ShapeDtypeStruct((2, SSM_TILES, SSM_TILE_W, LANES), F32),
               pl.BlockSpec((None, None, SSM_TILE_W, LANES), lambda i, j, k: (i, j, 0, 0)), (SSM_TILE_W, LANES))


def _ssm_param_inputs(w, l):
    rep = lambda a: jnp.repeat(a, SSM_GROUP, axis=0)
    lr, li, ld = w["lam_re"][l], w["lam_im"][l], w["log_dt"][l][:, None]
    br_t = jnp.transpose(w["b_re"][l], (0, 2, 1)).reshape(WIDTH_C, SSM_STATE)
    bi_t = jnp.transpose(w["b_im"][l], (0, 2, 1)).reshape(WIDTH_C, SSM_STATE)
    return lr, li, ld, rep(lr), rep(li), rep(ld), br_t, bi_t


def _ssm_params_fwd(pin):
    def body(*refs):
        res = _f_ssm_params(*[r[...] for r in refs[:8]])
        for r, v in zip(refs[8:], res):
            r[...] = v

    g, p = N_GROUPS_C, SSM_STATE
    return pl.pallas_call(body, out_shape=(jax.ShapeDtypeStruct((g, p), F32),) * 2
                          + (jax.ShapeDtypeStruct((WIDTH_C, p), F32),) * 2, name="ssm_params_fwd")(*pin)


def _ssm_params_bwd(pin, d_abr, d_abi, d_bbr, d_bbi):
    g, p = N_GROUPS_C, SSM_STATE
    group_sum = jnp.asarray(np.kron(np.eye(g, dtype=np.float32), np.ones((1, SSM_GROUP), np.float32)))

    def body(*refs):
        ins = [r[...] for r in refs[:8]]
        cts = tuple(r[...] for r in refs[8:12])
        gs = refs[12][...]
        d_lr_ref, d_li_ref, d_ld_ref, d_br_ref, d_bi_ref = refs[13:]
        _, vjp = jax.vjp(_f_ssm_params, *ins)
        d = vjp(cts)
        fold = lambda v: lax.dot_general(gs, v, _DIMS["nn"], preferred_element_type=F32, precision=lax.Precision.HIGHEST)
        d_lr_ref[...] = d[0] + fold(d[3])
        d_li_ref[...] = d[1] + fold(d[4])
        d_ld_ref[...] = d[2] + fold(jnp.broadcast_to(d[5], (WIDTH_C, p)))[:, 0:1]
        d_br_ref[...] = d[6]
        d_bi_ref[...] = d[7]

    return pl.pallas_call(body, out_shape=(jax.ShapeDtypeStruct((g, p), F32), jax.ShapeDtypeStruct((g, p), F32),
                                           jax.ShapeDtypeStruct((g, 1), F32), jax.ShapeDtypeStruct((WIDTH_C, p), F32),
                                           jax.ShapeDtypeStruct((WIDTH_C, p), F32)), name="ssm_params_bwd")(
        *pin, d_abr, d_abi, d_bbr, d_bbi, group_sum)


def _c_block_diag(w, l):
    def one(c):
        return jnp.transpose(_block_diag(c.reshape(WIDTH_C, SSM_STATE)), (0, 2, 1))
    return jnp.stack([one(w["c_re"][l]), -one(w["c_im"][l])], 0)


ANY = pl.BlockSpec(memory_space=pl.ANY)


def _ag8(name, xb):
    def body(x_ref, out_ref, send_sems, recv_sems, local_sem):
        x, y, c = lax.axis_index("x"), lax.axis_index("y"), lax.axis_index("c")
        me, sibling = (x, y, c), (x, y, 1 - c)
        chips = [(1 - x, y), (x, 1 - y), (1 - x, 1 - y)]

        def rows(px, py, pc):
            return out_ref.at[4 * px + 2 * py + pc]

        def copy(k, block, to, src=None):
            return pltpu.make_async_remote_copy(src_ref=rows(*block) if src is None else src, dst_ref=rows(*block),
                                                send_sem=send_sems.at[k], recv_sem=recv_sems.at[k],
                                                device_id=to, device_id_type=MESH)

        mine = pltpu.make_async_copy(x_ref, rows(*me), local_sem)
        mine.start()
        first = [copy(0, me, sibling, src=x_ref)]
        first += [copy(1 + j, me, (*chip, c), src=x_ref) for j, chip in enumerate(chips)]
        for cp in first:
            cp.start()
        passed = [copy(4 + j, (*chip, c), sibling) for j, chip in enumerate(chips)]
        for j, chip in enumerate(chips):
            copy(1 + j, (*chip, c), me).wait_recv()
            passed[j].start()
        copy(0, sibling, me).wait_recv()
        for j, chip in enumerate(chips):
            copy(4 + j, (*chip, 1 - c), me).wait_recv()
        for cp in first + passed:
            cp.wait_send()
        mine.wait()

    return pl.pallas_call(body, out_shape=jax.ShapeDtypeStruct((8,) + xb.shape, xb.dtype), in_specs=[ANY], out_specs=ANY,
                          scratch_shapes=[pltpu.SemaphoreType.DMA((7,)), pltpu.SemaphoreType.DMA((7,)),
                                          pltpu.SemaphoreType.DMA(())], name=name)(xb)


def _mesh_place():
    x, y, c = lax.axis_index("x"), lax.axis_index("y"), lax.axis_index("c")
    return x, y, c, [(1 - x, y), (x, 1 - y), (1 - x, 1 - y)]


ICI_CHUNK_BYTES = 2 << 20
D2D_CHUNK_BYTES = 1 << 20


def _split_rows(rows, row_bytes, chunk_bytes):
    k = 1
    for cand in range(1, max(1, (rows * row_bytes) // chunk_bytes) + 1):
        if rows % cand == 0 and (rows // cand) % SUBLANES_BF16 == 0:
            k = cand
    return [(j * (rows // k), rows // k) for j in range(k)]


def _run_side(side):
    ni, no = len(side["ins"]), len(side["out_sds"])

    def body(*refs):
        ins, outs, sems = refs[:ni], refs[ni:ni + no], refs[ni + no:]
        side["start"](ins, outs, sems)
        side["finish"](ins, outs, sems)

    return pl.pallas_call(body, out_shape=list(side["out_sds"]), in_specs=[ANY] * ni, out_specs=[ANY] * no,
                          scratch_shapes=list(side["sems"]), name=side["name"])(*side["ins"])


def _gather_side(wb, l):
    nw = len(wb)
    halves = [a.shape[1] // 2 for a in wb]
    pieces = [_split_rows(h, a.shape[2] * a.dtype.itemsize, ICI_CHUNK_BYTES) for a, h in zip(wb, halves)]

    def rows(ref, i, hc, r0=0, n=None):
        return ref.at[pl.ds(hc * halves[i] + r0, halves[i] if n is None else n)]

    def copier(sems):
        def copy(i, k, src, dst, to):
            return pltpu.make_async_remote_copy(src_ref=src, dst_ref=dst, send_sem=sems[0].at[6 * i + k],
                                                recv_sem=sems[1].at[6 * i + k], device_id=to, device_id_type=MESH)
        return copy

    def start(ins, outs, sems):
        x, y, c, chips = _mesh_place()
        me = 2 * x + y
        copy = copier(sems)
        for i in range(nw):
            for hc in range(2):
                for r0, n in pieces[i]:
                    pltpu.make_async_copy(rows(ins[i].at[l], i, hc, r0, n), rows(outs[i].at[me], i, hc, r0, n),
                                          sems[2].at[i]).start()
        for i in range(nw):
            for k, (px, py) in enumerate(chips):
                for r0, n in pieces[i]:
                    copy(i, k, rows(ins[i].at[l], i, c, r0, n), rows(outs[i].at[me], i, c, r0, n), (px, py, c)).start()

    def finish(ins, outs, sems):
        x, y, c, chips = _mesh_place()
        me = 2 * x + y
        copy = copier(sems)
        for k, (px, py) in enumerate(chips):
            for i in range(nw):
                src = outs[i].at[2 * px + py]
                copy(i, k, rows(src, i, c), rows(src, i, c), (px, py, c)).wait_recv()
                for r0, n in pieces[i]:
                    copy(i, 3 + k, rows(src, i, c, r0, n), rows(src, i, c, r0, n), (x, y, 1 - c)).start()
        for k, (px, py) in enumerate(chips):
            for i in range(nw):
                other = rows(outs[i].at[2 * px + py], i, 1 - c)
                copy(i, 3 + k, other, other, (x, y, 1 - c)).wait_recv()
        for i in range(nw):
            whole = rows(outs[i].at[me], i, c)
            for k in range(6):
                copy(i, k, whole, whole, (x, y, 1 - c)).wait_send()
            pltpu.make_async_copy(ins[i].at[l], outs[i].at[me], sems[2].at[i]).wait()

    return dict(name="gather_weights", ins=list(wb), out_sds=[jax.ShapeDtypeStruct((4,) + a.shape[1:], a.dtype) for a in wb],
                sems=[pltpu.SemaphoreType.DMA((6 * nw,)), pltpu.SemaphoreType.DMA((6 * nw,)), pltpu.SemaphoreType.DMA((nw,))],
                start=start, finish=finish)


def _grad_core_swap(g4):
    nw = len(g4)
    halves = [a.shape[1] // 2 for a in g4]
    pieces = [_split_rows(h, a.shape[2] * a.dtype.itemsize, D2D_CHUNK_BYTES) for a, h in zip(g4, halves)]

    def body(*refs):
        ins, theirs = refs[:nw], refs[nw:2 * nw]
        send_sems, recv_sems = refs[2 * nw:]
        x, y, c, _ = _mesh_place()

        def copy(i, src, dst):
            return pltpu.make_async_remote_copy(src_ref=src, dst_ref=dst, send_sem=send_sems.at[i], recv_sem=recv_sems.at[i],
                                                device_id=(x, y, 1 - c), device_id_type=MESH)

        for i in range(nw):
            for j in range(4):
                for r0, n in pieces[i]:
                    copy(i, ins[i].at[j, pl.ds((1 - c) * halves[i] + r0, n)], theirs[i].at[j, pl.ds(r0, n)]).start()
        for i in range(nw):
            copy(i, ins[i].at[:, pl.ds((1 - c) * halves[i], halves[i])], theirs[i]).wait()

    sds = [jax.ShapeDtypeStruct((4, h) + a.shape[2:], a.dtype) for a, h in zip(g4, halves)]
    return pl.pallas_call(body, out_shape=sds, in_specs=[ANY] * nw, out_specs=[ANY] * nw,
                          scratch_shapes=[pltpu.SemaphoreType.DMA((nw,)), pltpu.SemaphoreType.DMA((nw,))],
                          name="grad_core_swap")(*g4)


def _exchange_side(t4):
    nw = len(t4)
    pieces = [_split_rows(a.shape[1], a.shape[2] * a.dtype.itemsize, ICI_CHUNK_BYTES) for a in t4]

    def copier(sems, c):
        def copy(i, k, src, dst, px, py):
            return pltpu.make_async_remote_copy(src_ref=src, dst_ref=dst, send_sem=sems[0].at[3 * i + k],
                                                recv_sem=sems[1].at[3 * i + k], device_id=(px, py, c),
                                                device_id_type=MESH)
        return copy

    def start(ins, outs, sems):
        x, y, c, chips = _mesh_place()
        me = 2 * x + y
        copy = copier(sems, c)
        for i in range(nw):
            for r0, n in pieces[i]:
                for k, (px, py) in enumerate(chips):
                    copy(i, k, ins[i].at[2 * px + py, pl.ds(r0, n)], outs[i].at[me, pl.ds(r0, n)], px, py).start()
                pltpu.make_async_copy(ins[i].at[me, pl.ds(r0, n)], outs[i].at[me, pl.ds(r0, n)], sems[2].at[i]).start()

    def finish(ins, outs, sems):
        x, y, c, chips = _mesh_place()
        me = 2 * x + y
        copy = copier(sems, c)
        for i in range(nw):
            for k, (px, py) in enumerate(chips):
                copy(i, k, ins[i].at[me], outs[i].at[2 * px + py], px, py).wait_recv()
        for i in range(nw):
            for k, (px, py) in enumerate(chips):
                copy(i, k, ins[i].at[me], outs[i].at[me], px, py).wait_send()
            pltpu.make_async_copy(ins[i].at[me], outs[i].at[me], sems[2].at[i]).wait()

    return dict(name="grad_chip_exchange", ins=list(t4), out_sds=[jax.ShapeDtypeStruct(a.shape, a.dtype) for a in t4],
                sems=[pltpu.SemaphoreType.DMA((3 * nw,)), pltpu.SemaphoreType.DMA((3 * nw,)), pltpu.SemaphoreType.DMA((nw,))],
                start=start, finish=finish)


def _grad_half_swap(full, l):
    nw = len(full)
    halves = [a.shape[1] // 2 for a in full]
    pieces = [_split_rows(h, a.shape[2] * a.dtype.itemsize, D2D_CHUNK_BYTES) for a, h in zip(full, halves)]

    def body(*refs):
        bufs = refs[nw:2 * nw]
        send_sems, recv_sems = refs[2 * nw:]
        x, y, c, _ = _mesh_place()

        def copy(i, hc, r0, n):
            view = bufs[i].at[l, pl.ds(hc * halves[i] + r0, n)]
            return pltpu.make_async_remote_copy(src_ref=view, dst_ref=view, send_sem=send_sems.at[i], recv_sem=recv_sems.at[i],
                                                device_id=(x, y, 1 - c), device_id_type=MESH)

        for i in range(nw):
            for r0, n in pieces[i]:
                copy(i, c, r0, n).start()
        for i in range(nw):
            copy(i, c, 0, halves[i]).wait_send()
            copy(i, 1 - c, 0, halves[i]).wait_recv()

    return pl.pallas_call(body, out_shape=[jax.ShapeDtypeStruct(a.shape, a.dtype) for a in full],
                          in_specs=[ANY] * nw, out_specs=[ANY] * nw, input_output_aliases={i: i for i in range(nw)},
                          scratch_shapes=[pltpu.SemaphoreType.DMA((nw,)), pltpu.SemaphoreType.DMA((nw,))],
                          name="grad_half_swap")(*full)


def _core_add(g, theirs, core):
    _, rh, cc = theirs.shape
    tr = _tile(rh, max(SUBLANES_BF16, (1 << 19) // cc), SUBLANES_BF16)
    nt = rh // tr

    def body(c_ref, a_ref, b_ref, o_ref):
        o_ref[...] = (a_ref[...].astype(F32) + b_ref[...].astype(F32)).astype(o_ref.dtype)

    blk = (None, tr, cc)
    return pl.pallas_call(
        body, out_shape=jax.ShapeDtypeStruct(theirs.shape, theirs.dtype),
        grid_spec=pltpu.PrefetchScalarGridSpec(
            num_scalar_prefetch=1, grid=(4, nt),
            in_specs=[pl.BlockSpec(blk, lambda j, t, c_ref: (j, c_ref[0] * nt + t, 0)),
                      pl.BlockSpec(blk, lambda j, t, c_ref: (j, t, 0))],
            out_specs=pl.BlockSpec(blk, lambda j, t, c_ref: (j, t, 0))),
        compiler_params=_cparams(2), name="grad_core_add")(core, g, theirs)


def _sum_to_half(xb, core, l, depth, stacked=None):
    n, rh, cc = xb.shape
    tr = _tile(rh, max(SUBLANES_BF16, (1 << 18) // cc), SUBLANES_BF16)
    nt = rh // tr

    def body(c_ref, x_ref, *rest):
        acc = x_ref[0].astype(F32)
        for k in range(1, n):
            acc = acc + x_ref[k].astype(F32)
        rest[-1][...] = acc

    ins = (core, xb) if stacked is None else (core, xb, stacked)
    return pl.pallas_call(
        body, out_shape=jax.ShapeDtypeStruct((depth, 2 * rh, cc), F32),
        grid_spec=pltpu.PrefetchScalarGridSpec(
            num_scalar_prefetch=1, grid=(nt,),
            in_specs=[pl.BlockSpec((n, tr, cc), lambda t, c_ref: (0, t, 0))] + ([] if stacked is None else [ANY]),
            out_specs=pl.BlockSpec((None, tr, cc), lambda t, c_ref: (l, c_ref[0] * nt + t, 0))),
        input_output_aliases={} if stacked is None else {2: 0},
        compiler_params=_cparams(1), name="grad_chip_sum")(*ins)


def _sum_lead(name, xb, out_dtype=F32):
    n, rows, w = xb.shape
    tr = _tile(rows, max(SUBLANES_BF16, (1 << 18) // w), SUBLANES_BF16)

    def body(x_ref, o_ref):
        acc = x_ref[0].astype(F32)
        for k in range(1, n):
            acc = acc + x_ref[k].astype(F32)
        o_ref[...] = acc.astype(o_ref.dtype)

    return pl.pallas_call(body, out_shape=jax.ShapeDtypeStruct((rows, w), out_dtype), grid=(rows // tr,),
                          in_specs=[pl.BlockSpec((n, tr, w), lambda i: (0, i, 0))],
                          out_specs=pl.BlockSpec((tr, w), lambda i: (i, 0)), compiler_params=_cparams(1), name=name)(xb)


def _pad_to(v, mult):
    n = v.shape[-1]
    pad = (-n) % mult
    return v if pad == 0 else jnp.pad(v, [(0, 0)] * (v.ndim - 1) + [(0, pad)])


RELAYOUT_ROWS = 256


def _cols_from_shards(a):
    _, k, cc = a.shape
    tr = _tile(k, RELAYOUT_ROWS, SUBLANES_BF16)

    def body(i_ref, o_ref):
        for j in range(4):
            o_ref[:, j * cc:(j + 1) * cc] = i_ref[j]

    return pl.pallas_call(body, out_shape=jax.ShapeDtypeStruct((k, 4 * cc), a.dtype), grid=(k // tr,),
                          in_specs=[pl.BlockSpec((4, tr, cc), lambda i: (0, i, 0))],
                          out_specs=pl.BlockSpec((tr, 4 * cc), lambda i: (i, 0)),
                          compiler_params=_cparams(1), name="cols_from_shards")(a)


def _cols_to_shards(a):
    k, n = a.shape
    cc = n // 4
    tr = _tile(k, RELAYOUT_ROWS, SUBLANES_BF16)

    def body(i_ref, o_ref):
        for j in range(4):
            o_ref[j] = i_ref[:, j * cc:(j + 1) * cc]

    return pl.pallas_call(body, out_shape=jax.ShapeDtypeStruct((4, k, cc), a.dtype), grid=(k // tr,),
                          in_specs=[pl.BlockSpec((tr, n), lambda i: (i, 0))],
                          out_specs=pl.BlockSpec((4, tr, cc), lambda i: (0, i, 0)),
                          compiler_params=_cparams(1), name="cols_to_shards")(a)


COMM_GROUPS = (("w_in",), ("w_ffn_in",), ("w_glu", "w_pa", "w_pb", "w_pc", "w_o", "w_ffn_out"))


def _gather_sides(wb, l):
    return [_gather_side([wb[n] for n in grp], l) for grp in COMM_GROUPS]


def _first_weights(wb):
    sides = _gather_sides(wb, 0)
    return _next_weights(wb, 0, [_run_side(sides[0]), None, None]), sides


def _next_weights(wb, l, got):
    kinds = dict(BIG)
    out = {}
    for grp, arrs in zip(COMM_GROUPS, got):
        for n, a in zip(grp, arrs if arrs is not None else ()):
            if kinds[n] == "row":
                out[n] = a.reshape(4 * a.shape[1], a.shape[2])
            else:
                out[n] = _cols_from_shards(a) if n == "w_in" else a
    return out


def _start_reduce(g):
    g4 = []
    for n, kind in BIG:
        a = g[n]
        if kind == "row":
            a = a.reshape(4, a.shape[0] // 4, a.shape[1])
        elif a.ndim == 2:
            a = _cols_to_shards(a)
        g4.append(a)
    core = lax.axis_index("c").astype(jnp.int32).reshape(1)
    theirs = _grad_core_swap(g4)
    return {n: _core_add(a, b, core) for n, a, b in zip(BIG_NAMES, g4, theirs)}


def _exchange_sides(pending):
    return [_exchange_side([pending[n] for n in grp]) for grp in COMM_GROUPS]


def _run_exchange(pending):
    return [_run_side(s) for s in _exchange_sides(pending)]


def _finish_reduce(pending, arrived, l, depth, stacked):
    core = lax.axis_index("c").astype(jnp.int32).reshape(1)
    names = [n for grp in COMM_GROUPS for n in grp]
    arrs = [a for grp in arrived for a in grp]
    halves = [_sum_to_half(a, core, l, depth, None if stacked is None else stacked[n]) for n, a in zip(names, arrs)]
    return dict(zip(names, _grad_half_swap(halves, l)))


def _adam_fn(w, g, m, v):
    m = ADAM_B1 * m + (1.0 - ADAM_B1) * g
    v = ADAM_B2 * v + (1.0 - ADAM_B2) * (g * g)
    m_hat = m / (1.0 - ADAM_B1 ** ADAM_STEP)
    v_hat = v / (1.0 - ADAM_B2 ** ADAM_STEP)
    return -ADAM_LR * (m_hat / (jnp.sqrt(v_hat) + ADAM_EPS) + ADAM_WD * w), m, v


def _adamw(name, w, g, m, v):
    shape = w.shape
    cols = shape[-1]
    f = lambda a: a.reshape(-1, cols)
    rows = f(w).shape[0]
    tr = max(8, (1 << 19) // cols)
    d, nm, nv = _ew(name, _adam_fn, rows, [R(f(w)), R(f(g)), R(f(m)), R(f(v))], [(cols, F32)] * 3, tr=tr)
    return d.reshape(shape), nm.reshape(shape), nv.reshape(shape)


def _beside(sides, k, call):
    if sides is None:
        return call(side=None), None
    return call(side=sides[k])


def _layer_fwd(x, wl, bias, alpha, sides=None, own=None, wb=None):
    s, d = x.shape
    b_in = wl["b_in"][None, :]
    got = [None] * len(COMM_GROUPS)
    mine = [None] * len(COMM_GROUPS)
    qkv, mine[1] = _beside(own, 1, functools.partial(_mm_nn, "proj_qkv", x, wl["w_in"], n0=0, n=3 * QKV_W, bias=b_in,
                                                     out_dtype=BF16, tn=768))
    rest, mine[2] = _beside(own, 2, functools.partial(_mm_nn, "proj_rest", x, wl["w_in"], n0=3 * QKV_W, bias=b_in, tn=768))
    if own is not None:
        wl.update(_next_weights(wb, 0, mine))
    d_ff = wl["w_ffn_out"].shape[0]
    qkv_s = _to_streams(qkv)
    (o_s, l_s), got[1] = _attn_fwd(qkv_s, bias, side=sides[1] if sides else None)
    o_t, l_t = _streams_to_tok(o_s), _streams_to_tok(l_s)
    ya, lse = _ew("attn_combine", _f_combine, s, [R(a) for a in o_t + l_t], [(WIDTH_A, F32), (WIDTH_A, F32)])
    yb = _sgu_fwd(rest, wl["sgu_ln_g"][None], wl["sgu_ln_b"][None], wl["w_tril"], wl["b_col"])
    bu = _ssm_in(rest, REST_UC, wl["bd"])
    xs = _scan_fwd(bu, wl["a3"])
    y_lin = _ssm_out(xs, wl["cd"])
    uc_blk = REST_UC // WIDTH_C
    yc0 = _ew("ssm_skip_gelu", _f_ssm_out, s, [R(y_lin), R(rest, WIDTH_C, uc_blk), P(wl["d_skip"][None])], [(WIDTH_C, F32)])
    t_glu = _mm_nn("glu_proj", yc0, wl["w_glu"], bias=wl["b_glu"][None], tn=768)
    yc = _ew("glu", _f_glu, s, [R(yc0), R(t_glu)], [(WIDTH_C, BF16)])
    pa = _mm_nn("proj_a", ya, wl["w_pa"])
    pb = _mm_nn("proj_b", yb, wl["w_pb"])
    pc = _mm_nn("proj_c", yc, wl["w_pc"])
    gw = _tile(math.gcd(d, REST_GL), 256, LANES)
    gl_ins = [R(rest, gw, (REST_GL + i * d) // gw, True) for i in range(N_BRANCH)]
    merged = _ew("merge", _f_merge, s, gl_ins + [R(pa, gw, 0, True), R(pb, gw, 0, True), R(pc, gw, 0, True)],
                 [(gw, BF16)], nj=d // gw, tr=1024)
    h1 = _mm_nn("proj_o", merged, wl["w_o"], add=x, add_scale=alpha)
    x1 = _ew("ln1", _f_ln, s, [R(h1), P(wl["ln1_g"][None]), P(wl["ln1_b"][None])], [(d, F32)])
    ff, got[0] = _beside(sides, 0, functools.partial(_mm_nn, "ffn_in", x1, wl["w_ffn_in"], tn=1408))
    fw = _tile(d_ff, 512, LANES)
    act = _ew("swiglu", _f_swiglu, s, [R(ff, fw, 0, True), R(ff, fw, d_ff // fw, True)], [(fw, BF16)], nj=d_ff // fw, tr=1024)
    h2, got[2] = _beside(sides, 2, functools.partial(_mm_nn, "ffn_out", act, wl["w_ffn_out"], add=x1, add_scale=alpha))
    x2 = _ew("ln2", _f_ln, s, [R(h2), P(wl["ln2_g"][None]), P(wl["ln2_b"][None])], [(d, F32)])
    saved = dict(x=x, qkv_s=qkv_s, rest=rest, ya=ya, lse=lse, yb=yb, xs=xs, y_lin=y_lin, yc0=yc0, t_glu=t_glu, yc=yc,
                 pa=pa, pb=pb, pc=pc, merged=merged, h1=h1, x1=x1, ff=ff, act=act, h2=h2)
    return x2, saved, got


def _vjp_rows(f, n_primal):
    def fn(*args):
        n_ct = len(args) - n_primal
        cts, primals = args[:n_ct], args[n_ct:]
        out, vjp = jax.vjp(f, *primals)
        ct = tuple(c.astype(F32) for c in cts)
        return vjp(ct if isinstance(out, (tuple, list)) else ct[0])
    return fn


def _layer_bwd(dx2, sv, wl, bias, alpha, sides=None):
    s, d = dx2.shape
    arrived = [None] * len(COMM_GROUPS)
    d_ff = wl["w_ffn_out"].shape[0]
    g = {}
    row = lambda a: a[None]
    dh2, g["ln2_g"], g["ln2_b"] = _ew("ln2_bwd", _vjp_rows(_f_ln, 3), s,
                                      [R(dx2), R(sv["h2"]), P(row(wl["ln2_g"])), P(row(wl["ln2_b"]))],
                                      [(d, F32)], accs=[d, d])
    g["w_ffn_out"] = _mm_tn("ffn_out_wgrad", sv["act"], dh2)
    dact = _mm_nt("ffn_out_dgrad", dh2, wl["w_ffn_out"], tn=1408, tk=2048)
    fw = _tile(d_ff, 512, LANES)
    nf = d_ff // fw
    dgf, dup = _ew("swiglu_bwd", _vjp_rows(_f_swiglu, 2), s,
                   [R(dact, fw, 0, True), R(sv["ff"], fw, 0, True), R(sv["ff"], fw, nf, True)],
                   [(fw, BF16), (fw, BF16)], nj=nf, tr=1024)
    dff = jnp.concatenate([dgf, dup], axis=1)
    g["w_ffn_in"], arrived[0] = _beside(sides, 0, functools.partial(_mm_tn, "ffn_in_wgrad", sv["x1"], dff, tn=1408,
                                                                    col_shards=True))
    dx1, arrived[1] = _beside(sides, 1, functools.partial(_mm_nt, "ffn_in_dgrad", dff, wl["w_ffn_in"], add=dh2,
                                                          add_scale=alpha, tk=1408))
    dh1, g["ln1_g"], g["ln1_b"] = _ew("ln1_bwd", _vjp_rows(_f_ln, 3), s,
                                      [R(dx1), R(sv["h1"]), P(row(wl["ln1_g"])), P(row(wl["ln1_b"]))],
                                      [(d, F32)], accs=[d, d])
    g["w_o"] = _mm_tn("proj_o_wgrad", sv["merged"], dh1)
    dmerged = _mm_nt("proj_o_dgrad", dh1, wl["w_o"])
    rest = sv["rest"]
    gw = _tile(math.gcd(d, REST_GL), 256, LANES)
    gl_ins = [R(rest, gw, (REST_GL + i * d) // gw, True) for i in range(N_BRANCH)]
    dg0, dg1, dg2, dpa, dpb, dpc = _ew(
        "merge_bwd", _vjp_rows(_f_merge, 6), s,
        [R(dmerged, gw, 0, True)] + gl_ins + [R(sv[k], gw, 0, True) for k in ("pa", "pb", "pc")],
        [(gw, BF16)] * 6, nj=d // gw, tr=1024)
    g["w_pa"] = _mm_tn("proj_a_wgrad", sv["ya"], dpa, col_shards=True)
    g["w_pb"] = _mm_tn("proj_b_wgrad", sv["yb"], dpb, col_shards=True)
    g["w_pc"] = _mm_tn("proj_c_wgrad", sv["yc"], dpc, col_shards=True)
    dya = _mm_nt("proj_a_dgrad", dpa, wl["w_pa"], tn=512)
    dyb = _mm_nt("proj_b_dgrad", dpb, wl["w_pb"], tn=768)
    dyc = _mm_nt("proj_c_dgrad", dpc, wl["w_pc"], tn=768)
    dyc0_a, dt_glu, g["b_glu"] = _ew("glu_bwd", lambda ct, a, t: (lambda r: (r[0], r[1], _rowsum(r[1])))(_vjp_rows(_f_glu, 2)(ct, a, t)),
                                     s, [R(dyc), R(sv["yc0"]), R(sv["t_glu"])], [(WIDTH_C, F32), (WIDTH_C, BF16)], accs=[WIDTH_C])
    g["w_glu"] = _mm_tn("glu_wgrad", sv["yc0"], dt_glu)
    dyc0 = _mm_nt("glu_dgrad", dt_glu, wl["w_glu"], add=dyc0_a, tn=768)
    uc_blk = REST_UC // WIDTH_C
    dy_lin, duc_skip, g["d_skip"] = _ew("ssm_skip_gelu_bwd", _vjp_rows(_f_ssm_out, 3), s,
                                        [R(dyc0), R(sv["y_lin"]), R(rest, WIDTH_C, uc_blk), P(row(wl["d_skip"]))],
                                        [(WIDTH_C, F32), (WIDTH_C, F32)], accs=[WIDTH_C])
    d_cd = _ssm_out_wgrad(sv["xs"], dy_lin)
    dxs = _ssm_out_dgrad(dy_lin, wl["cd"])
    gs, da = _scan_bwd(dxs, sv["xs"], wl["a3"])
    d_bd = _ssm_in_wgrad(rest, REST_UC, gs)
    duc_lin = _ssm_in_dgrad(gs, wl["bd"])
    duc = _ew("ssm_duc", lambda a, b: a + b, s, [R(duc_lin), R(duc_skip)], [(WIDTH_C, BF16)])
    d_abr = da[0].reshape(N_GROUPS_C, SSM_STATE)
    d_abi = da[1].reshape(N_GROUPS_C, SSM_STATE)
    d_lr, d_li, d_ld, d_br_t, d_bi_t = _ssm_params_bwd(wl["ssm_pin"], d_abr, d_abi,
                                                       _block_diag_extract(d_bd[0]), _block_diag_extract(d_bd[1]))
    g["lam_re"], g["lam_im"], g["log_dt"] = d_lr, d_li, d_ld[:, 0]
    un_t = lambda a: jnp.transpose(a.reshape(N_GROUPS_C, SSM_GROUP, SSM_STATE), (0, 2, 1))
    g["b_re"], g["b_im"] = un_t(d_br_t), un_t(d_bi_t)
    cd_ex = lambda a: _block_diag_extract(jnp.transpose(a, (0, 2, 1))).reshape(N_GROUPS_C, SSM_GROUP, SSM_STATE)
    g["c_re"], g["c_im"] = cd_ex(d_cd[0]), -cd_ex(d_cd[1])
    dzu, dzv, dws, dbs, g["sgu_ln_g"], g["sgu_ln_b"] = _sgu_bwd(rest, row(wl["sgu_ln_g"]), row(wl["sgu_ln_b"]),
                                                                  wl["w_tril"], wl["b_col"], dyb)
    g["w_s"] = jnp.tril(dws)
    g["b_s"] = dbs[:, :, 0]
    do_s, ya_s, lse_s = _tok_to_streams(dya), _tok_to_streams(sv["ya"]), _tok_to_streams(sv["lse"])
    dq, dk, dv, dbias = _attn_bwd(sv["qkv_s"], bias, do_s, ya_s, lse_s)
    dqkv = jnp.stack([jnp.stack(_streams_to_tok(t), 1) for t in (dq, dk, dv)], 1)
    dproj = jnp.concatenate([dqkv.reshape(s, 3 * QKV_W).astype(BF16), dzu, dzv, duc, dg0, dg1, dg2], axis=1)
    n_in = dproj.shape[1]
    cw = _tile(n_in, 1024, LANES)
    g["b_in"] = _ew("b_in_grad", lambda a: _rowsum(a.astype(F32)), s, [R(dproj, cw, 0, True)], [], accs=[cw],
                    nj=n_in // cw, tr=512)
    g["w_in"], arrived[2] = _beside(sides, 2, functools.partial(_mm_tn, "proj_in_wgrad", sv["x"], dproj, tn=768))
    dx = _mm_nt("proj_in_dgrad", dproj, wl["w_in"], add=dh1, add_scale=alpha, tk=768)
    for k in ("ln2_g", "ln2_b", "ln1_g", "ln1_b", "b_glu", "d_skip", "sgu_ln_g", "sgu_ln_b", "b_in"):
        g[k] = g[k][0]
    return dx, g, dbias, arrived


def _loss_head(y, target):
    s, d = y.shape

    def fn(yb, tb):
        err = yb - tb
        return err / d, _rowsum(err * err)

    dy, sq = _ew("loss_head", fn, s, [R(y), R(target)], [(d, F32)], accs=[d])
    return dy, 0.5 * jnp.sum(sq) / d


def _step(x, target, w, m, v):
    depth = w["w_in"].shape[0]
    alpha = (2 * depth) ** 0.25
    bias = _band_bias(w["rel_bias"])
    wb = {n: w[n].astype(BF16) for n in BIG_NAMES}
    xl = x[0]
    layers, saved = [], []
    big, own = _first_weights(wb)
    for l in range(depth):
        wl = dict(big)
        for n in SMALL:
            if n != "rel_bias":
                wl[n] = w[n][l]
        wl["w_tril"] = jnp.tril(w["w_s"][l]).astype(BF16)
        wl["b_col"] = w["b_s"][l][:, :, None]
        pin = _ssm_param_inputs(w, l)
        abr, abi, bbr, bbi = _ssm_params_fwd(pin)
        wl["ssm_pin"] = pin
        wl["a3"] = jnp.stack([abr.reshape(-1, LANES), abi.reshape(-1, LANES)], 0)
        wl["bd"] = jnp.stack([_block_diag(bbr), _block_diag(bbi)], 0)
        wl["cd"] = _c_block_diag(w, l)
        xl, sv, got = _layer_fwd(xl, wl, bias, alpha, _gather_sides(wb, l + 1) if l + 1 < depth else None,
                                 own if l == 0 else None, wb)
        if l + 1 < depth:
            big = _next_weights(wb, l + 1, got)
        layers.append(wl)
        saved.append(sv)
    dx, loss = _loss_head(xl, target[0])
    loss = lax.psum(loss, ("x", "y", "c"))
    grads = {n: [None] * depth for n in SMALL if n != "rel_bias"}
    dbias_sum = None
    pending = None
    big_grads = None
    for l in reversed(range(depth)):
        dx, g, dbias, arrived = _layer_bwd(dx, saved[l], layers[l], bias, alpha,
                                           _exchange_sides(pending) if pending is not None else None)
        dbias_sum = dbias if dbias_sum is None else dbias_sum + dbias
        if pending is not None:
            big_grads = _finish_reduce(pending, arrived, l + 1, depth, big_grads)
        pending = _start_reduce({n: g[n] for n in BIG_NAMES})
        for n in SMALL:
            if n != "rel_bias":
                grads[n][l] = g[n]
    big_grads = _finish_reduce(pending, _run_exchange(pending), 0, depth, big_grads)
    grads = {n: jnp.stack(gl, 0) for n, gl in grads.items()}
    grads.update(big_grads)
    grads["rel_bias"] = _bias_to_buckets(dbias_sum)
    small_vec = _pad_to(jnp.concatenate([grads[n].reshape(-1) for n in SMALL]), 8 * LANES).reshape(-1, LANES)
    small_sum = _sum_lead("small_grad_sum", _ag8("small_grad_gather", small_vec)).reshape(-1)
    off = 0
    for n in SMALL:
        size = math.prod(w[n].shape)
        grads[n] = small_sum[off:off + size].reshape(w[n].shape)
        off += size
    pack = lambda t: _pad_to(jnp.concatenate([t[n].reshape(-1) for n in SMALL]), 8 * LANES).reshape(-1, LANES)
    sd, sm, sv_ = _adamw("adamw_small", pack(w), small_sum.reshape(-1, LANES), pack(m), pack(v))
    delta, new_m, new_v = {}, {}, {}
    off = 0
    for n in SMALL:
        size = math.prod(w[n].shape)
        take = lambda t: t.reshape(-1)[off:off + size].reshape(w[n].shape)
        delta[n], new_m[n], new_v[n] = take(sd), take(sm), take(sv_)
        off += size
    for n in BIG_NAMES:
        delta[n], new_m[n], new_v[n] = _adamw("adamw_" + n, w[n], grads[n], m[n], v[n])
    return loss, dx[None], grads, delta, new_m, new_v


def kernel(x, w_in, b_in, rel_bias, sgu_ln_g, sgu_ln_b, w_s, b_s, lam_re, lam_im, log_dt, b_re, b_im, c_re, c_im, d_skip, w_glu, b_glu, w_pa, w_pb, w_pc, w_o, ln1_g, ln1_b, w_ffn_in, w_ffn_out, ln2_g, ln2_b, loss_target, m_w_in, m_b_in, m_rel_bias, m_sgu_ln_g, m_sgu_ln_b, m_w_s, m_b_s, m_lam_re, m_lam_im, m_log_dt, m_b_re, m_b_im, m_c_re, m_c_im, m_d_skip, m_w_glu, m_b_glu, m_w_pa, m_w_pb, m_w_pc, m_w_o, m_ln1_g, m_ln1_b, m_w_ffn_in, m_w_ffn_out, m_ln2_g, m_ln2_b, v_w_in, v_b_in, v_rel_bias, v_sgu_ln_g, v_sgu_ln_b, v_w_s, v_b_s, v_lam_re, v_lam_im, v_log_dt, v_b_re, v_b_im, v_c_re, v_c_im, v_d_skip, v_w_glu, v_b_glu, v_w_pa, v_w_pb, v_w_pc, v_w_o, v_ln1_g, v_ln1_b, v_w_ffn_in, v_w_ffn_out, v_ln2_g, v_ln2_b):
    w = dict(w_in=w_in, b_in=b_in, rel_bias=rel_bias, sgu_ln_g=sgu_ln_g, sgu_ln_b=sgu_ln_b, w_s=w_s, b_s=b_s,
             lam_re=lam_re, lam_im=lam_im, log_dt=log_dt, b_re=b_re, b_im=b_im, c_re=c_re, c_im=c_im, d_skip=d_skip,
             w_glu=w_glu, b_glu=b_glu, w_pa=w_pa, w_pb=w_pb, w_pc=w_pc, w_o=w_o, ln1_g=ln1_g, ln1_b=ln1_b,
             w_ffn_in=w_ffn_in, w_ffn_out=w_ffn_out, ln2_g=ln2_g, ln2_b=ln2_b)
    m = dict(w_in=m_w_in, b_in=m_b_in, rel_bias=m_rel_bias, sgu_ln_g=m_sgu_ln_g, sgu_ln_b=m_sgu_ln_b, w_s=m_w_s,
             b_s=m_b_s, lam_re=m_lam_re, lam_im=m_lam_im, log_dt=m_log_dt, b_re=m_b_re, b_im=m_b_im, c_re=m_c_re,
             c_im=m_c_im, d_skip=m_d_skip, w_glu=m_w_glu, b_glu=m_b_glu, w_pa=m_w_pa, w_pb=m_w_pb, w_pc=m_w_pc,
             w_o=m_w_o, ln1_g=m_ln1_g, ln1_b=m_ln1_b, w_ffn_in=m_w_ffn_in, w_ffn_out=m_w_ffn_out, ln2_g=m_ln2_g,
             ln2_b=m_ln2_b)
    v = dict(w_in=v_w_in, b_in=v_b_in, rel_bias=v_rel_bias, sgu_ln_g=v_sgu_ln_g, sgu_ln_b=v_sgu_ln_b, w_s=v_w_s,
             b_s=v_b_s, lam_re=v_lam_re, lam_im=v_lam_im, log_dt=v_log_dt, b_re=v_b_re, b_im=v_b_im, c_re=v_c_re,
             c_im=v_c_im, d_skip=v_d_skip, w_glu=v_w_glu, b_glu=v_b_glu, w_pa=v_w_pa, w_pb=v_w_pb, w_pc=v_w_pc,
             w_o=v_w_o, ln1_g=v_ln1_g, ln1_b=v_ln1_b, w_ffn_in=v_w_ffn_in, w_ffn_out=v_w_ffn_out, ln2_g=v_ln2_g,
             ln2_b=v_ln2_b)
    loss, grad_x, grads, delta, new_m, new_v = _step(x, loss_target, w, m, v)
    return (loss, grad_x, *[grads[n] for n in WEIGHTS], *[delta[n] for n in WEIGHTS],
            *[new_m[n] for n in WEIGHTS], *[new_v[n] for n in WEIGHTS])
```

```python
import functools
import math

import numpy as np
import jax
import jax.numpy as jnp
from jax import lax
from jax.experimental import pallas as pl
from jax.experimental.pallas import tpu as pltpu

F32 = jnp.float32
BF16 = jnp.bfloat16
MESH = pl.DeviceIdType.MESH

ATT_PATTERNS = ((128, 1), (512, 4), (2048, 16))
N_GROUPS_A = 3
HEADS = 8
HEAD_DIM = 64
WIDTH_A = HEADS * HEAD_DIM
QKV_W = N_GROUPS_A * WIDTH_A
ATT_BLOCK = 128
N_REL_BUCKETS = 32
REL_MAX_DIST = 2048
NEG_INF = -1e30
CHUNK = 128
WIDTH_B = 768
N_GROUPS_B = 6
GROUP_B = 128
WIDTH_C = 768
SSM_GROUP = 16
N_GROUPS_C = 48
SSM_STATE = 64
SSM_CH = N_GROUPS_C * SSM_STATE
N_BRANCH = 3
LN_EPS = 1e-5
ADAM_LR = 0.001
ADAM_B1 = 0.9
ADAM_B2 = 0.999
ADAM_EPS = 1e-08
ADAM_WD = 0.01
ADAM_STEP = 10

LANES = 128
SUBLANES_BF16 = 16
V7X_VMEM_BYTES = 64 * 1024 * 1024
VMEM_LIMIT = V7X_VMEM_BYTES * 7 // 8

OFF_ZB = 3 * QKV_W
REST_ZB = 0
REST_UC = 2 * WIDTH_B
REST_GL = 2 * WIDTH_B + WIDTH_C

BIG = (("w_in", "col"), ("w_glu", "row"), ("w_pa", "col"), ("w_pb", "col"), ("w_pc", "col"),
       ("w_o", "row"), ("w_ffn_in", "col"), ("w_ffn_out", "row"))
WEIGHTS = ("w_in", "b_in", "rel_bias", "sgu_ln_g", "sgu_ln_b", "w_s", "b_s", "lam_re", "lam_im", "log_dt",
           "b_re", "b_im", "c_re", "c_im", "d_skip", "w_glu", "b_glu", "w_pa", "w_pb", "w_pc", "w_o",
           "ln1_g", "ln1_b", "w_ffn_in", "w_ffn_out", "ln2_g", "ln2_b")
BIG_NAMES = tuple(n for n, _ in BIG)
SMALL = tuple(n for n in WEIGHTS if n not in BIG_NAMES)


def _tile(n, target, mult):
    t = (min(target, n) // mult) * mult
    while t >= mult:
        if n % t == 0:
            return t
        t -= mult
    return n


def _cparams(n_axes):
    return pltpu.CompilerParams(dimension_semantics=("arbitrary",) * n_axes, vmem_limit_bytes=VMEM_LIMIT)


_DIMS = {"nn": (((1,), (0,)), ((), ())), "nt": (((1,), (1,)), ((), ())), "tn": (((0,), (0,)), ((), ()))}


def _mm(name, mode, grid, a, a_spec, b, b_spec, out_sds, o_spec, acc_shape, *, bias=None, bias_spec=None,
        add=None, add_spec=None, add_scale=1.0, exact=False, side=None):
    nk = grid[2]
    has_bias = bias is not None
    has_add = add is not None
    n_side_in = len(side["ins"]) if side else 0
    n_side_out = len(side["out_sds"]) if side else 0
    n_side_sem = len(side["sems"]) if side else 0

    def body(*refs):
        a_ref, b_ref = refs[0], refs[1]
        pos = 2
        bias_ref = add_ref = None
        if has_bias:
            bias_ref = refs[pos]
            pos += 1
        if has_add:
            add_ref = refs[pos]
            pos += 1
        side_ins = refs[pos:pos + n_side_in]
        pos += n_side_in
        o_ref = refs[pos]
        side_outs = refs[pos + 1:pos + 1 + n_side_out]
        side_sems = refs[len(refs) - n_side_sem:] if n_side_sem else ()
        pos += n_side_out
        if side:
            at = [pl.program_id(d) for d in range(3)]

            @pl.when((at[0] == 0) & (at[1] == 0) & (at[2] == 0))
            def _():
                side["start"](side_ins, side_outs, side_sems)
        if exact:
            part = lax.dot_general(a_ref[...].astype(F32), b_ref[...].astype(F32), _DIMS[mode],
                                   preferred_element_type=F32, precision=lax.Precision.HIGHEST)
        else:
            part = lax.dot_general(a_ref[...].astype(BF16), b_ref[...].astype(BF16), _DIMS[mode],
                                   preferred_element_type=F32)

        def finish(r):
            if has_bias:
                r = r + bias_ref[...]
            if has_add:
                r = r + add_scale * add_ref[...].astype(F32)
            o_ref[...] = r.astype(o_ref.dtype)

        if nk == 1:
            finish(part)
        else:
            acc_ref = refs[pos + 1]
            k = pl.program_id(2)

            @pl.when(k == 0)
            def _():
                acc_ref[...] = part

            @pl.when(k > 0)
            def _():
                acc_ref[...] += part

            @pl.when(k == nk - 1)
            def _():
                finish(acc_ref[...])

        if side:
            @pl.when((at[0] == grid[0] - 1) & (at[1] == grid[1] - 1) & (at[2] == grid[2] - 1))
            def _():
                side["finish"](side_ins, side_outs, side_sems)

    ins, specs = [a, b], [a_spec, b_spec]
    if has_bias:
        ins.append(bias)
        specs.append(bias_spec)
    if has_add:
        ins.append(add)
        specs.append(add_spec)
    scratch = [pltpu.VMEM(acc_shape, F32)] if nk > 1 else []
    if not side:
        return pl.pallas_call(body, out_shape=out_sds, grid=grid, in_specs=specs, out_specs=o_spec,
                              scratch_shapes=scratch, compiler_params=_cparams(3), name=name)(*ins)
    res = pl.pallas_call(body, out_shape=[out_sds] + list(side["out_sds"]), grid=grid,
                         in_specs=specs + [ANY] * n_side_in, out_specs=[o_spec] + [ANY] * n_side_out,
                         scratch_shapes=scratch + list(side["sems"]), compiler_params=_cparams(3),
                         name=name + "_" + side["name"])(*ins, *side["ins"])
    return res[0], res[1:]


def _mm_nn(name, a, b, *, n0=0, n=None, a0=0, bias=None, add=None, add_scale=1.0, out_dtype=F32,
           tm=1024, tn=512, tk=None, side=None):
    m = a.shape[0]
    if b.ndim == 3:
        k, cc = b.shape[1:]
        n = 4 * cc
        tn = _tile(cc, tn, LANES)
        per = cc // tn
        tk = _tile(k, k if tk is None else tk, LANES)
        b_spec = pl.BlockSpec((None, tk, tn), lambda i, j, kk: (j // per, kk, j % per))
    else:
        k = b.shape[0]
        n = b.shape[1] - n0 if n is None else n
        tn = _tile(math.gcd(n, n0) if n0 else n, tn, LANES)
        tk = _tile(math.gcd(k, a0) if a0 else k, k if tk is None else tk, LANES)
        b_spec = pl.BlockSpec((tk, tn), lambda i, j, kk: (kk, n0 // tn + j))
    tm = _tile(m, tm, SUBLANES_BF16)
    jn0, ka0 = n0 // tn, a0 // tk
    o_spec = pl.BlockSpec((tm, tn), lambda i, j, kk: (i, j))
    return _mm(name, "nn", (m // tm, n // tn, k // tk),
               a, pl.BlockSpec((tm, tk), lambda i, j, kk: (i, ka0 + kk)), b, b_spec,
               jax.ShapeDtypeStruct((m, n), out_dtype), o_spec, (tm, tn),
               bias=bias, bias_spec=pl.BlockSpec((1, tn), lambda i, j, kk: (0, jn0 + j)),
               add=add, add_spec=o_spec, add_scale=add_scale, side=side)


def _mm_nt(name, a, b, *, n0=0, add=None, add_scale=1.0, out_dtype=F32, tm=512, tn=2048, tk=512, side=None):
    m, n = a.shape
    if b.ndim == 3:
        k, cc = b.shape[1:]
        tn = _tile(k, tn, LANES)
        tk = _tile(cc, tk, LANES)
        per = cc // tk
        b_spec = pl.BlockSpec((None, tn, tk), lambda i, j, kk: (kk // per, j, kk % per))
    else:
        k = b.shape[0]
        tn = _tile(k, tn, LANES)
        tk = _tile(math.gcd(n, n0) if n0 else n, tk, LANES)
        b_spec = pl.BlockSpec((tn, tk), lambda i, j, kk: (j, n0 // tk + kk))
    tm = _tile(m, tm, SUBLANES_BF16)
    o_spec = pl.BlockSpec((tm, tn), lambda i, j, kk: (i, j))
    return _mm(name, "nt", (m // tm, k // tn, n // tk),
               a, pl.BlockSpec((tm, tk), lambda i, j, kk: (i, kk)), b, b_spec,
               jax.ShapeDtypeStruct((m, k), out_dtype), o_spec, (tm, tn),
               add=add, add_spec=o_spec, add_scale=add_scale, side=side)


def _mm_tn(name, a, b, *, a0=0, ka=None, out_dtype=BF16, tm=2048, tn=1024, tk=1024, col_shards=False, side=None):
    s = a.shape[0]
    ka = a.shape[1] - a0 if ka is None else ka
    n = b.shape[1]
    tm = _tile(math.gcd(ka, a0) if a0 else ka, tm, LANES)
    tk = _tile(s, tk, SUBLANES_BF16)
    ia0 = a0 // tm
    if col_shards:
        cc = n // 4
        tn = _tile(cc, tn, LANES)
        per = cc // tn
        out_sds = jax.ShapeDtypeStruct((4, ka, cc), out_dtype)
        o_spec = pl.BlockSpec((None, tm, tn), lambda i, j, kk: (j // per, i, j % per))
    else:
        tn = _tile(n, tn, LANES)
        out_sds = jax.ShapeDtypeStruct((ka, n), out_dtype)
        o_spec = pl.BlockSpec((tm, tn), lambda i, j, kk: (i, j))
    return _mm(name, "tn", (ka // tm, n // tn, s // tk),
               a, pl.BlockSpec((tk, tm), lambda i, j, kk: (kk, ia0 + i)),
               b, pl.BlockSpec((tk, tn), lambda i, j, kk: (kk, j)), out_sds, o_spec, (tm, tn), side=side)


def R(arr, bw=None, c0=0, j=False):
    return ("r", arr, arr.shape[1] if bw is None else bw, c0, j)


def P(arr, bw=None, c0=0, j=False):
    return ("p", arr, arr.shape[1] if bw is None else bw, c0, j)


def _ew(name, fn, rows, ins, outs, accs=(), *, tr=256, nj=1):
    tr = _tile(rows, tr, SUBLANES_BF16)
    ni = rows // tr
    n_in, n_out, n_acc = len(ins), len(outs), len(accs)

    def spec(kind, bw, c0, follows):
        rows_b = tr if kind == "r" else 1
        if kind == "r":
            return pl.BlockSpec((rows_b, bw), (lambda j, i: (i, c0 + j)) if follows else (lambda j, i: (i, c0)))
        return pl.BlockSpec((rows_b, bw), (lambda j, i: (0, c0 + j)) if follows else (lambda j, i: (0, c0)))

    def body(*refs):
        res = fn(*[r[...] for r in refs[:n_in]])
        res = tuple(res) if isinstance(res, (tuple, list)) else (res,)
        for r, v in zip(refs[n_in:n_in + n_out], res[:n_out]):
            r[...] = v.astype(r.dtype)
        if n_acc:
            i = pl.program_id(1)
            for r, v in zip(refs[n_in + n_out:], res[n_out:]):
                @pl.when(i == 0)
                def _(r=r, v=v):
                    r[...] = v

                @pl.when(i > 0)
                def _(r=r, v=v):
                    r[...] += v

    out_shape = [jax.ShapeDtypeStruct((rows, bw * nj), dt) for bw, dt in outs]
    out_shape += [jax.ShapeDtypeStruct((1, bw * nj), F32) for bw in accs]
    out_specs = [pl.BlockSpec((tr, bw), lambda j, i: (i, j)) for bw, _ in outs]
    out_specs += [pl.BlockSpec((1, bw), lambda j, i: (0, j)) for bw in accs]
    res = pl.pallas_call(body, out_shape=out_shape, grid=(nj, ni),
                         in_specs=[spec(k, bw, c0, f) for k, _, bw, c0, f in ins], out_specs=out_specs,
                         compiler_params=_cparams(2), name=name)(*[a for _, a, _, _, _ in ins])
    return res if len(res) > 1 else res[0]


def _gelu(x):
    c = math.sqrt(2.0 / math.pi)
    return 0.5 * x * (1.0 + jnp.tanh(c * (x + 0.044715 * (x * x * x))))


def _sigmoid(x):
    return lax.logistic(x)


def _f_ln(h, g, b):
    mu = jnp.mean(h, axis=-1, keepdims=True)
    xc = h - mu
    var = jnp.mean(xc * xc, axis=-1, keepdims=True)
    return xc * lax.rsqrt(var + LN_EPS) * g + b


def _f_combine(o1, o2, o3, l1, l2, l3):
    m = jnp.maximum(jnp.maximum(l1, l2), l3)
    e1, e2, e3 = jnp.exp(l1 - m), jnp.exp(l2 - m), jnp.exp(l3 - m)
    den = e1 + e2 + e3
    return (e1 * o1 + e2 * o2 + e3 * o3) / den, m + jnp.log(den)


def _f_ssm_out(y_lin, uc, d_skip):
    return _gelu(y_lin + d_skip * uc)


def _f_glu(yc0, t):
    return yc0 * _sigmoid(t)


def _f_merge(g0, g1, g2, pa, pb, pc):
    return _sigmoid(g0) * pa + _sigmoid(g1) * pb + _sigmoid(g2) * pc


def _f_swiglu(gf, up):
    return gf * _sigmoid(gf) * up


def _f_sgu_pre(zu, zv, g, b):
    return _gelu(zu), _f_ln(_gelu(zv), g, b)


def _f_ssm_params(lr, li, ld, lr_rep, li_rep, ld_rep, br_t, bi_t):
    def disc(lr, li, ld):
        dt = jnp.exp(ld)
        mag = jnp.exp(lr * dt)
        th = li * dt
        abr, abi = mag * jnp.cos(th), mag * jnp.sin(th)
        nrm = lr * lr + li * li
        cr = ((abr - 1.0) * lr + abi * li) / nrm
        ci = (abi * lr - (abr - 1.0) * li) / nrm
        return abr, abi, cr, ci

    abr, abi, _, _ = disc(lr, li, ld)
    _, _, cr, ci = disc(lr_rep, li_rep, ld_rep)
    return abr, abi, cr * br_t - ci * bi_t, cr * bi_t + ci * br_t


def _rowsum(x):
    return jnp.sum(x, axis=0, keepdims=True)


def _to_streams(qkv):
    s = qkv.shape[0]
    t = qkv.reshape(s, 3, N_GROUPS_A, HEADS, HEAD_DIM)
    outs = []
    for g, (_, dil) in enumerate(ATT_PATTERNS):
        tg = t[:, :, g].reshape(s // dil, dil, 3, HEADS, HEAD_DIM)
        outs.append(tg.transpose(2, 3, 1, 0, 4).reshape(3, HEADS, s, HEAD_DIM))
    return jnp.stack(outs, 0)


def _tok_to_streams(a):
    s = a.shape[0]
    outs = []
    for _, dil in ATT_PATTERNS:
        t = a.reshape(s // dil, dil, HEADS, HEAD_DIM)
        outs.append(t.transpose(2, 1, 0, 3).reshape(HEADS, s, HEAD_DIM))
    return jnp.stack(outs, 0)


def _streams_to_tok(o):
    s = o.shape[2]
    outs = []
    for g, (_, dil) in enumerate(ATT_PATTERNS):
        t = o[g].reshape(HEADS, dil, s // dil, HEAD_DIM)
        outs.append(t.transpose(2, 1, 0, 3).reshape(s, WIDTH_A))
    return outs


def _t5_bucket(dist):
    max_exact = N_REL_BUCKETS // 2
    d = np.maximum(dist, 1).astype(np.float32)
    scale = (N_REL_BUCKETS - max_exact) / math.log(REL_MAX_DIST / max_exact)
    large = max_exact + (np.log(d / max_exact) * scale).astype(np.int32)
    large = np.minimum(large, N_REL_BUCKETS - 1)
    return np.where(dist < max_exact, dist, large).astype(np.int32)


def _bucket_maps():
    i = np.arange(ATT_BLOCK)[:, None]
    kk = np.arange(2 * ATT_BLOCK)[None, :]
    steps = np.maximum(ATT_BLOCK + i - kk, 0)
    return np.stack([_t5_bucket(steps * dil) for _, dil in ATT_PATTERNS], 0)


def _band_bias(rel_bias):
    q = ATT_BLOCK

    def body(rel_ref, m_ref, o_ref):
        g = pl.program_id(0)
        bm = m_ref[...]
        for h in range(HEADS):
            acc = jnp.zeros((q, 2 * q), F32)
            for bk in range(N_REL_BUCKETS):
                acc = jnp.where(bm == bk, rel_ref[bk, g * HEADS + h], acc)
            o_ref[h] = acc

    return pl.pallas_call(body, out_shape=jax.ShapeDtypeStruct((N_GROUPS_A, HEADS, q, 2 * q), F32), grid=(N_GROUPS_A,),
                          in_specs=[pl.BlockSpec(memory_space=pltpu.SMEM),
                                    pl.BlockSpec((None, q, 2 * q), lambda g: (g, 0, 0))],
                          out_specs=pl.BlockSpec((None, HEADS, q, 2 * q), lambda g: (g, 0, 0, 0)),
                          compiler_params=_cparams(1), name="rel_bias_band")(rel_bias, jnp.asarray(_bucket_maps()))


def _attn_masks(nbs):
    q = ATT_BLOCK
    g, b = pl.program_id(0), pl.program_id(1)
    nb = jnp.where(g == 0, nbs[0], jnp.where(g == 1, nbs[1], nbs[2]))
    shift = jnp.where(lax.rem(b, nb) != 0, 0, q)
    ii = lax.broadcasted_iota(jnp.int32, (q, q), 0)
    kk = lax.broadcasted_iota(jnp.int32, (q, q), 1)
    return kk >= ii + shift, kk <= ii


def _attn_logits(q, kp, kc, bias_h, mask_p, mask_c):
    scale = HEAD_DIM ** -0.5
    sp = lax.dot_general(q, kp, _DIMS["nt"], preferred_element_type=F32) * scale + bias_h[:, :ATT_BLOCK]
    sc = lax.dot_general(q, kc, _DIMS["nt"], preferred_element_type=F32) * scale + bias_h[:, ATT_BLOCK:]
    return jnp.where(mask_p, sp, NEG_INF), jnp.where(mask_c, sc, NEG_INF)


def _attn_specs(s):
    q, h, e = ATT_BLOCK, HEADS, HEAD_DIM
    blk = (None, None, h, q, e)
    prev = lambda b: jnp.maximum(b - 1, 0)
    qkv_specs = [pl.BlockSpec(blk, lambda g, b: (g, 0, 0, b, 0)),
                 pl.BlockSpec(blk, lambda g, b: (g, 1, 0, prev(b), 0)),
                 pl.BlockSpec(blk, lambda g, b: (g, 1, 0, b, 0)),
                 pl.BlockSpec(blk, lambda g, b: (g, 2, 0, prev(b), 0)),
                 pl.BlockSpec(blk, lambda g, b: (g, 2, 0, b, 0))]
    bias_spec = pl.BlockSpec((None, h, q, 2 * q), lambda g, b: (g, 0, 0, 0))
    row_spec = pl.BlockSpec((None, h, q, e), lambda g, b: (g, 0, b, 0))
    return qkv_specs, bias_spec, row_spec


def _attn_fwd(qkv_s, bias, side=None):
    s = qkv_s.shape[3]
    nblk = s // ATT_BLOCK
    nbs = tuple(s // dil // ATT_BLOCK for _, dil in ATT_PATTERNS)
    qkv_specs, bias_spec, row_spec = _attn_specs(s)
    n_side_in = len(side["ins"]) if side else 0
    n_side_out = len(side["out_sds"]) if side else 0

    def body(q_ref, kp_ref, kc_ref, vp_ref, vc_ref, b_ref, *rest):
        side_ins, (o_ref, l_ref) = rest[:n_side_in], rest[n_side_in:n_side_in + 2]
        side_outs = rest[n_side_in + 2:n_side_in + 2 + n_side_out]
        side_sems = rest[n_side_in + 2 + n_side_out:]
        if side:
            @pl.when((pl.program_id(0) == 0) & (pl.program_id(1) == 0))
            def _():
                side["start"](side_ins, side_outs, side_sems)

        mask_p, mask_c = _attn_masks(nbs)
        for h in range(HEADS):
            sp, sc = _attn_logits(q_ref[h], kp_ref[h], kc_ref[h], b_ref[h], mask_p, mask_c)
            m = jnp.maximum(jnp.max(sp, axis=1, keepdims=True), jnp.max(sc, axis=1, keepdims=True))
            pp, pc = jnp.exp(sp - m), jnp.exp(sc - m)
            den = jnp.sum(pp, axis=1, keepdims=True) + jnp.sum(pc, axis=1, keepdims=True)
            o = (lax.dot_general(pp.astype(BF16), vp_ref[h], _DIMS["nn"], preferred_element_type=F32)
                 + lax.dot_general(pc.astype(BF16), vc_ref[h], _DIMS["nn"], preferred_element_type=F32))
            o_ref[h] = o / den
            l_ref[h] = jnp.broadcast_to(m + jnp.log(den), (ATT_BLOCK, HEAD_DIM))

        if side:
            @pl.when((pl.program_id(0) == N_GROUPS_A - 1) & (pl.program_id(1) == nblk - 1))
            def _():
                side["finish"](side_ins, side_outs, side_sems)

    sds = jax.ShapeDtypeStruct((N_GROUPS_A, HEADS, s, HEAD_DIM), F32)
    if not side:
        return pl.pallas_call(body, out_shape=(sds, sds), grid=(N_GROUPS_A, nblk),
                              in_specs=qkv_specs + [bias_spec], out_specs=(row_spec, row_spec),
                              compiler_params=_cparams(2), name="attn_fwd")(qkv_s, qkv_s, qkv_s, qkv_s, qkv_s, bias), None
    res = pl.pallas_call(body, out_shape=[sds, sds] + list(side["out_sds"]), grid=(N_GROUPS_A, nblk),
                         in_specs=qkv_specs + [bias_spec] + [ANY] * n_side_in,
                         out_specs=[row_spec, row_spec] + [ANY] * n_side_out, scratch_shapes=list(side["sems"]),
                         compiler_params=_cparams(2), name="attn_fwd_" + side["name"])(
        qkv_s, qkv_s, qkv_s, qkv_s, qkv_s, bias, *side["ins"])
    return (res[0], res[1]), res[2:]


def _attn_bwd(qkv_s, bias, do_s, ya_s, lse_s):
    s = qkv_s.shape[3]
    nblk = s // ATT_BLOCK
    nbs = tuple(s // dil // ATT_BLOCK for _, dil in ATT_PATTERNS)
    scale = HEAD_DIM ** -0.5
    q, h, e = ATT_BLOCK, HEADS, HEAD_DIM

    def body(q_ref, kp_ref, kc_ref, vp_ref, vc_ref, b_ref, do_ref, ya_ref, l_ref,
             dq_ref, dk_ref, dv_ref, db_ref, dk_own, dv_own):
        b = pl.program_id(1)
        mask_p, mask_c = _attn_masks(nbs)

        @pl.when(b == 0)
        def _():
            db_ref[...] = jnp.zeros_like(db_ref)
            dk_own[...] = jnp.zeros_like(dk_own)
            dv_own[...] = jnp.zeros_like(dv_own)

        @pl.when(b < nblk)
        def _():
            for hh in range(h):
                qh, kp, kc, vp, vc = q_ref[hh], kp_ref[hh], kc_ref[hh], vp_ref[hh], vc_ref[hh]
                sp, sc = _attn_logits(qh, kp, kc, b_ref[hh], mask_p, mask_c)
                lse = l_ref[hh][:, 0:1]
                pp, pc = jnp.exp(sp - lse), jnp.exp(sc - lse)
                do = do_ref[hh]
                dsum = jnp.sum(do * ya_ref[hh], axis=1, keepdims=True)
                dob = do.astype(BF16)
                dsp = pp * (lax.dot_general(dob, vp, _DIMS["nt"], preferred_element_type=F32) - dsum)
                dsc = pc * (lax.dot_general(dob, vc, _DIMS["nt"], preferred_element_type=F32) - dsum)
                db_ref[hh, :, :q] += dsp
                db_ref[hh, :, q:] += dsc
                dspb, dscb = dsp.astype(BF16), dsc.astype(BF16)
                dq_ref[hh] = (scale * (lax.dot_general(dspb, kp, _DIMS["nn"], preferred_element_type=F32)
                                       + lax.dot_general(dscb, kc, _DIMS["nn"], preferred_element_type=F32))).astype(dq_ref.dtype)
                dkp = scale * lax.dot_general(dspb, qh, _DIMS["tn"], preferred_element_type=F32)
                dvp = lax.dot_general(pp.astype(BF16), dob, _DIMS["tn"], preferred_element_type=F32)

                dk_ref[hh] = (dk_own[hh] + dkp).astype(dk_ref.dtype)
                dv_ref[hh] = (dv_own[hh] + dvp).astype(dv_ref.dtype)
                dk_own[hh] = scale * lax.dot_general(dscb, qh, _DIMS["tn"], preferred_element_type=F32)
                dv_own[hh] = lax.dot_general(pc.astype(BF16), dob, _DIMS["tn"], preferred_element_type=F32)

        @pl.when(b == nblk)
        def _():
            dk_ref[...] = dk_own[...].astype(dk_ref.dtype)
            dv_ref[...] = dv_own[...].astype(dv_ref.dtype)

    blk5 = (None, None, h, q, e)
    cur = lambda b: jnp.minimum(b, nblk - 1)
    prev = lambda b: jnp.maximum(cur(b) - 1, 0)
    qkv_specs = [pl.BlockSpec(blk5, lambda g, b: (g, 0, 0, cur(b), 0)),
                 pl.BlockSpec(blk5, lambda g, b: (g, 1, 0, prev(b), 0)),
                 pl.BlockSpec(blk5, lambda g, b: (g, 1, 0, cur(b), 0)),
                 pl.BlockSpec(blk5, lambda g, b: (g, 2, 0, prev(b), 0)),
                 pl.BlockSpec(blk5, lambda g, b: (g, 2, 0, cur(b), 0))]
    bias_spec = pl.BlockSpec((None, h, q, 2 * q), lambda g, b: (g, 0, 0, 0))
    row_spec = pl.BlockSpec((None, h, q, e), lambda g, b: (g, 0, cur(b), 0))
    lag_spec = pl.BlockSpec((None, h, q, e), lambda g, b: (g, 0, jnp.maximum(b - 1, 0), 0))
    sds = jax.ShapeDtypeStruct((N_GROUPS_A, h, s, e), BF16)
    own = pltpu.VMEM((h, q, e), F32)
    return pl.pallas_call(body, out_shape=(sds,) * 3 + (jax.ShapeDtypeStruct(bias.shape, F32),),
                          grid=(N_GROUPS_A, nblk + 1),
                          in_specs=qkv_specs + [bias_spec, row_spec, row_spec, row_spec],
                          out_specs=(row_spec, lag_spec, lag_spec, bias_spec), scratch_shapes=[own, own],
                          compiler_params=_cparams(2), name="attn_bwd")(
        qkv_s, qkv_s, qkv_s, qkv_s, qkv_s, bias, do_s, ya_s, lse_s)


def _bias_to_buckets(dbias):
    bmap = jnp.asarray(_bucket_maps())
    q = ATT_BLOCK

    def body(db_ref, m_ref, o_ref):
        lane = lax.broadcasted_iota(jnp.int32, (HEADS, LANES), 1)
        row = lax.broadcasted_iota(jnp.int32, (HEADS, LANES), 0)
        acc = jnp.zeros((HEADS, LANES), F32)
        bm = m_ref[...]
        for h in range(HEADS):
            dbh = db_ref[h]
            for bk in range(N_REL_BUCKETS):
                sv = jnp.sum(jnp.sum(jnp.where(bm == bk, dbh, 0.0), axis=1, keepdims=True), axis=0, keepdims=True)
                acc = acc + jnp.where((lane == bk) & (row == h), sv, 0.0)
        o_ref[...] = acc

    out = pl.pallas_call(body, out_shape=jax.ShapeDtypeStruct((N_GROUPS_A, HEADS, LANES), F32), grid=(N_GROUPS_A,),
                         in_specs=[pl.BlockSpec((None, HEADS, q, 2 * q), lambda g: (g, 0, 0, 0)),
                                   pl.BlockSpec((None, q, 2 * q), lambda g: (g, 0, 0))],
                         out_specs=pl.BlockSpec((None, HEADS, LANES), lambda g: (g, 0, 0)),
                         compiler_params=_cparams(1), name="rel_bias_grad")(dbias, bmap)
    return out[:, :, :N_REL_BUCKETS].reshape(N_GROUPS_A * HEADS, N_REL_BUCKETS).T


def _sgu_specs():
    c, w = CHUNK, WIDTH_B
    return [pl.BlockSpec((c, w), lambda i: (i, 0)), pl.BlockSpec((c, w), lambda i: (i, 1)),
            pl.BlockSpec((1, w), lambda i: (0, 0)), pl.BlockSpec((1, w), lambda i: (0, 0)),
            pl.BlockSpec((N_GROUPS_B, c, c), lambda i: (0, 0, 0)), pl.BlockSpec((N_GROUPS_B, c, 1), lambda i: (0, 0, 0))]


def _sgu_fwd(rest, ln_g, ln_b, w_tril, b_col):
    s = rest.shape[0]

    def body(zu_ref, zv_ref, g_ref, b_ref, w_ref, bs_ref, y_ref):
        u, vn = _f_sgu_pre(zu_ref[...], zv_ref[...], g_ref[...], b_ref[...])
        for gi in range(N_GROUPS_B):
            sl = slice(gi * GROUP_B, (gi + 1) * GROUP_B)
            mixed = lax.dot_general(w_ref[gi], vn[:, sl].astype(BF16), _DIMS["nn"], preferred_element_type=F32)
            y_ref[:, sl] = (u[:, sl] * (mixed + bs_ref[gi])).astype(y_ref.dtype)

    return pl.pallas_call(body, out_shape=jax.ShapeDtypeStruct((s, WIDTH_B), BF16), grid=(s // CHUNK,),
                          in_specs=_sgu_specs(), out_specs=pl.BlockSpec((CHUNK, WIDTH_B), lambda i: (i, 0)),
                          compiler_params=_cparams(1), name="sgu_fwd")(rest, rest, ln_g, ln_b, w_tril, b_col)


def _sgu_bwd(rest, ln_g, ln_b, w_tril, b_col, dyb):
    s = rest.shape[0]
    c, w, ng = CHUNK, WIDTH_B, N_GROUPS_B

    def body(zu_ref, zv_ref, g_ref, b_ref, w_ref, bs_ref, dy_ref, dzu_ref, dzv_ref, dw_ref, dbs_ref, dg_ref, db_ref):
        (u, vn), vjp = jax.vjp(_f_sgu_pre, zu_ref[...], zv_ref[...], g_ref[...], b_ref[...])
        first = pl.program_id(0) == 0
        du, dvn = [], []
        for gi in range(ng):
            sl = slice(gi * GROUP_B, (gi + 1) * GROUP_B)
            vg = vn[:, sl].astype(BF16)
            mixed = lax.dot_general(w_ref[gi], vg, _DIMS["nn"], preferred_element_type=F32) + bs_ref[gi]
            dy = dy_ref[:, sl]
            dmix = dy * u[:, sl]
            du.append(dy * mixed)
            dmb = dmix.astype(BF16)
            dvn.append(lax.dot_general(w_ref[gi], dmb, _DIMS["tn"], preferred_element_type=F32))
            dwg = lax.dot_general(dmb, vg, _DIMS["nt"], preferred_element_type=F32)
            dbg = jnp.sum(dmix, axis=1, keepdims=True)

            @pl.when(first)
            def _(gi=gi, dwg=dwg, dbg=dbg):
                dw_ref[gi] = dwg
                dbs_ref[gi] = dbg

            @pl.when(jnp.logical_not(first))
            def _(gi=gi, dwg=dwg, dbg=dbg):
                dw_ref[gi] += dwg
                dbs_ref[gi] += dbg

        dzu, dzv, dg, db = vjp((jnp.concatenate(du, axis=1), jnp.concatenate(dvn, axis=1)))
        dzu_ref[...] = dzu.astype(dzu_ref.dtype)
        dzv_ref[...] = dzv.astype(dzv_ref.dtype)

        @pl.when(first)
        def _():
            dg_ref[...] = dg
            db_ref[...] = db

        @pl.when(jnp.logical_not(first))
        def _():
            dg_ref[...] += dg
            db_ref[...] += db

    row = pl.BlockSpec((c, w), lambda i: (i, 0))
    par = pl.BlockSpec((1, w), lambda i: (0, 0))
    return pl.pallas_call(
        body, grid=(s // c,),
        out_shape=(jax.ShapeDtypeStruct((s, w), BF16), jax.ShapeDtypeStruct((s, w), BF16),
                   jax.ShapeDtypeStruct((ng, c, c), F32), jax.ShapeDtypeStruct((ng, c, 1), F32),
                   jax.ShapeDtypeStruct((1, w), F32), jax.ShapeDtypeStruct((1, w), F32)),
        in_specs=_sgu_specs() + [row],
        out_specs=(row, row, pl.BlockSpec((ng, c, c), lambda i: (0, 0, 0)), pl.BlockSpec((ng, c, 1), lambda i: (0, 0, 0)),
                   par, par),
        compiler_params=_cparams(1), name="sgu_bwd")(rest, rest, ln_g, ln_b, w_tril, b_col, dyb)


SCAN_T = 128


def _to_time_major(dst, src_ref, lead):
    for r in range(SSM_CH // LANES):
        dst[:, r, :] = src_ref[lead, :, r * LANES:(r + 1) * LANES]


def _from_time_major(dst_ref, lead, src):
    for r in range(SSM_CH // LANES):
        dst_ref[lead, :, r * LANES:(r + 1) * LANES] = src[:, r, :]


def _scan_fwd(bu, a3):
    s = bu.shape[1]
    t_blk = _tile(s, SCAN_T, 8)
    rows = SSM_CH // LANES

    def body(bu_ref, a_ref, x_ref, b3r, b3i, x3r, x3i, carry):
        @pl.when(pl.program_id(0) == 0)
        def _():
            carry[...] = jnp.zeros_like(carry)

        _to_time_major(b3r, bu_ref, 0)
        _to_time_major(b3i, bu_ref, 1)
        ar, ai = a_ref[0], a_ref[1]

        def step(t, c):
            xr, xi = c
            nr = ar * xr - ai * xi + b3r[t]
            ni = ar * xi + ai * xr + b3i[t]
            x3r[t] = nr
            x3i[t] = ni
            return nr, ni

        xr, xi = lax.fori_loop(0, t_blk, step, (carry[0], carry[1]), unroll=8)
        carry[0] = xr
        carry[1] = xi
        _from_time_major(x_ref, 0, x3r)
        _from_time_major(x_ref, 1, x3i)

    blk = pl.BlockSpec((2, t_blk, SSM_CH), lambda i: (0, i, 0))
    tm = pltpu.VMEM((t_blk, rows, LANES), F32)
    return pl.pallas_call(body, out_shape=jax.ShapeDtypeStruct(bu.shape, F32), grid=(s // t_blk,),
                          in_specs=[blk, pl.BlockSpec((2, rows, LANES), lambda i: (0, 0, 0))], out_specs=blk,
                          scratch_shapes=[tm, tm, tm, tm, pltpu.VMEM((2, rows, LANES), F32)],
                          compiler_params=_cparams(1), name="ssm_scan_fwd")(bu, a3)


def _scan_bwd(dx, x, a3):
    s = dx.shape[1]
    t_blk = _tile(s, SCAN_T, 8)
    nb = s // t_blk
    rows = SSM_CH // LANES

    def body(dx_ref, x_ref, a_ref, g_ref, da_ref, d3r, d3i, x3r, x3i, g3r, g3i, carry):
        first = pl.program_id(0) == 0

        @pl.when(first)
        def _():
            carry[...] = jnp.zeros_like(carry)

        _to_time_major(d3r, dx_ref, 0)
        _to_time_major(d3i, dx_ref, 1)
        _to_time_major(x3r, x_ref, 0)
        _to_time_major(x3i, x_ref, 1)
        ar, ai = a_ref[0], a_ref[1]

        def step(k, c):
            t = t_blk - 1 - k
            gr, gi, dar, dai = c
            xr, xi = x3r[t], x3i[t]
            dar = dar + gr * xr + gi * xi
            dai = dai + gi * xr - gr * xi
            ngr = d3r[t] + ar * gr + ai * gi
            ngi = d3i[t] + ar * gi - ai * gr
            g3r[t] = ngr
            g3i[t] = ngi
            return ngr, ngi, dar, dai

        zero = jnp.zeros((rows, LANES), F32)
        gr, gi, dar, dai = lax.fori_loop(0, t_blk, step, (carry[0], carry[1], zero, zero), unroll=8)
        carry[0] = gr
        carry[1] = gi

        @pl.when(first)
        def _():
            da_ref[0] = dar
            da_ref[1] = dai

        @pl.when(jnp.logical_not(first))
        def _():
            da_ref[0] += dar
            da_ref[1] += dai

        _from_time_major(g_ref, 0, g3r)
        _from_time_major(g_ref, 1, g3i)

    blk = pl.BlockSpec((2, t_blk, SSM_CH), lambda i: (0, nb - 1 - i, 0))
    par = pl.BlockSpec((2, rows, LANES), lambda i: (0, 0, 0))
    tm = pltpu.VMEM((t_blk, rows, LANES), F32)
    return pl.pallas_call(body, out_shape=(jax.ShapeDtypeStruct(dx.shape, F32), jax.ShapeDtypeStruct((2, rows, LANES), F32)),
                          grid=(nb,), in_specs=[blk, blk, par], out_specs=(blk, par),
                          scratch_shapes=[tm] * 6 + [pltpu.VMEM((2, rows, LANES), F32)],
                          compiler_params=_cparams(1), name="ssm_scan_bwd")(dx, x, a3)


SSM_TILES = WIDTH_C // LANES
SSM_TILE_W = SSM_CH // SSM_TILES


def _block_diag(m):
    gpt = N_GROUPS_C // SSM_TILES
    t = m.reshape(SSM_TILES, gpt, SSM_GROUP, SSM_STATE)
    eye = jnp.eye(gpt, dtype=m.dtype)
    return (t[:, :, :, None, :] * eye[None, :, None, :, None]).reshape(SSM_TILES, LANES, SSM_TILE_W)


def _block_diag_extract(t):
    gpt = N_GROUPS_C // SSM_TILES
    t = t.reshape(SSM_TILES, gpt, SSM_GROUP, gpt, SSM_STATE)
    eye = jnp.eye(gpt, dtype=t.dtype)
    return jnp.sum(t * eye[None, :, None, :, None], axis=3).reshape(WIDTH_C, SSM_STATE)


def _ssm_in(uc_src, uc_col0, bd):
    s = uc_src.shape[0]
    tm = _tile(s, 2048, 8)
    j0 = uc_col0 // LANES
    return _mm("ssm_in", "nn", (s // tm, 2 * SSM_TILES, 1),
               uc_src, pl.BlockSpec((tm, LANES), lambda i, j, k: (i, j0 + j % SSM_TILES)),
               bd, pl.BlockSpec((None, None, LANES, SSM_TILE_W), lambda i, j, k: (j // SSM_TILES, j % SSM_TILES, 0, 0)),
               jax.ShapeDtypeStruct((2, s, SSM_CH), F32),
               pl.BlockSpec((None, tm, SSM_TILE_W), lambda i, j, k: (j // SSM_TILES, i, j % SSM_TILES)), None)


def _ssm_in_dgrad(g, bd):
    s = g.shape[1]
    tm = _tile(s, 2048, 8)
    return _mm("ssm_in_dgrad", "nt", (s // tm, SSM_TILES, 2),
               g, pl.BlockSpec((None, tm, SSM_TILE_W), lambda i, j, k: (k, i, j)),
               bd, pl.BlockSpec((None, None, LANES, SSM_TILE_W), lambda i, j, k: (k, j, 0, 0)),
               jax.ShapeDtypeStruct((s, WIDTH_C), F32), pl.BlockSpec((tm, LANES), lambda i, j, k: (i, j)),
               (tm, LANES))


def _ssm_in_wgrad(uc_src, uc_col0, g):
    s = g.shape[1]
    tk = _tile(s, 2048, 8)
    j0 = uc_col0 // LANES
    return _mm("ssm_in_wgrad", "tn", (2, SSM_TILES, s // tk),
               uc_src, pl.BlockSpec((tk, LANES), lambda i, j, k: (k, j0 + j)),
               g, pl.BlockSpec((None, tk, SSM_TILE_W), lambda i, j, k: (i, k, j)),
               jax.ShapeDtypeStruct((2, SSM_TILES, LANES, SSM_TILE_W), F32),
               pl.BlockSpec((None, None, LANES, SSM_TILE_W), lambda i, j, k: (i, j, 0, 0)), (LANES, SSM_TILE_W))


def _ssm_out(x, cd):
    s = x.shape[1]
    tm = _tile(s, 2048, 8)
    return _mm("ssm_out", "nn", (s // tm, SSM_TILES, 2),
               x, pl.BlockSpec((None, tm, SSM_TILE_W), lambda i, j, k: (k, i, j)),
               cd, pl.BlockSpec((None, None, SSM_TILE_W, LANES), lambda i, j, k: (k, j, 0, 0)),
               jax.ShapeDtypeStruct((s, WIDTH_C), F32), pl.BlockSpec((tm, LANES), lambda i, j, k: (i, j)),
               (tm, LANES))


def _ssm_out_dgrad(dy, cd):
    s = dy.shape[0]
    tm = _tile(s, 2048, 8)
    return _mm("ssm_out_dgrad", "nt", (s // tm, 2 * SSM_TILES, 1),
               dy, pl.BlockSpec((tm, LANES), lambda i, j, k: (i, j % SSM_TILES)),
               cd, pl.BlockSpec((None, None, SSM_TILE_W, LANES), lambda i, j, k: (j // SSM_TILES, j % SSM_TILES, 0, 0)),
               jax.ShapeDtypeStruct((2, s, SSM_CH), F32),
               pl.BlockSpec((None, tm, SSM_TILE_W), lambda i, j, k: (j // SSM_TILES, i, j % SSM_TILES)), None)


def _ssm_out_wgrad(x, dy):
    s = dy.shape[0]
    tk = _tile(s, 2048, 8)
    return _mm("ssm_out_wgrad", "tn", (2, SSM_TILES, s // tk),
               x, pl.BlockSpec((None, tk, SSM_TILE_W), lambda i, j, k: (i, k, j)),
               dy, pl.BlockSpec((tk, LANES), lambda i, j, k: (k, j)),
               jax.ShapeDtypeStruct((2, SSM_TILES, SSM_TILE_W, LANES), F32),
               pl.BlockSpec((None, None, SSM_TILE_W, LANES), lambda i, j, k: (i, j, 0, 0)), (SSM_TILE_W, LANES))


def _ssm_param_inputs(w, l):
    rep = lambda a: jnp.repeat(a, SSM_GROUP, axis=0)
    lr, li, ld = w["lam_re"][l], w["lam_im"][l], w["log_dt"][l][:, None]
    br_t = jnp.transpose(w["b_re"][l], (0, 2, 1)).reshape(WIDTH_C, SSM_STATE)
    bi_t = jnp.transpose(w["b_im"][l], (0, 2, 1)).reshape(WIDTH_C, SSM_STATE)
    return lr, li, ld, rep(lr), rep(li), rep(ld), br_t, bi_t


def _ssm_params_fwd(pin):
    def body(*refs):
        res = _f_ssm_params(*[r[...] for r in refs[:8]])
        for r, v in zip(refs[8:], res):
            r[...] = v

    g, p = N_GROUPS_C, SSM_STATE
    return pl.pallas_call(body, out_shape=(jax.ShapeDtypeStruct((g, p), F32),) * 2
                          + (jax.ShapeDtypeStruct((WIDTH_C, p), F32),) * 2, name="ssm_params_fwd")(*pin)


def _ssm_params_bwd(pin, d_abr, d_abi, d_bbr, d_bbi):
    g, p = N_GROUPS_C, SSM_STATE
    group_sum = jnp.asarray(np.kron(np.eye(g, dtype=np.float32), np.ones((1, SSM_GROUP), np.float32)))

    def body(*refs):
        ins = [r[...] for r in refs[:8]]
        cts = tuple(r[...] for r in refs[8:12])
        gs = refs[12][...]
        d_lr_ref, d_li_ref, d_ld_ref, d_br_ref, d_bi_ref = refs[13:]
        _, vjp = jax.vjp(_f_ssm_params, *ins)
        d = vjp(cts)
        fold = lambda v: lax.dot_general(gs, v, _DIMS["nn"], preferred_element_type=F32, precision=lax.Precision.HIGHEST)
        d_lr_ref[...] = d[0] + fold(d[3])
        d_li_ref[...] = d[1] + fold(d[4])
        d_ld_ref[...] = d[2] + fold(jnp.broadcast_to(d[5], (WIDTH_C, p)))[:, 0:1]
        d_br_ref[...] = d[6]
        d_bi_ref[...] = d[7]

    return pl.pallas_call(body, out_shape=(jax.ShapeDtypeStruct((g, p), F32), jax.ShapeDtypeStruct((g, p), F32),
                                           jax.ShapeDtypeStruct((g, 1), F32), jax.ShapeDtypeStruct((WIDTH_C, p), F32),
                                           jax.ShapeDtypeStruct((WIDTH_C, p), F32)), name="ssm_params_bwd")(
        *pin, d_abr, d_abi, d_bbr, d_bbi, group_sum)


def _c_block_diag(w, l):
    def one(c):
        return jnp.transpose(_block_diag(c.reshape(WIDTH_C, SSM_STATE)), (0, 2, 1))
    return jnp.stack([one(w["c_re"][l]), -one(w["c_im"][l])], 0)


ANY = pl.BlockSpec(memory_space=pl.ANY)


def _ag8(name, xb):
    def body(x_ref, out_ref, send_sems, recv_sems, local_sem):
        x, y, c = lax.axis_index("x"), lax.axis_index("y"), lax.axis_index("c")
        me, sibling = (x, y, c), (x, y, 1 - c)
        chips = [(1 - x, y), (x, 1 - y), (1 - x, 1 - y)]

        def rows(px, py, pc):
            return out_ref.at[4 * px + 2 * py + pc]

        def copy(k, block, to, src=None):
            return pltpu.make_async_remote_copy(src_ref=rows(*block) if src is None else src, dst_ref=rows(*block),
                                                send_sem=send_sems.at[k], recv_sem=recv_sems.at[k],
                                                device_id=to, device_id_type=MESH)

        mine = pltpu.make_async_copy(x_ref, rows(*me), local_sem)
        mine.start()
        first = [copy(0, me, sibling, src=x_ref)]
        first += [copy(1 + j, me, (*chip, c), src=x_ref) for j, chip in enumerate(chips)]
        for cp in first:
            cp.start()
        passed = [copy(4 + j, (*chip, c), sibling) for j, chip in enumerate(chips)]
        for j, chip in enumerate(chips):
            copy(1 + j, (*chip, c), me).wait_recv()
            passed[j].start()
        copy(0, sibling, me).wait_recv()
        for j, chip in enumerate(chips):
            copy(4 + j, (*chip, 1 - c), me).wait_recv()
        for cp in first + passed:
            cp.wait_send()
        mine.wait()

    return pl.pallas_call(body, out_shape=jax.ShapeDtypeStruct((8,) + xb.shape, xb.dtype), in_specs=[ANY], out_specs=ANY,
                          scratch_shapes=[pltpu.SemaphoreType.DMA((7,)), pltpu.SemaphoreType.DMA((7,)),
                                          pltpu.SemaphoreType.DMA(())], name=name)(xb)


def _mesh_place():
    x, y, c = lax.axis_index("x"), lax.axis_index("y"), lax.axis_index("c")
    return x, y, c, [(1 - x, y), (x, 1 - y), (1 - x, 1 - y)]


ICI_CHUNK_BYTES = 2 << 20
D2D_CHUNK_BYTES = 1 << 20


def _split_rows(rows, row_bytes, chunk_bytes):
    k = 1
    for cand in range(1, max(1, (rows * row_bytes) // chunk_bytes) + 1):
        if rows % cand == 0 and (rows // cand) % SUBLANES_BF16 == 0:
            k = cand
    return [(j * (rows // k), rows // k) for j in range(k)]


def _run_side(side):
    ni, no = len(side["ins"]), len(side["out_sds"])

    def body(*refs):
        ins, outs, sems = refs[:ni], refs[ni:ni + no], refs[ni + no:]
        side["start"](ins, outs, sems)
        side["finish"](ins, outs, sems)

    return pl.pallas_call(body, out_shape=list(side["out_sds"]), in_specs=[ANY] * ni, out_specs=[ANY] * no,
                          scratch_shapes=list(side["sems"]), name=side["name"])(*side["ins"])


def _gather_side(wb, l):
    nw = len(wb)
    halves = [a.shape[1] // 2 for a in wb]
    pieces = [_split_rows(h, a.shape[2] * a.dtype.itemsize, ICI_CHUNK_BYTES) for a, h in zip(wb, halves)]

    def rows(ref, i, hc, r0=0, n=None):
        return ref.at[pl.ds(hc * halves[i] + r0, halves[i] if n is None else n)]

    def copier(sems):
        def copy(i, k, src, dst, to):
            return pltpu.make_async_remote_copy(src_ref=src, dst_ref=dst, send_sem=sems[0].at[6 * i + k],
                                                recv_sem=sems[1].at[6 * i + k], device_id=to, device_id_type=MESH)
        return copy

    def start(ins, outs, sems):
        x, y, c, chips = _mesh_place()
        me = 2 * x + y
        copy = copier(sems)
        for i in range(nw):
            for hc in range(2):
                for r0, n in pieces[i]:
                    pltpu.make_async_copy(rows(ins[i].at[l], i, hc, r0, n), rows(outs[i].at[me], i, hc, r0, n),
                                          sems[2].at[i]).start()
        for i in range(nw):
            for k, (px, py) in enumerate(chips):
                for r0, n in pieces[i]:
                    copy(i, k, rows(ins[i].at[l], i, c, r0, n), rows(outs[i].at[me], i, c, r0, n), (px, py, c)).start()

    def finish(ins, outs, sems):
        x, y, c, chips = _mesh_place()
        me = 2 * x + y
        copy = copier(sems)
        for k, (px, py) in enumerate(chips):
            for i in range(nw):
                src = outs[i].at[2 * px + py]
                copy(i, k, rows(src, i, c), rows(src, i, c), (px, py, c)).wait_recv()
                for r0, n in pieces[i]:
                    copy(i, 3 + k, rows(src, i, c, r0, n), rows(src, i, c, r0, n), (x, y, 1 - c)).start()
        for k, (px, py) in enumerate(chips):
            for i in range(nw):
                other = rows(outs[i].at[2 * px + py], i, 1 - c)
                copy(i, 3 + k, other, other, (x, y, 1 - c)).wait_recv()
        for i in range(nw):
            whole = rows(outs[i].at[me], i, c)
            for k in range(6):
                copy(i, k, whole, whole, (x, y, 1 - c)).wait_send()
            pltpu.make_async_copy(ins[i].at[l], outs[i].at[me], sems[2].at[i]).wait()

    return dict(name="gather_weights", ins=list(wb), out_sds=[jax.ShapeDtypeStruct((4,) + a.shape[1:], a.dtype) for a in wb],
                sems=[pltpu.SemaphoreType.DMA((6 * nw,)), pltpu.SemaphoreType.DMA((6 * nw,)), pltpu.SemaphoreType.DMA((nw,))],
                start=start, finish=finish)


def _grad_core_swap(g4):
    nw = len(g4)
    halves = [a.shape[1] // 2 for a in g4]
    pieces = [_split_rows(h, a.shape[2] * a.dtype.itemsize, D2D_CHUNK_BYTES) for a, h in zip(g4, halves)]

    def body(*refs):
        ins, theirs = refs[:nw], refs[nw:2 * nw]
        send_sems, recv_sems = refs[2 * nw:]
        x, y, c, _ = _mesh_place()

        def copy(i, src, dst):
            return pltpu.make_async_remote_copy(src_ref=src, dst_ref=dst, send_sem=send_sems.at[i], recv_sem=recv_sems.at[i],
                                                device_id=(x, y, 1 - c), device_id_type=MESH)

        for i in range(nw):
            for j in range(4):
                for r0, n in pieces[i]:
                    copy(i, ins[i].at[j, pl.ds((1 - c) * halves[i] + r0, n)], theirs[i].at[j, pl.ds(r0, n)]).start()
        for i in range(nw):
            copy(i, ins[i].at[:, pl.ds((1 - c) * halves[i], halves[i])], theirs[i]).wait()

    sds = [jax.ShapeDtypeStruct((4, h) + a.shape[2:], a.dtype) for a, h in zip(g4, halves)]
    return pl.pallas_call(body, out_shape=sds, in_specs=[ANY] * nw, out_specs=[ANY] * nw,
                          scratch_shapes=[pltpu.SemaphoreType.DMA((nw,)), pltpu.SemaphoreType.DMA((nw,))],
                          name="grad_core_swap")(*g4)


def _exchange_side(t4):
    nw = len(t4)
    pieces = [_split_rows(a.shape[1], a.shape[2] * a.dtype.itemsize, ICI_CHUNK_BYTES) for a in t4]

    def copier(sems, c):
        def copy(i, k, src, dst, px, py):
            return pltpu.make_async_remote_copy(src_ref=src, dst_ref=dst, send_sem=sems[0].at[3 * i + k],
                                                recv_sem=sems[1].at[3 * i + k], device_id=(px, py, c),
                                                device_id_type=MESH)
        return copy

    def start(ins, outs, sems):
        x, y, c, chips = _mesh_place()
        me = 2 * x + y
        copy = copier(sems, c)
        for i in range(nw):
            for r0, n in pieces[i]:
                for k, (px, py) in enumerate(chips):
                    copy(i, k, ins[i].at[2 * px + py, pl.ds(r0, n)], outs[i].at[me, pl.ds(r0, n)], px, py).start()
                pltpu.make_async_copy(ins[i].at[me, pl.ds(r0, n)], outs[i].at[me, pl.ds(r0, n)], sems[2].at[i]).start()

    def finish(ins, outs, sems):
        x, y, c, chips = _mesh_place()
        me = 2 * x + y
        copy = copier(sems, c)
        for i in range(nw):
            for k, (px, py) in enumerate(chips):
                copy(i, k, ins[i].at[me], outs[i].at[2 * px + py], px, py).wait_recv()
        for i in range(nw):
            for k, (px, py) in enumerate(chips):
                copy(i, k, ins[i].at[me], outs[i].at[me], px, py).wait_send()
            pltpu.make_async_copy(ins[i].at[me], outs[i].at[me], sems[2].at[i]).wait()

    return dict(name="grad_chip_exchange", ins=list(t4), out_sds=[jax.ShapeDtypeStruct(a.shape, a.dtype) for a in t4],
                sems=[pltpu.SemaphoreType.DMA((3 * nw,)), pltpu.SemaphoreType.DMA((3 * nw,)), pltpu.SemaphoreType.DMA((nw,))],
                start=start, finish=finish)


def _grad_half_swap(full, l):
    nw = len(full)
    halves = [a.shape[1] // 2 for a in full]
    pieces = [_split_rows(h, a.shape[2] * a.dtype.itemsize, D2D_CHUNK_BYTES) for a, h in zip(full, halves)]

    def body(*refs):
        bufs = refs[nw:2 * nw]
        send_sems, recv_sems = refs[2 * nw:]
        x, y, c, _ = _mesh_place()

        def copy(i, hc, r0, n):
            view = bufs[i].at[l, pl.ds(hc * halves[i] + r0, n)]
            return pltpu.make_async_remote_copy(src_ref=view, dst_ref=view, send_sem=send_sems.at[i], recv_sem=recv_sems.at[i],
                                                device_id=(x, y, 1 - c), device_id_type=MESH)

        for i in range(nw):
            for r0, n in pieces[i]:
                copy(i, c, r0, n).start()
        for i in range(nw):
            copy(i, c, 0, halves[i]).wait_send()
            copy(i, 1 - c, 0, halves[i]).wait_recv()

    return pl.pallas_call(body, out_shape=[jax.ShapeDtypeStruct(a.shape, a.dtype) for a in full],
                          in_specs=[ANY] * nw, out_specs=[ANY] * nw, input_output_aliases={i: i for i in range(nw)},
                          scratch_shapes=[pltpu.SemaphoreType.DMA((nw,)), pltpu.SemaphoreType.DMA((nw,))],
                          name="grad_half_swap")(*full)


def _core_add(g, theirs, core):
    _, rh, cc = theirs.shape
    tr = _tile(rh, max(SUBLANES_BF16, (1 << 19) // cc), SUBLANES_BF16)
    nt = rh // tr

    def body(c_ref, a_ref, b_ref, o_ref):
        o_ref[...] = (a_ref[...].astype(F32) + b_ref[...].astype(F32)).astype(o_ref.dtype)

    blk = (None, tr, cc)
    return pl.pallas_call(
        body, out_shape=jax.ShapeDtypeStruct(theirs.shape, theirs.dtype),
        grid_spec=pltpu.PrefetchScalarGridSpec(
            num_scalar_prefetch=1, grid=(4, nt),
            in_specs=[pl.BlockSpec(blk, lambda j, t, c_ref: (j, c_ref[0] * nt + t, 0)),
                      pl.BlockSpec(blk, lambda j, t, c_ref: (j, t, 0))],
            out_specs=pl.BlockSpec(blk, lambda j, t, c_ref: (j, t, 0))),
        compiler_params=_cparams(2), name="grad_core_add")(core, g, theirs)


def _sum_to_half(xb, core, l, depth, stacked=None):
    n, rh, cc = xb.shape
    tr = _tile(rh, max(SUBLANES_BF16, (1 << 18) // cc), SUBLANES_BF16)
    nt = rh // tr

    def body(c_ref, x_ref, *rest):
        acc = x_ref[0].astype(F32)
        for k in range(1, n):
            acc = acc + x_ref[k].astype(F32)
        rest[-1][...] = acc

    ins = (core, xb) if stacked is None else (core, xb, stacked)
    return pl.pallas_call(
        body, out_shape=jax.ShapeDtypeStruct((depth, 2 * rh, cc), F32),
        grid_spec=pltpu.PrefetchScalarGridSpec(
            num_scalar_prefetch=1, grid=(nt,),
            in_specs=[pl.BlockSpec((n, tr, cc), lambda t, c_ref: (0, t, 0))] + ([] if stacked is None else [ANY]),
            out_specs=pl.BlockSpec((None, tr, cc), lambda t, c_ref: (l, c_ref[0] * nt + t, 0))),
        input_output_aliases={} if stacked is None else {2: 0},
        compiler_params=_cparams(1), name="grad_chip_sum")(*ins)


def _sum_lead(name, xb, out_dtype=F32):
    n, rows, w = xb.shape
    tr = _tile(rows, max(SUBLANES_BF16, (1 << 18) // w), SUBLANES_BF16)

    def body(x_ref, o_ref):
        acc = x_ref[0].astype(F32)
        for k in range(1, n):
            acc = acc + x_ref[k].astype(F32)
        o_ref[...] = acc.astype(o_ref.dtype)

    return pl.pallas_call(body, out_shape=jax.ShapeDtypeStruct((rows, w), out_dtype), grid=(rows // tr,),
                          in_specs=[pl.BlockSpec((n, tr, w), lambda i: (0, i, 0))],
                          out_specs=pl.BlockSpec((tr, w), lambda i: (i, 0)), compiler_params=_cparams(1), name=name)(xb)


def _pad_to(v, mult):
    n = v.shape[-1]
    pad = (-n) % mult
    return v if pad == 0 else jnp.pad(v, [(0, 0)] * (v.ndim - 1) + [(0, pad)])


RELAYOUT_ROWS = 256


def _cols_from_shards(a):
    _, k, cc = a.shape
    tr = _tile(k, RELAYOUT_ROWS, SUBLANES_BF16)

    def body(i_ref, o_ref):
        for j in range(4):
            o_ref[:, j * cc:(j + 1) * cc] = i_ref[j]

    return pl.pallas_call(body, out_shape=jax.ShapeDtypeStruct((k, 4 * cc), a.dtype), grid=(k // tr,),
                          in_specs=[pl.BlockSpec((4, tr, cc), lambda i: (0, i, 0))],
                          out_specs=pl.BlockSpec((tr, 4 * cc), lambda i: (i, 0)),
                          compiler_params=_cparams(1), name="cols_from_shards")(a)


def _cols_to_shards(a):
    k, n = a.shape
    cc = n // 4
    tr = _tile(k, RELAYOUT_ROWS, SUBLANES_BF16)

    def body(i_ref, o_ref):
        for j in range(4):
            o_ref[j] = i_ref[:, j * cc:(j + 1) * cc]

    return pl.pallas_call(body, out_shape=jax.ShapeDtypeStruct((4, k, cc), a.dtype), grid=(k // tr,),
                          in_specs=[pl.BlockSpec((tr, n), lambda i: (i, 0))],
                          out_specs=pl.BlockSpec((4, tr, cc), lambda i: (0, i, 0)),
                          compiler_params=_cparams(1), name="cols_to_shards")(a)


COMM_GROUPS = (("w_in",), ("w_ffn_in",), ("w_glu", "w_pa", "w_pb", "w_pc", "w_o", "w_ffn_out"))


def _gather_sides(wb, l):
    return [_gather_side([wb[n] for n in grp], l) for grp in COMM_GROUPS]


def _first_weights(wb):
    sides = _gather_sides(wb, 0)
    return _next_weights(wb, 0, [_run_side(sides[0]), None, None]), sides


def _next_weights(wb, l, got):
    kinds = dict(BIG)
    out = {}
    for grp, arrs in zip(COMM_GROUPS, got):
        for n, a in zip(grp, arrs if arrs is not None else ()):
            if kinds[n] == "row":
                out[n] = a.reshape(4 * a.shape[1], a.shape[2])
            else:
                out[n] = _cols_from_shards(a) if n == "w_in" else a
    return out


def _start_reduce(g):
    g4 = []
    for n, kind in BIG:
        a = g[n]
        if kind == "row":
            a = a.reshape(4, a.shape[0] // 4, a.shape[1])
        elif a.ndim == 2:
            a = _cols_to_shards(a)
        g4.append(a)
    core = lax.axis_index("c").astype(jnp.int32).reshape(1)
    theirs = _grad_core_swap(g4)
    return {n: _core_add(a, b, core) for n, a, b in zip(BIG_NAMES, g4, theirs)}


def _exchange_sides(pending):
    return [_exchange_side([pending[n] for n in grp]) for grp in COMM_GROUPS]


def _run_exchange(pending):
    return [_run_side(s) for s in _exchange_sides(pending)]


def _finish_reduce(pending, arrived, l, depth, stacked):
    core = lax.axis_index("c").astype(jnp.int32).reshape(1)
    names = [n for grp in COMM_GROUPS for n in grp]
    arrs = [a for grp in arrived for a in grp]
    halves = [_sum_to_half(a, core, l, depth, None if stacked is None else stacked[n]) for n, a in zip(names, arrs)]
    return dict(zip(names, _grad_half_swap(halves, l)))


def _adam_fn(w, g, m, v):
    m = ADAM_B1 * m + (1.0 - ADAM_B1) * g
    v = ADAM_B2 * v + (1.0 - ADAM_B2) * (g * g)
    m_hat = m / (1.0 - ADAM_B1 ** ADAM_STEP)
    v_hat = v / (1.0 - ADAM_B2 ** ADAM_STEP)
    return -ADAM_LR * (m_hat / (jnp.sqrt(v_hat) + ADAM_EPS) + ADAM_WD * w), m, v


def _adamw(name, w, g, m, v):
    shape = w.shape
    cols = shape[-1]
    f = lambda a: a.reshape(-1, cols)
    rows = f(w).shape[0]
    tr = max(8, (1 << 19) // cols)
    d, nm, nv = _ew(name, _adam_fn, rows, [R(f(w)), R(f(g)), R(f(m)), R(f(v))], [(cols, F32)] * 3, tr=tr)
    return d.reshape(shape), nm.reshape(shape), nv.reshape(shape)


def _beside(sides, k, call):
    if sides is None:
        return call(side=None), None
    return call(side=sides[k])


def _layer_fwd(x, wl, bias, alpha, sides=None, own=None, wb=None):
    s, d = x.shape
    b_in = wl["b_in"][None, :]
    got = [None] * len(COMM_GROUPS)
    mine = [None] * len(COMM_GROUPS)
    qkv, mine[1] = _beside(own, 1, functools.partial(_mm_nn, "proj_qkv", x, wl["w_in"], n0=0, n=3 * QKV_W, bias=b_in,
                                                     out_dtype=BF16, tn=768))
    rest, mine[2] = _beside(own, 2, functools.partial(_mm_nn, "proj_rest", x, wl["w_in"], n0=3 * QKV_W, bias=b_in, tn=768))
    if own is not None:
        wl.update(_next_weights(wb, 0, mine))
    d_ff = wl["w_ffn_out"].shape[0]
    qkv_s = _to_streams(qkv)
    (o_s, l_s), got[1] = _attn_fwd(qkv_s, bias, side=sides[1] if sides else None)
    o_t, l_t = _streams_to_tok(o_s), _streams_to_tok(l_s)
    ya, lse = _ew("attn_combine", _f_combine, s, [R(a) for a in o_t + l_t], [(WIDTH_A, F32), (WIDTH_A, F32)])
    yb = _sgu_fwd(rest, wl["sgu_ln_g"][None], wl["sgu_ln_b"][None], wl["w_tril"], wl["b_col"])
    bu = _ssm_in(rest, REST_UC, wl["bd"])
    xs = _scan_fwd(bu, wl["a3"])
    y_lin = _ssm_out(xs, wl["cd"])
    uc_blk = REST_UC // WIDTH_C
    yc0 = _ew("ssm_skip_gelu", _f_ssm_out, s, [R(y_lin), R(rest, WIDTH_C, uc_blk), P(wl["d_skip"][None])], [(WIDTH_C, F32)])
    t_glu = _mm_nn("glu_proj", yc0, wl["w_glu"], bias=wl["b_glu"][None], tn=768)
    yc = _ew("glu", _f_glu, s, [R(yc0), R(t_glu)], [(WIDTH_C, BF16)])
    pa = _mm_nn("proj_a", ya, wl["w_pa"])
    pb = _mm_nn("proj_b", yb, wl["w_pb"])
    pc = _mm_nn("proj_c", yc, wl["w_pc"])
    gw = _tile(math.gcd(d, REST_GL), 256, LANES)
    gl_ins = [R(rest, gw, (REST_GL + i * d) // gw, True) for i in range(N_BRANCH)]
    merged = _ew("merge", _f_merge, s, gl_ins + [R(pa, gw, 0, True), R(pb, gw, 0, True), R(pc, gw, 0, True)],
                 [(gw, BF16)], nj=d // gw, tr=1024)
    h1 = _mm_nn("proj_o", merged, wl["w_o"], add=x, add_scale=alpha)
    x1 = _ew("ln1", _f_ln, s, [R(h1), P(wl["ln1_g"][None]), P(wl["ln1_b"][None])], [(d, F32)])
    ff, got[0] = _beside(sides, 0, functools.partial(_mm_nn, "ffn_in", x1, wl["w_ffn_in"], tn=1408))
    fw = _tile(d_ff, 512, LANES)
    act = _ew("swiglu", _f_swiglu, s, [R(ff, fw, 0, True), R(ff, fw, d_ff // fw, True)], [(fw, BF16)], nj=d_ff // fw, tr=1024)
    h2, got[2] = _beside(sides, 2, functools.partial(_mm_nn, "ffn_out", act, wl["w_ffn_out"], add=x1, add_scale=alpha))
    x2 = _ew("ln2", _f_ln, s, [R(h2), P(wl["ln2_g"][None]), P(wl["ln2_b"][None])], [(d, F32)])
    saved = dict(x=x, qkv_s=qkv_s, rest=rest, ya=ya, lse=lse, yb=yb, xs=xs, y_lin=y_lin, yc0=yc0, t_glu=t_glu, yc=yc,
                 pa=pa, pb=pb, pc=pc, merged=merged, h1=h1, x1=x1, ff=ff, act=act, h2=h2)
    return x2, saved, got


def _vjp_rows(f, n_primal):
    def fn(*args):
        n_ct = len(args) - n_primal
        cts, primals = args[:n_ct], args[n_ct:]
        out, vjp = jax.vjp(f, *primals)
        ct = tuple(c.astype(F32) for c in cts)
        return vjp(ct if isinstance(out, (tuple, list)) else ct[0])
    return fn


def _layer_bwd(dx2, sv, wl, bias, alpha, sides=None):
    s, d = dx2.shape
    arrived = [None] * len(COMM_GROUPS)
    d_ff = wl["w_ffn_out"].shape[0]
    g = {}
    row = lambda a: a[None]
    dh2, g["ln2_g"], g["ln2_b"] = _ew("ln2_bwd", _vjp_rows(_f_ln, 3), s,
                                      [R(dx2), R(sv["h2"]), P(row(wl["ln2_g"])), P(row(wl["ln2_b"]))],
                                      [(d, F32)], accs=[d, d])
    g["w_ffn_out"] = _mm_tn("ffn_out_wgrad", sv["act"], dh2)
    dact = _mm_nt("ffn_out_dgrad", dh2, wl["w_ffn_out"], tn=1408, tk=2048, out_dtype=BF16)
    fw = _tile(d_ff, 512, LANES)
    nf = d_ff // fw
    dgf, dup = _ew("swiglu_bwd", _vjp_rows(_f_swiglu, 2), s,
                   [R(dact, fw, 0, True), R(sv["ff"], fw, 0, True), R(sv["ff"], fw, nf, True)],
                   [(fw, BF16), (fw, BF16)], nj=nf, tr=1024)
    dff = jnp.concatenate([dgf, dup], axis=1)
    g["w_ffn_in"], arrived[0] = _beside(sides, 0, functools.partial(_mm_tn, "ffn_in_wgrad", sv["x1"], dff, tn=1408,
                                                                    col_shards=True))
    dx1, arrived[1] = _beside(sides, 1, functools.partial(_mm_nt, "ffn_in_dgrad", dff, wl["w_ffn_in"], add=dh2,
                                                          add_scale=alpha, tk=1408))
    dh1, g["ln1_g"], g["ln1_b"] = _ew("ln1_bwd", _vjp_rows(_f_ln, 3), s,
                                      [R(dx1), R(sv["h1"]), P(row(wl["ln1_g"])), P(row(wl["ln1_b"]))],
                                      [(d, F32)], accs=[d, d])
    g["w_o"] = _mm_tn("proj_o_wgrad", sv["merged"], dh1)
    dmerged = _mm_nt("proj_o_dgrad", dh1, wl["w_o"])
    rest = sv["rest"]
    gw = _tile(math.gcd(d, REST_GL), 256, LANES)
    gl_ins = [R(rest, gw, (REST_GL + i * d) // gw, True) for i in range(N_BRANCH)]
    dg0, dg1, dg2, dpa, dpb, dpc = _ew(
        "merge_bwd", _vjp_rows(_f_merge, 6), s,
        [R(dmerged, gw, 0, True)] + gl_ins + [R(sv[k], gw, 0, True) for k in ("pa", "pb", "pc")],
        [(gw, BF16)] * 6, nj=d // gw, tr=1024)
    g["w_pa"] = _mm_tn("proj_a_wgrad", sv["ya"], dpa, col_shards=True)
    g["w_pb"] = _mm_tn("proj_b_wgrad", sv["yb"], dpb, col_shards=True)
    g["w_pc"] = _mm_tn("proj_c_wgrad", sv["yc"], dpc, col_shards=True)
    dya = _mm_nt("proj_a_dgrad", dpa, wl["w_pa"], tn=512)
    dyb = _mm_nt("proj_b_dgrad", dpb, wl["w_pb"], tn=768)
    dyc = _mm_nt("proj_c_dgrad", dpc, wl["w_pc"], tn=768)
    dyc0_a, dt_glu, g["b_glu"] = _ew("glu_bwd", lambda ct, a, t: (lambda r: (r[0], r[1], _rowsum(r[1])))(_vjp_rows(_f_glu, 2)(ct, a, t)),
                                     s, [R(dyc), R(sv["yc0"]), R(sv["t_glu"])], [(WIDTH_C, F32), (WIDTH_C, BF16)], accs=[WIDTH_C])
    g["w_glu"] = _mm_tn("glu_wgrad", sv["yc0"], dt_glu)
    dyc0 = _mm_nt("glu_dgrad", dt_glu, wl["w_glu"], add=dyc0_a, tn=768)
    uc_blk = REST_UC // WIDTH_C
    dy_lin, duc_skip, g["d_skip"] = _ew("ssm_skip_gelu_bwd", _vjp_rows(_f_ssm_out, 3), s,
                                        [R(dyc0), R(sv["y_lin"]), R(rest, WIDTH_C, uc_blk), P(row(wl["d_skip"]))],
                                        [(WIDTH_C, F32), (WIDTH_C, F32)], accs=[WIDTH_C])
    d_cd = _ssm_out_wgrad(sv["xs"], dy_lin)
    dxs = _ssm_out_dgrad(dy_lin, wl["cd"])
    gs, da = _scan_bwd(dxs, sv["xs"], wl["a3"])
    d_bd = _ssm_in_wgrad(rest, REST_UC, gs)
    duc_lin = _ssm_in_dgrad(gs, wl["bd"])
    duc = _ew("ssm_duc", lambda a, b: a + b, s, [R(duc_lin), R(duc_skip)], [(WIDTH_C, BF16)])
    d_abr = da[0].reshape(N_GROUPS_C, SSM_STATE)
    d_abi = da[1].reshape(N_GROUPS_C, SSM_STATE)
    d_lr, d_li, d_ld, d_br_t, d_bi_t = _ssm_params_bwd(wl["ssm_pin"], d_abr, d_abi,
                                                       _block_diag_extract(d_bd[0]), _block_diag_extract(d_bd[1]))
    g["lam_re"], g["lam_im"], g["log_dt"] = d_lr, d_li, d_ld[:, 0]
    un_t = lambda a: jnp.transpose(a.reshape(N_GROUPS_C, SSM_GROUP, SSM_STATE), (0, 2, 1))
    g["b_re"], g["b_im"] = un_t(d_br_t), un_t(d_bi_t)
    cd_ex = lambda a: _block_diag_extract(jnp.transpose(a, (0, 2, 1))).reshape(N_GROUPS_C, SSM_GROUP, SSM_STATE)
    g["c_re"], g["c_im"] = cd_ex(d_cd[0]), -cd_ex(d_cd[1])
    dzu, dzv, dws, dbs, g["sgu_ln_g"], g["sgu_ln_b"] = _sgu_bwd(rest, row(wl["sgu_ln_g"]), row(wl["sgu_ln_b"]),
                                                                  wl["w_tril"], wl["b_col"], dyb)
    g["w_s"] = jnp.tril(dws)
    g["b_s"] = dbs[:, :, 0]
    do_s, ya_s, lse_s = _tok_to_streams(dya), _tok_to_streams(sv["ya"]), _tok_to_streams(sv["lse"])
    dq, dk, dv, dbias = _attn_bwd(sv["qkv_s"], bias, do_s, ya_s, lse_s)
    dqkv = jnp.stack([jnp.stack(_streams_to_tok(t), 1) for t in (dq, dk, dv)], 1)
    dproj = jnp.concatenate([dqkv.reshape(s, 3 * QKV_W).astype(BF16), dzu, dzv, duc, dg0, dg1, dg2], axis=1)
    n_in = dproj.shape[1]
    cw = _tile(n_in, 1024, LANES)
    g["b_in"] = _ew("b_in_grad", lambda a: _rowsum(a.astype(F32)), s, [R(dproj, cw, 0, True)], [], accs=[cw],
                    nj=n_in // cw, tr=512)
    g["w_in"], arrived[2] = _beside(sides, 2, functools.partial(_mm_tn, "proj_in_wgrad", sv["x"], dproj, tn=768))
    dx = _mm_nt("proj_in_dgrad", dproj, wl["w_in"], add=dh1, add_scale=alpha, tk=768)
    for k in ("ln2_g", "ln2_b", "ln1_g", "ln1_b", "b_glu", "d_skip", "sgu_ln_g", "sgu_ln_b", "b_in"):
        g[k] = g[k][0]
    return dx, g, dbias, arrived


def _loss_head(y, target):
    s, d = y.shape

    def fn(yb, tb):
        err = yb - tb
        return err / d, _rowsum(err * err)

    dy, sq = _ew("loss_head", fn, s, [R(y), R(target)], [(d, F32)], accs=[d])
    return dy, 0.5 * jnp.sum(sq) / d


def _step(x, target, w, m, v):
    depth = w["w_in"].shape[0]
    alpha = (2 * depth) ** 0.25
    bias = _band_bias(w["rel_bias"])
    wb = {n: w[n].astype(BF16) for n in BIG_NAMES}
    xl = x[0]
    layers, saved = [], []
    big, own = _first_weights(wb)
    for l in range(depth):
        wl = dict(big)
        for n in SMALL:
            if n != "rel_bias":
                wl[n] = w[n][l]
        wl["w_tril"] = jnp.tril(w["w_s"][l]).astype(BF16)
        wl["b_col"] = w["b_s"][l][:, :, None]
        pin = _ssm_param_inputs(w, l)
        abr, abi, bbr, bbi = _ssm_params_fwd(pin)
        wl["ssm_pin"] = pin
        wl["a3"] = jnp.stack([abr.reshape(-1, LANES), abi.reshape(-1, LANES)], 0)
        wl["bd"] = jnp.stack([_block_diag(bbr), _block_diag(bbi)], 0)
        wl["cd"] = _c_block_diag(w, l)
        xl, sv, got = _layer_fwd(xl, wl, bias, alpha, _gather_sides(wb, l + 1) if l + 1 < depth else None,
                                 own if l == 0 else None, wb)
        if l + 1 < depth:
            big = _next_weights(wb, l + 1, got)
        layers.append(wl)
        saved.append(sv)
    dx, loss = _loss_head(xl, target[0])
    loss = lax.psum(loss, ("x", "y", "c"))
    grads = {n: [None] * depth for n in SMALL if n != "rel_bias"}
    dbias_sum = None
    pending = None
    big_grads = None
    for l in reversed(range(depth)):
        dx, g, dbias, arrived = _layer_bwd(dx, saved[l], layers[l], bias, alpha,
                                           _exchange_sides(pending) if pending is not None else None)
        dbias_sum = dbias if dbias_sum is None else dbias_sum + dbias
        if pending is not None:
            big_grads = _finish_reduce(pending, arrived, l + 1, depth, big_grads)
        pending = _start_reduce({n: g[n] for n in BIG_NAMES})
        for n in SMALL:
            if n != "rel_bias":
                grads[n][l] = g[n]
    big_grads = _finish_reduce(pending, _run_exchange(pending), 0, depth, big_grads)
    grads = {n: jnp.stack(gl, 0) for n, gl in grads.items()}
    grads.update(big_grads)
    grads["rel_bias"] = _bias_to_buckets(dbias_sum)
    small_vec = _pad_to(jnp.concatenate([grads[n].reshape(-1) for n in SMALL]), 8 * LANES).reshape(-1, LANES)
    small_sum = _sum_lead("small_grad_sum", _ag8("small_grad_gather", small_vec)).reshape(-1)
    off = 0
    for n in SMALL:
        size = math.prod(w[n].shape)
        grads[n] = small_sum[off:off + size].reshape(w[n].shape)
        off += size
    pack = lambda t: _pad_to(jnp.concatenate([t[n].reshape(-1) for n in SMALL]), 8 * LANES).reshape(-1, LANES)
    sd, sm, sv_ = _adamw("adamw_small", pack(w), small_sum.reshape(-1, LANES), pack(m), pack(v))
    delta, new_m, new_v = {}, {}, {}
    off = 0
    for n in SMALL:
        size = math.prod(w[n].shape)
        take = lambda t: t.reshape(-1)[off:off + size].reshape(w[n].shape)
        delta[n], new_m[n], new_v[n] = take(sd), take(sm), take(sv_)
        off += size
    for n in BIG_NAMES:
        delta[n], new_m[n], new_v[n] = _adamw("adamw_" + n, w[n], grads[n], m[n], v[n])
    return loss, dx[None], grads, delta, new_m, new_v


def kernel(x, w_in, b_in, rel_bias, sgu_ln_g, sgu_ln_b, w_s, b_s, lam_re, lam_im, log_dt, b_re, b_im, c_re, c_im, d_skip, w_glu, b_glu, w_pa, w_pb, w_pc, w_o, ln1_g, ln1_b, w_ffn_in, w_ffn_out, ln2_g, ln2_b, loss_target, m_w_in, m_b_in, m_rel_bias, m_sgu_ln_g, m_sgu_ln_b, m_w_s, m_b_s, m_lam_re, m_lam_im, m_log_dt, m_b_re, m_b_im, m_c_re, m_c_im, m_d_skip, m_w_glu, m_b_glu, m_w_pa, m_w_pb, m_w_pc, m_w_o, m_ln1_g, m_ln1_b, m_w_ffn_in, m_w_ffn_out, m_ln2_g, m_ln2_b, v_w_in, v_b_in, v_rel_bias, v_sgu_ln_g, v_sgu_ln_b, v_w_s, v_b_s, v_lam_re, v_lam_im, v_log_dt, v_b_re, v_b_im, v_c_re, v_c_im, v_d_skip, v_w_glu, v_b_glu, v_w_pa, v_w_pb, v_w_pc, v_w_o, v_ln1_g, v_ln1_b, v_w_ffn_in, v_w_ffn_out, v_ln2_g, v_ln2_b):
    w = dict(w_in=w_in, b_in=b_in, rel_bias=rel_bias, sgu_ln_g=sgu_ln_g, sgu_ln_b=sgu_ln_b, w_s=w_s, b_s=b_s,
             lam_re=lam_re, lam_im=lam_im, log_dt=log_dt, b_re=b_re, b_im=b_im, c_re=c_re, c_im=c_im, d_skip=d_skip,
             w_glu=w_glu, b_glu=b_glu, w_pa=w_pa, w_pb=w_pb, w_pc=w_pc, w_o=w_o, ln1_g=ln1_g, ln1_b=ln1_b,
             w_ffn_in=w_ffn_in, w_ffn_out=w_ffn_out, ln2_g=ln2_g, ln2_b=ln2_b)
    m = dict(w_in=m_w_in, b_in=m_b_in, rel_bias=m_rel_bias, sgu_ln_g=m_sgu_ln_g, sgu_ln_b=m_sgu_ln_b, w_s=m_w_s,
             b_s=m_b_s, lam_re=m_lam_re, lam_im=m_lam_im, log_dt=m_log_dt, b_re=m_b_re, b_im=m_b_im, c_re=m_c_re,
             c_im=m_c_im, d_skip=m_d_skip, w_glu=m_w_glu, b_glu=m_b_glu, w_pa=m_w_pa, w_pb=m_w_pb, w_pc=m_w_pc,
             w_o=m_w_o, ln1_g=m_ln1_g, ln1_b=m_ln1_b, w_ffn_in=m_w_ffn_in, w_ffn_out=m_w_ffn_out, ln2_g=m_ln2_g,
             ln2_b=m_ln2_b)
    v = dict(w_in=v_w_in, b_in=v_b_in, rel_bias=v_rel_bias, sgu_ln_g=v_sgu_ln_g, sgu_ln_b=v_sgu_ln_b, w_s=v_w_s,
             b_s=v_b_s, lam_re=v_lam_re, lam_im=v_lam_im, log_dt=v_log_dt, b_re=v_b_re, b_im=v_b_im, c_re=v_c_re,
             c_im=v_c_im, d_skip=v_d_skip, w_glu=v_w_glu, b_glu=v_b_glu, w_pa=v_w_pa, w_pb=v_w_pb, w_pc=v_w_pc,
             w_o=v_w_o, ln1_g=v_ln1_g, ln1_b=v_ln1_b, w_ffn_in=v_w_ffn_in, w_ffn_out=v_w_ffn_out, ln2_g=v_ln2_g,
             ln2_b=v_ln2_b)
    loss, grad_x, grads, delta, new_m, new_v = _step(x, loss_target, w, m, v)
    return (loss, grad_x, *[grads[n] for n in WEIGHTS], *[delta[n] for n in WEIGHTS],
            *[new_m[n] for n in WEIGHTS], *[new_v[n] for n in WEIGHTS])
```
